```python
import jax, jax.numpy as jnp
from jax import lax
import numpy as np

D_MODEL = 2048
BATCH = 8
SEQ = 8192
DEPTH = 2

EPS = 1e-5
N_MEM = 256
D_FF = 5632
GM_CHUNK = 128
GM_GROUPS = 4
GM_WIDTH = D_MODEL
GM_GDIM = GM_WIDTH // GM_GROUPS
SSD_WIDTH = D_MODEL
SSD_HEAD_DIM = 64
SSD_HEADS = SSD_WIDTH // SSD_HEAD_DIM
SSD_GROUPS = 4
SSD_HPG = SSD_HEADS // SSD_GROUPS
SSD_STATE = 128
SSD_CONV = 4
SSD_CHUNK = 128
SSD_BC = SSD_GROUPS * SSD_STATE
SSD_CONV_DIM = SSD_WIDTH + 2 * SSD_BC
EVEN_IN = 2 * GM_WIDTH + SSD_WIDTH + SSD_CONV_DIM + SSD_HEADS
EVEN_MIX = GM_WIDTH + SSD_WIDTH
ATT_HEADS = 32
ATT_KV_HEADS = 4
ATT_HEAD_DIM = D_MODEL // ATT_HEADS
ATT_REP = ATT_HEADS // ATT_KV_HEADS
WINDOW = 128
ATT_SCALE = ATT_HEAD_DIM ** -0.5
ROT_DIM = ATT_HEAD_DIM // 4
ROPE_THETA = 500000.0
ODD_IN = (ATT_HEADS + 2 * ATT_KV_HEADS) * ATT_HEAD_DIM
X_HEADS = 4
X_HEAD_DIM = 128
X_WIDTH = X_HEADS * X_HEAD_DIM
X_SCALE = X_HEAD_DIM ** -0.5
N_EVEN = (DEPTH + 1) // 2
N_ODD = DEPTH // 2

kernel_name = 'hybrid_gmlp_ssd_swa_macaron'


def rmsnorm(x, g):
    xf = x.astype(jnp.float32)
    y = xf * lax.rsqrt(jnp.mean(xf * xf, -1, keepdims=True) + EPS)
    return (y * g.astype(jnp.float32)).astype(x.dtype)


def swiglu(h, w_gu, w_down):
    g, u = jnp.split(h @ w_gu, 2, axis=-1)
    return (jax.nn.silu(g) * u) @ w_down


def chunked_gmlp(uv, ln_g, ln_b, w_s, b_s):
    bsz, L, _ = uv.shape
    nc = L // GM_CHUNK
    u, v = jnp.split(jax.nn.gelu(uv, approximate=False), 2, axis=-1)
    vf = v.reshape(bsz, nc, GM_CHUNK, GM_GROUPS, GM_GDIM).astype(jnp.float32)
    mu = jnp.mean(vf, -1, keepdims=True)
    var = jnp.mean(jnp.square(vf - mu), -1, keepdims=True)
    vn = ((vf - mu) * lax.rsqrt(var + EPS)).astype(v.dtype)
    vn = vn * ln_g.reshape(GM_GROUPS, GM_GDIM) + ln_b.reshape(GM_GROUPS, GM_GDIM)
    causal = jnp.tril(jnp.ones((GM_CHUNK, GM_CHUNK), dtype=bool))
    ws = jnp.where(causal[None], w_s, 0.0)
    s = jnp.einsum('gij,bcjgd->bcigd', ws, vn) + b_s.T[None, None, :, :, None]
    return u * s.reshape(bsz, L, GM_WIDTH)


def causal_dwconv(x, w, b):
    K = w.shape[0]
    L = x.shape[1]
    xp = jnp.pad(x, ((0, 0), (K - 1, 0), (0, 0)))
    y = xp[:, 0:L] * w[0]
    for k in range(1, K):
        y = y + xp[:, k:k + L] * w[k]
    return y + b


def ssd_scan(xs, dt, A, Bm, Cm):
    bsz, L = xs.shape[:2]
    nc = L // SSD_CHUNK
    Q, G, R, P, N = SSD_CHUNK, SSD_GROUPS, SSD_HPG, SSD_HEAD_DIM, SSD_STATE
    x = xs.astype(jnp.float32).reshape(bsz, nc, Q, G, R, P)
    dtc = dt.reshape(bsz, nc, Q, G, R)
    Bc = Bm.astype(jnp.float32).reshape(bsz, nc, Q, G, N)
    Cc = Cm.astype(jnp.float32).reshape(bsz, nc, Q, G, N)
    a = jnp.moveaxis(dtc * A.reshape(G, R), 2, -1)
    a_cs = jnp.cumsum(a, axis=-1)
    xdt = x * dtc[..., None]
    causal = jnp.tril(jnp.ones((Q, Q), dtype=bool))
    seg = a_cs[..., :, None] - a_cs[..., None, :]
    Lmat = jnp.where(causal, jnp.exp(jnp.where(causal, seg, 0.0)), 0.0)
    cb = jnp.einsum('bcign,bcjgn->bcgij', Cc, Bc)
    y_diag = jnp.einsum('bcgij,bcgrij,bcjgrp->bcigrp', cb, Lmat, xdt)
    decay_states = jnp.exp(a_cs[..., -1:] - a_cs)
    states = jnp.einsum('bcjgn,bcgrj,bcjgrp->bcgrpn', Bc, decay_states, xdt)
    chunk_decay = jnp.exp(a_cs[..., -1])

    def step(h, inp):
        s_c, d_c = inp
        return h * d_c[..., None, None] + s_c, h

    h0 = jnp.zeros((bsz, G, R, P, N), jnp.float32)
    _, prev = lax.scan(step, h0, (jnp.moveaxis(states, 1, 0), jnp.moveaxis(chunk_decay, 1, 0)))
    prev = jnp.moveaxis(prev, 0, 1)
    y_off = jnp.einsum('bcign,bcgrpn,bcgri->bcigrp', Cc, prev, jnp.exp(a_cs))
    return (y_diag + y_off).reshape(bsz, L, G * R * P)


def even_mixer(h, w_in, gm_ln_g, gm_ln_b, gm_ws, gm_bs, conv_w, conv_b, dt_bias, a_log, d_skip, ssd_norm, w_out):
    bsz, L, _ = h.shape
    proj = h @ w_in
    c0 = 2 * GM_WIDTH
    c1 = c0 + SSD_WIDTH
    c2 = c1 + SSD_CONV_DIM
    uv, z, xbc, dt_raw = jnp.split(proj, [c0, c1, c2], axis=-1)
    a_out = chunked_gmlp(uv, gm_ln_g, gm_ln_b, gm_ws, gm_bs)
    xbc = jax.nn.silu(causal_dwconv(xbc, conv_w, conv_b))
    xs, Bm, Cm = jnp.split(xbc, [SSD_WIDTH, SSD_WIDTH + SSD_BC], axis=-1)
    dt = jax.nn.softplus(dt_raw.astype(jnp.float32) + dt_bias.astype(jnp.float32))
    A = -jnp.exp(a_log.astype(jnp.float32))
    xh = xs.reshape(bsz, L, SSD_HEADS, SSD_HEAD_DIM)
    y = ssd_scan(xh, dt, A, Bm.reshape(bsz, L, SSD_GROUPS, SSD_STATE), Cm.reshape(bsz, L, SSD_GROUPS, SSD_STATE))
    y = y + (xh.astype(jnp.float32) * d_skip.astype(jnp.float32)[:, None]).reshape(bsz, L, SSD_WIDTH)
    yg = (y * jax.nn.silu(z.astype(jnp.float32))).reshape(bsz, L, SSD_GROUPS, SSD_WIDTH // SSD_GROUPS)
    yg = yg * lax.rsqrt(jnp.mean(yg * yg, -1, keepdims=True) + EPS)
    b_out = (yg.reshape(bsz, L, SSD_WIDTH) * ssd_norm.astype(jnp.float32)).astype(h.dtype)
    return jnp.concatenate([a_out, b_out], axis=-1) @ w_out


def rope_partial(x, cos, sin):
    half = ROT_DIM // 2
    x1 = x[..., :half]
    x2 = x[..., half:ROT_DIM]
    return jnp.concatenate([x1 * cos - x2 * sin, x2 * cos + x1 * sin, x[..., ROT_DIM:]], axis=-1)


def swa_sinks(h, w_qkv, b_qkv, sinks, w_o, cos, sin):
    bsz, L, _ = h.shape
    nb = L // WINDOW
    W = WINDOW
    qkv = h @ w_qkv + b_qkv
    q, k, v = jnp.split(qkv, [ATT_HEADS * ATT_HEAD_DIM, (ATT_HEADS + ATT_KV_HEADS) * ATT_HEAD_DIM], axis=-1)
    q = rope_partial(q.reshape(bsz, L, ATT_HEADS, ATT_HEAD_DIM), cos, sin)
    k = rope_partial(k.reshape(bsz, L, ATT_KV_HEADS, ATT_HEAD_DIM), cos, sin)
    q = q.reshape(bsz, nb, W, ATT_KV_HEADS, ATT_REP, ATT_HEAD_DIM)
    kb = k.reshape(bsz, nb, W, ATT_KV_HEADS, ATT_HEAD_DIM)
    vb = v.reshape(bsz, nb, W, ATT_KV_HEADS, ATT_HEAD_DIM)
    pad = ((0, 0), (1, 0), (0, 0), (0, 0), (0, 0))
    kcat = jnp.concatenate([jnp.pad(kb, pad)[:, :-1], kb], axis=2)
    vcat = jnp.concatenate([jnp.pad(vb, pad)[:, :-1], vb], axis=2)
    s = jnp.einsum('bnqkrd,bnskd->bnkrqs', q, kcat).astype(jnp.float32) * ATT_SCALE
    iq = jnp.arange(W)[:, None]
    js = jnp.arange(2 * W)[None, :]
    rel = iq + W - js
    band = (rel >= 0) & (rel < WINDOW)
    blk = jnp.arange(nb)[:, None, None]
    mask = band[None] & ((blk > 0) | (js >= W)[None])
    s = jnp.where(mask[None, :, None, None], s, -jnp.inf)
    sink = sinks.astype(jnp.float32).reshape(ATT_KV_HEADS, ATT_REP)[None, None, :, :, None, None]
    m = jnp.maximum(jnp.max(s, -1, keepdims=True), sink)
    p = jnp.exp(s - m)
    pr = (p / (jnp.sum(p, -1, keepdims=True) + jnp.exp(sink - m))).astype(vcat.dtype)
    o = jnp.einsum('bnkrqs,bnskd->bnqkrd', pr, vcat).reshape(bsz, L, ATT_HEADS * ATT_HEAD_DIM)
    return o @ w_o


def mem_cross_attn(h, mem_n, w_q, w_kv, w_o):
    bsz, L, _ = h.shape
    q = (h @ w_q).reshape(bsz, L, X_HEADS, X_HEAD_DIM)
    k, v = jnp.split(mem_n @ w_kv, 2, axis=-1)
    k = k.reshape(bsz, -1, X_HEADS, X_HEAD_DIM)
    v = v.reshape(bsz, -1, X_HEADS, X_HEAD_DIM)
    s = jnp.einsum('blhd,bmhd->bhlm', q, k).astype(jnp.float32) * X_SCALE
    p = jax.nn.softmax(s, axis=-1).astype(v.dtype)
    o = jnp.einsum('bhlm,bmhd->blhd', p, v).reshape(bsz, L, X_WIDTH)
    return o @ w_o


def _fwd_setup_inputs(seed: int = 0) -> dict:
    key = jax.random.key(seed)
    ks = list(jax.random.split(key, 40))
    f32 = jnp.float32

    def nrm(i, shape, scale):
        return jax.random.normal(ks[i], shape, f32) * scale

    def gain(i, shape):
        return 1.0 + 0.02 * jax.random.normal(ks[i], shape, f32)

    x = nrm(0, (BATCH, SEQ, D_MODEL), 1.0)
    mem = nrm(1, (BATCH, N_MEM, D_MODEL), 1.0)
    start = jax.random.randint(ks[2], (BATCH, 1), 0, 4096, dtype=jnp.int32)
    positions = start + jnp.arange(SEQ, dtype=jnp.int32)[None, :]
    dt0 = jnp.exp(jax.random.uniform(ks[20], (N_EVEN, SSD_HEADS), f32, np.log(1e-3), np.log(1e-1)))
    return {
        'x': x,
        'mem': mem,
        'positions': positions,
        'norm_ffn1': gain(3, (DEPTH, D_MODEL)),
        'w_ffn1_gu': nrm(4, (DEPTH, D_MODEL, 2 * D_FF), D_MODEL ** -0.5),
        'w_ffn1_down': nrm(5, (DEPTH, D_FF, D_MODEL), D_FF ** -0.5),
        'norm_mix': gain(6, (DEPTH, D_MODEL)),
        'w_in_even': nrm(7, (N_EVEN, D_MODEL, EVEN_IN), D_MODEL ** -0.5),
        'gm_ln_g': gain(8, (N_EVEN, GM_WIDTH)),
        'gm_ln_b': nrm(9, (N_EVEN, GM_WIDTH), 0.02),
        'gm_ws': nrm(10, (N_EVEN, GM_GROUPS, GM_CHUNK, GM_CHUNK), GM_CHUNK ** -0.5),
        'gm_bs': 1.0 + nrm(11, (N_EVEN, GM_GROUPS, GM_CHUNK), 0.1),
        'conv_w': nrm(12, (N_EVEN, SSD_CONV, SSD_CONV_DIM), SSD_CONV ** -0.5),
        'conv_b': nrm(13, (N_EVEN, SSD_CONV_DIM), 0.02),
        'dt_bias': dt0 + jnp.log(-jnp.expm1(-dt0)),
        'a_log': jnp.log(jax.random.uniform(ks[14], (N_EVEN, SSD_HEADS), f32, 1.0, 16.0)),
        'd_skip': 1.0 + nrm(15, (N_EVEN, SSD_HEADS), 0.1),
        'ssd_norm': gain(16, (N_EVEN, SSD_WIDTH)),
        'w_out_even': nrm(17, (N_EVEN, EVEN_MIX, D_MODEL), EVEN_MIX ** -0.5),
        'w_qkv': nrm(18, (N_ODD, D_MODEL, ODD_IN), D_MODEL ** -0.5),
        'b_qkv': nrm(19, (N_ODD, ODD_IN), 0.02),
        'sinks': nrm(21, (N_ODD, ATT_HEADS), 0.5),
        'w_o_odd': nrm(22, (N_ODD, ATT_HEADS * ATT_HEAD_DIM, D_MODEL), (ATT_HEADS * ATT_HEAD_DIM) ** -0.5),
        'norm_xq': gain(23, (DEPTH, D_MODEL)),
        'norm_mem': gain(24, (DEPTH, D_MODEL)),
        'w_xq': nrm(25, (DEPTH, D_MODEL, X_WIDTH), D_MODEL ** -0.5),
        'w_xkv': nrm(26, (DEPTH, D_MODEL, 2 * X_WIDTH), D_MODEL ** -0.5),
        'w_xo': nrm(27, (DEPTH, X_WIDTH, D_MODEL), X_WIDTH ** -0.5),
        'norm_ffn2': gain(28, (DEPTH, D_MODEL)),
        'w_ffn2_gu': nrm(29, (DEPTH, D_MODEL, 2 * D_FF), D_MODEL ** -0.5),
        'w_ffn2_down': nrm(30, (DEPTH, D_FF, D_MODEL), D_FF ** -0.5),
        'final_norm': gain(31, (D_MODEL,)),
    }


def _fwd_reference(x, mem, positions, norm_ffn1, w_ffn1_gu, w_ffn1_down, norm_mix, w_in_even, gm_ln_g, gm_ln_b,
              gm_ws, gm_bs, conv_w, conv_b, dt_bias, a_log, d_skip, ssd_norm, w_out_even, w_qkv, b_qkv, sinks,
              w_o_odd, norm_xq, norm_mem, w_xq, w_xkv, w_xo, norm_ffn2, w_ffn2_gu, w_ffn2_down, final_norm):
    inv_freq = ROPE_THETA ** (-jnp.arange(0, ROT_DIM, 2, dtype=jnp.float32) / ROT_DIM)
    ang = positions.astype(jnp.float32)[..., None] * inv_freq
    cos = jnp.cos(ang)[:, :, None, :].astype(x.dtype)
    sin = jnp.sin(ang)[:, :, None, :].astype(x.dtype)
    for i in range(DEPTH):
        j = i // 2
        x = x + 0.5 * swiglu(rmsnorm(x, norm_ffn1[i]), w_ffn1_gu[i], w_ffn1_down[i])
        h = rmsnorm(x, norm_mix[i])
        if i % 2 == 0:
            x = x + even_mixer(h, w_in_even[j], gm_ln_g[j], gm_ln_b[j], gm_ws[j], gm_bs[j], conv_w[j], conv_b[j],
                               dt_bias[j], a_log[j], d_skip[j], ssd_norm[j], w_out_even[j])
        else:
            x = x + swa_sinks(h, w_qkv[j], b_qkv[j], sinks[j], w_o_odd[j], cos, sin)
        x = x + mem_cross_attn(rmsnorm(x, norm_xq[i]), rmsnorm(mem, norm_mem[i]), w_xq[i], w_xkv[i], w_xo[i])
        x = x + 0.5 * swiglu(rmsnorm(x, norm_ffn2[i]), w_ffn2_gu[i], w_ffn2_down[i])
    return rmsnorm(x, final_norm)


import jax as _jax
import jax.numpy as _jnp

TWIN_FORMAT = 'train_step'
FWD_PARAMS = ['x', 'mem', 'positions', 'norm_ffn1', 'w_ffn1_gu', 'w_ffn1_down', 'norm_mix', 'w_in_even', 'gm_ln_g', 'gm_ln_b', 'gm_ws', 'gm_bs', 'conv_w', 'conv_b', 'dt_bias', 'a_log', 'd_skip', 'ssd_norm', 'w_out_even', 'w_qkv', 'b_qkv', 'sinks', 'w_o_odd', 'norm_xq', 'norm_mem', 'w_xq', 'w_xkv', 'w_xo', 'norm_ffn2', 'w_ffn2_gu', 'w_ffn2_down', 'final_norm']
TWIN_WEIGHTS = ['norm_ffn1', 'w_ffn1_gu', 'w_ffn1_down', 'norm_mix', 'w_in_even', 'gm_ln_g', 'gm_ln_b', 'gm_ws', 'gm_bs', 'conv_w', 'conv_b', 'dt_bias', 'a_log', 'd_skip', 'ssd_norm', 'w_out_even', 'w_qkv', 'b_qkv', 'sinks', 'w_o_odd', 'norm_xq', 'norm_mem', 'w_xq', 'w_xkv', 'w_xo', 'norm_ffn2', 'w_ffn2_gu', 'w_ffn2_down', 'final_norm']
TWIN_DIFF_INPUT = 'x'
TWIN_INPUTS = ['x', 'mem', 'positions', 'norm_ffn1', 'w_ffn1_gu', 'w_ffn1_down', 'norm_mix', 'w_in_even', 'gm_ln_g', 'gm_ln_b', 'gm_ws', 'gm_bs', 'conv_w', 'conv_b', 'dt_bias', 'a_log', 'd_skip', 'ssd_norm', 'w_out_even', 'w_qkv', 'b_qkv', 'sinks', 'w_o_odd', 'norm_xq', 'norm_mem', 'w_xq', 'w_xkv', 'w_xo', 'norm_ffn2', 'w_ffn2_gu', 'w_ffn2_down', 'final_norm', 'loss_target', 'm_norm_ffn1', 'm_w_ffn1_gu', 'm_w_ffn1_down', 'm_norm_mix', 'm_w_in_even', 'm_gm_ln_g', 'm_gm_ln_b', 'm_gm_ws', 'm_gm_bs', 'm_conv_w', 'm_conv_b', 'm_dt_bias', 'm_a_log', 'm_d_skip', 'm_ssd_norm', 'm_w_out_even', 'm_w_qkv', 'm_b_qkv', 'm_sinks', 'm_w_o_odd', 'm_norm_xq', 'm_norm_mem', 'm_w_xq', 'm_w_xkv', 'm_w_xo', 'm_norm_ffn2', 'm_w_ffn2_gu', 'm_w_ffn2_down', 'm_final_norm', 'v_norm_ffn1', 'v_w_ffn1_gu', 'v_w_ffn1_down', 'v_norm_mix', 'v_w_in_even', 'v_gm_ln_g', 'v_gm_ln_b', 'v_gm_ws', 'v_gm_bs', 'v_conv_w', 'v_conv_b', 'v_dt_bias', 'v_a_log', 'v_d_skip', 'v_ssd_norm', 'v_w_out_even', 'v_w_qkv', 'v_b_qkv', 'v_sinks', 'v_w_o_odd', 'v_norm_xq', 'v_norm_mem', 'v_w_xq', 'v_w_xkv', 'v_w_xo', 'v_norm_ffn2', 'v_w_ffn2_gu', 'v_w_ffn2_down', 'v_final_norm']
TWIN_OUTPUTS = ['loss', 'grad_x', 'grad_norm_ffn1', 'grad_w_ffn1_gu', 'grad_w_ffn1_down', 'grad_norm_mix', 'grad_w_in_even', 'grad_gm_ln_g', 'grad_gm_ln_b', 'grad_gm_ws', 'grad_gm_bs', 'grad_conv_w', 'grad_conv_b', 'grad_dt_bias', 'grad_a_log', 'grad_d_skip', 'grad_ssd_norm', 'grad_w_out_even', 'grad_w_qkv', 'grad_b_qkv', 'grad_sinks', 'grad_w_o_odd', 'grad_norm_xq', 'grad_norm_mem', 'grad_w_xq', 'grad_w_xkv', 'grad_w_xo', 'grad_norm_ffn2', 'grad_w_ffn2_gu', 'grad_w_ffn2_down', 'grad_final_norm', 'delta_norm_ffn1', 'delta_w_ffn1_gu', 'delta_w_ffn1_down', 'delta_norm_mix', 'delta_w_in_even', 'delta_gm_ln_g', 'delta_gm_ln_b', 'delta_gm_ws', 'delta_gm_bs', 'delta_conv_w', 'delta_conv_b', 'delta_dt_bias', 'delta_a_log', 'delta_d_skip', 'delta_ssd_norm', 'delta_w_out_even', 'delta_w_qkv', 'delta_b_qkv', 'delta_sinks', 'delta_w_o_odd', 'delta_norm_xq', 'delta_norm_mem', 'delta_w_xq', 'delta_w_xkv', 'delta_w_xo', 'delta_norm_ffn2', 'delta_w_ffn2_gu', 'delta_w_ffn2_down', 'delta_final_norm', 'new_m_norm_ffn1', 'new_m_w_ffn1_gu', 'new_m_w_ffn1_down', 'new_m_norm_mix', 'new_m_w_in_even', 'new_m_gm_ln_g', 'new_m_gm_ln_b', 'new_m_gm_ws', 'new_m_gm_bs', 'new_m_conv_w', 'new_m_conv_b', 'new_m_dt_bias', 'new_m_a_log', 'new_m_d_skip', 'new_m_ssd_norm', 'new_m_w_out_even', 'new_m_w_qkv', 'new_m_b_qkv', 'new_m_sinks', 'new_m_w_o_odd', 'new_m_norm_xq', 'new_m_norm_mem', 'new_m_w_xq', 'new_m_w_xkv', 'new_m_w_xo', 'new_m_norm_ffn2', 'new_m_w_ffn2_gu', 'new_m_w_ffn2_down', 'new_m_final_norm', 'new_v_norm_ffn1', 'new_v_w_ffn1_gu', 'new_v_w_ffn1_down', 'new_v_norm_mix', 'new_v_w_in_even', 'new_v_gm_ln_g', 'new_v_gm_ln_b', 'new_v_gm_ws', 'new_v_gm_bs', 'new_v_conv_w', 'new_v_conv_b', 'new_v_dt_bias', 'new_v_a_log', 'new_v_d_skip', 'new_v_ssd_norm', 'new_v_w_out_even', 'new_v_w_qkv', 'new_v_b_qkv', 'new_v_sinks', 'new_v_w_o_odd', 'new_v_norm_xq', 'new_v_norm_mem', 'new_v_w_xq', 'new_v_w_xkv', 'new_v_w_xo', 'new_v_norm_ffn2', 'new_v_w_ffn2_gu', 'new_v_w_ffn2_down', 'new_v_final_norm']
TWIN_LEAF_KINDS = {'loss': 'loss', 'grad_x': 'grad_x', 'grad_norm_ffn1': 'grad_w', 'grad_w_ffn1_gu': 'grad_w', 'grad_w_ffn1_down': 'grad_w', 'grad_norm_mix': 'grad_w', 'grad_w_in_even': 'grad_w', 'grad_gm_ln_g': 'grad_w', 'grad_gm_ln_b': 'grad_w', 'grad_gm_ws': 'grad_w', 'grad_gm_bs': 'grad_w', 'grad_conv_w': 'grad_w', 'grad_conv_b': 'grad_w', 'grad_dt_bias': 'grad_w', 'grad_a_log': 'grad_w', 'grad_d_skip': 'grad_w', 'grad_ssd_norm': 'grad_w', 'grad_w_out_even': 'grad_w', 'grad_w_qkv': 'grad_w', 'grad_b_qkv': 'grad_w', 'grad_sinks': 'grad_w', 'grad_w_o_odd': 'grad_w', 'grad_norm_xq': 'grad_w', 'grad_norm_mem': 'grad_w', 'grad_w_xq': 'grad_w', 'grad_w_xkv': 'grad_w', 'grad_w_xo': 'grad_w', 'grad_norm_ffn2': 'grad_w', 'grad_w_ffn2_gu': 'grad_w', 'grad_w_ffn2_down': 'grad_w', 'grad_final_norm': 'grad_w', 'delta_norm_ffn1': 'delta_w', 'delta_w_ffn1_gu': 'delta_w', 'delta_w_ffn1_down': 'delta_w', 'delta_norm_mix': 'delta_w', 'delta_w_in_even': 'delta_w', 'delta_gm_ln_g': 'delta_w', 'delta_gm_ln_b': 'delta_w', 'delta_gm_ws': 'delta_w', 'delta_gm_bs': 'delta_w', 'delta_conv_w': 'delta_w', 'delta_conv_b': 'delta_w', 'delta_dt_bias': 'delta_w', 'delta_a_log': 'delta_w', 'delta_d_skip': 'delta_w', 'delta_ssd_norm': 'delta_w', 'delta_w_out_even': 'delta_w', 'delta_w_qkv': 'delta_w', 'delta_b_qkv': 'delta_w', 'delta_sinks': 'delta_w', 'delta_w_o_odd': 'delta_w', 'delta_norm_xq': 'delta_w', 'delta_norm_mem': 'delta_w', 'delta_w_xq': 'delta_w', 'delta_w_xkv': 'delta_w', 'delta_w_xo': 'delta_w', 'delta_norm_ffn2': 'delta_w', 'delta_w_ffn2_gu': 'delta_w', 'delta_w_ffn2_down': 'delta_w', 'delta_final_norm': 'delta_w', 'new_m_norm_ffn1': 'new_m', 'new_m_w_ffn1_gu': 'new_m', 'new_m_w_ffn1_down': 'new_m', 'new_m_norm_mix': 'new_m', 'new_m_w_in_even': 'new_m', 'new_m_gm_ln_g': 'new_m', 'new_m_gm_ln_b': 'new_m', 'new_m_gm_ws': 'new_m', 'new_m_gm_bs': 'new_m', 'new_m_conv_w': 'new_m', 'new_m_conv_b': 'new_m', 'new_m_dt_bias': 'new_m', 'new_m_a_log': 'new_m', 'new_m_d_skip': 'new_m', 'new_m_ssd_norm': 'new_m', 'new_m_w_out_even': 'new_m', 'new_m_w_qkv': 'new_m', 'new_m_b_qkv': 'new_m', 'new_m_sinks': 'new_m', 'new_m_w_o_odd': 'new_m', 'new_m_norm_xq': 'new_m', 'new_m_norm_mem': 'new_m', 'new_m_w_xq': 'new_m', 'new_m_w_xkv': 'new_m', 'new_m_w_xo': 'new_m', 'new_m_norm_ffn2': 'new_m', 'new_m_w_ffn2_gu': 'new_m', 'new_m_w_ffn2_down': 'new_m', 'new_m_final_norm': 'new_m', 'new_v_norm_ffn1': 'new_v', 'new_v_w_ffn1_gu': 'new_v', 'new_v_w_ffn1_down': 'new_v', 'new_v_norm_mix': 'new_v', 'new_v_w_in_even': 'new_v', 'new_v_gm_ln_g': 'new_v', 'new_v_gm_ln_b': 'new_v', 'new_v_gm_ws': 'new_v', 'new_v_gm_bs': 'new_v', 'new_v_conv_w': 'new_v', 'new_v_conv_b': 'new_v', 'new_v_dt_bias': 'new_v', 'new_v_a_log': 'new_v', 'new_v_d_skip': 'new_v', 'new_v_ssd_norm': 'new_v', 'new_v_w_out_even': 'new_v', 'new_v_w_qkv': 'new_v', 'new_v_b_qkv': 'new_v', 'new_v_sinks': 'new_v', 'new_v_w_o_odd': 'new_v', 'new_v_norm_xq': 'new_v', 'new_v_norm_mem': 'new_v', 'new_v_w_xq': 'new_v', 'new_v_w_xkv': 'new_v', 'new_v_w_xo': 'new_v', 'new_v_norm_ffn2': 'new_v', 'new_v_w_ffn2_gu': 'new_v', 'new_v_w_ffn2_down': 'new_v', 'new_v_final_norm': 'new_v'}


def _forward(args):
    return _fwd_reference(*[args[k] for k in FWD_PARAMS])


def _output_shape():
    def fwd():
        inp = _fwd_setup_inputs(0)
        return _fwd_reference(*[inp[k] for k in FWD_PARAMS])
    out = _jax.eval_shape(fwd)
    return out.shape, out.dtype

N_MICROBATCH = 1
ADAM_LR = 0.001
ADAM_B1 = 0.9
ADAM_B2 = 0.999
ADAM_EPS = 1e-08
ADAM_WD = 0.01
ADAM_STEP = 10
PER_EXAMPLE_BATCH_AXIS = {'x': 0, 'mem': 0, 'positions': 0, 'loss_target': 0}
SHARED_INPUTS = []
_WEIGHT_DTYPES = {'norm_ffn1': _jnp.float32, 'w_ffn1_gu': _jnp.float32, 'w_ffn1_down': _jnp.float32, 'norm_mix': _jnp.float32, 'w_in_even': _jnp.float32, 'gm_ln_g': _jnp.float32, 'gm_ln_b': _jnp.float32, 'gm_ws': _jnp.float32, 'gm_bs': _jnp.float32, 'conv_w': _jnp.float32, 'conv_b': _jnp.float32, 'dt_bias': _jnp.float32, 'a_log': _jnp.float32, 'd_skip': _jnp.float32, 'ssd_norm': _jnp.float32, 'w_out_even': _jnp.float32, 'w_qkv': _jnp.float32, 'b_qkv': _jnp.float32, 'sinks': _jnp.float32, 'w_o_odd': _jnp.float32, 'norm_xq': _jnp.float32, 'norm_mem': _jnp.float32, 'w_xq': _jnp.float32, 'w_xkv': _jnp.float32, 'w_xo': _jnp.float32, 'norm_ffn2': _jnp.float32, 'w_ffn2_gu': _jnp.float32, 'w_ffn2_down': _jnp.float32, 'final_norm': _jnp.float32}
MOMENT_SCALE = {'norm_ffn1': 5.499250e-02, 'w_ffn1_gu': 2.293551e-02, 'w_ffn1_down': 3.743395e-02, 'norm_mix': 9.109020e-02, 'w_in_even': 5.842846e-02, 'gm_ln_g': 3.233218e-02, 'gm_ln_b': 3.347667e-02, 'gm_ws': 6.356798e-02, 'gm_bs': 8.575598e-02, 'conv_w': 6.110804e-02, 'conv_b': 8.043642e-02, 'dt_bias': 1.328918e-01, 'a_log': 3.814071e-01, 'd_skip': 4.130929e-01, 'ssd_norm': 7.018817e-02, 'w_out_even': 9.291201e-02, 'w_qkv': 2.782948e-02, 'b_qkv': 1.208805e-01, 'sinks': 1.740500e-02, 'w_o_odd': 2.313875e-02, 'norm_xq': 9.432881e-03, 'norm_mem': 1.427814e-02, 'w_xq': 1.902307e-02, 'w_xkv': 1.981780e-02, 'w_xo': 1.030404e-02, 'norm_ffn2': 3.843086e-02, 'w_ffn2_gu': 1.611168e-02, 'w_ffn2_down': 2.632177e-02, 'final_norm': 3.196566e+01}


def _to_microbatches(a, axis):
    t = _jnp.moveaxis(a, axis, 0)
    t = t.reshape((N_MICROBATCH, t.shape[0] // N_MICROBATCH) + t.shape[1:])
    return _jnp.moveaxis(t, 1, axis + 1)


def setup_inputs(seed: int = 0) -> dict:
    inp = _fwd_setup_inputs(seed)
    key = _jax.random.fold_in(_jax.random.key(seed), 7919)
    shape, _ = _output_shape()
    out = dict(inp)
    out["loss_target"] = _jax.random.normal(_jax.random.fold_in(key, 0), shape, _jnp.float32)
    for i, name in enumerate(TWIN_WEIGHTS):
        w = inp[name].astype(_jnp.float32)
        if MOMENT_SCALE is None:
            s = _jnp.sqrt(_jnp.mean(_jnp.square(w)) + 1e-30)
        else:
            s = MOMENT_SCALE[name]
        km, kv = _jax.random.split(_jax.random.fold_in(key, i + 1))
        out[name] = w
        out["m_" + name] = s * _jax.random.normal(km, w.shape, _jnp.float32)
        out["v_" + name] = (s * s) * _jax.random.uniform(kv, w.shape, _jnp.float32, 0.5, 1.5)
    if N_MICROBATCH > 1:
        for name, axis in PER_EXAMPLE_BATCH_AXIS.items():
            out[name] = _to_microbatches(out[name], axis)
    return {'x': out['x'], 'mem': out['mem'], 'positions': out['positions'], 'norm_ffn1': out['norm_ffn1'], 'w_ffn1_gu': out['w_ffn1_gu'], 'w_ffn1_down': out['w_ffn1_down'], 'norm_mix': out['norm_mix'], 'w_in_even': out['w_in_even'], 'gm_ln_g': out['gm_ln_g'], 'gm_ln_b': out['gm_ln_b'], 'gm_ws': out['gm_ws'], 'gm_bs': out['gm_bs'], 'conv_w': out['conv_w'], 'conv_b': out['conv_b'], 'dt_bias': out['dt_bias'], 'a_log': out['a_log'], 'd_skip': out['d_skip'], 'ssd_norm': out['ssd_norm'], 'w_out_even': out['w_out_even'], 'w_qkv': out['w_qkv'], 'b_qkv': out['b_qkv'], 'sinks': out['sinks'], 'w_o_odd': out['w_o_odd'], 'norm_xq': out['norm_xq'], 'norm_mem': out['norm_mem'], 'w_xq': out['w_xq'], 'w_xkv': out['w_xkv'], 'w_xo': out['w_xo'], 'norm_ffn2': out['norm_ffn2'], 'w_ffn2_gu': out['w_ffn2_gu'], 'w_ffn2_down': out['w_ffn2_down'], 'final_norm': out['final_norm'], 'loss_target': out['loss_target'], 'm_norm_ffn1': out['m_norm_ffn1'], 'm_w_ffn1_gu': out['m_w_ffn1_gu'], 'm_w_ffn1_down': out['m_w_ffn1_down'], 'm_norm_mix': out['m_norm_mix'], 'm_w_in_even': out['m_w_in_even'], 'm_gm_ln_g': out['m_gm_ln_g'], 'm_gm_ln_b': out['m_gm_ln_b'], 'm_gm_ws': out['m_gm_ws'], 'm_gm_bs': out['m_gm_bs'], 'm_conv_w': out['m_conv_w'], 'm_conv_b': out['m_conv_b'], 'm_dt_bias': out['m_dt_bias'], 'm_a_log': out['m_a_log'], 'm_d_skip': out['m_d_skip'], 'm_ssd_norm': out['m_ssd_norm'], 'm_w_out_even': out['m_w_out_even'], 'm_w_qkv': out['m_w_qkv'], 'm_b_qkv': out['m_b_qkv'], 'm_sinks': out['m_sinks'], 'm_w_o_odd': out['m_w_o_odd'], 'm_norm_xq': out['m_norm_xq'], 'm_norm_mem': out['m_norm_mem'], 'm_w_xq': out['m_w_xq'], 'm_w_xkv': out['m_w_xkv'], 'm_w_xo': out['m_w_xo'], 'm_norm_ffn2': out['m_norm_ffn2'], 'm_w_ffn2_gu': out['m_w_ffn2_gu'], 'm_w_ffn2_down': out['m_w_ffn2_down'], 'm_final_norm': out['m_final_norm'], 'v_norm_ffn1': out['v_norm_ffn1'], 'v_w_ffn1_gu': out['v_w_ffn1_gu'], 'v_w_ffn1_down': out['v_w_ffn1_down'], 'v_norm_mix': out['v_norm_mix'], 'v_w_in_even': out['v_w_in_even'], 'v_gm_ln_g': out['v_gm_ln_g'], 'v_gm_ln_b': out['v_gm_ln_b'], 'v_gm_ws': out['v_gm_ws'], 'v_gm_bs': out['v_gm_bs'], 'v_conv_w': out['v_conv_w'], 'v_conv_b': out['v_conv_b'], 'v_dt_bias': out['v_dt_bias'], 'v_a_log': out['v_a_log'], 'v_d_skip': out['v_d_skip'], 'v_ssd_norm': out['v_ssd_norm'], 'v_w_out_even': out['v_w_out_even'], 'v_w_qkv': out['v_w_qkv'], 'v_b_qkv': out['v_b_qkv'], 'v_sinks': out['v_sinks'], 'v_w_o_odd': out['v_w_o_odd'], 'v_norm_xq': out['v_norm_xq'], 'v_norm_mem': out['v_norm_mem'], 'v_w_xq': out['v_w_xq'], 'v_w_xkv': out['v_w_xkv'], 'v_w_xo': out['v_w_xo'], 'v_norm_ffn2': out['v_norm_ffn2'], 'v_w_ffn2_gu': out['v_w_ffn2_gu'], 'v_w_ffn2_down': out['v_w_ffn2_down'], 'v_final_norm': out['v_final_norm']}


def _loss(weights, diff, rest, loss_target):
    with _jax.named_scope("forward"):
        args = {**rest, TWIN_DIFF_INPUT: diff, **{k: w.astype(_WEIGHT_DTYPES[k]) for k, w in weights.items()}}
        y = _forward(args)
    with _jax.named_scope("loss_head"):
        err = _jnp.square(y.astype(_jnp.float32) - loss_target)
        return 0.5 * _jnp.sum(_jnp.mean(err, axis=-1)) if err.ndim else 0.5 * err


def _adamw(w, g, m, v):
    m = ADAM_B1 * m + (1.0 - ADAM_B1) * g
    v = ADAM_B2 * v + (1.0 - ADAM_B2) * _jnp.square(g)
    m_hat = m / (1.0 - ADAM_B1 ** ADAM_STEP)
    v_hat = v / (1.0 - ADAM_B2 ** ADAM_STEP)
    delta = -ADAM_LR * (m_hat / (_jnp.sqrt(v_hat) + ADAM_EPS) + ADAM_WD * w)
    return delta, m, v


def reference(x, mem, positions, norm_ffn1, w_ffn1_gu, w_ffn1_down, norm_mix, w_in_even, gm_ln_g, gm_ln_b, gm_ws, gm_bs, conv_w, conv_b, dt_bias, a_log, d_skip, ssd_norm, w_out_even, w_qkv, b_qkv, sinks, w_o_odd, norm_xq, norm_mem, w_xq, w_xkv, w_xo, norm_ffn2, w_ffn2_gu, w_ffn2_down, final_norm, loss_target, m_norm_ffn1, m_w_ffn1_gu, m_w_ffn1_down, m_norm_mix, m_w_in_even, m_gm_ln_g, m_gm_ln_b, m_gm_ws, m_gm_bs, m_conv_w, m_conv_b, m_dt_bias, m_a_log, m_d_skip, m_ssd_norm, m_w_out_even, m_w_qkv, m_b_qkv, m_sinks, m_w_o_odd, m_norm_xq, m_norm_mem, m_w_xq, m_w_xkv, m_w_xo, m_norm_ffn2, m_w_ffn2_gu, m_w_ffn2_down, m_final_norm, v_norm_ffn1, v_w_ffn1_gu, v_w_ffn1_down, v_norm_mix, v_w_in_even, v_gm_ln_g, v_gm_ln_b, v_gm_ws, v_gm_bs, v_conv_w, v_conv_b, v_dt_bias, v_a_log, v_d_skip, v_ssd_norm, v_w_out_even, v_w_qkv, v_b_qkv, v_sinks, v_w_o_odd, v_norm_xq, v_norm_mem, v_w_xq, v_w_xkv, v_w_xo, v_norm_ffn2, v_w_ffn2_gu, v_w_ffn2_down, v_final_norm):
    given = dict(x=x, mem=mem, positions=positions, norm_ffn1=norm_ffn1, w_ffn1_gu=w_ffn1_gu, w_ffn1_down=w_ffn1_down, norm_mix=norm_mix, w_in_even=w_in_even, gm_ln_g=gm_ln_g, gm_ln_b=gm_ln_b, gm_ws=gm_ws, gm_bs=gm_bs, conv_w=conv_w, conv_b=conv_b, dt_bias=dt_bias, a_log=a_log, d_skip=d_skip, ssd_norm=ssd_norm, w_out_even=w_out_even, w_qkv=w_qkv, b_qkv=b_qkv, sinks=sinks, w_o_odd=w_o_odd, norm_xq=norm_xq, norm_mem=norm_mem, w_xq=w_xq, w_xkv=w_xkv, w_xo=w_xo, norm_ffn2=norm_ffn2, w_ffn2_gu=w_ffn2_gu, w_ffn2_down=w_ffn2_down, final_norm=final_norm, loss_target=loss_target, m_norm_ffn1=m_norm_ffn1, m_w_ffn1_gu=m_w_ffn1_gu, m_w_ffn1_down=m_w_ffn1_down, m_norm_mix=m_norm_mix, m_w_in_even=m_w_in_even, m_gm_ln_g=m_gm_ln_g, m_gm_ln_b=m_gm_ln_b, m_gm_ws=m_gm_ws, m_gm_bs=m_gm_bs, m_conv_w=m_conv_w, m_conv_b=m_conv_b, m_dt_bias=m_dt_bias, m_a_log=m_a_log, m_d_skip=m_d_skip, m_ssd_norm=m_ssd_norm, m_w_out_even=m_w_out_even, m_w_qkv=m_w_qkv, m_b_qkv=m_b_qkv, m_sinks=m_sinks, m_w_o_odd=m_w_o_odd, m_norm_xq=m_norm_xq, m_norm_mem=m_norm_mem, m_w_xq=m_w_xq, m_w_xkv=m_w_xkv, m_w_xo=m_w_xo, m_norm_ffn2=m_norm_ffn2, m_w_ffn2_gu=m_w_ffn2_gu, m_w_ffn2_down=m_w_ffn2_down, m_final_norm=m_final_norm, v_norm_ffn1=v_norm_ffn1, v_w_ffn1_gu=v_w_ffn1_gu, v_w_ffn1_down=v_w_ffn1_down, v_norm_mix=v_norm_mix, v_w_in_even=v_w_in_even, v_gm_ln_g=v_gm_ln_g, v_gm_ln_b=v_gm_ln_b, v_gm_ws=v_gm_ws, v_gm_bs=v_gm_bs, v_conv_w=v_conv_w, v_conv_b=v_conv_b, v_dt_bias=v_dt_bias, v_a_log=v_a_log, v_d_skip=v_d_skip, v_ssd_norm=v_ssd_norm, v_w_out_even=v_w_out_even, v_w_qkv=v_w_qkv, v_b_qkv=v_b_qkv, v_sinks=v_sinks, v_w_o_odd=v_w_o_odd, v_norm_xq=v_norm_xq, v_norm_mem=v_norm_mem, v_w_xq=v_w_xq, v_w_xkv=v_w_xkv, v_w_xo=v_w_xo, v_norm_ffn2=v_norm_ffn2, v_w_ffn2_gu=v_w_ffn2_gu, v_w_ffn2_down=v_w_ffn2_down, v_final_norm=v_final_norm)
    weights = {n: given[n] for n in TWIN_WEIGHTS}
    shared = {n: given[n] for n in SHARED_INPUTS}
    per_example = {n: given[n] for n in ['x', 'mem', 'positions']}
    grad_fn = _jax.value_and_grad(_loss, argnums=(0, 1))

    def one_microbatch(ex, loss_target):
        ex = dict(ex)
        diff = ex.pop(TWIN_DIFF_INPUT)
        return grad_fn(weights, diff, {**shared, **ex}, loss_target)

    if N_MICROBATCH == 1:
        loss, (grad_w, grad_x) = one_microbatch(per_example, given["loss_target"])
    else:
        def body(carry, xs):
            loss_sum, grad_sum = carry
            l_k, (gw_k, gx_k) = one_microbatch(xs[0], xs[1])
            with _jax.named_scope("update"):
                return (loss_sum + l_k, _jax.tree.map(_jnp.add, grad_sum, gw_k)), gx_k

        init = (_jnp.zeros((), _jnp.float32), _jax.tree.map(_jnp.zeros_like, weights))
        (loss, grad_w), grad_x = _jax.lax.scan(body, init, (per_example, given["loss_target"]))
    with _jax.named_scope("update"):
        delta_w, new_m, new_v = {}, {}, {}
        for n in TWIN_WEIGHTS:
            delta_w[n], new_m[n], new_v[n] = _adamw(weights[n], grad_w[n], given["m_" + n], given["v_" + n])
    return (loss, grad_x, *[grad_w[n] for n in TWIN_WEIGHTS], *[delta_w[n] for n in TWIN_WEIGHTS],
            *[new_m[n] for n in TWIN_WEIGHTS], *[new_v[n] for n in TWIN_WEIGHTS])
```

```python
import functools

import jax
import jax.numpy as jnp
from jax import lax
from jax.experimental import pallas as pl
from jax.experimental.pallas import tpu as pltpu

F32, BF16 = jnp.float32, jnp.bfloat16
HIGHEST = lax.Precision.HIGHEST

N_DEV = 8
D_MODEL = 2048
D_FF = 5632
EPS = 1e-5
CHUNK = 128
GM_GROUPS, GM_GDIM = 4, 512
SSD_HEADS, SSD_HEAD_DIM, SSD_GROUPS, SSD_STATE = 32, 64, 4, 128
ATT_HEADS, ATT_KV_HEADS, ATT_HEAD_DIM, ATT_REP = 32, 4, 64, 8
HEAD_PAD = 128
ROT_HALF = 8
ROPE_THETA = 500000.0
ATT_SCALE = ATT_HEAD_DIM ** -0.5
X_HEADS, X_HEAD_DIM = 4, 128
X_SCALE = X_HEAD_DIM ** -0.5
ADAM_LR, ADAM_B1, ADAM_B2, ADAM_EPS, ADAM_WD, ADAM_STEP = 0.001, 0.9, 0.999, 1e-08, 0.01, 10

VMEM_LIMIT_BYTES = 56 * 1024 * 1024
MESH_ID = pl.DeviceIdType.MESH


def _params(*sem):
    return pltpu.CompilerParams(dimension_semantics=sem, vmem_limit_bytes=VMEM_LIMIT_BYTES)


def _pick(n, cands):
    for c in cands:
        if n % c == 0:
            return c
    return n


def _dg(a, b, ca, cb, precision=None):
    return lax.dot_general(a, b, (((ca,), (cb,)), ((), ())), precision=precision, preferred_element_type=F32)


@jax.custom_vjp
def _bdot(a, b):
    return _dg(a.astype(BF16), b.astype(BF16), 1, 0)


def _bdot_fwd(a, b):
    return _bdot(a, b), (a, b)


def _bdot_bwd(r, g):
    a, b = r
    g = g.astype(BF16)
    return _dg(g, b.astype(BF16), 1, 1), _dg(a.astype(BF16), g, 0, 0)


_bdot.defvjp(_bdot_fwd, _bdot_bwd)


@jax.custom_vjp
def _bdot_nt(a, b):
    return _dg(a.astype(BF16), b.astype(BF16), 1, 1)


def _bdot_nt_fwd(a, b):
    return _bdot_nt(a, b), (a, b)


def _bdot_nt_bwd(r, g):
    a, b = r
    g = g.astype(BF16)
    return _dg(g, b.astype(BF16), 1, 0), _dg(g, a.astype(BF16), 0, 0)


_bdot_nt.defvjp(_bdot_nt_fwd, _bdot_nt_bwd)


@jax.custom_vjp
def _bdot_tn(a, b):
    return _dg(a.astype(BF16), b.astype(BF16), 0, 0)


def _bdot_tn_fwd(a, b):
    return _bdot_tn(a, b), (a, b)


def _bdot_tn_bwd(r, g):
    a, b = r
    g = g.astype(BF16)
    return _dg(b.astype(BF16), g, 1, 1), _dg(a.astype(BF16), g, 1, 0)


_bdot_tn.defvjp(_bdot_tn_fwd, _bdot_tn_bwd)


@jax.custom_vjp
def _hdot(a, b):
    return _dg(a, b, 1, 0, HIGHEST)


def _hdot_fwd(a, b):
    return _hdot(a, b), (a, b)


def _hdot_bwd(r, g):
    a, b = r
    return _dg(g, b, 1, 1, HIGHEST), _dg(a, g, 0, 0, HIGHEST)


_hdot.defvjp(_hdot_fwd, _hdot_bwd)


def _sigmoid(x):
    return 1.0 / (1.0 + jnp.exp(-x))


def _silu(x):
    return x * _sigmoid(x)


def _gelu(x):
    return 0.5 * x * (1.0 + lax.erf(x * 0.7071067811865476))


def _softplus(x):
    return jnp.maximum(x, 0.0) + jnp.log1p(jnp.exp(-jnp.abs(x)))


def _rms(x, g):
    return x * lax.rsqrt(jnp.mean(x * x, -1, keepdims=True) + EPS) * g


def _iota(shape, dim):
    return lax.broadcasted_iota(jnp.int32, shape, dim)


def _matmul(a, b, *, ta=False, tb=False, out_dtype=F32, alpha=1.0, bias=None, res=None, name):
    if ta:
        kk, m = a.shape
    else:
        m, kk = a.shape
    if tb:
        n, k2 = b.shape
    else:
        k2, n = b.shape
    assert kk == k2, (a.shape, b.shape, ta, tb)
    tm = _pick(m, (1024, 512, 256, 128))
    tn = _pick(n, (1024, 512, 256, 128))
    tk = _pick(kk, (2048, 1024, 512, 256, 128))
    if tk == 2048 and F32 in (a.dtype, b.dtype):
        tk = 1024
    nk = kk // tk
    has_bias, has_res = bias is not None, res is not None

    def body(*refs):
        a_ref, b_ref = refs[0], refs[1]
        pos = 2
        bias_ref = res_ref = None
        if has_bias:
            bias_ref = refs[pos]
            pos += 1
        if has_res:
            res_ref = refs[pos]
            pos += 1
        o_ref, acc_ref = refs[pos], refs[pos + 1]
        k = pl.program_id(2)

        @pl.when(k == 0)
        def _():
            acc_ref[...] = jnp.zeros_like(acc_ref)

        acc_ref[...] += _dg(a_ref[...].astype(BF16), b_ref[...].astype(BF16), 0 if ta else 1, 1 if tb else 0)

        @pl.when(k == nk - 1)
        def _():
            r = acc_ref[...]
            if alpha != 1.0:
                r = r * alpha
            if has_bias:
                r = r + bias_ref[...]
            if has_res:
                r = r + res_ref[...]
            o_ref[...] = r.astype(out_dtype)

    in_specs = [
        pl.BlockSpec((tk, tm), lambda i, j, k: (k, i)) if ta else pl.BlockSpec((tm, tk), lambda i, j, k: (i, k)),
        pl.BlockSpec((tn, tk), lambda i, j, k: (j, k)) if tb else pl.BlockSpec((tk, tn), lambda i, j, k: (k, j)),
    ]
    args = [a, b]
    if has_bias:
        in_specs.append(pl.BlockSpec((1, tn), lambda i, j, k: (0, j)))
        args.append(bias)
    if has_res:
        in_specs.append(pl.BlockSpec((tm, tn), lambda i, j, k: (i, j)))
        args.append(res)
    return pl.pallas_call(
        body, name=name, grid=(m // tm, n // tn, nk), in_specs=in_specs,
        out_specs=pl.BlockSpec((tm, tn), lambda i, j, k: (i, j)),
        out_shape=jax.ShapeDtypeStruct((m, n), out_dtype),
        scratch_shapes=[pltpu.VMEM((tm, tn), F32)],
        compiler_params=_params("parallel", "parallel", "arbitrary"),
    )(*args)


def _row(arr, width=None, cidx=0, shift=0):
    return (arr, arr.shape[-1] if width is None else width, cidx, shift)


def _rows(fn, rows, consts, outs, accs=(), *, tm, ncol=1, name):
    n = rows[0][0].shape[-2]
    assert n % tm == 0, (n, tm, name)
    nb = n // tm
    n_in = len(rows) + len(consts)
    n_out = len(outs)

    def cfun(cidx):
        return cidx if callable(cidx) else (lambda j, c=cidx: c)

    in_specs = []
    for arr, width, cidx, shift in rows:
        cf = cfun(cidx)
        if shift:
            rf = lambda i, s=shift: jnp.clip(i + s, 0, nb - 1)
        else:
            rf = lambda i: i
        if arr.ndim == 3:
            in_specs.append(pl.BlockSpec((arr.shape[0], tm, width), lambda i, j, rf=rf, cf=cf: (0, rf(i), cf(j))))
        else:
            in_specs.append(pl.BlockSpec((tm, width), lambda i, j, rf=rf, cf=cf: (rf(i), cf(j))))
    for c in consts:
        in_specs.append(pl.BlockSpec(c.shape, lambda i, j, nd=c.ndim: (0,) * nd))
    out_shape, out_specs = [], []
    for o in outs:
        width, dt = o[0], o[1]
        total = o[2] if len(o) > 2 else width * ncol
        out_shape.append(jax.ShapeDtypeStruct((n, total), dt))
        out_specs.append(pl.BlockSpec((tm, width), lambda i, j: (i, j)))
    for shp in accs:
        out_shape.append(jax.ShapeDtypeStruct(shp, F32))
        out_specs.append(pl.BlockSpec(shp, lambda i, j, nd=len(shp): (0,) * nd))

    def body(*refs):
        i, j = pl.program_id(0), pl.program_id(1)
        ins = [r[...] for r in refs[:n_in]]
        ro, ao = fn(i, j, *ins)
        for r, val in zip(refs[n_in:n_in + n_out], ro):
            r[...] = val.astype(r.dtype)
        if accs:
            acc_refs = refs[n_in + n_out:]

            @pl.when((i == 0) & (j == 0))
            def _():
                for r in acc_refs:
                    r[...] = jnp.zeros_like(r)

            for r, val in zip(acc_refs, ao):
                r[...] += val

    res = pl.pallas_call(
        body, name=name, grid=(nb, ncol), in_specs=in_specs, out_specs=out_specs, out_shape=out_shape,
        compiler_params=_params("arbitrary", "arbitrary"),
    )(*[r[0] for r in rows], *consts)
    return res


def _rms_fwd(x, g, name):
    def fn(i, j, x, g):
        return (_rms(x, g),), ()
    return _rows(fn, [_row(x)], [g], [(x.shape[1], BF16)], tm=_pick(x.shape[0], (512, 256)), name=name)[0]


def _rms_bwd(x, dh, dres, g, name):
    def fn(i, j, x, dh, dres, g):
        _, vjp = jax.vjp(_rms, x, g)
        dx, dg = vjp(dh)
        return (dx + dres,), (dg,)
    dx, dg = _rows(fn, [_row(x), _row(dh), _row(dres)], [g], [(x.shape[1], F32)], [g.shape],
                   tm=_pick(x.shape[0], (256,)), name=name)
    return dx, dg


def _rms_bwd_gain(x, dh, g, name):
    def fn(i, j, x, dh, g):
        _, vjp = jax.vjp(_rms, x, g)
        return (), (vjp(dh)[1],)
    return _rows(fn, [_row(x), _row(dh)], [g], [], [g.shape], tm=_pick(x.shape[0], (256,)), name=name)[0]


def _swiglu_fwd(g, u, name):
    def fn(i, j, g, u):
        return (_silu(g.astype(F32)) * u.astype(F32),), ()
    n, w = g.shape
    return _rows(fn, [_row(g, 512, lambda j: j), _row(u, 512, lambda j: j)], [], [(512, BF16)],
                 tm=_pick(n, (1024, 512, 256)), ncol=w // 512, name=name)[0]


def _swiglu_bwd(g, u, dact, name):
    def fn(i, j, g, u, da):
        g, u, da = g.astype(F32), u.astype(F32), da.astype(F32)
        s = _sigmoid(g)
        return (da * u * (s * (1.0 + g * (1.0 - s))), da * (g * s)), ()
    n, w = g.shape
    return _rows(fn, [_row(g, 512, lambda j: j), _row(u, 512, lambda j: j), _row(dact, 512, lambda j: j)], [],
                 [(512, BF16), (512, BF16)], tm=_pick(n, (1024, 512, 256)), ncol=w // 512, name=name)


def _gmlp_fn(uv, lng, lnb, ws0, ws1, ws2, ws3, bst):
    ws = (ws0, ws1, ws2, ws3)
    q = uv.shape[0]
    u = _gelu(uv[:, :D_MODEL])
    v = _gelu(uv[:, D_MODEL:])
    tril = _iota((q, q), 0) >= _iota((q, q), 1)
    outs = []
    for g in range(GM_GROUPS):
        sl = slice(GM_GDIM * g, GM_GDIM * (g + 1))
        vg = v[:, sl]
        mu = jnp.mean(vg, -1, keepdims=True)
        var = jnp.mean(jnp.square(vg - mu), -1, keepdims=True)
        vn = (vg - mu) * lax.rsqrt(var + EPS) * lng[:, sl] + lnb[:, sl]
        w = jnp.where(tril, ws[g], 0.0)
        bcol = jnp.sum(bst * (_iota((1, 128), 1) == g).astype(F32), axis=1, keepdims=True)
        outs.append(u[:, sl] * (_bdot(w, vn) + bcol))
    return jnp.concatenate(outs, axis=1)


def _gmlp_fwd(uv, lng, lnb, ws, bst, name):
    def fn(i, j, uv, lng, lnb, ws, bst):
        return (_gmlp_fn(uv, lng, lnb, ws[0], ws[1], ws[2], ws[3], bst),), ()
    return _rows(fn, [_row(uv)], [lng, lnb, ws, bst], [(D_MODEL, BF16)], tm=CHUNK, name=name)[0]


def _gmlp_bwd(uv, da, lng, lnb, ws, bst, name):
    def fn(i, j, uv, da, lng, lnb, ws, bst):
        _, vjp = jax.vjp(_gmlp_fn, uv, lng, lnb, ws[0], ws[1], ws[2], ws[3], bst)
        duv, dlng, dlnb, d0, d1, d2, d3, dbst = vjp(da)
        return (duv,), (dlng, dlnb, jnp.stack([d0, d1, d2, d3]), dbst)
    return _rows(fn, [_row(uv), _row(da)], [lng, lnb, ws, bst], [(2 * D_MODEL, BF16)],
                 [lng.shape, lnb.shape, ws.shape, bst.shape], tm=CHUNK, name=name)


def _conv_taps(scr, x, halo, first_row):
    q = x.shape[0]
    scr[pl.ds(0, 8), :] = halo
    scr[pl.ds(8, q), :] = x
    return [scr[pl.ds(first_row + k, q), :] for k in range(4)]


def _conv_fwd(xbc, w, b, name):
    n, c = xbc.shape
    q = CHUNK

    def body(x_ref, xp_ref, w_ref, b_ref, o_ref, scr):
        i = pl.program_id(0)
        halo = jnp.where(i > 0, xp_ref[pl.ds(q - 8, 8), :], 0.0)
        taps = _conv_taps(scr, x_ref[...], halo, 5)
        pre = b_ref[...] + sum(taps[k] * w_ref[pl.ds(k, 1), :] for k in range(4))
        o_ref[...] = _silu(pre)

    return pl.pallas_call(
        body, name=name, grid=(n // q,),
        in_specs=[pl.BlockSpec((q, c), lambda i: (i, 0)), pl.BlockSpec((q, c), lambda i: (jnp.maximum(i - 1, 0), 0)),
                  pl.BlockSpec(w.shape, lambda i: (0, 0)), pl.BlockSpec(b.shape, lambda i: (0, 0))],
        out_specs=pl.BlockSpec((q, c), lambda i: (i, 0)), out_shape=jax.ShapeDtypeStruct((n, c), F32),
        scratch_shapes=[pltpu.VMEM((q + 8, c), F32)], compiler_params=_params("arbitrary"),
    )(xbc, xbc, w, b)


def _conv_bwd_pre(xbc, dy, w, b, name):
    n, c = xbc.shape
    q = CHUNK

    def body(x_ref, xp_ref, dy_ref, w_ref, b_ref, dp_ref, dw_ref, db_ref, scr):
        i = pl.program_id(0)
        halo = jnp.where(i > 0, xp_ref[pl.ds(q - 8, 8), :], 0.0)
        taps = _conv_taps(scr, x_ref[...], halo, 5)
        pre = b_ref[...] + sum(taps[k] * w_ref[pl.ds(k, 1), :] for k in range(4))
        s = _sigmoid(pre)
        dp = dy_ref[...] * (s * (1.0 + pre * (1.0 - s)))
        dp_ref[...] = dp

        @pl.when(i == 0)
        def _():
            dw_ref[...] = jnp.zeros_like(dw_ref)
            db_ref[...] = jnp.zeros_like(db_ref)

        db_ref[...] += jnp.sum(dp, axis=0, keepdims=True)
        for k in range(4):
            dw_ref[pl.ds(k, 1), :] += jnp.sum(dp * taps[k], axis=0, keepdims=True)

    return pl.pallas_call(
        body, name=name, grid=(n // q,),
        in_specs=[pl.BlockSpec((q, c), lambda i: (i, 0)), pl.BlockSpec((q, c), lambda i: (jnp.maximum(i - 1, 0), 0)),
                  pl.BlockSpec((q, c), lambda i: (i, 0)),
                  pl.BlockSpec(w.shape, lambda i: (0, 0)), pl.BlockSpec(b.shape, lambda i: (0, 0))],
        out_specs=[pl.BlockSpec((q, c), lambda i: (i, 0)), pl.BlockSpec(w.shape, lambda i: (0, 0)),
                   pl.BlockSpec(b.shape, lambda i: (0, 0))],
        out_shape=[jax.ShapeDtypeStruct((n, c), F32), jax.ShapeDtypeStruct(w.shape, F32), jax.ShapeDtypeStruct(b.shape, F32)],
        scratch_shapes=[pltpu.VMEM((q + 8, c), F32)], compiler_params=_params("arbitrary"),
    )(xbc, xbc, dy, w, b)


def _conv_bwd_x(dpre, w, name):
    n, c = dpre.shape
    q = CHUNK
    nb = n // q

    def body(d_ref, dn_ref, w_ref, o_ref, scr):
        i = pl.program_id(0)
        scr[pl.ds(0, q), :] = d_ref[...]
        scr[pl.ds(q, 8), :] = jnp.where(i < nb - 1, dn_ref[pl.ds(0, 8), :], 0.0)
        o_ref[...] = sum(scr[pl.ds(3 - k, q), :] * w_ref[pl.ds(k, 1), :] for k in range(4)).astype(o_ref.dtype)

    return pl.pallas_call(
        body, name=name, grid=(nb,),
        in_specs=[pl.BlockSpec((q, c), lambda i: (i, 0)), pl.BlockSpec((q, c), lambda i: (jnp.minimum(i + 1, nb - 1), 0)),
                  pl.BlockSpec(w.shape, lambda i: (0, 0))],
        out_specs=pl.BlockSpec((q, c), lambda i: (i, 0)), out_shape=jax.ShapeDtypeStruct((n, c), BF16),
        scratch_shapes=[pltpu.VMEM((q + 8, c), F32)], compiler_params=_params("arbitrary"),
    )(dpre, dpre, w)


def _ssd_chunk(xs, bc, dtr, dtb, alog, dsk, expand, hp):
    q = xs.shape[0]
    tril = _iota((q, q), 0) >= _iota((q, q), 1)
    dt = _softplus(dtr + dtb)
    a = dt * (-jnp.exp(alog))
    cs = _hdot(tril.astype(F32), a)
    cs_t = cs.T
    last = (_iota((q, 1), 0) == q - 1).astype(F32)
    gw = SSD_HEADS // SSD_GROUPS * SSD_HEAD_DIM
    lane = _iota((1, gw), 1)
    ys, hs = [], []
    for g in range(SSD_GROUPS):
        sl = slice(gw * g, gw * (g + 1))
        eg = expand[:, sl]
        dt_e, cs_e = _hdot(dt, eg), _hdot(cs, eg)
        cl_e = jnp.sum(cs_e * last, axis=0, keepdims=True)
        d_e = jnp.sum(_hdot(jnp.broadcast_to(dsk, (8, 128)), eg), axis=0, keepdims=True) * 0.125
        xg = xs[:, sl]
        xdt = xg * dt_e
        bg = bc[:, SSD_STATE * g:SSD_STATE * (g + 1)]
        cg = bc[:, SSD_GROUPS * SSD_STATE + SSD_STATE * g:SSD_GROUPS * SSD_STATE + SSD_STATE * (g + 1)]
        cb = _bdot_nt(cg, bg)
        ms, xm = [], []
        for r in range(SSD_HEADS // SSD_GROUPS):
            h = g * (SSD_HEADS // SSD_GROUPS) + r
            col = jnp.sum(cs * (_iota((1, 128), 1) == h).astype(F32), axis=1, keepdims=True)
            row = jnp.sum(cs_t * (_iota((128, 1), 0) == h).astype(F32), axis=0, keepdims=True)
            decay = jnp.where(tril, jnp.exp(jnp.where(tril, col - row, 0.0)), 0.0)
            ms.append(cb * decay)
            xm.append(xdt * ((lane >= SSD_HEAD_DIM * r) & (lane < SSD_HEAD_DIM * (r + 1))).astype(F32))
        y_diag = _bdot(jnp.concatenate(ms, axis=1), jnp.concatenate(xm, axis=0))
        hg = hp[:, sl]
        y_off = _bdot(cg, hg) * jnp.exp(cs_e)
        states = _bdot_tn(bg, xdt * jnp.exp(cl_e - cs_e))
        hs.append(hg * jnp.exp(cl_e) + states)
        ys.append(y_diag + y_off + xg * d_e)
    return jnp.concatenate(ys, axis=1), jnp.concatenate(hs, axis=1)


def _ssd_fwd(xbc, dtr, dtb, alog, dsk, expand, name):
    n = xbc.shape[0]
    q, w = CHUNK, D_MODEL
    nc = n // q

    def body(xs_ref, bc_ref, dtr_ref, dtb_ref, alog_ref, dsk_ref, e_ref, y_ref, hp_ref, h_scr):
        @pl.when(pl.program_id(0) == 0)
        def _():
            h_scr[...] = jnp.zeros_like(h_scr)

        hp = h_scr[...]
        y, hn = _ssd_chunk(xs_ref[...], bc_ref[...], dtr_ref[...], dtb_ref[...], alog_ref[...], dsk_ref[...], e_ref[...], hp)
        y_ref[...] = y
        hp_ref[...] = hp
        h_scr[...] = hn

    small = pl.BlockSpec((1, 128), lambda c: (0, 0))
    return pl.pallas_call(
        body, name=name, grid=(nc,),
        in_specs=[pl.BlockSpec((q, w), lambda c: (c, 0)), pl.BlockSpec((q, 1024), lambda c: (c, 2)),
                  pl.BlockSpec((q, 128), lambda c: (c, 0)), small, small, small, pl.BlockSpec((128, w), lambda c: (0, 0))],
        out_specs=[pl.BlockSpec((q, w), lambda c: (c, 0)), pl.BlockSpec((SSD_STATE, w), lambda c: (c, 0))],
        out_shape=[jax.ShapeDtypeStruct((n, w), F32), jax.ShapeDtypeStruct((nc * SSD_STATE, w), F32)],
        scratch_shapes=[pltpu.VMEM((SSD_STATE, w), F32)], compiler_params=_params("arbitrary"),
    )(xbc, xbc, dtr, dtb, alog, dsk, expand)


def _ssd_bwd(xbc, dtr, dtb, alog, dsk, expand, hp_all, dy, name):
    n = xbc.shape[0]
    q, w = CHUNK, D_MODEL
    nc = n // q

    def body(xs_ref, bc_ref, dtr_ref, dtb_ref, alog_ref, dsk_ref, e_ref, hp_ref, dy_ref,
             dxbc_ref, ddtr_ref, ddtb_ref, dalog_ref, ddsk_ref, dh_scr):
        @pl.when(pl.program_id(0) == 0)
        def _():
            dh_scr[...] = jnp.zeros_like(dh_scr)
            ddtb_ref[...] = jnp.zeros_like(ddtb_ref)
            dalog_ref[...] = jnp.zeros_like(dalog_ref)
            ddsk_ref[...] = jnp.zeros_like(ddsk_ref)

        e = e_ref[...]
        _, vjp = jax.vjp(lambda xs, bc, dtr, dtb, alog, dsk, hp: _ssd_chunk(xs, bc, dtr, dtb, alog, dsk, e, hp),
                         xs_ref[...], bc_ref[...], dtr_ref[...], dtb_ref[...], alog_ref[...], dsk_ref[...], hp_ref[...])
        dxs, dbc, ddtr, ddtb, dalog, ddsk, dhp = vjp((dy_ref[...], dh_scr[...]))
        dxbc_ref[...] = jnp.concatenate([dxs, dbc], axis=1)
        ddtr_ref[...] = ddtr
        ddtb_ref[...] += ddtb
        dalog_ref[...] += dalog
        ddsk_ref[...] += ddsk
        dh_scr[...] = dhp

    rev = lambda c: nc - 1 - c
    small = pl.BlockSpec((1, 128), lambda c: (0, 0))
    return pl.pallas_call(
        body, name=name, grid=(nc,),
        in_specs=[pl.BlockSpec((q, w), lambda c: (rev(c), 0)), pl.BlockSpec((q, 1024), lambda c: (rev(c), 2)),
                  pl.BlockSpec((q, 128), lambda c: (rev(c), 0)), small, small, small,
                  pl.BlockSpec((128, w), lambda c: (0, 0)), pl.BlockSpec((SSD_STATE, w), lambda c: (rev(c), 0)),
                  pl.BlockSpec((q, w), lambda c: (rev(c), 0))],
        out_specs=[pl.BlockSpec((q, w + 1024), lambda c: (rev(c), 0)),
                   pl.BlockSpec((q, 128), lambda c: (rev(c), 0)), small, small, small],
        out_shape=[jax.ShapeDtypeStruct((n, w + 1024), F32), jax.ShapeDtypeStruct((n, 128), F32),
                   jax.ShapeDtypeStruct((1, 128), F32), jax.ShapeDtypeStruct((1, 128), F32), jax.ShapeDtypeStruct((1, 128), F32)],
        scratch_shapes=[pltpu.VMEM((SSD_STATE, w), F32)], compiler_params=_params("arbitrary"),
    )(xbc, xbc, dtr, dtb, alog, dsk, expand, hp_all, dy)


def _gate_fn(y, z, g):
    outs = []
    for k in range(SSD_GROUPS):
        sl = slice(512 * k, 512 * (k + 1))
        yg = y[:, sl] * _silu(z[:, sl])
        outs.append(yg * lax.rsqrt(jnp.mean(yg * yg, -1, keepdims=True) + EPS) * g[:, sl])
    return jnp.concatenate(outs, axis=1)


def _gate_fwd(y, z, g, name):
    def fn(i, j, y, z, g):
        return (_gate_fn(y, z, g),), ()
    return _rows(fn, [_row(y), _row(z)], [g], [(D_MODEL, BF16)], tm=_pick(y.shape[0], (256, 128)), name=name)[0]


def _gate_bwd(y, z, db, g, name):
    def fn(i, j, y, z, db, g):
        _, vjp = jax.vjp(_gate_fn, y, z, g)
        dy, dz, dg = vjp(db)
        return (dy, dz), (dg,)
    return _rows(fn, [_row(y), _row(z), _row(db)], [g], [(D_MODEL, F32), (D_MODEL, BF16)], [g.shape],
                 tm=_pick(y.shape[0], (256, 128)), name=name)


def _rope(x, tab, sign=1.0):
    return x * tab[:, 0:128] + sign * (pltpu.roll(x, 128 - ROT_HALF, 1) * tab[:, 128:256] + pltpu.roll(x, ROT_HALF, 1) * tab[:, 256:384])


def _swa_core(qq, kc, vc, sinks, kv_head, first):
    q = kc.shape[0] // 2
    s = _bdot_nt(qq, kc) * ATT_SCALE
    rows = ATT_REP * q
    iq = _iota((rows, 2 * q), 0) & (q - 1)
    js = _iota((rows, 2 * q), 1)
    rel = iq + q - js
    mask = (rel >= 0) & (rel < q) & ((js >= q) | jnp.logical_not(first))
    s = jnp.where(mask, s, -jnp.inf)
    rep = lax.shift_right_logical(_iota((rows, 1), 0), q.bit_length() - 1)
    sink = jnp.zeros((rows, 1), F32)
    for r in range(ATT_REP):
        s_r = jnp.sum(sinks * (_iota((1, 128), 1) == kv_head * ATT_REP + r).astype(F32), axis=1, keepdims=True)
        sink = sink + jnp.where(rep == r, s_r, 0.0)
    m = jnp.maximum(jnp.max(s, -1, keepdims=True), sink)
    p = jnp.exp(s - m)
    pr = p / (jnp.sum(p, -1, keepdims=True) + jnp.exp(sink - m))
    return _bdot(pr, vc)


def _swa_prep(q, kv, kvp, tab, tabp, kv_head):
    w = HEAD_PAD
    kc = jnp.concatenate([_rope(kvp[:, w * kv_head:w * (kv_head + 1)], tabp), _rope(kv[:, w * kv_head:w * (kv_head + 1)], tab)], axis=0)
    o = ATT_KV_HEADS * w
    vc = jnp.concatenate([kvp[:, o + w * kv_head:o + w * (kv_head + 1)], kv[:, o + w * kv_head:o + w * (kv_head + 1)]], axis=0)
    qq = jnp.concatenate([_rope(q[:, w * (kv_head * ATT_REP + r):w * (kv_head * ATT_REP + r + 1)], tab) for r in range(ATT_REP)], axis=0)
    return qq, kc, vc


def _swa_fwd(q, kv, tab, sinks, name):
    def fn(i, j, q, kv, kvp, tab, tabp, sinks):
        first = i == 0
        outs = []
        for h in range(ATT_KV_HEADS):
            qq, kc, vc = _swa_prep(q, kv, kvp, tab, tabp, h)
            o = _swa_core(qq, kc, vc, sinks, h, first)
            outs += [o[CHUNK * r:CHUNK * (r + 1)] for r in range(ATT_REP)]
        return (jnp.concatenate(outs, axis=1),), ()
    return _rows(fn, [_row(q), _row(kv), _row(kv, shift=-1), _row(tab), _row(tab, shift=-1)], [sinks],
                 [(q.shape[1], BF16)], tm=CHUNK, name=name)[0]


def _swa_bwd(q, kv, tab, sinks, do, name):
    w = HEAD_PAD

    def fn(i, j, q, kv, kvp, tab, tabp, do, sinks):
        first = i == 0
        dqs, dkc, dkp, dvc, dvp = [], [], [], [], []
        dsink = jnp.zeros_like(sinks)
        for h in range(ATT_KV_HEADS):
            qq, kc, vc = _swa_prep(q, kv, kvp, tab, tabp, h)
            dout = jnp.concatenate([do[:, w * (h * ATT_REP + r):w * (h * ATT_REP + r + 1)] for r in range(ATT_REP)], axis=0)
            _, vjp = jax.vjp(lambda a, b, c, s: _swa_core(a, b, c, s, h, first), qq, kc, vc, sinks)
            dqq, dk, dv, ds = vjp(dout.astype(F32))
            dsink = dsink + ds
            dqs += [_rope(dqq[CHUNK * r:CHUNK * (r + 1)], tab, -1.0) for r in range(ATT_REP)]
            dkp.append(_rope(dk[:CHUNK], tabp, -1.0))
            dkc.append(_rope(dk[CHUNK:], tab, -1.0))
            dvp.append(dv[:CHUNK])
            dvc.append(dv[CHUNK:])
        return (jnp.concatenate(dqs, axis=1), jnp.concatenate(dkc + dvc, axis=1), jnp.concatenate(dkp + dvp, axis=1)), (dsink,)
    return _rows(fn, [_row(q), _row(kv), _row(kv, shift=-1), _row(tab), _row(tab, shift=-1), _row(do)], [sinks],
                 [(q.shape[1], BF16), (kv.shape[1], F32), (kv.shape[1], F32)], [sinks.shape], tm=CHUNK, name=name)


def _swa_combine(dkv_cur, dkv_prev, dq, name):
    nb = dq.shape[0] // CHUNK

    def fn(i, j, cur, nxt, dq):
        dkv = (cur + jnp.where(i < nb - 1, nxt, 0.0)).astype(BF16)
        return (dkv,), (jnp.sum(dq.astype(F32), axis=0, keepdims=True), jnp.sum(dkv.astype(F32), axis=0, keepdims=True))
    return _rows(fn, [_row(dkv_cur), _row(dkv_prev, shift=1), _row(dq)], [], [(dkv_cur.shape[1], BF16)],
                 [(1, dq.shape[1]), (1, dkv_cur.shape[1])], tm=CHUNK, name=name)


def _xattn_fn(q, k, v):
    outs = []
    for h in range(X_HEADS):
        sl = slice(X_HEAD_DIM * h, X_HEAD_DIM * (h + 1))
        s = _bdot_nt(q[:, sl], k[:, sl]) * X_SCALE
        p = jnp.exp(s - jnp.max(s, -1, keepdims=True))
        outs.append(_bdot(p / jnp.sum(p, -1, keepdims=True), v[:, sl]))
    return jnp.concatenate(outs, axis=1)


def _xattn_fwd(q, k, v, name):
    def fn(i, j, q, k, v):
        return (_xattn_fn(q, k, v),), ()
    return _rows(fn, [_row(q)], [k, v], [(q.shape[1], BF16)], tm=_pick(q.shape[0], (512, 256)), name=name)[0]


def _xattn_bwd(q, k, v, do, name):
    def fn(i, j, q, do, k, v):
        _, vjp = jax.vjp(_xattn_fn, q, k, v)
        dq, dk, dv = vjp(do)
        return (dq,), (dk, dv)
    return _rows(fn, [_row(q), _row(do)], [k, v], [(q.shape[1], BF16)], [k.shape, v.shape],
                 tm=_pick(q.shape[0], (512, 256)), name=name)


def _loss_fn(x, t, g):
    return 0.5 * jnp.sum(jnp.mean(jnp.square(_rms(x, g) - t), axis=-1))


def _loss_fwd_bwd(x, t, g, name):
    def fn(i, j, x, t, g):
        loss, vjp = jax.vjp(_loss_fn, x, t, g)
        dx, _, dg = vjp(jnp.ones((), F32))
        return (dx,), (jnp.broadcast_to(loss, (8, 128)), dg)
    dx, loss, dg = _rows(fn, [_row(x), _row(t)], [g], [(x.shape[1], F32)], [(8, 128), g.shape],
                         tm=_pick(x.shape[0], (256,)), name=name)
    return loss, dx, dg


def _adamw(parts, w, m, v, name):
    def fn(i, j, parts, w, m, v):
        g = parts[0].astype(F32)
        for k in range(1, N_DEV):
            g = g + parts[k].astype(F32)
        m2 = ADAM_B1 * m + (1.0 - ADAM_B1) * g
        v2 = ADAM_B2 * v + (1.0 - ADAM_B2) * jnp.square(g)
        m_hat = m2 / (1.0 - ADAM_B1 ** ADAM_STEP)
        v_hat = v2 / (1.0 - ADAM_B2 ** ADAM_STEP)
        delta = -ADAM_LR * (m_hat / (jnp.sqrt(v_hat) + ADAM_EPS) + ADAM_WD * w)
        return (g, delta, m2, v2), ()
    r, c = w.shape
    return _rows(fn, [_row(parts), _row(w), _row(m), _row(v)], [], [(c, F32)] * 4,
                 tm=_pick(r, (128, 64, 32, 16, 8)), name=name)


def _peer(k):
    return (k // 4, (k // 2) % 2, k % 2)


def _exchange(x, gather, name):
    shape = x.shape[-2:]

    def body(x_ref, o_ref, send_sems, recv_sems, local_sem):
        me = 4 * lax.axis_index("x") + 2 * lax.axis_index("y") + lax.axis_index("c")

        def src(k):
            return x_ref if gather else x_ref.at[k]

        mine = pltpu.make_async_copy(src(me), o_ref.at[me], local_sem)
        mine.start()
        sends = []
        for d in range(1, N_DEV):
            to = (me + d) % N_DEV
            cp = pltpu.make_async_remote_copy(src_ref=src(to), dst_ref=o_ref.at[me], send_sem=send_sems.at[d - 1],
                                              recv_sem=recv_sems.at[d - 1], device_id=_peer(to), device_id_type=MESH_ID)
            cp.start()
            sends.append(cp)
        for d in range(1, N_DEV):
            frm = (me + N_DEV - d) % N_DEV
            pltpu.make_async_remote_copy(src_ref=src(frm), dst_ref=o_ref.at[frm], send_sem=send_sems.at[d - 1],
                                         recv_sem=recv_sems.at[d - 1], device_id=_peer(frm), device_id_type=MESH_ID).wait_recv()
        for cp in sends:
            cp.wait_send()
        mine.wait()

    return pl.pallas_call(
        body, name=name, in_specs=[pl.BlockSpec(memory_space=pl.ANY)], out_specs=pl.BlockSpec(memory_space=pl.ANY),
        out_shape=jax.ShapeDtypeStruct((N_DEV,) + shape, x.dtype),
        scratch_shapes=[pltpu.SemaphoreType.DMA((N_DEV - 1,)), pltpu.SemaphoreType.DMA((N_DEV - 1,)), pltpu.SemaphoreType.DMA],
    )(x)


def _all_gather(x, name):
    return _exchange(x.reshape(-1, x.shape[-1]), True, name).reshape((N_DEV,) + x.shape)


def _all_to_all(x, name):
    return _exchange(x.reshape(N_DEV, -1, x.shape[-1]), False, name).reshape(x.shape)


def _pack(arrs):
    flat = []
    for a in arrs:
        a = a.reshape(-1).astype(F32)
        flat.append(jnp.pad(a, (0, (-a.shape[0]) % 1024)))
    return jnp.concatenate(flat).reshape(-1, 128)


def _unpack(p, shapes):
    p = p.reshape(-1)
    out, off = [], 0
    for s in shapes:
        n = 1
        for d in s:
            n *= d
        out.append(p[off:off + n].reshape(s))
        off += n + (-n) % 1024
    return out


def _cols_to_full(g):
    return g.transpose(1, 0, 2).reshape(g.shape[1], -1)


def _full_to_cols(w, shards=N_DEV):
    return w.reshape(w.shape[0], shards, -1).transpose(1, 0, 2)


SMALL = ("norm_ffn1", "norm_mix", "gm_ln_g", "gm_ln_b", "gm_ws", "gm_bs", "conv_b", "dt_bias", "a_log", "d_skip", "ssd_norm",
         "sinks", "norm_xq", "norm_mem", "norm_ffn2", "final_norm")
SHARDED = ("w_ffn1_gu", "w_ffn1_down", "w_in_even", "w_out_even", "w_qkv", "w_o_odd", "w_xq", "w_xkv", "w_xo", "w_ffn2_gu",
           "w_ffn2_down")
SMALL_SHARDED = ("conv_w", "b_qkv")
ORDER = ("norm_ffn1", "w_ffn1_gu", "w_ffn1_down", "norm_mix", "w_in_even", "gm_ln_g", "gm_ln_b", "gm_ws", "gm_bs", "conv_w",
         "conv_b", "dt_bias", "a_log", "d_skip", "ssd_norm", "w_out_even", "w_qkv", "b_qkv", "sinks", "w_o_odd", "norm_xq",
         "norm_mem", "w_xq", "w_xkv", "w_xo", "norm_ffn2", "w_ffn2_gu", "w_ffn2_down", "final_norm")


def _ffn_fwd(x, gain, wg, wu, wd, tag):
    h = _rms_fwd(x, gain, f"{tag}_norm")
    g = _matmul(h, wg, out_dtype=BF16, name=f"{tag}_gate")
    u = _matmul(h, wu, out_dtype=BF16, name=f"{tag}_up")
    act = _swiglu_fwd(g, u, f"{tag}_act")
    y = _matmul(act, wd, alpha=0.5, res=x, name=f"{tag}_down")
    return y, (x, h, g, u, act)


def _ffn_bwd(dy, saved, gain, wg, wu, wd, tag):
    x, h, g, u, act = saved
    dact = _matmul(dy, wd, tb=True, alpha=0.5, out_dtype=BF16, name=f"{tag}_dact")
    dwd = _matmul(act, dy, ta=True, alpha=0.5, out_dtype=BF16, name=f"{tag}_dwd")
    dg, du = _swiglu_bwd(g, u, dact, f"{tag}_dgu")
    dh = _matmul(dg, wg, tb=True, name=f"{tag}_dh_g")
    dh = _matmul(du, wu, tb=True, res=dh, name=f"{tag}_dh_u")
    dwg = _matmul(h, dg, ta=True, out_dtype=BF16, name=f"{tag}_dwg")
    dwu = _matmul(h, du, ta=True, out_dtype=BF16, name=f"{tag}_dwu")
    dx, dgain = _rms_bwd(x, dh, dy, gain, f"{tag}_dnorm")
    return dx, dgain, dwg, dwu, dwd


def _xattn_layer_fwd(x, mem, gq, gm, wq, wkv, wo, tag):
    hq = _rms_fwd(x, gq, f"{tag}_normq")
    mn = _rms_fwd(mem, gm, f"{tag}_normm")
    q = _matmul(hq, wq, name=f"{tag}_q")
    kv = _matmul(mn, wkv, name=f"{tag}_kv")
    k, v = kv[:, :X_HEADS * X_HEAD_DIM], kv[:, X_HEADS * X_HEAD_DIM:]
    o = _xattn_fwd(q, k, v, f"{tag}_attn")
    y = _matmul(o, wo, res=x, name=f"{tag}_o")
    return y, (x, hq, mn, q, k, v, o)


def _xattn_layer_bwd(dy, saved, mem, gq, gm, wq, wkv, wo, tag):
    x, hq, mn, q, k, v, o = saved
    do = _matmul(dy, wo, tb=True, name=f"{tag}_do")
    dwo = _matmul(o, dy, ta=True, out_dtype=BF16, name=f"{tag}_dwo")
    dq, dk, dv = _xattn_bwd(q, k, v, do, f"{tag}_dattn")
    dkv = jnp.concatenate([dk, dv], axis=1)
    dwq = _matmul(hq, dq, ta=True, out_dtype=BF16, name=f"{tag}_dwq")
    dwkv = _matmul(mn, dkv, ta=True, out_dtype=BF16, name=f"{tag}_dwkv")
    dhq = _matmul(dq, wq, tb=True, name=f"{tag}_dhq")
    dmn = _matmul(dkv, wkv, tb=True, name=f"{tag}_dmn")
    dx, dgq = _rms_bwd(x, dhq, dy, gq, f"{tag}_dnormq")
    dgm = _rms_bwd_gain(mem, dmn, gm, f"{tag}_dnormm")
    return dx, dgq, dgm, dwq, dwkv, dwo


def _even_fwd(x, weights, params):
    w_uv, w_z, w_xbc, w_dt, w_out_a, w_out_b = weights
    gain, lng, lnb, ws, bst, conv_w, conv_b, dtb, alog, dsk, ssd_norm, expand = params
    hm = _rms_fwd(x, gain, "l0_normmix")
    uv = _matmul(hm, w_uv, name="l0_uv")
    zz = _matmul(hm, w_z, name="l0_z")
    xbc_raw = _matmul(hm, w_xbc, name="l0_xbc")
    dtr = _matmul(hm, w_dt, name="l0_dt")
    a_out = _gmlp_fwd(uv, lng, lnb, ws, bst, "l0_gmlp")
    xbc = _conv_fwd(xbc_raw, conv_w, conv_b, "l0_conv")
    y_ssd, hp_all = _ssd_fwd(xbc, dtr, dtb, alog, dsk, expand, "l0_ssd")
    b_out = _gate_fwd(y_ssd, zz, ssd_norm, "l0_gate")
    y = _matmul(a_out, w_out_a, res=x, name="l0_out_a")
    y = _matmul(b_out, w_out_b, res=y, name="l0_out_b")
    return y, (x, hm, uv, zz, xbc_raw, dtr, a_out, xbc, y_ssd, hp_all, b_out)


def _even_bwd(dx, saved, weights, params):
    w_uv, w_z, w_xbc, w_dt, w_out_a, w_out_b = weights
    gain, lng, lnb, ws, bst, conv_w, conv_b, dtb, alog, dsk, ssd_norm, expand = params
    x, hm, uv, zz, xbc_raw, dtr, a_out, xbc, y_ssd, hp_all, b_out = saved
    da_out = _matmul(dx, w_out_a, tb=True, name="l0_da")
    db_out = _matmul(dx, w_out_b, tb=True, name="l0_db")
    dw_out_a = _matmul(a_out, dx, ta=True, out_dtype=BF16, name="l0_dwout_a")
    dw_out_b = _matmul(b_out, dx, ta=True, out_dtype=BF16, name="l0_dwout_b")
    dy_ssd, dzz, d_ssd_norm = _gate_bwd(y_ssd, zz, db_out, ssd_norm, "l0_dgate")
    dxbc, ddtr, d_dtb, d_alog, d_dsk = _ssd_bwd(xbc, dtr, dtb, alog, dsk, expand, hp_all, dy_ssd, "l0_dssd")
    dpre, d_conv_w, d_conv_b = _conv_bwd_pre(xbc_raw, dxbc, conv_w, conv_b, "l0_dconv_pre")
    dxbc_raw = _conv_bwd_x(dpre, conv_w, "l0_dconv_x")
    duv, d_lng, d_lnb, d_ws, d_bst = _gmlp_bwd(uv, da_out, lng, lnb, ws, bst, "l0_dgmlp")
    ddtr16 = ddtr.astype(BF16)
    dhm = _matmul(duv, w_uv, tb=True, name="l0_dh_uv")
    dhm = _matmul(dzz, w_z, tb=True, res=dhm, name="l0_dh_z")
    dhm = _matmul(dxbc_raw, w_xbc, tb=True, res=dhm, name="l0_dh_xbc")
    dhm = _matmul(ddtr16, w_dt, tb=True, res=dhm, name="l0_dh_dt")
    dw_uv = _matmul(hm, duv, ta=True, out_dtype=BF16, name="l0_dwuv")
    dw_z = _matmul(hm, dzz, ta=True, out_dtype=BF16, name="l0_dwz")
    dw_xbc = _matmul(hm, dxbc_raw, ta=True, out_dtype=BF16, name="l0_dwxbc")
    dw_dt = _matmul(hm, ddtr16, ta=True, out_dtype=BF16, name="l0_dwdt")
    dx, d_gain = _rms_bwd(x, dhm, dx, gain, "l0_dnormmix")
    small = (d_gain, d_lng, d_lnb, d_ws, d_bst, d_conv_w, d_conv_b, d_dtb, d_alog, d_dsk, d_ssd_norm)
    return dx, small, (dw_uv, dw_z, dw_xbc, dw_dt, dw_out_a, dw_out_b)


def _odd_fwd(x, weights, params):
    w_q, w_kv, w_o = weights
    gain, b_q, b_kv, tab, snk = params
    hm = _rms_fwd(x, gain, "l1_normmix")
    q = _matmul(hm, w_q, bias=b_q, name="l1_q")
    kv = _matmul(hm, w_kv, bias=b_kv, name="l1_kv")
    o = _swa_fwd(q, kv, tab, snk, "l1_swa")
    y = _matmul(o, w_o, res=x, name="l1_o")
    return y, (x, hm, q, kv, o)


def _odd_bwd(dx, saved, weights, params):
    w_q, w_kv, w_o = weights
    gain, b_q, b_kv, tab, snk = params
    x, hm, q, kv, o = saved
    do = _matmul(dx, w_o, tb=True, out_dtype=BF16, name="l1_do")
    dw_o = _matmul(o, dx, ta=True, out_dtype=BF16, name="l1_dwo")
    dq, dkv_cur, dkv_prev, d_snk = _swa_bwd(q, kv, tab, snk, do, "l1_dswa")
    dkv, db_q, db_kv = _swa_combine(dkv_cur, dkv_prev, dq, "l1_dkv")
    dhm = _matmul(dq, w_q, tb=True, name="l1_dh_q")
    dhm = _matmul(dkv, w_kv, tb=True, res=dhm, name="l1_dh_kv")
    dw_q = _matmul(hm, dq, ta=True, out_dtype=BF16, name="l1_dwq")
    dw_kv = _matmul(hm, dkv, ta=True, out_dtype=BF16, name="l1_dwkv")
    dx, d_gain = _rms_bwd(x, dhm, dx, gain, "l1_dnormmix")
    return dx, (d_gain, db_q, db_kv, d_snk), (dw_q, dw_kv, dw_o)


def _rope_table(positions):
    seq = positions.size
    inv_freq = ROPE_THETA ** (-jnp.arange(0, 2 * ROT_HALF, 2, dtype=F32) / (2 * ROT_HALF))
    ang = positions.reshape(seq, 1).astype(F32) * inv_freq
    cos, sin, zero = jnp.cos(ang), jnp.sin(ang), jnp.zeros((seq, 128 - 2 * ROT_HALF), F32)
    z8 = jnp.zeros((seq, ROT_HALF), F32)
    return jnp.concatenate([cos, cos, zero + 1.0, -sin, z8, zero, z8, sin, zero], axis=1)


def _pad_heads_cols(w, heads):
    k = w.shape[0]
    return jnp.pad(w.reshape(k, heads, ATT_HEAD_DIM), ((0, 0), (0, 0), (0, HEAD_PAD - ATT_HEAD_DIM))).reshape(k, heads * HEAD_PAD)


def _unpad_heads_cols(w, heads):
    k = w.shape[0]
    return w.reshape(k, heads, HEAD_PAD)[:, :, :ATT_HEAD_DIM].reshape(k, heads * ATT_HEAD_DIM)


def kernel(x, mem, positions, norm_ffn1, w_ffn1_gu, w_ffn1_down, norm_mix, w_in_even, gm_ln_g, gm_ln_b, gm_ws, gm_bs, conv_w, conv_b, dt_bias, a_log, d_skip, ssd_norm, w_out_even, w_qkv, b_qkv, sinks, w_o_odd, norm_xq, norm_mem, w_xq, w_xkv, w_xo, norm_ffn2, w_ffn2_gu, w_ffn2_down, final_norm, loss_target, m_norm_ffn1, m_w_ffn1_gu, m_w_ffn1_down, m_norm_mix, m_w_in_even, m_gm_ln_g, m_gm_ln_b, m_gm_ws, m_gm_bs, m_conv_w, m_conv_b, m_dt_bias, m_a_log, m_d_skip, m_ssd_norm, m_w_out_even, m_w_qkv, m_b_qkv, m_sinks, m_w_o_odd, m_norm_xq, m_norm_mem, m_w_xq, m_w_xkv, m_w_xo, m_norm_ffn2, m_w_ffn2_gu, m_w_ffn2_down, m_final_norm, v_norm_ffn1, v_w_ffn1_gu, v_w_ffn1_down, v_norm_mix, v_w_in_even, v_gm_ln_g, v_gm_ln_b, v_gm_ws, v_gm_bs, v_conv_w, v_conv_b, v_dt_bias, v_a_log, v_d_skip, v_ssd_norm, v_w_out_even, v_w_qkv, v_b_qkv, v_sinks, v_w_o_odd, v_norm_xq, v_norm_mem, v_w_xq, v_w_xkv, v_w_xo, v_norm_ffn2, v_w_ffn2_gu, v_w_ffn2_down, v_final_norm):
    env = dict(locals())
    W = {n: env[n] for n in ORDER}
    M = {n: env["m_" + n] for n in ORDER}
    V = {n: env["v_" + n] for n in ORDER}
    seq = x.shape[1]
    x0 = x.reshape(seq, D_MODEL)
    mem2 = mem.reshape(-1, D_MODEL)
    target = loss_target.reshape(seq, D_MODEL)

    G = {n: _all_gather(W[n].astype(BF16), f"ag_{n}") for n in SHARDED}
    gs = _all_gather(_pack([conv_w, b_qkv]), "ag_small")
    gs = [_unpack(gs[k], [conv_w.shape, b_qkv.shape]) for k in range(N_DEV)]
    conv_w_full = jnp.concatenate([g[0][0] for g in gs], axis=1)
    b_qkv_full = jnp.concatenate([g[1][0] for g in gs], axis=0)

    def gu(name, l):
        g = G[name][:, l]
        return _cols_to_full(g[:4]), _cols_to_full(g[4:])

    wg1, wu1 = zip(*[gu("w_ffn1_gu", l) for l in range(2)])
    wg2, wu2 = zip(*[gu("w_ffn2_gu", l) for l in range(2)])
    wd1 = [G["w_ffn1_down"][:, l].reshape(D_FF, D_MODEL) for l in range(2)]
    wd2 = [G["w_ffn2_down"][:, l].reshape(D_FF, D_MODEL) for l in range(2)]
    w_in = _cols_to_full(G["w_in_even"][:, 0])
    w_uv, w_z, w_xbc = w_in[:, :4096], w_in[:, 4096:6144], w_in[:, 6144:9216]
    w_dt = jnp.pad(w_in[:, 9216:], ((0, 0), (0, 128 - SSD_HEADS)))
    w_out = G["w_out_even"][:, 0].reshape(2 * D_MODEL, D_MODEL)
    w_out_a, w_out_b = w_out[:D_MODEL], w_out[D_MODEL:]
    nq = ATT_HEADS * ATT_HEAD_DIM
    wqkv = _cols_to_full(G["w_qkv"][:, 0])
    w_q = _pad_heads_cols(wqkv[:, :nq], ATT_HEADS)
    w_kv = _pad_heads_cols(wqkv[:, nq:], 2 * ATT_KV_HEADS)
    b_q = _pad_heads_cols(b_qkv_full[None, :nq], ATT_HEADS)
    b_kv = _pad_heads_cols(b_qkv_full[None, nq:], 2 * ATT_KV_HEADS)
    w_o = _pad_heads_cols(G["w_o_odd"][:, 0].reshape(D_MODEL, D_MODEL).T, ATT_HEADS).T
    wxq = [G["w_xq"][:, l].reshape(D_MODEL, -1) for l in range(2)]
    wxkv = [G["w_xkv"][:, l].reshape(D_MODEL, -1) for l in range(2)]
    wxo = [_cols_to_full(G["w_xo"][:, l]) for l in range(2)]

    row = lambda a: a.reshape(1, -1)
    pad128 = lambda a: jnp.pad(a.reshape(1, -1), ((0, 0), (0, 128 - a.size)))
    bst = jnp.pad(gm_bs[0].T, ((0, 0), (0, 128 - GM_GROUPS)))
    ws = gm_ws[0]
    dtb, alog, dsk, snk = pad128(dt_bias), pad128(a_log), pad128(d_skip), pad128(sinks)
    expand = (jnp.arange(128)[:, None] == (jnp.arange(D_MODEL) // SSD_HEAD_DIM)[None, :]).astype(F32)
    tab = _rope_table(positions)

    xa, s_f1a = _ffn_fwd(x0, row(norm_ffn1[0]), wg1[0], wu1[0], wd1[0], "l0f1")
    even_w = (w_uv, w_z, w_xbc, w_dt, w_out_a, w_out_b)
    even_p = (row(norm_mix[0]), gm_ln_g, gm_ln_b, ws, bst, conv_w_full, conv_b, dtb, alog, dsk, ssd_norm, expand)
    xb, s_even = _even_fwd(xa, even_w, even_p)
    xc, s_x0 = _xattn_layer_fwd(xb, mem2, row(norm_xq[0]), row(norm_mem[0]), wxq[0], wxkv[0], wxo[0], "l0x")
    xd, s_f2a = _ffn_fwd(xc, row(norm_ffn2[0]), wg2[0], wu2[0], wd2[0], "l0f2")
    xe, s_f1b = _ffn_fwd(xd, row(norm_ffn1[1]), wg1[1], wu1[1], wd1[1], "l1f1")
    odd_w = (w_q, w_kv, w_o)
    odd_p = (row(norm_mix[1]), b_q, b_kv, tab, snk)
    xf, s_odd = _odd_fwd(xe, odd_w, odd_p)
    xg, s_x1 = _xattn_layer_fwd(xf, mem2, row(norm_xq[1]), row(norm_mem[1]), wxq[1], wxkv[1], wxo[1], "l1x")
    xh, s_f2b = _ffn_fwd(xg, row(norm_ffn2[1]), wg2[1], wu2[1], wd2[1], "l1f2")
    loss8, dx, d_final = _loss_fwd_bwd(xh, target, row(final_norm), "loss")
    loss = lax.psum(loss8[0, 0], ("x", "y", "c"))

    dx, dn_f2b, dwg, dwu, dwd = _ffn_bwd(dx, s_f2b, row(norm_ffn2[1]), wg2[1], wu2[1], wd2[1], "l1f2")
    g_f2 = {1: (dwg, dwu, dwd)}
    dx, dn_xq1, dn_mem1, dwxq1, dwxkv1, dwxo1 = _xattn_layer_bwd(dx, s_x1, mem2, row(norm_xq[1]), row(norm_mem[1]), wxq[1], wxkv[1], wxo[1], "l1x")
    dx, (dn_mix1, db_q, db_kv, d_snk), (dw_q, dw_kv, dw_o) = _odd_bwd(dx, s_odd, odd_w, odd_p)
    dx, dn_f1b, dwg, dwu, dwd = _ffn_bwd(dx, s_f1b, row(norm_ffn1[1]), wg1[1], wu1[1], wd1[1], "l1f1")
    g_f1 = {1: (dwg, dwu, dwd)}
    dx, dn_f2a, dwg, dwu, dwd = _ffn_bwd(dx, s_f2a, row(norm_ffn2[0]), wg2[0], wu2[0], wd2[0], "l0f2")
    g_f2[0] = (dwg, dwu, dwd)
    dx, dn_xq0, dn_mem0, dwxq0, dwxkv0, dwxo0 = _xattn_layer_bwd(dx, s_x0, mem2, row(norm_xq[0]), row(norm_mem[0]), wxq[0], wxkv[0], wxo[0], "l0x")
    dx, small_even, (dw_uv, dw_z, dw_xbc, dw_dt, dw_out_a, dw_out_b) = _even_bwd(dx, s_even, even_w, even_p)
    dn_mix0, d_lng, d_lnb, d_ws, d_bst, d_conv_w, d_conv_b, d_dtb, d_alog, d_dsk, d_ssd_norm = small_even
    dx, dn_f1a, dwg, dwu, dwd = _ffn_bwd(dx, s_f1a, row(norm_ffn1[0]), wg1[0], wu1[0], wd1[0], "l0f1")
    g_f1[0] = (dwg, dwu, dwd)
    grad_x = dx.reshape(x.shape)

    def gu_parts(gf):
        half = N_DEV // 2
        return jnp.stack([jnp.concatenate([_full_to_cols(gf[l][0], half), _full_to_cols(gf[l][1], half)], axis=0) for l in range(2)], axis=1)

    def rows_parts(gl):
        return jnp.stack([g.reshape(N_DEV, -1, g.shape[1]) for g in gl], axis=1)

    unpad_cols = _unpad_heads_cols
    P = {
        "w_ffn1_gu": gu_parts(g_f1), "w_ffn2_gu": gu_parts(g_f2),
        "w_ffn1_down": rows_parts([g_f1[0][2], g_f1[1][2]]), "w_ffn2_down": rows_parts([g_f2[0][2], g_f2[1][2]]),
        "w_in_even": _full_to_cols(jnp.concatenate([dw_uv, dw_z, dw_xbc, dw_dt[:, :SSD_HEADS]], axis=1))[:, None],
        "w_out_even": rows_parts([jnp.concatenate([dw_out_a, dw_out_b], axis=0)]),
        "w_qkv": _full_to_cols(jnp.concatenate([unpad_cols(dw_q, ATT_HEADS), unpad_cols(dw_kv, 2 * ATT_KV_HEADS)], axis=1))[:, None],
        "w_o_odd": rows_parts([_unpad_heads_cols(dw_o.T, ATT_HEADS).T]),
        "w_xq": rows_parts([dwxq0, dwxq1]), "w_xkv": rows_parts([dwxkv0, dwxkv1]),
        "w_xo": jnp.stack([_full_to_cols(dwxo0), _full_to_cols(dwxo1)], axis=1),
    }
    out = {}
    for n in SHARDED:
        recv = _all_to_all(P[n], f"a2a_{n}")
        shp = W[n].shape
        two = lambda a: a.reshape(-1, shp[-1])
        res = _adamw(recv.reshape(N_DEV, -1, shp[-1]), two(W[n]), two(M[n]), two(V[n]), f"adam_{n}")
        out[n] = [r.reshape(shp) for r in res]

    db_qkv = jnp.concatenate([_unpad_heads_cols(db_q, ATT_HEADS), _unpad_heads_cols(db_kv, 2 * ATT_KV_HEADS)], axis=1).reshape(-1)
    cw_parts = _full_to_cols(d_conv_w)
    bq_parts = db_qkv.reshape(N_DEV, -1)
    ss_parts = jnp.stack([_pack([cw_parts[k], bq_parts[k]]) for k in range(N_DEV)])
    recv = _all_to_all(ss_parts, "a2a_small")
    res = _adamw(recv, _pack([conv_w, b_qkv]), _pack([m_conv_w, m_b_qkv]), _pack([v_conv_w, v_b_qkv]), "adam_small_sharded")
    res = [_unpack(r, [conv_w.shape, b_qkv.shape]) for r in res]
    out["conv_w"] = [r[0] for r in res]
    out["b_qkv"] = [r[1] for r in res]

    small_grads = {
        "norm_ffn1": jnp.concatenate([dn_f1a, dn_f1b]), "norm_mix": jnp.concatenate([dn_mix0, dn_mix1]),
        "gm_ln_g": d_lng, "gm_ln_b": d_lnb, "gm_ws": d_ws[None], "gm_bs": d_bst[:, :GM_GROUPS].T[None],
        "conv_b": d_conv_b, "dt_bias": d_dtb[:, :SSD_HEADS], "a_log": d_alog[:, :SSD_HEADS], "d_skip": d_dsk[:, :SSD_HEADS],
        "ssd_norm": d_ssd_norm, "sinks": d_snk[:, :ATT_HEADS], "norm_xq": jnp.concatenate([dn_xq0, dn_xq1]),
        "norm_mem": jnp.concatenate([dn_mem0, dn_mem1]), "norm_ffn2": jnp.concatenate([dn_f2a, dn_f2b]),
        "final_norm": d_final.reshape(-1),
    }
    shapes = [W[n].shape for n in SMALL]
    recv = _all_gather(_pack([small_grads[n] for n in SMALL]), "ag_small_grads")
    res = _adamw(recv, _pack([W[n] for n in SMALL]), _pack([M[n] for n in SMALL]), _pack([V[n] for n in SMALL]), "adam_small")
    res = [_unpack(r, shapes) for r in res]
    for i, n in enumerate(SMALL):
        out[n] = [r[i] for r in res]

    return (loss, grad_x, *[out[n][0] for n in ORDER], *[out[n][1] for n in ORDER], *[out[n][2] for n in ORDER],
            *[out[n][3] for n in ORDER])
```

```python
import functools

import jax
import jax.numpy as jnp
from jax import lax
from jax.experimental import pallas as pl
from jax.experimental.pallas import tpu as pltpu

F32, BF16 = jnp.float32, jnp.bfloat16
HIGHEST = lax.Precision.HIGHEST

N_DEV = 8
D_MODEL = 2048
D_FF = 5632
EPS = 1e-5
CHUNK = 128
GM_GROUPS, GM_GDIM = 4, 512
SSD_HEADS, SSD_HEAD_DIM, SSD_GROUPS, SSD_STATE = 32, 64, 4, 128
XBC_WIDTH = D_MODEL + 2 * SSD_GROUPS * SSD_STATE
Z_COL = 2 * GM_GROUPS * GM_GDIM // D_MODEL
XBC_COL = (2 * GM_GROUPS * GM_GDIM + D_MODEL) // XBC_WIDTH
ATT_HEADS, ATT_KV_HEADS, ATT_HEAD_DIM, ATT_REP = 32, 4, 64, 8
HEAD_PAD = 128
ROT_HALF = 8
ROPE_THETA = 500000.0
ATT_SCALE = ATT_HEAD_DIM ** -0.5
X_HEADS, X_HEAD_DIM = 4, 128
X_SCALE = X_HEAD_DIM ** -0.5
ADAM_LR, ADAM_B1, ADAM_B2, ADAM_EPS, ADAM_WD, ADAM_STEP = 0.001, 0.9, 0.999, 1e-08, 0.01, 10

VMEM_LIMIT_BYTES = 56 * 1024 * 1024
MESH_ID = pl.DeviceIdType.MESH


def _params(*sem):
    return pltpu.CompilerParams(dimension_semantics=sem, vmem_limit_bytes=VMEM_LIMIT_BYTES)


def _pick(n, cands):
    for c in cands:
        if n % c == 0:
            return c
    return n


def _dg(a, b, ca, cb, precision=None):
    return lax.dot_general(a, b, (((ca,), (cb,)), ((), ())), precision=precision, preferred_element_type=F32)


@jax.custom_vjp
def _bdot(a, b):
    return _dg(a.astype(BF16), b.astype(BF16), 1, 0)


def _bdot_fwd(a, b):
    return _bdot(a, b), (a, b)


def _bdot_bwd(r, g):
    a, b = r
    g = g.astype(BF16)
    return _dg(g, b.astype(BF16), 1, 1), _dg(a.astype(BF16), g, 0, 0)


_bdot.defvjp(_bdot_fwd, _bdot_bwd)


@jax.custom_vjp
def _bdot_nt(a, b):
    return _dg(a.astype(BF16), b.astype(BF16), 1, 1)


def _bdot_nt_fwd(a, b):
    return _bdot_nt(a, b), (a, b)


def _bdot_nt_bwd(r, g):
    a, b = r
    g = g.astype(BF16)
    return _dg(g, b.astype(BF16), 1, 0), _dg(g, a.astype(BF16), 0, 0)


_bdot_nt.defvjp(_bdot_nt_fwd, _bdot_nt_bwd)


@jax.custom_vjp
def _bdot_tn(a, b):
    return _dg(a.astype(BF16), b.astype(BF16), 0, 0)


def _bdot_tn_fwd(a, b):
    return _bdot_tn(a, b), (a, b)


def _bdot_tn_bwd(r, g):
    a, b = r
    g = g.astype(BF16)
    return _dg(b.astype(BF16), g, 1, 1), _dg(a.astype(BF16), g, 1, 0)


_bdot_tn.defvjp(_bdot_tn_fwd, _bdot_tn_bwd)


@jax.custom_vjp
def _hdot(a, b):
    return _dg(a, b, 1, 0, HIGHEST)


def _hdot_fwd(a, b):
    return _hdot(a, b), (a, b)


def _hdot_bwd(r, g):
    a, b = r
    return _dg(g, b, 1, 1, HIGHEST), _dg(a, g, 0, 0, HIGHEST)


_hdot.defvjp(_hdot_fwd, _hdot_bwd)


def _sigmoid(x):
    return 1.0 / (1.0 + jnp.exp(-x))


def _silu(x):
    return x * _sigmoid(x)


def _gelu(x):
    return 0.5 * x * (1.0 + lax.erf(x * 0.7071067811865476))


def _softplus(x):
    return jnp.maximum(x, 0.0) + jnp.log1p(jnp.exp(-jnp.abs(x)))


def _rms(x, g):
    return x * lax.rsqrt(jnp.mean(x * x, -1, keepdims=True) + EPS) * g


def _iota(shape, dim):
    return lax.broadcasted_iota(jnp.int32, shape, dim)


def _matmul(a, b, *, ta=False, tb=False, out_dtype=F32, alpha=1.0, bias=None, res=None, name, freight=None):
    cargo = freight.take(name) if freight is not None else []
    if ta:
        kk, m = a.shape
    else:
        m, kk = a.shape
    if tb:
        n, k2 = b.shape
    else:
        k2, n = b.shape
    assert kk == k2, (a.shape, b.shape, ta, tb)
    tm = _pick(m, (1024, 512, 256, 128))
    tn = _pick(n, (1024, 512, 256, 128))
    tk = _pick(kk, (2048, 1024, 512, 256, 128))
    if tk == 2048 and F32 in (a.dtype, b.dtype):
        tk = 1024
    nk = kk // tk
    nc = len(cargo)
    has_bias, has_res = bias is not None, res is not None

    def body(*refs):
        a_ref, b_ref = refs[0], refs[1]
        pos = 2
        bias_ref = res_ref = None
        if has_bias:
            bias_ref = refs[pos]
            pos += 1
        if has_res:
            res_ref = refs[pos]
            pos += 1
        cargo_in = refs[pos:pos + nc]
        pos += nc
        o_ref = refs[pos]
        cargo_out = refs[pos + 1:pos + 1 + nc]
        acc_ref = refs[pos + 1 + nc]
        sems = refs[pos + 2 + nc:]
        i, j, k = pl.program_id(0), pl.program_id(1), pl.program_id(2)

        if nc:
            @pl.when((i == 0) & (j == 0) & (k == 0))
            def _():
                for c, (_, _, kind) in enumerate(cargo):
                    _EX_START[kind](cargo_in[c], cargo_out[c], *sems[3 * c:3 * c + 3])

        @pl.when(k == 0)
        def _():
            acc_ref[...] = jnp.zeros_like(acc_ref)

        acc_ref[...] += _dg(a_ref[...].astype(BF16), b_ref[...].astype(BF16), 0 if ta else 1, 1 if tb else 0)

        @pl.when(k == nk - 1)
        def _():
            r = acc_ref[...]
            if alpha != 1.0:
                r = r * alpha
            if has_bias:
                r = r + bias_ref[...]
            if has_res:
                r = r + res_ref[...]
            o_ref[...] = r.astype(out_dtype)

        if nc:
            @pl.when((i == m // tm - 1) & (j == n // tn - 1) & (k == nk - 1))
            def _():
                for c, (_, _, kind) in enumerate(cargo):
                    _EX_FINISH[kind](cargo_in[c], cargo_out[c], *sems[3 * c:3 * c + 3])

    in_specs = [
        pl.BlockSpec((tk, tm), lambda i, j, k: (k, i)) if ta else pl.BlockSpec((tm, tk), lambda i, j, k: (i, k)),
        pl.BlockSpec((tn, tk), lambda i, j, k: (j, k)) if tb else pl.BlockSpec((tk, tn), lambda i, j, k: (k, j)),
    ]
    args = [a, b]
    if has_bias:
        in_specs.append(pl.BlockSpec((1, tn), lambda i, j, k: (0, j)))
        args.append(bias)
    if has_res:
        in_specs.append(pl.BlockSpec((tm, tn), lambda i, j, k: (i, j)))
        args.append(res)
    anyspec = pl.BlockSpec(memory_space=pl.ANY)
    out_specs = [pl.BlockSpec((tm, tn), lambda i, j, k: (i, j))] + [anyspec] * nc
    out_shape = [jax.ShapeDtypeStruct((m, n), out_dtype)] + [_ex_out_shape(arr, kind) for _, arr, kind in cargo]
    scratch = [pltpu.VMEM((tm, tn), F32)]
    for _ in cargo:
        scratch += _ex_scratch()
    res_all = pl.pallas_call(
        body, name=name, grid=(m // tm, n // tn, nk), in_specs=in_specs + [anyspec] * nc,
        out_specs=out_specs, out_shape=out_shape, scratch_shapes=scratch,
        compiler_params=_params("arbitrary", "arbitrary", "arbitrary") if nc else _params("parallel", "parallel", "arbitrary"),
    )(*args, *[arr for _, arr, _ in cargo])
    for (key, _, _), landed in zip(cargo, res_all[1:]):
        freight.landed[key] = landed
    return res_all[0]


def _row(arr, width=None, cidx=0, shift=0):
    return (arr, arr.shape[-1] if width is None else width, cidx, shift)


def _rows(fn, rows, consts, outs, accs=(), *, tm, ncol=1, n_rows=None, name):
    n = rows[0][0].shape[-2] if n_rows is None else n_rows
    assert n % tm == 0, (n, tm, name)
    nb = n // tm
    n_in = len(rows) + len(consts)
    n_out = len(outs)

    def cfun(cidx):
        return cidx if callable(cidx) else (lambda j, c=cidx: c)

    in_specs = []
    for arr, width, cidx, shift in rows:
        cf = cfun(cidx)
        if callable(shift):
            rf = shift
        elif shift:
            rf = lambda i, s=shift: jnp.clip(i + s, 0, nb - 1)
        else:
            rf = lambda i: i
        if arr.ndim == 3:
            in_specs.append(pl.BlockSpec((arr.shape[0], tm, width), lambda i, j, rf=rf, cf=cf: (0, rf(i), cf(j))))
        else:
            in_specs.append(pl.BlockSpec((tm, width), lambda i, j, rf=rf, cf=cf: (rf(i), cf(j))))
    for c in consts:
        in_specs.append(pl.BlockSpec(c.shape, lambda i, j, nd=c.ndim: (0,) * nd))
    out_shape, out_specs = [], []
    for o in outs:
        width, dt = o[0], o[1]
        total = o[2] if len(o) > 2 else width * ncol
        out_shape.append(jax.ShapeDtypeStruct((n, total), dt))
        out_specs.append(pl.BlockSpec((tm, width), lambda i, j: (i, j)))
    for shp in accs:
        out_shape.append(jax.ShapeDtypeStruct(shp, F32))
        out_specs.append(pl.BlockSpec(shp, lambda i, j, nd=len(shp): (0,) * nd))

    def body(*refs):
        i, j = pl.program_id(0), pl.program_id(1)
        ins = [r[...] for r in refs[:n_in]]
        ro, ao = fn(i, j, *ins)
        for r, val in zip(refs[n_in:n_in + n_out], ro):
            r[...] = val.astype(r.dtype)
        if accs:
            acc_refs = refs[n_in + n_out:]

            @pl.when((i == 0) & (j == 0))
            def _():
                for r in acc_refs:
                    r[...] = jnp.zeros_like(r)

            for r, val in zip(acc_refs, ao):
                r[...] += val

    res = pl.pallas_call(
        body, name=name, grid=(nb, ncol), in_specs=in_specs, out_specs=out_specs, out_shape=out_shape,
        compiler_params=_params("arbitrary", "arbitrary"),
    )(*[r[0] for r in rows], *consts)
    return res


def _rms_fwd(x, g, name):
    def fn(i, j, x, g):
        return (_rms(x, g),), ()
    return _rows(fn, [_row(x)], [g], [(x.shape[1], BF16)], tm=_pick(x.shape[0], (512, 256)), name=name)[0]


def _rms_bwd(x, dh, dres, g, name):
    def fn(i, j, x, dh, dres, g):
        _, vjp = jax.vjp(_rms, x, g)
        dx, dg = vjp(dh)
        return (dx + dres,), (dg,)
    dx, dg = _rows(fn, [_row(x), _row(dh), _row(dres)], [g], [(x.shape[1], F32)], [g.shape],
                   tm=_pick(x.shape[0], (256,)), name=name)
    return dx, dg


def _rms_bwd_gain(x, dh, g, name):
    def fn(i, j, x, dh, g):
        _, vjp = jax.vjp(_rms, x, g)
        return (), (vjp(dh)[1],)
    return _rows(fn, [_row(x), _row(dh)], [g], [], [g.shape], tm=_pick(x.shape[0], (256,)), name=name)[0]


def _swiglu_fwd(g, u, name):
    def fn(i, j, g, u):
        return (_silu(g.astype(F32)) * u.astype(F32),), ()
    n, w = g.shape
    return _rows(fn, [_row(g, 512, lambda j: j), _row(u, 512, lambda j: j)], [], [(512, BF16)],
                 tm=_pick(n, (1024, 512, 256)), ncol=w // 512, name=name)[0]


def _swiglu_bwd(g, u, dact, name):
    def fn(i, j, g, u, da):
        g, u, da = g.astype(F32), u.astype(F32), da.astype(F32)
        s = _sigmoid(g)
        return (da * u * (s * (1.0 + g * (1.0 - s))), da * (g * s)), ()
    n, w = g.shape
    return _rows(fn, [_row(g, 512, lambda j: j), _row(u, 512, lambda j: j), _row(dact, 512, lambda j: j)], [],
                 [(512, BF16), (512, BF16)], tm=_pick(n, (1024, 512, 256)), ncol=w // 512, name=name)


def _gmlp_fn(uv, lng, lnb, ws0, ws1, ws2, ws3, bst):
    ws = (ws0, ws1, ws2, ws3)
    q = uv.shape[0]
    u = _gelu(uv[:, :D_MODEL])
    v = _gelu(uv[:, D_MODEL:])
    tril = _iota((q, q), 0) >= _iota((q, q), 1)
    outs = []
    for g in range(GM_GROUPS):
        sl = slice(GM_GDIM * g, GM_GDIM * (g + 1))
        vg = v[:, sl]
        mu = jnp.mean(vg, -1, keepdims=True)
        var = jnp.mean(jnp.square(vg - mu), -1, keepdims=True)
        vn = (vg - mu) * lax.rsqrt(var + EPS) * lng[:, sl] + lnb[:, sl]
        w = jnp.where(tril, ws[g], 0.0)
        bcol = jnp.sum(bst * (_iota((1, 128), 1) == g).astype(F32), axis=1, keepdims=True)
        outs.append(u[:, sl] * (_bdot(w, vn) + bcol))
    return jnp.concatenate(outs, axis=1)


def _gmlp_fwd(uv, lng, lnb, ws, bst, name):
    def fn(i, j, uv, lng, lnb, ws, bst):
        return (_gmlp_fn(uv, lng, lnb, ws[0], ws[1], ws[2], ws[3], bst),), ()
    return _rows(fn, [_row(uv, 2 * D_MODEL, 0)], [lng, lnb, ws, bst], [(D_MODEL, BF16)], tm=CHUNK, name=name)[0]


def _gmlp_bwd(uv, da, lng, lnb, ws, bst, name):
    def fn(i, j, uv, da, lng, lnb, ws, bst):
        _, vjp = jax.vjp(_gmlp_fn, uv, lng, lnb, ws[0], ws[1], ws[2], ws[3], bst)
        duv, dlng, dlnb, d0, d1, d2, d3, dbst = vjp(da)
        return (duv,), (dlng, dlnb, jnp.stack([d0, d1, d2, d3]), dbst)
    return _rows(fn, [_row(uv, 2 * D_MODEL, 0), _row(da)], [lng, lnb, ws, bst], [(2 * D_MODEL, BF16)],
                 [lng.shape, lnb.shape, ws.shape, bst.shape], tm=CHUNK, name=name)


def _conv_taps(scr, x, halo, first_row):
    q = x.shape[0]
    scr[pl.ds(0, 8), :] = halo
    scr[pl.ds(8, q), :] = x
    return [scr[pl.ds(first_row + k, q), :] for k in range(4)]


def _conv_fwd(proj, w, b, name):
    n, c = proj.shape[0], w.shape[1]
    xbc = proj
    q = CHUNK

    def body(x_ref, xp_ref, w_ref, b_ref, o_ref, scr):
        i = pl.program_id(0)
        halo = jnp.where(i > 0, xp_ref[pl.ds(q - 8, 8), :], 0.0)
        taps = _conv_taps(scr, x_ref[...], halo, 5)
        pre = b_ref[...] + sum(taps[k] * w_ref[pl.ds(k, 1), :] for k in range(4))
        o_ref[...] = _silu(pre)

    return pl.pallas_call(
        body, name=name, grid=(n // q,),
        in_specs=[pl.BlockSpec((q, c), lambda i: (i, XBC_COL)), pl.BlockSpec((q, c), lambda i: (jnp.maximum(i - 1, 0), XBC_COL)),
                  pl.BlockSpec(w.shape, lambda i: (0, 0)), pl.BlockSpec(b.shape, lambda i: (0, 0))],
        out_specs=pl.BlockSpec((q, c), lambda i: (i, 0)), out_shape=jax.ShapeDtypeStruct((n, c), F32),
        scratch_shapes=[pltpu.VMEM((q + 8, c), F32)], compiler_params=_params("arbitrary"),
    )(xbc, xbc, w, b)


def _conv_bwd_pre(xbc, dy, w, b, name):
    n, c = xbc.shape[0], w.shape[1]
    q = CHUNK

    def body(x_ref, xp_ref, dy_ref, w_ref, b_ref, dp_ref, dw_ref, db_ref, scr):
        i = pl.program_id(0)
        halo = jnp.where(i > 0, xp_ref[pl.ds(q - 8, 8), :], 0.0)
        taps = _conv_taps(scr, x_ref[...], halo, 5)
        pre = b_ref[...] + sum(taps[k] * w_ref[pl.ds(k, 1), :] for k in range(4))
        s = _sigmoid(pre)
        dp = dy_ref[...] * (s * (1.0 + pre * (1.0 - s)))
        dp_ref[...] = dp

        @pl.when(i == 0)
        def _():
            dw_ref[...] = jnp.zeros_like(dw_ref)
            db_ref[...] = jnp.zeros_like(db_ref)

        db_ref[...] += jnp.sum(dp, axis=0, keepdims=True)
        for k in range(4):
            dw_ref[pl.ds(k, 1), :] += jnp.sum(dp * taps[k], axis=0, keepdims=True)

    return pl.pallas_call(
        body, name=name, grid=(n // q,),
        in_specs=[pl.BlockSpec((q, c), lambda i: (i, XBC_COL)), pl.BlockSpec((q, c), lambda i: (jnp.maximum(i - 1, 0), XBC_COL)),
                  pl.BlockSpec((q, c), lambda i: (i, 0)),
                  pl.BlockSpec(w.shape, lambda i: (0, 0)), pl.BlockSpec(b.shape, lambda i: (0, 0))],
        out_specs=[pl.BlockSpec((q, c), lambda i: (i, 0)), pl.BlockSpec(w.shape, lambda i: (0, 0)),
                   pl.BlockSpec(b.shape, lambda i: (0, 0))],
        out_shape=[jax.ShapeDtypeStruct((n, c), F32), jax.ShapeDtypeStruct(w.shape, F32), jax.ShapeDtypeStruct(b.shape, F32)],
        scratch_shapes=[pltpu.VMEM((q + 8, c), F32)], compiler_params=_params("arbitrary"),
    )(xbc, xbc, dy, w, b)


def _conv_bwd_x(dpre, w, name):
    n, c = dpre.shape
    q = CHUNK
    nb = n // q

    def body(d_ref, dn_ref, w_ref, o_ref, scr):
        i = pl.program_id(0)
        scr[pl.ds(0, q), :] = d_ref[...]
        scr[pl.ds(q, 8), :] = jnp.where(i < nb - 1, dn_ref[pl.ds(0, 8), :], 0.0)
        o_ref[...] = sum(scr[pl.ds(3 - k, q), :] * w_ref[pl.ds(k, 1), :] for k in range(4)).astype(o_ref.dtype)

    return pl.pallas_call(
        body, name=name, grid=(nb,),
        in_specs=[pl.BlockSpec((q, c), lambda i: (i, 0)), pl.BlockSpec((q, c), lambda i: (jnp.minimum(i + 1, nb - 1), 0)),
                  pl.BlockSpec(w.shape, lambda i: (0, 0))],
        out_specs=pl.BlockSpec((q, c), lambda i: (i, 0)), out_shape=jax.ShapeDtypeStruct((n, c), BF16),
        scratch_shapes=[pltpu.VMEM((q + 8, c), F32)], compiler_params=_params("arbitrary"),
    )(dpre, dpre, w)


def _ssd_chunk(xs, bc, dtr, dtb, alog, dsk, expand, hp):
    q = xs.shape[0]
    tril = _iota((q, q), 0) >= _iota((q, q), 1)
    dt = _softplus(dtr + dtb)
    a = dt * (-jnp.exp(alog))
    cs = _hdot(tril.astype(F32), a)
    cs_t = cs.T
    last = (_iota((q, 1), 0) == q - 1).astype(F32)
    gw = SSD_HEADS // SSD_GROUPS * SSD_HEAD_DIM
    lane = _iota((1, gw), 1)
    ys, hs = [], []
    for g in range(SSD_GROUPS):
        sl = slice(gw * g, gw * (g + 1))
        eg = expand[:, sl]
        dt_e, cs_e = _hdot(dt, eg), _hdot(cs, eg)
        cl_e = jnp.sum(cs_e * last, axis=0, keepdims=True)
        d_e = jnp.sum(_hdot(jnp.broadcast_to(dsk, (8, 128)), eg), axis=0, keepdims=True) * 0.125
        xg = xs[:, sl]
        xdt = xg * dt_e
        bg = bc[:, SSD_STATE * g:SSD_STATE * (g + 1)]
        cg = bc[:, SSD_GROUPS * SSD_STATE + SSD_STATE * g:SSD_GROUPS * SSD_STATE + SSD_STATE * (g + 1)]
        cb = _bdot_nt(cg, bg)
        ms, xm = [], []
        for r in range(SSD_HEADS // SSD_GROUPS):
            h = g * (SSD_HEADS // SSD_GROUPS) + r
            col = jnp.sum(cs * (_iota((1, 128), 1) == h).astype(F32), axis=1, keepdims=True)
            row = jnp.sum(cs_t * (_iota((128, 1), 0) == h).astype(F32), axis=0, keepdims=True)
            decay = jnp.where(tril, jnp.exp(jnp.where(tril, col - row, 0.0)), 0.0)
            ms.append(cb * decay)
            xm.append(xdt * ((lane >= SSD_HEAD_DIM * r) & (lane < SSD_HEAD_DIM * (r + 1))).astype(F32))
        y_diag = _bdot(jnp.concatenate(ms, axis=1), jnp.concatenate(xm, axis=0))
        hg = hp[:, sl]
        y_off = _bdot(cg, hg) * jnp.exp(cs_e)
        states = _bdot_tn(bg, xdt * jnp.exp(cl_e - cs_e))
        hs.append(hg * jnp.exp(cl_e) + states)
        ys.append(y_diag + y_off + xg * d_e)
    return jnp.concatenate(ys, axis=1), jnp.concatenate(hs, axis=1)


def _ssd_fwd(xbc, dtr, dtb, alog, dsk, expand, name):
    n = xbc.shape[0]
    q, w = CHUNK, D_MODEL
    nc = n // q

    def body(xs_ref, bc_ref, dtr_ref, dtb_ref, alog_ref, dsk_ref, e_ref, y_ref, hp_ref, h_scr):
        @pl.when(pl.program_id(0) == 0)
        def _():
            h_scr[...] = jnp.zeros_like(h_scr)

        hp = h_scr[...]
        y, hn = _ssd_chunk(xs_ref[...], bc_ref[...], dtr_ref[...], dtb_ref[...], alog_ref[...], dsk_ref[...], e_ref[...], hp)
        y_ref[...] = y
        hp_ref[...] = hp
        h_scr[...] = hn

    small = pl.BlockSpec((1, 128), lambda c: (0, 0))
    return pl.pallas_call(
        body, name=name, grid=(nc,),
        in_specs=[pl.BlockSpec((q, w), lambda c: (c, 0)), pl.BlockSpec((q, 1024), lambda c: (c, 2)),
                  pl.BlockSpec((q, 128), lambda c: (c, 0)), small, small, small, pl.BlockSpec((128, w), lambda c: (0, 0))],
        out_specs=[pl.BlockSpec((q, w), lambda c: (c, 0)), pl.BlockSpec((SSD_STATE, w), lambda c: (c, 0))],
        out_shape=[jax.ShapeDtypeStruct((n, w), F32), jax.ShapeDtypeStruct((nc * SSD_STATE, w), F32)],
        scratch_shapes=[pltpu.VMEM((SSD_STATE, w), F32)], compiler_params=_params("arbitrary"),
    )(xbc, xbc, dtr, dtb, alog, dsk, expand)


def _ssd_bwd(xbc, dtr, dtb, alog, dsk, expand, hp_all, dy, name):
    n = xbc.shape[0]
    q, w = CHUNK, D_MODEL
    nc = n // q

    def body(xs_ref, bc_ref, dtr_ref, dtb_ref, alog_ref, dsk_ref, e_ref, hp_ref, dy_ref,
             dxbc_ref, ddtr_ref, ddtb_ref, dalog_ref, ddsk_ref, dh_scr):
        @pl.when(pl.program_id(0) == 0)
        def _():
            dh_scr[...] = jnp.zeros_like(dh_scr)
            ddtb_ref[...] = jnp.zeros_like(ddtb_ref)
            dalog_ref[...] = jnp.zeros_like(dalog_ref)
            ddsk_ref[...] = jnp.zeros_like(ddsk_ref)

        e = e_ref[...]
        _, vjp = jax.vjp(lambda xs, bc, dtr, dtb, alog, dsk, hp: _ssd_chunk(xs, bc, dtr, dtb, alog, dsk, e, hp),
                         xs_ref[...], bc_ref[...], dtr_ref[...], dtb_ref[...], alog_ref[...], dsk_ref[...], hp_ref[...])
        dxs, dbc, ddtr, ddtb, dalog, ddsk, dhp = vjp((dy_ref[...], dh_scr[...]))
        dxbc_ref[...] = jnp.concatenate([dxs, dbc], axis=1)
        ddtr_ref[...] = ddtr
        ddtb_ref[...] += ddtb
        dalog_ref[...] += dalog
        ddsk_ref[...] += ddsk
        dh_scr[...] = dhp

    rev = lambda c: nc - 1 - c
    small = pl.BlockSpec((1, 128), lambda c: (0, 0))
    return pl.pallas_call(
        body, name=name, grid=(nc,),
        in_specs=[pl.BlockSpec((q, w), lambda c: (rev(c), 0)), pl.BlockSpec((q, 1024), lambda c: (rev(c), 2)),
                  pl.BlockSpec((q, 128), lambda c: (rev(c), 0)), small, small, small,
                  pl.BlockSpec((128, w), lambda c: (0, 0)), pl.BlockSpec((SSD_STATE, w), lambda c: (rev(c), 0)),
                  pl.BlockSpec((q, w), lambda c: (rev(c), 0))],
        out_specs=[pl.BlockSpec((q, w + 1024), lambda c: (rev(c), 0)),
                   pl.BlockSpec((q, 128), lambda c: (rev(c), 0)), small, small, small],
        out_shape=[jax.ShapeDtypeStruct((n, w + 1024), F32), jax.ShapeDtypeStruct((n, 128), F32),
                   jax.ShapeDtypeStruct((1, 128), F32), jax.ShapeDtypeStruct((1, 128), F32), jax.ShapeDtypeStruct((1, 128), F32)],
        scratch_shapes=[pltpu.VMEM((SSD_STATE, w), F32)], compiler_params=_params("arbitrary"),
    )(xbc, xbc, dtr, dtb, alog, dsk, expand, hp_all, dy)


def _gate_fn(y, z, g):
    outs = []
    for k in range(SSD_GROUPS):
        sl = slice(512 * k, 512 * (k + 1))
        yg = y[:, sl] * _silu(z[:, sl])
        outs.append(yg * lax.rsqrt(jnp.mean(yg * yg, -1, keepdims=True) + EPS) * g[:, sl])
    return jnp.concatenate(outs, axis=1)


def _gate_fwd(y, z, g, name):
    def fn(i, j, y, z, g):
        return (_gate_fn(y, z, g),), ()
    return _rows(fn, [_row(y), _row(z, D_MODEL, Z_COL)], [g], [(D_MODEL, BF16)], tm=_pick(y.shape[0], (256, 128)), name=name)[0]


def _gate_bwd(y, z, db, g, name):
    def fn(i, j, y, z, db, g):
        _, vjp = jax.vjp(_gate_fn, y, z, g)
        dy, dz, dg = vjp(db)
        return (dy, dz), (dg,)
    return _rows(fn, [_row(y), _row(z, D_MODEL, Z_COL), _row(db)], [g], [(D_MODEL, F32), (D_MODEL, BF16)], [g.shape],
                 tm=_pick(y.shape[0], (256, 128)), name=name)


def _rope(x, tab, sign=1.0):
    return x * tab[:, 0:128] + sign * (pltpu.roll(x, 128 - ROT_HALF, 1) * tab[:, 128:256] + pltpu.roll(x, ROT_HALF, 1) * tab[:, 256:384])


def _swa_core(qq, kc, vc, sinks, kv_head, first):
    q = kc.shape[0] // 2
    s = _bdot_nt(qq, kc) * ATT_SCALE
    rows = ATT_REP * q
    iq = _iota((rows, 2 * q), 0) & (q - 1)
    js = _iota((rows, 2 * q), 1)
    rel = iq + q - js
    mask = (rel >= 0) & (rel < q) & ((js >= q) | jnp.logical_not(first))
    s = jnp.where(mask, s, -jnp.inf)
    rep = lax.shift_right_logical(_iota((rows, 1), 0), q.bit_length() - 1)
    sink = jnp.zeros((rows, 1), F32)
    for r in range(ATT_REP):
        s_r = jnp.sum(sinks * (_iota((1, 128), 1) == kv_head * ATT_REP + r).astype(F32), axis=1, keepdims=True)
        sink = sink + jnp.where(rep == r, s_r, 0.0)
    m = jnp.maximum(jnp.max(s, -1, keepdims=True), sink)
    p = jnp.exp(s - m)
    pr = p / (jnp.sum(p, -1, keepdims=True) + jnp.exp(sink - m))
    return _bdot(pr, vc)


def _swa_prep(q, kv, kvp, tab, tabp, kv_head):
    w = HEAD_PAD
    kc = jnp.concatenate([_rope(kvp[:, w * kv_head:w * (kv_head + 1)], tabp), _rope(kv[:, w * kv_head:w * (kv_head + 1)], tab)], axis=0)
    o = ATT_KV_HEADS * w
    vc = jnp.concatenate([kvp[:, o + w * kv_head:o + w * (kv_head + 1)], kv[:, o + w * kv_head:o + w * (kv_head + 1)]], axis=0)
    qq = jnp.concatenate([_rope(q[:, w * (kv_head * ATT_REP + r):w * (kv_head * ATT_REP + r + 1)], tab) for r in range(ATT_REP)], axis=0)
    return qq, kc, vc


def _swa_fwd(q, kv, tab, sinks, name):
    def fn(i, j, q, kv, kvp, tab, tabp, sinks):
        first = i == 0
        outs = []
        for h in range(ATT_KV_HEADS):
            qq, kc, vc = _swa_prep(q, kv, kvp, tab, tabp, h)
            o = _swa_core(qq, kc, vc, sinks, h, first)
            outs += [o[CHUNK * r:CHUNK * (r + 1)] for r in range(ATT_REP)]
        return (jnp.concatenate(outs, axis=1),), ()
    return _rows(fn, [_row(q), _row(kv), _row(kv, shift=-1), _row(tab), _row(tab, shift=-1)], [sinks],
                 [(q.shape[1], BF16)], tm=CHUNK, name=name)[0]


def _swa_bwd(q, kv, tab, sinks, do, name):
    w = HEAD_PAD

    def fn(i, j, q, kv, kvp, tab, tabp, do, sinks):
        first = i == 0
        dqs, dkc, dkp, dvc, dvp = [], [], [], [], []
        dsink = jnp.zeros_like(sinks)
        for h in range(ATT_KV_HEADS):
            qq, kc, vc = _swa_prep(q, kv, kvp, tab, tabp, h)
            dout = jnp.concatenate([do[:, w * (h * ATT_REP + r):w * (h * ATT_REP + r + 1)] for r in range(ATT_REP)], axis=0)
            _, vjp = jax.vjp(lambda a, b, c, s: _swa_core(a, b, c, s, h, first), qq, kc, vc, sinks)
            dqq, dk, dv, ds = vjp(dout.astype(F32))
            dsink = dsink + ds
            dqs += [_rope(dqq[CHUNK * r:CHUNK * (r + 1)], tab, -1.0) for r in range(ATT_REP)]
            dkp.append(_rope(dk[:CHUNK], tabp, -1.0))
            dkc.append(_rope(dk[CHUNK:], tab, -1.0))
            dvp.append(dv[:CHUNK])
            dvc.append(dv[CHUNK:])
        return (jnp.concatenate(dqs, axis=1), jnp.concatenate(dkc + dvc, axis=1), jnp.concatenate(dkp + dvp, axis=1)), (dsink,)
    return _rows(fn, [_row(q), _row(kv), _row(kv, shift=-1), _row(tab), _row(tab, shift=-1), _row(do)], [sinks],
                 [(q.shape[1], BF16), (kv.shape[1], F32), (kv.shape[1], F32)], [sinks.shape], tm=CHUNK, name=name)


def _swa_combine(dkv_cur, dkv_prev, dq, name):
    nb = dq.shape[0] // CHUNK

    def fn(i, j, cur, nxt, dq):
        dkv = (cur + jnp.where(i < nb - 1, nxt, 0.0)).astype(BF16)
        return (dkv,), (jnp.sum(dq.astype(F32), axis=0, keepdims=True), jnp.sum(dkv.astype(F32), axis=0, keepdims=True))
    return _rows(fn, [_row(dkv_cur), _row(dkv_prev, shift=1), _row(dq)], [], [(dkv_cur.shape[1], BF16)],
                 [(1, dq.shape[1]), (1, dkv_cur.shape[1])], tm=CHUNK, name=name)


def _xattn_fn(q, k, v):
    outs = []
    for h in range(X_HEADS):
        sl = slice(X_HEAD_DIM * h, X_HEAD_DIM * (h + 1))
        s = _bdot_nt(q[:, sl], k[:, sl]) * X_SCALE
        p = jnp.exp(s - jnp.max(s, -1, keepdims=True))
        outs.append(_bdot(p / jnp.sum(p, -1, keepdims=True), v[:, sl]))
    return jnp.concatenate(outs, axis=1)


def _xattn_fwd(q, k, v, name):
    def fn(i, j, q, k, v):
        return (_xattn_fn(q, k, v),), ()
    return _rows(fn, [_row(q)], [k, v], [(q.shape[1], BF16)], tm=_pick(q.shape[0], (512, 256)), name=name)[0]


def _xattn_bwd(q, k, v, do, name):
    def fn(i, j, q, do, k, v):
        _, vjp = jax.vjp(_xattn_fn, q, k, v)
        dq, dk, dv = vjp(do)
        return (dq,), (dk, dv)
    return _rows(fn, [_row(q), _row(do)], [k, v], [(q.shape[1], BF16)], [k.shape, v.shape],
                 tm=_pick(q.shape[0], (512, 256)), name=name)


def _loss_fn(x, t, g):
    return 0.5 * jnp.sum(jnp.mean(jnp.square(_rms(x, g) - t), axis=-1))


def _loss_fwd_bwd(x, t, g, name):
    def fn(i, j, x, t, g):
        loss, vjp = jax.vjp(_loss_fn, x, t, g)
        dx, _, dg = vjp(jnp.ones((), F32))
        return (dx,), (jnp.broadcast_to(loss, (8, 128)), dg)
    dx, loss, dg = _rows(fn, [_row(x), _row(t)], [g], [(x.shape[1], F32)], [(8, 128), g.shape],
                         tm=_pick(x.shape[0], (256,)), name=name)
    return loss, dx, dg


def _adamw(parts, w, m, v, name):
    plist = list(parts) if isinstance(parts, (list, tuple)) else [parts]
    nl = len(plist)
    r, c = w.shape
    tm = _pick(r // nl, (128, 64, 32, 16, 8))
    nbl = r // nl // tm

    def fn(i, j, *blocks):
        parts, (w, m, v) = blocks[0], blocks[nl:]
        for l in range(1, nl):
            parts = jnp.where(i >= l * nbl, blocks[l], parts)
        g = parts[0].astype(F32)
        for k in range(1, N_DEV):
            g = g + parts[k].astype(F32)
        m2 = ADAM_B1 * m + (1.0 - ADAM_B1) * g
        v2 = ADAM_B2 * v + (1.0 - ADAM_B2) * jnp.square(g)
        m_hat = m2 / (1.0 - ADAM_B1 ** ADAM_STEP)
        v_hat = v2 / (1.0 - ADAM_B2 ** ADAM_STEP)
        delta = -ADAM_LR * (m_hat / (jnp.sqrt(v_hat) + ADAM_EPS) + ADAM_WD * w)
        return (g, delta, m2, v2), ()
    prow = [_row(p, shift=(lambda i, l=l: jnp.clip(i - l * nbl, 0, nbl - 1))) for l, p in enumerate(plist)]
    return _rows(fn, prow + [_row(w), _row(m), _row(v)], [], [(c, F32)] * 4, tm=tm, n_rows=r, name=name)


def _peer(k):
    return (k // 4, (k // 2) % 2, k % 2)


GATHER, SCATTER = "gather", "scatter"


def _ex_scratch():
    return [pltpu.SemaphoreType.DMA((N_DEV - 1,)), pltpu.SemaphoreType.DMA((N_DEV - 1,)), pltpu.SemaphoreType.DMA]


def _ex_out_shape(x, kind):
    return jax.ShapeDtypeStruct((N_DEV,) + x.shape[-2:], x.dtype)


def _gather_plan():
    x, y, c = lax.axis_index("x"), lax.axis_index("y"), lax.axis_index("c")
    return (x, y, c), (x, y, 1 - c), [(1 - x, y), (x, 1 - y), (1 - x, 1 - y)], c


def _gather_copy(x_ref, o_ref, send, recv, k, block, to, from_input=False):
    slot = o_ref.at[4 * block[0] + 2 * block[1] + block[2]]
    return pltpu.make_async_remote_copy(src_ref=x_ref if from_input else slot, dst_ref=slot, send_sem=send.at[k],
                                        recv_sem=recv.at[k], device_id=to, device_id_type=MESH_ID)


def _gather_start(x_ref, o_ref, send, recv, local):
    me, sib, chips, c = _gather_plan()
    pltpu.make_async_copy(x_ref, o_ref.at[4 * me[0] + 2 * me[1] + me[2]], local).start()
    _gather_copy(x_ref, o_ref, send, recv, 0, me, sib, True).start()
    for j, chip in enumerate(chips):
        _gather_copy(x_ref, o_ref, send, recv, 1 + j, me, (*chip, c), True).start()


def _gather_finish(x_ref, o_ref, send, recv, local):
    me, sib, chips, c = _gather_plan()
    for j, chip in enumerate(chips):
        _gather_copy(x_ref, o_ref, send, recv, 1 + j, (*chip, c), me).wait_recv()
        _gather_copy(x_ref, o_ref, send, recv, 4 + j, (*chip, c), sib).start()
    _gather_copy(x_ref, o_ref, send, recv, 0, sib, me).wait_recv()
    for j, chip in enumerate(chips):
        _gather_copy(x_ref, o_ref, send, recv, 4 + j, (*chip, 1 - c), me).wait_recv()
    _gather_copy(x_ref, o_ref, send, recv, 0, me, sib, True).wait_send()
    for j, chip in enumerate(chips):
        _gather_copy(x_ref, o_ref, send, recv, 1 + j, me, (*chip, c), True).wait_send()
        _gather_copy(x_ref, o_ref, send, recv, 4 + j, (*chip, c), sib).wait_send()
    pltpu.make_async_copy(x_ref, o_ref.at[4 * me[0] + 2 * me[1] + me[2]], local).wait()


def _scatter_copy(x_ref, o_ref, send, recv, d, frm, to):
    return pltpu.make_async_remote_copy(src_ref=x_ref.at[to], dst_ref=o_ref.at[frm], send_sem=send.at[d - 1],
                                        recv_sem=recv.at[d - 1], device_id=_peer(to), device_id_type=MESH_ID)


def _scatter_start(x_ref, o_ref, send, recv, local):
    me = 4 * lax.axis_index("x") + 2 * lax.axis_index("y") + lax.axis_index("c")
    pltpu.make_async_copy(x_ref.at[me], o_ref.at[me], local).start()
    for d in range(1, N_DEV):
        _scatter_copy(x_ref, o_ref, send, recv, d, me, (me + d) % N_DEV).start()


def _scatter_finish(x_ref, o_ref, send, recv, local):
    me = 4 * lax.axis_index("x") + 2 * lax.axis_index("y") + lax.axis_index("c")
    for d in range(1, N_DEV):
        _scatter_copy(x_ref, o_ref, send, recv, d, (me + N_DEV - d) % N_DEV, me).wait_recv()
    for d in range(1, N_DEV):
        _scatter_copy(x_ref, o_ref, send, recv, d, me, (me + d) % N_DEV).wait_send()
    pltpu.make_async_copy(x_ref.at[me], o_ref.at[me], local).wait()


_EX_START = {GATHER: _gather_start, SCATTER: _scatter_start}
_EX_FINISH = {GATHER: _gather_finish, SCATTER: _scatter_finish}


def _exchange(x, kind, name):
    def body(x_ref, o_ref, send, recv, local):
        _EX_START[kind](x_ref, o_ref, send, recv, local)
        _EX_FINISH[kind](x_ref, o_ref, send, recv, local)

    return pl.pallas_call(
        body, name=name, in_specs=[pl.BlockSpec(memory_space=pl.ANY)], out_specs=pl.BlockSpec(memory_space=pl.ANY),
        out_shape=_ex_out_shape(x, kind), scratch_shapes=_ex_scratch(),
    )(x)


class _Freight:
    def __init__(self):
        self.load = {}
        self.landed = {}

    def put(self, carrier, key, arr, kind):
        self.load.setdefault(carrier, []).append((key, arr, kind))

    def take(self, carrier):
        return self.load.pop(carrier, [])


def _all_gather(x, name):
    return _exchange(x.reshape(-1, x.shape[-1]), GATHER, name).reshape((N_DEV,) + x.shape)


def _all_to_all(x, name):
    return _exchange(x.reshape(N_DEV, -1, x.shape[-1]), SCATTER, name).reshape(x.shape)


def _pack(arrs):
    flat = []
    for a in arrs:
        a = a.reshape(-1).astype(F32)
        flat.append(jnp.pad(a, (0, (-a.shape[0]) % 1024)))
    return jnp.concatenate(flat).reshape(-1, 128)


def _unpack(p, shapes):
    p = p.reshape(-1)
    out, off = [], 0
    for s in shapes:
        n = 1
        for d in s:
            n *= d
        out.append(p[off:off + n].reshape(s))
        off += n + (-n) % 1024
    return out


PACK16 = 16 * 1024


def _pack16(arrs, lead=0):
    flat = []
    for a in arrs:
        a = a.astype(BF16).reshape(a.shape[:lead] + (-1,))
        flat.append(jnp.pad(a, [(0, 0)] * lead + [(0, (-a.shape[-1]) % PACK16)]))
    p = jnp.concatenate(flat, axis=-1)
    return p.reshape(p.shape[:lead] + (-1, 1024))


def _unpack16(p, shapes):
    p = p.reshape(p.shape[0], -1)
    out, off = [], 0
    for s in shapes:
        n = 1
        for d in s:
            n *= d
        out.append(p[:, off:off + n].reshape((p.shape[0],) + tuple(s)))
        off += n + (-n) % PACK16
    return out


def _cols_to_full(g):
    return g.transpose(1, 0, 2).reshape(g.shape[1], -1)


def _full_to_cols(w, shards=N_DEV):
    return w.reshape(w.shape[0], shards, -1).transpose(1, 0, 2)


SMALL = ("norm_ffn1", "norm_mix", "gm_ln_g", "gm_ln_b", "gm_ws", "gm_bs", "conv_b", "dt_bias", "a_log", "d_skip", "ssd_norm",
         "sinks", "norm_xq", "norm_mem", "norm_ffn2", "final_norm")
SHARDED = ("w_ffn1_gu", "w_ffn1_down", "w_in_even", "w_out_even", "w_qkv", "w_o_odd", "w_xq", "w_xkv", "w_xo", "w_ffn2_gu",
           "w_ffn2_down")
SMALL_SHARDED = ("conv_w", "b_qkv")
ORDER = ("norm_ffn1", "w_ffn1_gu", "w_ffn1_down", "norm_mix", "w_in_even", "gm_ln_g", "gm_ln_b", "gm_ws", "gm_bs", "conv_w",
         "conv_b", "dt_bias", "a_log", "d_skip", "ssd_norm", "w_out_even", "w_qkv", "b_qkv", "sinks", "w_o_odd", "norm_xq",
         "norm_mem", "w_xq", "w_xkv", "w_xo", "norm_ffn2", "w_ffn2_gu", "w_ffn2_down", "final_norm")


def _ffn_fwd(x, gain, wg, wu, wd, tag, freight=None):
    mm = functools.partial(_matmul, freight=freight)
    h = _rms_fwd(x, gain, f"{tag}_norm")
    g = mm(h, wg, out_dtype=BF16, name=f"{tag}_gate")
    u = mm(h, wu, out_dtype=BF16, name=f"{tag}_up")
    act = _swiglu_fwd(g, u, f"{tag}_act")
    wd = wd() if callable(wd) else wd
    y = mm(act, wd, alpha=0.5, res=x, name=f"{tag}_down")
    return y, (x, h, g, u, act), wd


def _ffn_bwd(dy, saved, gain, wg, wu, wd, tag, freight=None, ship=None):
    mm = functools.partial(_matmul, freight=freight)
    x, h, g, u, act = saved
    dact = mm(dy, wd, tb=True, alpha=0.5, out_dtype=BF16, name=f"{tag}_dact")
    dwd = mm(act, dy, ta=True, alpha=0.5, out_dtype=BF16, name=f"{tag}_dwd")
    if ship is not None:
        ship(f"{tag}_dwg", "dn", dwd.reshape(N_DEV, -1, dwd.shape[1]))
    dg, du = _swiglu_bwd(g, u, dact, f"{tag}_dgu")
    dwg = mm(h, dg, ta=True, out_dtype=BF16, name=f"{tag}_dwg")
    dwu = mm(h, du, ta=True, out_dtype=BF16, name=f"{tag}_dwu")
    if ship is not None:
        half = N_DEV // 2
        parts = jnp.concatenate([_full_to_cols(dwg, half), _full_to_cols(dwu, half)], axis=0)
        rows = parts.shape[1] // 2
        ship(f"{tag}_dh_g", "gu_a", parts[:, :rows])
        ship(f"{tag}_dh_u", "gu_b", parts[:, rows:])
    dh = mm(dg, wg, tb=True, name=f"{tag}_dh_g")
    dh = mm(du, wu, tb=True, res=dh, name=f"{tag}_dh_u")
    dx, dgain = _rms_bwd(x, dh, dy, gain, f"{tag}_dnorm")
    return dx, dgain, dwg, dwu, dwd


def _xattn_layer_fwd(x, mem, gq, gm, wq, wkv, wo, tag, freight=None):
    mm = functools.partial(_matmul, freight=freight)
    hq = _rms_fwd(x, gq, f"{tag}_normq")
    mn = _rms_fwd(mem, gm, f"{tag}_normm")
    q = mm(hq, wq, name=f"{tag}_q")
    kv = mm(mn, wkv, name=f"{tag}_kv")
    k, v = kv[:, :X_HEADS * X_HEAD_DIM], kv[:, X_HEADS * X_HEAD_DIM:]
    o = _xattn_fwd(q, k, v, f"{tag}_attn")
    y = mm(o, wo, res=x, name=f"{tag}_o")
    return y, (x, hq, mn, q, k, v, o)


def _xattn_layer_bwd(dy, saved, mem, gq, gm, wq, wkv, wo, tag, freight=None):
    mm = functools.partial(_matmul, freight=freight)
    x, hq, mn, q, k, v, o = saved
    do = mm(dy, wo, tb=True, name=f"{tag}_do")
    dwo = mm(o, dy, ta=True, out_dtype=BF16, name=f"{tag}_dwo")
    dq, dk, dv = _xattn_bwd(q, k, v, do, f"{tag}_dattn")
    dkv = jnp.concatenate([dk, dv], axis=1)
    dwq = mm(hq, dq, ta=True, out_dtype=BF16, name=f"{tag}_dwq")
    dwkv = mm(mn, dkv, ta=True, out_dtype=BF16, name=f"{tag}_dwkv")
    dhq = mm(dq, wq, tb=True, name=f"{tag}_dhq")
    dmn = mm(dkv, wkv, tb=True, name=f"{tag}_dmn")
    dx, dgq = _rms_bwd(x, dhq, dy, gq, f"{tag}_dnormq")
    dgm = _rms_bwd_gain(mem, dmn, gm, f"{tag}_dnormm")
    return dx, dgq, dgm, dwq, dwkv, dwo


def _even_fwd(x, weights, params, freight=None):
    mm = functools.partial(_matmul, freight=freight)
    w_main, w_dt, w_out_a, w_out_b = weights
    gain, lng, lnb, ws, bst, conv_w, conv_b, dtb, alog, dsk, ssd_norm, expand = params
    hm = _rms_fwd(x, gain, "l0_normmix")
    proj = mm(hm, w_main, name="l0_proj")
    dtr = mm(hm, w_dt, name="l0_dt")
    a_out = _gmlp_fwd(proj, lng, lnb, ws, bst, "l0_gmlp")
    xbc = _conv_fwd(proj, conv_w, conv_b, "l0_conv")
    y_ssd, hp_all = _ssd_fwd(xbc, dtr, dtb, alog, dsk, expand, "l0_ssd")
    b_out = _gate_fwd(y_ssd, proj, ssd_norm, "l0_gate")
    y = mm(a_out, w_out_a, res=x, name="l0_out_a")
    y = mm(b_out, w_out_b, res=y, name="l0_out_b")
    return y, (x, hm, proj, dtr, a_out, xbc, y_ssd, hp_all, b_out)


def _even_bwd(dx, saved, weights, params, freight=None):
    mm = functools.partial(_matmul, freight=freight)
    w_main, w_dt, w_out_a, w_out_b = weights
    w_uv, w_z, w_xbc = w_main[:, :4096], w_main[:, 4096:6144], w_main[:, 6144:]
    gain, lng, lnb, ws, bst, conv_w, conv_b, dtb, alog, dsk, ssd_norm, expand = params
    x, hm, proj, dtr, a_out, xbc, y_ssd, hp_all, b_out = saved
    uv = zz = xbc_raw = proj
    da_out = mm(dx, w_out_a, tb=True, name="l0_da")
    db_out = mm(dx, w_out_b, tb=True, name="l0_db")
    dw_out_a = mm(a_out, dx, ta=True, out_dtype=BF16, name="l0_dwout_a")
    dw_out_b = mm(b_out, dx, ta=True, out_dtype=BF16, name="l0_dwout_b")
    dy_ssd, dzz, d_ssd_norm = _gate_bwd(y_ssd, zz, db_out, ssd_norm, "l0_dgate")
    dxbc, ddtr, d_dtb, d_alog, d_dsk = _ssd_bwd(xbc, dtr, dtb, alog, dsk, expand, hp_all, dy_ssd, "l0_dssd")
    dpre, d_conv_w, d_conv_b = _conv_bwd_pre(xbc_raw, dxbc, conv_w, conv_b, "l0_dconv_pre")
    dxbc_raw = _conv_bwd_x(dpre, conv_w, "l0_dconv_x")
    duv, d_lng, d_lnb, d_ws, d_bst = _gmlp_bwd(uv, da_out, lng, lnb, ws, bst, "l0_dgmlp")
    ddtr16 = ddtr.astype(BF16)
    dhm = mm(duv, w_uv, tb=True, name="l0_dh_uv")
    dhm = mm(dzz, w_z, tb=True, res=dhm, name="l0_dh_z")
    dhm = mm(dxbc_raw, w_xbc, tb=True, res=dhm, name="l0_dh_xbc")
    dhm = mm(ddtr16, w_dt, tb=True, res=dhm, name="l0_dh_dt")
    dw_uv = mm(hm, duv, ta=True, out_dtype=BF16, name="l0_dwuv")
    dw_z = mm(hm, dzz, ta=True, out_dtype=BF16, name="l0_dwz")
    dw_xbc = mm(hm, dxbc_raw, ta=True, out_dtype=BF16, name="l0_dwxbc")
    dw_dt = mm(hm, ddtr16, ta=True, out_dtype=BF16, name="l0_dwdt")
    dx, d_gain = _rms_bwd(x, dhm, dx, gain, "l0_dnormmix")
    small = (d_gain, d_lng, d_lnb, d_ws, d_bst, d_conv_w, d_conv_b, d_dtb, d_alog, d_dsk, d_ssd_norm)
    return dx, small, (dw_uv, dw_z, dw_xbc, dw_dt, dw_out_a, dw_out_b)


def _odd_fwd(x, weights, params, freight=None):
    mm = functools.partial(_matmul, freight=freight)
    w_q, w_kv, w_o = weights
    gain, b_q, b_kv, tab, snk = params
    hm = _rms_fwd(x, gain, "l1_normmix")
    q = mm(hm, w_q, bias=b_q, name="l1_q")
    kv = mm(hm, w_kv, bias=b_kv, name="l1_kv")
    o = _swa_fwd(q, kv, tab, snk, "l1_swa")
    y = mm(o, w_o, res=x, name="l1_o")
    return y, (x, hm, q, kv, o)


def _odd_bwd(dx, saved, weights, params, freight=None):
    mm = functools.partial(_matmul, freight=freight)
    w_q, w_kv, w_o = weights
    gain, b_q, b_kv, tab, snk = params
    x, hm, q, kv, o = saved
    do = mm(dx, w_o, tb=True, out_dtype=BF16, name="l1_do")
    dw_o = mm(o, dx, ta=True, out_dtype=BF16, name="l1_dwo")
    dq, dkv_cur, dkv_prev, d_snk = _swa_bwd(q, kv, tab, snk, do, "l1_dswa")
    dkv, db_q, db_kv = _swa_combine(dkv_cur, dkv_prev, dq, "l1_dkv")
    dhm = mm(dq, w_q, tb=True, name="l1_dh_q")
    dhm = mm(dkv, w_kv, tb=True, res=dhm, name="l1_dh_kv")
    dw_q = mm(hm, dq, ta=True, out_dtype=BF16, name="l1_dwq")
    dw_kv = mm(hm, dkv, ta=True, out_dtype=BF16, name="l1_dwkv")
    dx, d_gain = _rms_bwd(x, dhm, dx, gain, "l1_dnormmix")
    return dx, (d_gain, db_q, db_kv, d_snk), (dw_q, dw_kv, dw_o)


def _rope_table(positions):
    seq = positions.size
    inv_freq = ROPE_THETA ** (-jnp.arange(0, 2 * ROT_HALF, 2, dtype=F32) / (2 * ROT_HALF))
    ang = positions.reshape(seq, 1).astype(F32) * inv_freq
    cos, sin, zero = jnp.cos(ang), jnp.sin(ang), jnp.zeros((seq, 128 - 2 * ROT_HALF), F32)
    z8 = jnp.zeros((seq, ROT_HALF), F32)
    return jnp.concatenate([cos, cos, zero + 1.0, -sin, z8, zero, z8, sin, zero], axis=1)


def _pad_heads_cols(w, heads):
    k = w.shape[0]
    return jnp.pad(w.reshape(k, heads, ATT_HEAD_DIM), ((0, 0), (0, 0), (0, HEAD_PAD - ATT_HEAD_DIM))).reshape(k, heads * HEAD_PAD)


def _unpad_heads_cols(w, heads):
    k = w.shape[0]
    return w.reshape(k, heads, HEAD_PAD)[:, :, :ATT_HEAD_DIM].reshape(k, heads * ATT_HEAD_DIM)


def kernel(x, mem, positions, norm_ffn1, w_ffn1_gu, w_ffn1_down, norm_mix, w_in_even, gm_ln_g, gm_ln_b, gm_ws, gm_bs, conv_w, conv_b, dt_bias, a_log, d_skip, ssd_norm, w_out_even, w_qkv, b_qkv, sinks, w_o_odd, norm_xq, norm_mem, w_xq, w_xkv, w_xo, norm_ffn2, w_ffn2_gu, w_ffn2_down, final_norm, loss_target, m_norm_ffn1, m_w_ffn1_gu, m_w_ffn1_down, m_norm_mix, m_w_in_even, m_gm_ln_g, m_gm_ln_b, m_gm_ws, m_gm_bs, m_conv_w, m_conv_b, m_dt_bias, m_a_log, m_d_skip, m_ssd_norm, m_w_out_even, m_w_qkv, m_b_qkv, m_sinks, m_w_o_odd, m_norm_xq, m_norm_mem, m_w_xq, m_w_xkv, m_w_xo, m_norm_ffn2, m_w_ffn2_gu, m_w_ffn2_down, m_final_norm, v_norm_ffn1, v_w_ffn1_gu, v_w_ffn1_down, v_norm_mix, v_w_in_even, v_gm_ln_g, v_gm_ln_b, v_gm_ws, v_gm_bs, v_conv_w, v_conv_b, v_dt_bias, v_a_log, v_d_skip, v_ssd_norm, v_w_out_even, v_w_qkv, v_b_qkv, v_sinks, v_w_o_odd, v_norm_xq, v_norm_mem, v_w_xq, v_w_xkv, v_w_xo, v_norm_ffn2, v_w_ffn2_gu, v_w_ffn2_down, v_final_norm):
    env = dict(locals())
    W = {n: env[n] for n in ORDER}
    M = {n: env["m_" + n] for n in ORDER}
    V = {n: env["v_" + n] for n in ORDER}
    seq = x.shape[1]
    x0 = x.reshape(seq, D_MODEL)
    mem2 = mem.reshape(-1, D_MODEL)
    target = loss_target.reshape(seq, D_MODEL)

    fr = _Freight()
    b16 = lambda a: a.astype(BF16)
    grp_a = [w_out_even[0], w_xq[0], w_xkv[0], w_xo[0]]
    grp_b = [w_qkv[0], w_o_odd[0], w_xq[1], w_xkv[1], w_xo[1]]
    fr.put("l0f1_gate", "dn1_0", b16(w_ffn1_down[0]), GATHER)
    fr.put("l0f1_up", "in", b16(w_in_even[0]), GATHER)
    fr.put("l0f1_down", "grp_a", _pack16(grp_a), GATHER)
    fr.put("l0f1_down", "dn2_0", b16(w_ffn2_down[0]), GATHER)
    fr.put("l0_proj", "gu2_0", b16(w_ffn2_gu[0]), GATHER)
    fr.put("l0f2_gate", "gu1_1", b16(w_ffn1_gu[1]), GATHER)
    fr.put("l0f2_up", "dn1_1", b16(w_ffn1_down[1]), GATHER)
    fr.put("l0f2_down", "grp_b", _pack16(grp_b), GATHER)
    fr.put("l0f2_down", "dn2_1", b16(w_ffn2_down[1]), GATHER)
    fr.put("l1f1_gate", "gu2_1", b16(w_ffn2_gu[1]), GATHER)
    gu_first = _exchange(b16(w_ffn1_gu[0]), GATHER, "ag_l0f1_gu")
    gs = _all_gather(_pack([conv_w, b_qkv]), "ag_small")
    gs = [_unpack(gs[k], [conv_w.shape, b_qkv.shape]) for k in range(N_DEV)]
    conv_w_full = jnp.concatenate([g[0][0] for g in gs], axis=1)
    b_qkv_full = jnp.concatenate([g[1][0] for g in gs], axis=0)

    def gate_up(g):
        return _cols_to_full(g[:N_DEV // 2]), _cols_to_full(g[N_DEV // 2:])

    down = lambda key: fr.landed[key].reshape(D_FF, D_MODEL)

    row = lambda a: a.reshape(1, -1)
    pad128 = lambda a: jnp.pad(a.reshape(1, -1), ((0, 0), (0, 128 - a.size)))
    bst = jnp.pad(gm_bs[0].T, ((0, 0), (0, 128 - GM_GROUPS)))
    ws = gm_ws[0]
    dtb, alog, dsk, snk = pad128(dt_bias), pad128(a_log), pad128(d_skip), pad128(sinks)
    expand = (jnp.arange(128)[:, None] == (jnp.arange(D_MODEL) // SSD_HEAD_DIM)[None, :]).astype(F32)
    tab = _rope_table(positions)

    wg_f1a, wu_f1a = gate_up(gu_first)
    xa, s_f1a, wd_f1a = _ffn_fwd(x0, row(norm_ffn1[0]), wg_f1a, wu_f1a, lambda: down("dn1_0"), "l0f1", fr)
    w_in = _cols_to_full(fr.landed["in"])
    n_main = 2 * GM_GROUPS * GM_GDIM + D_MODEL + XBC_WIDTH
    w_main = w_in[:, :n_main]
    w_dt = jnp.pad(w_in[:, n_main:], ((0, 0), (0, 128 - SSD_HEADS)))
    a_out_even, a_xq, a_xkv, a_xo = _unpack16(fr.landed["grp_a"], [a.shape for a in grp_a])
    w_out = a_out_even.reshape(2 * D_MODEL, D_MODEL)
    even_w = (w_main, w_dt, w_out[:D_MODEL], w_out[D_MODEL:])
    even_p = (row(norm_mix[0]), gm_ln_g, gm_ln_b, ws, bst, conv_w_full, conv_b, dtb, alog, dsk, ssd_norm, expand)
    xb, s_even = _even_fwd(xa, even_w, even_p, fr)
    x0_w = (a_xq.reshape(D_MODEL, -1), a_xkv.reshape(D_MODEL, -1), _cols_to_full(a_xo))
    xc, s_x0 = _xattn_layer_fwd(xb, mem2, row(norm_xq[0]), row(norm_mem[0]), *x0_w, "l0x", fr)
    wg_f2a, wu_f2a = gate_up(fr.landed["gu2_0"])
    xd, s_f2a, wd_f2a = _ffn_fwd(xc, row(norm_ffn2[0]), wg_f2a, wu_f2a, down("dn2_0"), "l0f2", fr)
    wg_f1b, wu_f1b = gate_up(fr.landed["gu1_1"])
    xe, s_f1b, wd_f1b = _ffn_fwd(xd, row(norm_ffn1[1]), wg_f1b, wu_f1b, down("dn1_1"), "l1f1", fr)
    b_qkv_w, b_o, b_xq, b_xkv, b_xo = _unpack16(fr.landed["grp_b"], [a.shape for a in grp_b])
    nq = ATT_HEADS * ATT_HEAD_DIM
    wqkv = _cols_to_full(b_qkv_w)
    w_o = _pad_heads_cols(b_o.reshape(D_MODEL, D_MODEL).T, ATT_HEADS).T
    odd_w = (_pad_heads_cols(wqkv[:, :nq], ATT_HEADS), _pad_heads_cols(wqkv[:, nq:], 2 * ATT_KV_HEADS), w_o)
    odd_p = (row(norm_mix[1]), _pad_heads_cols(b_qkv_full[None, :nq], ATT_HEADS),
             _pad_heads_cols(b_qkv_full[None, nq:], 2 * ATT_KV_HEADS), tab, snk)
    xf, s_odd = _odd_fwd(xe, odd_w, odd_p, fr)
    x1_w = (b_xq.reshape(D_MODEL, -1), b_xkv.reshape(D_MODEL, -1), _cols_to_full(b_xo))
    xg, s_x1 = _xattn_layer_fwd(xf, mem2, row(norm_xq[1]), row(norm_mem[1]), *x1_w, "l1x", fr)
    wg_f2b, wu_f2b = gate_up(fr.landed["gu2_1"])
    xh, s_f2b, wd_f2b = _ffn_fwd(xg, row(norm_ffn2[1]), wg_f2b, wu_f2b, down("dn2_1"), "l1f2", fr)
    loss8, dx, d_final = _loss_fwd_bwd(xh, target, row(final_norm), "loss")
    loss = lax.psum(loss8[0, 0], ("x", "y", "c"))
    assert not fr.load, sorted(fr.load)

    def shipper(tag):
        return lambda carrier, key, parts: fr.put(carrier, f"{tag}_{key}", parts, SCATTER)

    dx, dn_f2b, _, _, _ = _ffn_bwd(dx, s_f2b, row(norm_ffn2[1]), wg_f2b, wu_f2b, wd_f2b, "l1f2", fr, shipper("l1f2"))
    dx, dn_xq1, dn_mem1, dwxq1, dwxkv1, dwxo1 = _xattn_layer_bwd(dx, s_x1, mem2, row(norm_xq[1]), row(norm_mem[1]), *x1_w, "l1x", fr)
    dx, (dn_mix1, db_q, db_kv, d_snk), (dw_q, dw_kv, dw_o) = _odd_bwd(dx, s_odd, odd_w, odd_p, fr)
    rows_parts = lambda g: g.reshape(N_DEV, -1, g.shape[1])
    dwqkv = jnp.concatenate([_unpad_heads_cols(dw_q, ATT_HEADS), _unpad_heads_cols(dw_kv, 2 * ATT_KV_HEADS)], axis=1)
    parts_b = [_full_to_cols(dwqkv), rows_parts(_unpad_heads_cols(dw_o.T, ATT_HEADS).T), rows_parts(dwxq1), rows_parts(dwxkv1),
               _full_to_cols(dwxo1)]
    fr.put("l1f1_dact", "d_grp_b", _pack16(parts_b, lead=1), SCATTER)
    dx, dn_f1b, _, _, _ = _ffn_bwd(dx, s_f1b, row(norm_ffn1[1]), wg_f1b, wu_f1b, wd_f1b, "l1f1", fr, shipper("l1f1"))
    dx, dn_f2a, _, _, _ = _ffn_bwd(dx, s_f2a, row(norm_ffn2[0]), wg_f2a, wu_f2a, wd_f2a, "l0f2", fr, shipper("l0f2"))
    dx, dn_xq0, dn_mem0, dwxq0, dwxkv0, dwxo0 = _xattn_layer_bwd(dx, s_x0, mem2, row(norm_xq[0]), row(norm_mem[0]), *x0_w, "l0x", fr)
    dx, small_even, (dw_uv, dw_z, dw_xbc, dw_dt, dw_out_a, dw_out_b) = _even_bwd(dx, s_even, even_w, even_p, fr)
    dn_mix0, d_lng, d_lnb, d_ws, d_bst, d_conv_w, d_conv_b, d_dtb, d_alog, d_dsk, d_ssd_norm = small_even
    fr.put("l0f1_dact", "d_in", _full_to_cols(jnp.concatenate([dw_uv, dw_z, dw_xbc, dw_dt[:, :SSD_HEADS]], axis=1)), SCATTER)
    parts_a = [rows_parts(jnp.concatenate([dw_out_a, dw_out_b], axis=0)), rows_parts(dwxq0), rows_parts(dwxkv0), _full_to_cols(dwxo0)]
    fr.put("l0f1_dwd", "d_grp_a", _pack16(parts_a, lead=1), SCATTER)
    dx, dn_f1a, _, _, _ = _ffn_bwd(dx, s_f1a, row(norm_ffn1[0]), wg_f1a, wu_f1a, wd_f1a, "l0f1", fr, shipper("l0f1"))
    assert not fr.load, sorted(fr.load)
    grad_x = dx.reshape(x.shape)

    ra_out, ra_xq, ra_xkv, ra_xo = _unpack16(fr.landed["d_grp_a"], [a.shape for a in grp_a])
    rb_qkv, rb_o, rb_xq, rb_xkv, rb_xo = _unpack16(fr.landed["d_grp_b"], [a.shape for a in grp_b])
    got = fr.landed
    received = {
        "w_ffn1_gu": [got["l0f1_gu_a"], got["l0f1_gu_b"], got["l1f1_gu_a"], got["l1f1_gu_b"]],
        "w_ffn2_gu": [got["l0f2_gu_a"], got["l0f2_gu_b"], got["l1f2_gu_a"], got["l1f2_gu_b"]],
        "w_ffn1_down": [got["l0f1_dn"], got["l1f1_dn"]], "w_ffn2_down": [got["l0f2_dn"], got["l1f2_dn"]],
        "w_in_even": [got["d_in"]], "w_out_even": [ra_out], "w_qkv": [rb_qkv], "w_o_odd": [rb_o],
        "w_xq": [ra_xq, rb_xq], "w_xkv": [ra_xkv, rb_xkv], "w_xo": [ra_xo, rb_xo],
    }
    out = {}
    for n in SHARDED:
        shp = W[n].shape
        two = lambda a: a.reshape(-1, shp[-1])
        res = _adamw(received[n], two(W[n]), two(M[n]), two(V[n]), f"adam_{n}")
        out[n] = [r.reshape(shp) for r in res]

    db_qkv = jnp.concatenate([_unpad_heads_cols(db_q, ATT_HEADS), _unpad_heads_cols(db_kv, 2 * ATT_KV_HEADS)], axis=1).reshape(-1)
    cw_parts = _full_to_cols(d_conv_w)
    bq_parts = db_qkv.reshape(N_DEV, -1)
    ss_parts = jnp.stack([_pack([cw_parts[k], bq_parts[k]]) for k in range(N_DEV)])
    recv = _all_to_all(ss_parts, "a2a_small")
    res = _adamw(recv, _pack([conv_w, b_qkv]), _pack([m_conv_w, m_b_qkv]), _pack([v_conv_w, v_b_qkv]), "adam_small_sharded")
    res = [_unpack(r, [conv_w.shape, b_qkv.shape]) for r in res]
    out["conv_w"] = [r[0] for r in res]
    out["b_qkv"] = [r[1] for r in res]

    small_grads = {
        "norm_ffn1": jnp.concatenate([dn_f1a, dn_f1b]), "norm_mix": jnp.concatenate([dn_mix0, dn_mix1]),
        "gm_ln_g": d_lng, "gm_ln_b": d_lnb, "gm_ws": d_ws[None], "gm_bs": d_bst[:, :GM_GROUPS].T[None],
        "conv_b": d_conv_b, "dt_bias": d_dtb[:, :SSD_HEADS], "a_log": d_alog[:, :SSD_HEADS], "d_skip": d_dsk[:, :SSD_HEADS],
        "ssd_norm": d_ssd_norm, "sinks": d_snk[:, :ATT_HEADS], "norm_xq": jnp.concatenate([dn_xq0, dn_xq1]),
        "norm_mem": jnp.concatenate([dn_mem0, dn_mem1]), "norm_ffn2": jnp.concatenate([dn_f2a, dn_f2b]),
        "final_norm": d_final.reshape(-1),
    }
    shapes = [W[n].shape for n in SMALL]
    recv = _all_gather(_pack([small_grads[n] for n in SMALL]), "ag_small_grads")
    res = _adamw(recv, _pack([W[n] for n in SMALL]), _pack([M[n] for n in SMALL]), _pack([V[n] for n in SMALL]), "adam_small")
    res = [_unpack(r, shapes) for r in res]
    for i, n in enumerate(SMALL):
        out[n] = [r[i] for r in res]

    return (loss, grad_x, *[out[n][0] for n in ORDER], *[out[n][1] for n in ORDER], *[out[n][2] for n in ORDER],
            *[out[n][3] for n in ORDER])
```

```python
import functools

import jax
import jax.numpy as jnp
from jax import lax
from jax.experimental import pallas as pl
from jax.experimental.pallas import tpu as pltpu

F32, BF16 = jnp.float32, jnp.bfloat16
HIGHEST = lax.Precision.HIGHEST

N_DEV = 8
D_MODEL = 2048
D_FF = 5632
EPS = 1e-5
CHUNK = 128
GM_GROUPS, GM_GDIM = 4, 512
SSD_HEADS, SSD_HEAD_DIM, SSD_GROUPS, SSD_STATE = 32, 64, 4, 128
XBC_WIDTH = D_MODEL + 2 * SSD_GROUPS * SSD_STATE
Z_COL = 2 * GM_GROUPS * GM_GDIM // D_MODEL
XBC_COL = (2 * GM_GROUPS * GM_GDIM + D_MODEL) // XBC_WIDTH
ATT_HEADS, ATT_KV_HEADS, ATT_HEAD_DIM, ATT_REP = 32, 4, 64, 8
HEAD_PAD = 128
ROT_HALF = 8
ROPE_THETA = 500000.0
ATT_SCALE = ATT_HEAD_DIM ** -0.5
X_HEADS, X_HEAD_DIM = 4, 128
X_SCALE = X_HEAD_DIM ** -0.5
ADAM_LR, ADAM_B1, ADAM_B2, ADAM_EPS, ADAM_WD, ADAM_STEP = 0.001, 0.9, 0.999, 1e-08, 0.01, 10

VMEM_LIMIT_BYTES = 56 * 1024 * 1024
MESH_ID = pl.DeviceIdType.MESH


def _params(*sem):
    return pltpu.CompilerParams(dimension_semantics=sem, vmem_limit_bytes=VMEM_LIMIT_BYTES)


def _pick(n, cands):
    for c in cands:
        if n % c == 0:
            return c
    return n


def _dg(a, b, ca, cb, precision=None):
    return lax.dot_general(a, b, (((ca,), (cb,)), ((), ())), precision=precision, preferred_element_type=F32)


@jax.custom_vjp
def _bdot(a, b):
    return _dg(a.astype(BF16), b.astype(BF16), 1, 0)


def _bdot_fwd(a, b):
    return _bdot(a, b), (a, b)


def _bdot_bwd(r, g):
    a, b = r
    g = g.astype(BF16)
    return _dg(g, b.astype(BF16), 1, 1), _dg(a.astype(BF16), g, 0, 0)


_bdot.defvjp(_bdot_fwd, _bdot_bwd)


@jax.custom_vjp
def _bdot_nt(a, b):
    return _dg(a.astype(BF16), b.astype(BF16), 1, 1)


def _bdot_nt_fwd(a, b):
    return _bdot_nt(a, b), (a, b)


def _bdot_nt_bwd(r, g):
    a, b = r
    g = g.astype(BF16)
    return _dg(g, b.astype(BF16), 1, 0), _dg(g, a.astype(BF16), 0, 0)


_bdot_nt.defvjp(_bdot_nt_fwd, _bdot_nt_bwd)


@jax.custom_vjp
def _bdot_tn(a, b):
    return _dg(a.astype(BF16), b.astype(BF16), 0, 0)


def _bdot_tn_fwd(a, b):
    return _bdot_tn(a, b), (a, b)


def _bdot_tn_bwd(r, g):
    a, b = r
    g = g.astype(BF16)
    return _dg(b.astype(BF16), g, 1, 1), _dg(a.astype(BF16), g, 1, 0)


_bdot_tn.defvjp(_bdot_tn_fwd, _bdot_tn_bwd)


@jax.custom_vjp
def _hdot(a, b):
    return _dg(a, b, 1, 0, HIGHEST)


def _hdot_fwd(a, b):
    return _hdot(a, b), (a, b)


def _hdot_bwd(r, g):
    a, b = r
    return _dg(g, b, 1, 1, HIGHEST), _dg(a, g, 0, 0, HIGHEST)


_hdot.defvjp(_hdot_fwd, _hdot_bwd)


def _sigmoid(x):
    return 1.0 / (1.0 + jnp.exp(-x))


def _silu(x):
    return x * _sigmoid(x)


def _gelu(x):
    return 0.5 * x * (1.0 + lax.erf(x * 0.7071067811865476))


def _softplus(x):
    return jnp.maximum(x, 0.0) + jnp.log1p(jnp.exp(-jnp.abs(x)))


def _rms(x, g):
    return x * lax.rsqrt(jnp.mean(x * x, -1, keepdims=True) + EPS) * g


def _iota(shape, dim):
    return lax.broadcasted_iota(jnp.int32, shape, dim)


MXU_DIM = 256
MATMUL_VMEM_BYTES = 44 * 1024 * 1024
MXU_FLOPS = 9.0e14
HBM_BYTES_PER_S = 3.0e12
STEP_SECONDS = 0.35e-6


def _matmul_tiles(m, n, kk, a_item, b_item, o_item, has_res):
    def divisors(d, cands):
        return [c for c in cands if d % c == 0] or [d]

    def pad(d):
        return -(-d // MXU_DIM) * MXU_DIM

    best = None
    for tm in divisors(m, (1024, 512, 256, 128)):
        for tn in divisors(n, (2816, 2048, 1408, 1024, 512, 256, 128)):
            for tk in divisors(kk, (2816, 2048, 1408, 1024, 512, 256, 128)):
                o_bytes = tm * tn * (o_item + (4 if has_res else 0))
                if 2 * (tm * tk * a_item + tk * tn * b_item + o_bytes) + tm * tn * 4 > MATMUL_VMEM_BYTES:
                    continue
                nk = kk // tk
                a_bytes = tm * tk * a_item / (1 if nk > 1 else n // tn)
                seconds = max(2 * tm * pad(tn) * pad(tk) / MXU_FLOPS, (a_bytes + tk * tn * b_item + o_bytes / nk) / HBM_BYTES_PER_S)
                total = (m // tm) * (n // tn) * nk * (seconds + STEP_SECONDS)
                if best is None or total < best[0]:
                    best = (total, tm, tn, tk)
    return best[1:]


def _matmul(a, b, *, ta=False, tb=False, out_dtype=F32, alpha=1.0, bias=None, res=None, name, freight=None):
    cargo = freight.take(name) if freight is not None else []
    if ta:
        kk, m = a.shape
    else:
        m, kk = a.shape
    if tb:
        n, k2 = b.shape
    else:
        k2, n = b.shape
    assert kk == k2, (a.shape, b.shape, ta, tb)
    tm, tn, tk = _matmul_tiles(m, n, kk, a.dtype.itemsize, b.dtype.itemsize, jnp.dtype(out_dtype).itemsize, res is not None)
    nk = kk // tk
    nc = len(cargo)
    has_bias, has_res = bias is not None, res is not None

    def body(*refs):
        a_ref, b_ref = refs[0], refs[1]
        pos = 2
        bias_ref = res_ref = None
        if has_bias:
            bias_ref = refs[pos]
            pos += 1
        if has_res:
            res_ref = refs[pos]
            pos += 1
        cargo_in = refs[pos:pos + nc]
        pos += nc
        o_ref = refs[pos]
        cargo_out = refs[pos + 1:pos + 1 + nc]
        acc_ref = refs[pos + 1 + nc]
        sems = refs[pos + 2 + nc:]
        i, j, k = pl.program_id(0), pl.program_id(1), pl.program_id(2)

        if nc:
            @pl.when((i == 0) & (j == 0) & (k == 0))
            def _():
                for c, (_, _, kind) in enumerate(cargo):
                    _EX_START[kind](cargo_in[c], cargo_out[c], *sems[3 * c:3 * c + 3])

        @pl.when(k == 0)
        def _():
            acc_ref[...] = jnp.zeros_like(acc_ref)

        acc_ref[...] += _dg(a_ref[...].astype(BF16), b_ref[...].astype(BF16), 0 if ta else 1, 1 if tb else 0)

        @pl.when(k == nk - 1)
        def _():
            r = acc_ref[...]
            if alpha != 1.0:
                r = r * alpha
            if has_bias:
                r = r + bias_ref[...]
            if has_res:
                r = r + res_ref[...]
            o_ref[...] = r.astype(out_dtype)

        if nc:
            @pl.when((i == m // tm - 1) & (j == n // tn - 1) & (k == nk - 1))
            def _():
                for c, (_, _, kind) in enumerate(cargo):
                    _EX_FINISH[kind](cargo_in[c], cargo_out[c], *sems[3 * c:3 * c + 3])

    in_specs = [
        pl.BlockSpec((tk, tm), lambda i, j, k: (k, i)) if ta else pl.BlockSpec((tm, tk), lambda i, j, k: (i, k)),
        pl.BlockSpec((tn, tk), lambda i, j, k: (j, k)) if tb else pl.BlockSpec((tk, tn), lambda i, j, k: (k, j)),
    ]
    args = [a, b]
    if has_bias:
        in_specs.append(pl.BlockSpec((1, tn), lambda i, j, k: (0, j)))
        args.append(bias)
    if has_res:
        in_specs.append(pl.BlockSpec((tm, tn), lambda i, j, k: (i, j)))
        args.append(res)
    anyspec = pl.BlockSpec(memory_space=pl.ANY)
    out_specs = [pl.BlockSpec((tm, tn), lambda i, j, k: (i, j))] + [anyspec] * nc
    out_shape = [jax.ShapeDtypeStruct((m, n), out_dtype)] + [_ex_out_shape(arr, kind) for _, arr, kind in cargo]
    scratch = [pltpu.VMEM((tm, tn), F32)]
    for _ in cargo:
        scratch += _ex_scratch()
    res_all = pl.pallas_call(
        body, name=name, grid=(m // tm, n // tn, nk), in_specs=in_specs + [anyspec] * nc,
        out_specs=out_specs, out_shape=out_shape, scratch_shapes=scratch,
        compiler_params=_params("arbitrary", "arbitrary", "arbitrary") if nc else _params("parallel", "parallel", "arbitrary"),
    )(*args, *[arr for _, arr, _ in cargo])
    for (key, _, _), landed in zip(cargo, res_all[1:]):
        freight.landed[key] = landed
    return res_all[0]


def _row(arr, width=None, cidx=0, shift=0):
    return (arr, arr.shape[-1] if width is None else width, cidx, shift)


def _rows(fn, rows, consts, outs, accs=(), *, tm, ncol=1, n_rows=None, name):
    n = rows[0][0].shape[-2] if n_rows is None else n_rows
    assert n % tm == 0, (n, tm, name)
    nb = n // tm
    n_in = len(rows) + len(consts)
    n_out = len(outs)

    def cfun(cidx):
        return cidx if callable(cidx) else (lambda j, c=cidx: c)

    in_specs = []
    for arr, width, cidx, shift in rows:
        cf = cfun(cidx)
        if callable(shift):
            rf = shift
        elif shift:
            rf = lambda i, s=shift: jnp.clip(i + s, 0, nb - 1)
        else:
            rf = lambda i: i
        if arr.ndim == 3:
            in_specs.append(pl.BlockSpec((arr.shape[0], tm, width), lambda i, j, rf=rf, cf=cf: (0, rf(i), cf(j))))
        else:
            in_specs.append(pl.BlockSpec((tm, width), lambda i, j, rf=rf, cf=cf: (rf(i), cf(j))))
    for c in consts:
        in_specs.append(pl.BlockSpec(c.shape, lambda i, j, nd=c.ndim: (0,) * nd))
    out_shape, out_specs = [], []
    for o in outs:
        width, dt = o[0], o[1]
        total = o[2] if len(o) > 2 else width * ncol
        out_shape.append(jax.ShapeDtypeStruct((n, total), dt))
        out_specs.append(pl.BlockSpec((tm, width), lambda i, j: (i, j)))
    for shp in accs:
        out_shape.append(jax.ShapeDtypeStruct(shp, F32))
        out_specs.append(pl.BlockSpec(shp, lambda i, j, nd=len(shp): (0,) * nd))

    def body(*refs):
        i, j = pl.program_id(0), pl.program_id(1)
        ins = [r[...] for r in refs[:n_in]]
        ro, ao = fn(i, j, *ins)
        for r, val in zip(refs[n_in:n_in + n_out], ro):
            r[...] = val.astype(r.dtype)
        if accs:
            acc_refs = refs[n_in + n_out:]

            @pl.when((i == 0) & (j == 0))
            def _():
                for r in acc_refs:
                    r[...] = jnp.zeros_like(r)

            for r, val in zip(acc_refs, ao):
                r[...] += val

    res = pl.pallas_call(
        body, name=name, grid=(nb, ncol), in_specs=in_specs, out_specs=out_specs, out_shape=out_shape,
        compiler_params=_params("arbitrary", "arbitrary"),
    )(*[r[0] for r in rows], *consts)
    return res


def _rms_fwd(x, g, name):
    def fn(i, j, x, g):
        return (_rms(x, g),), ()
    return _rows(fn, [_row(x)], [g], [(x.shape[1], BF16)], tm=_pick(x.shape[0], (512, 256)), name=name)[0]


def _rms_bwd(x, dh, dres, g, name):
    def fn(i, j, x, dh, dres, g):
        _, vjp = jax.vjp(_rms, x, g)
        dx, dg = vjp(dh)
        return (dx + dres, dx + dres), (dg,)
    dx, dx16, dg = _rows(fn, [_row(x), _row(dh), _row(dres)], [g], [(x.shape[1], F32), (x.shape[1], BF16)], [g.shape],
                         tm=_pick(x.shape[0], (256,)), name=name)
    return (dx, dx16), dg


def _rms_bwd_gain(x, dh, g, name):
    def fn(i, j, x, dh, g):
        _, vjp = jax.vjp(_rms, x, g)
        return (), (vjp(dh)[1],)
    return _rows(fn, [_row(x), _row(dh)], [g], [], [g.shape], tm=_pick(x.shape[0], (256,)), name=name)[0]


def _swiglu_fwd(g, u, name):
    def fn(i, j, g, u):
        return (_silu(g.astype(F32)) * u.astype(F32),), ()
    n, w = g.shape
    return _rows(fn, [_row(g, 512, lambda j: j), _row(u, 512, lambda j: j)], [], [(512, BF16)],
                 tm=_pick(n, (1024, 512, 256)), ncol=w // 512, name=name)[0]


def _swiglu_bwd(g, u, dact, name):
    def fn(i, j, g, u, da):
        g, u, da = g.astype(F32), u.astype(F32), da.astype(F32)
        s = _sigmoid(g)
        return (da * u * (s * (1.0 + g * (1.0 - s))), da * (g * s)), ()
    n, w = g.shape
    return _rows(fn, [_row(g, 512, lambda j: j), _row(u, 512, lambda j: j), _row(dact, 512, lambda j: j)], [],
                 [(512, BF16), (512, BF16)], tm=_pick(n, (1024, 512, 256)), ncol=w // 512, name=name)


def _gmlp_fn(uv, lng, lnb, ws0, ws1, ws2, ws3, bst):
    ws = (ws0, ws1, ws2, ws3)
    q = uv.shape[0]
    u = _gelu(uv[:, :D_MODEL])
    v = _gelu(uv[:, D_MODEL:])
    tril = _iota((q, q), 0) >= _iota((q, q), 1)
    outs = []
    for g in range(GM_GROUPS):
        sl = slice(GM_GDIM * g, GM_GDIM * (g + 1))
        vg = v[:, sl]
        mu = jnp.mean(vg, -1, keepdims=True)
        var = jnp.mean(jnp.square(vg - mu), -1, keepdims=True)
        vn = (vg - mu) * lax.rsqrt(var + EPS) * lng[:, sl] + lnb[:, sl]
        w = jnp.where(tril, ws[g], 0.0)
        bcol = jnp.sum(bst * (_iota((1, 128), 1) == g).astype(F32), axis=1, keepdims=True)
        outs.append(u[:, sl] * (_bdot(w, vn) + bcol))
    return jnp.concatenate(outs, axis=1)


def _gmlp_fwd(uv, lng, lnb, ws, bst, name):
    def fn(i, j, uv, lng, lnb, ws, bst):
        return (_gmlp_fn(uv, lng, lnb, ws[0], ws[1], ws[2], ws[3], bst),), ()
    return _rows(fn, [_row(uv, 2 * D_MODEL, 0)], [lng, lnb, ws, bst], [(D_MODEL, BF16)], tm=CHUNK, name=name)[0]


def _gmlp_bwd(uv, da, lng, lnb, ws, bst, name):
    def fn(i, j, uv, da, lng, lnb, ws, bst):
        _, vjp = jax.vjp(_gmlp_fn, uv, lng, lnb, ws[0], ws[1], ws[2], ws[3], bst)
        duv, dlng, dlnb, d0, d1, d2, d3, dbst = vjp(da)
        return (duv,), (dlng, dlnb, jnp.stack([d0, d1, d2, d3]), dbst)
    return _rows(fn, [_row(uv, 2 * D_MODEL, 0), _row(da)], [lng, lnb, ws, bst], [(2 * D_MODEL, BF16)],
                 [lng.shape, lnb.shape, ws.shape, bst.shape], tm=CHUNK, name=name)


def _conv_taps(scr, x, halo, first_row):
    q = x.shape[0]
    scr[pl.ds(0, 8), :] = halo
    scr[pl.ds(8, q), :] = x
    return [scr[pl.ds(first_row + k, q), :] for k in range(4)]


def _conv_fwd(proj, w, b, name):
    n, c = proj.shape[0], w.shape[1]
    xbc = proj
    q = CHUNK

    def body(x_ref, xp_ref, w_ref, b_ref, o_ref, scr):
        i = pl.program_id(0)
        halo = jnp.where(i > 0, xp_ref[pl.ds(q - 8, 8), :], 0.0)
        taps = _conv_taps(scr, x_ref[...], halo, 5)
        pre = b_ref[...] + sum(taps[k] * w_ref[pl.ds(k, 1), :] for k in range(4))
        o_ref[...] = _silu(pre)

    return pl.pallas_call(
        body, name=name, grid=(n // q,),
        in_specs=[pl.BlockSpec((q, c), lambda i: (i, XBC_COL)), pl.BlockSpec((q, c), lambda i: (jnp.maximum(i - 1, 0), XBC_COL)),
                  pl.BlockSpec(w.shape, lambda i: (0, 0)), pl.BlockSpec(b.shape, lambda i: (0, 0))],
        out_specs=pl.BlockSpec((q, c), lambda i: (i, 0)), out_shape=jax.ShapeDtypeStruct((n, c), F32),
        scratch_shapes=[pltpu.VMEM((q + 8, c), F32)], compiler_params=_params("arbitrary"),
    )(xbc, xbc, w, b)


def _conv_bwd_pre(xbc, dy, w, b, name):
    n, c = xbc.shape[0], w.shape[1]
    q = CHUNK

    def body(x_ref, xp_ref, dy_ref, w_ref, b_ref, dp_ref, dw_ref, db_ref, scr):
        i = pl.program_id(0)
        halo = jnp.where(i > 0, xp_ref[pl.ds(q - 8, 8), :], 0.0)
        taps = _conv_taps(scr, x_ref[...], halo, 5)
        pre = b_ref[...] + sum(taps[k] * w_ref[pl.ds(k, 1), :] for k in range(4))
        s = _sigmoid(pre)
        dp = dy_ref[...] * (s * (1.0 + pre * (1.0 - s)))
        dp_ref[...] = dp

        @pl.when(i == 0)
        def _():
            dw_ref[...] = jnp.zeros_like(dw_ref)
            db_ref[...] = jnp.zeros_like(db_ref)

        db_ref[...] += jnp.sum(dp, axis=0, keepdims=True)
        for k in range(4):
            dw_ref[pl.ds(k, 1), :] += jnp.sum(dp * taps[k], axis=0, keepdims=True)

    return pl.pallas_call(
        body, name=name, grid=(n // q,),
        in_specs=[pl.BlockSpec((q, c), lambda i: (i, XBC_COL)), pl.BlockSpec((q, c), lambda i: (jnp.maximum(i - 1, 0), XBC_COL)),
                  pl.BlockSpec((q, c), lambda i: (i, 0)),
                  pl.BlockSpec(w.shape, lambda i: (0, 0)), pl.BlockSpec(b.shape, lambda i: (0, 0))],
        out_specs=[pl.BlockSpec((q, c), lambda i: (i, 0)), pl.BlockSpec(w.shape, lambda i: (0, 0)),
                   pl.BlockSpec(b.shape, lambda i: (0, 0))],
        out_shape=[jax.ShapeDtypeStruct((n, c), F32), jax.ShapeDtypeStruct(w.shape, F32), jax.ShapeDtypeStruct(b.shape, F32)],
        scratch_shapes=[pltpu.VMEM((q + 8, c), F32)], compiler_params=_params("arbitrary"),
    )(xbc, xbc, dy, w, b)


def _conv_bwd_x(dpre, w, name):
    n, c = dpre.shape
    q = CHUNK
    nb = n // q

    def body(d_ref, dn_ref, w_ref, o_ref, scr):
        i = pl.program_id(0)
        scr[pl.ds(0, q), :] = d_ref[...]
        scr[pl.ds(q, 8), :] = jnp.where(i < nb - 1, dn_ref[pl.ds(0, 8), :], 0.0)
        o_ref[...] = sum(scr[pl.ds(3 - k, q), :] * w_ref[pl.ds(k, 1), :] for k in range(4)).astype(o_ref.dtype)

    return pl.pallas_call(
        body, name=name, grid=(nb,),
        in_specs=[pl.BlockSpec((q, c), lambda i: (i, 0)), pl.BlockSpec((q, c), lambda i: (jnp.minimum(i + 1, nb - 1), 0)),
                  pl.BlockSpec(w.shape, lambda i: (0, 0))],
        out_specs=pl.BlockSpec((q, c), lambda i: (i, 0)), out_shape=jax.ShapeDtypeStruct((n, c), BF16),
        scratch_shapes=[pltpu.VMEM((q + 8, c), F32)], compiler_params=_params("arbitrary"),
    )(dpre, dpre, w)


def _ssd_chunk(xs, bc, dtr, dtb, alog, dsk, expand, hp):
    q = xs.shape[0]
    tril = _iota((q, q), 0) >= _iota((q, q), 1)
    dt = _softplus(dtr + dtb)
    a = dt * (-jnp.exp(alog))
    cs = _hdot(tril.astype(F32), a)
    cs_t = cs.T
    last = (_iota((q, 1), 0) == q - 1).astype(F32)
    gw = SSD_HEADS // SSD_GROUPS * SSD_HEAD_DIM
    lane = _iota((1, gw), 1)
    ys, hs = [], []
    for g in range(SSD_GROUPS):
        sl = slice(gw * g, gw * (g + 1))
        eg = expand[:, sl]
        dt_e, cs_e = _hdot(dt, eg), _hdot(cs, eg)
        cl_e = jnp.sum(cs_e * last, axis=0, keepdims=True)
        d_e = jnp.sum(_hdot(jnp.broadcast_to(dsk, (8, 128)), eg), axis=0, keepdims=True) * 0.125
        xg = xs[:, sl]
        xdt = xg * dt_e
        bg = bc[:, SSD_STATE * g:SSD_STATE * (g + 1)]
        cg = bc[:, SSD_GROUPS * SSD_STATE + SSD_STATE * g:SSD_GROUPS * SSD_STATE + SSD_STATE * (g + 1)]
        cb = _bdot_nt(cg, bg)
        ms, xm = [], []
        for r in range(SSD_HEADS // SSD_GROUPS):
            h = g * (SSD_HEADS // SSD_GROUPS) + r
            col = jnp.sum(cs * (_iota((1, 128), 1) == h).astype(F32), axis=1, keepdims=True)
            row = jnp.sum(cs_t * (_iota((128, 1), 0) == h).astype(F32), axis=0, keepdims=True)
            decay = jnp.where(tril, jnp.exp(jnp.where(tril, col - row, 0.0)), 0.0)
            ms.append(cb * decay)
            xm.append(xdt * ((lane >= SSD_HEAD_DIM * r) & (lane < SSD_HEAD_DIM * (r + 1))).astype(F32))
        y_diag = _bdot(jnp.concatenate(ms, axis=1), jnp.concatenate(xm, axis=0))
        hg = hp[:, sl]
        y_off = _bdot(cg, hg) * jnp.exp(cs_e)
        states = _bdot_tn(bg, xdt * jnp.exp(cl_e - cs_e))
        hs.append(hg * jnp.exp(cl_e) + states)
        ys.append(y_diag + y_off + xg * d_e)
    return jnp.concatenate(ys, axis=1), jnp.concatenate(hs, axis=1)


def _ssd_fwd(xbc, dtr, dtb, alog, dsk, expand, name):
    n = xbc.shape[0]
    q, w = CHUNK, D_MODEL
    nc = n // q

    def body(xs_ref, bc_ref, dtr_ref, dtb_ref, alog_ref, dsk_ref, e_ref, y_ref, hp_ref, h_scr):
        @pl.when(pl.program_id(0) == 0)
        def _():
            h_scr[...] = jnp.zeros_like(h_scr)

        hp = h_scr[...]
        y, hn = _ssd_chunk(xs_ref[...], bc_ref[...], dtr_ref[...], dtb_ref[...], alog_ref[...], dsk_ref[...], e_ref[...], hp)
        y_ref[...] = y
        hp_ref[...] = hp
        h_scr[...] = hn

    small = pl.BlockSpec((1, 128), lambda c: (0, 0))
    return pl.pallas_call(
        body, name=name, grid=(nc,),
        in_specs=[pl.BlockSpec((q, w), lambda c: (c, 0)), pl.BlockSpec((q, 1024), lambda c: (c, 2)),
                  pl.BlockSpec((q, 128), lambda c: (c, 0)), small, small, small, pl.BlockSpec((128, w), lambda c: (0, 0))],
        out_specs=[pl.BlockSpec((q, w), lambda c: (c, 0)), pl.BlockSpec((SSD_STATE, w), lambda c: (c, 0))],
        out_shape=[jax.ShapeDtypeStruct((n, w), F32), jax.ShapeDtypeStruct((nc * SSD_STATE, w), F32)],
        scratch_shapes=[pltpu.VMEM((SSD_STATE, w), F32)], compiler_params=_params("arbitrary"),
    )(xbc, xbc, dtr, dtb, alog, dsk, expand)


def _ssd_bwd(xbc, dtr, dtb, alog, dsk, expand, hp_all, dy, name):
    n = xbc.shape[0]
    q, w = CHUNK, D_MODEL
    nc = n // q

    def body(xs_ref, bc_ref, dtr_ref, dtb_ref, alog_ref, dsk_ref, e_ref, hp_ref, dy_ref,
             dxbc_ref, ddtr_ref, ddtb_ref, dalog_ref, ddsk_ref, dh_scr):
        @pl.when(pl.program_id(0) == 0)
        def _():
            dh_scr[...] = jnp.zeros_like(dh_scr)
            ddtb_ref[...] = jnp.zeros_like(ddtb_ref)
            dalog_ref[...] = jnp.zeros_like(dalog_ref)
            ddsk_ref[...] = jnp.zeros_like(ddsk_ref)

        e = e_ref[...]
        _, vjp = jax.vjp(lambda xs, bc, dtr, dtb, alog, dsk, hp: _ssd_chunk(xs, bc, dtr, dtb, alog, dsk, e, hp),
                         xs_ref[...], bc_ref[...], dtr_ref[...], dtb_ref[...], alog_ref[...], dsk_ref[...], hp_ref[...])
        dxs, dbc, ddtr, ddtb, dalog, ddsk, dhp = vjp((dy_ref[...], dh_scr[...]))
        dxbc_ref[...] = jnp.concatenate([dxs, dbc], axis=1)
        ddtr_ref[...] = ddtr
        ddtb_ref[...] += ddtb
        dalog_ref[...] += dalog
        ddsk_ref[...] += ddsk
        dh_scr[...] = dhp

    rev = lambda c: nc - 1 - c
    small = pl.BlockSpec((1, 128), lambda c: (0, 0))
    return pl.pallas_call(
        body, name=name, grid=(nc,),
        in_specs=[pl.BlockSpec((q, w), lambda c: (rev(c), 0)), pl.BlockSpec((q, 1024), lambda c: (rev(c), 2)),
                  pl.BlockSpec((q, 128), lambda c: (rev(c), 0)), small, small, small,
                  pl.BlockSpec((128, w), lambda c: (0, 0)), pl.BlockSpec((SSD_STATE, w), lambda c: (rev(c), 0)),
                  pl.BlockSpec((q, w), lambda c: (rev(c), 0))],
        out_specs=[pl.BlockSpec((q, w + 1024), lambda c: (rev(c), 0)),
                   pl.BlockSpec((q, 128), lambda c: (rev(c), 0)), small, small, small],
        out_shape=[jax.ShapeDtypeStruct((n, w + 1024), F32), jax.ShapeDtypeStruct((n, 128), F32),
                   jax.ShapeDtypeStruct((1, 128), F32), jax.ShapeDtypeStruct((1, 128), F32), jax.ShapeDtypeStruct((1, 128), F32)],
        scratch_shapes=[pltpu.VMEM((SSD_STATE, w), F32)], compiler_params=_params("arbitrary"),
    )(xbc, xbc, dtr, dtb, alog, dsk, expand, hp_all, dy)


def _gate_fn(y, z, g):
    outs = []
    for k in range(SSD_GROUPS):
        sl = slice(512 * k, 512 * (k + 1))
        yg = y[:, sl] * _silu(z[:, sl])
        outs.append(yg * lax.rsqrt(jnp.mean(yg * yg, -1, keepdims=True) + EPS) * g[:, sl])
    return jnp.concatenate(outs, axis=1)


def _gate_fwd(y, z, g, name):
    def fn(i, j, y, z, g):
        return (_gate_fn(y, z, g),), ()
    return _rows(fn, [_row(y), _row(z, D_MODEL, Z_COL)], [g], [(D_MODEL, BF16)], tm=_pick(y.shape[0], (256, 128)), name=name)[0]


def _gate_bwd(y, z, db, g, name):
    def fn(i, j, y, z, db, g):
        _, vjp = jax.vjp(_gate_fn, y, z, g)
        dy, dz, dg = vjp(db)
        return (dy, dz), (dg,)
    return _rows(fn, [_row(y), _row(z, D_MODEL, Z_COL), _row(db)], [g], [(D_MODEL, F32), (D_MODEL, BF16)], [g.shape],
                 tm=_pick(y.shape[0], (256, 128)), name=name)


def _rope(x, tab, sign=1.0):
    return x * tab[:, 0:128] + sign * (pltpu.roll(x, 128 - ROT_HALF, 1) * tab[:, 128:256] + pltpu.roll(x, ROT_HALF, 1) * tab[:, 256:384])


def _swa_core(qq, kc, vc, sinks, kv_head, first):
    q = kc.shape[0] // 2
    s = _bdot_nt(qq, kc) * ATT_SCALE
    rows = ATT_REP * q
    iq = _iota((rows, 2 * q), 0) & (q - 1)
    js = _iota((rows, 2 * q), 1)
    rel = iq + q - js
    mask = (rel >= 0) & (rel < q) & ((js >= q) | jnp.logical_not(first))
    s = jnp.where(mask, s, -jnp.inf)
    rep = lax.shift_right_logical(_iota((rows, 1), 0), q.bit_length() - 1)
    sink = jnp.zeros((rows, 1), F32)
    for r in range(ATT_REP):
        s_r = jnp.sum(sinks * (_iota((1, 128), 1) == kv_head * ATT_REP + r).astype(F32), axis=1, keepdims=True)
        sink = sink + jnp.where(rep == r, s_r, 0.0)
    m = jnp.maximum(jnp.max(s, -1, keepdims=True), sink)
    p = jnp.exp(s - m)
    pr = p / (jnp.sum(p, -1, keepdims=True) + jnp.exp(sink - m))
    return _bdot(pr, vc)


def _swa_prep(q, kv, kvp, tab, tabp, kv_head):
    w = HEAD_PAD
    kc = jnp.concatenate([_rope(kvp[:, w * kv_head:w * (kv_head + 1)], tabp), _rope(kv[:, w * kv_head:w * (kv_head + 1)], tab)], axis=0)
    o = ATT_KV_HEADS * w
    vc = jnp.concatenate([kvp[:, o + w * kv_head:o + w * (kv_head + 1)], kv[:, o + w * kv_head:o + w * (kv_head + 1)]], axis=0)
    qq = jnp.concatenate([_rope(q[:, w * (kv_head * ATT_REP + r):w * (kv_head * ATT_REP + r + 1)], tab) for r in range(ATT_REP)], axis=0)
    return qq, kc, vc


def _swa_fwd(q, kv, tab, sinks, name):
    def fn(i, j, q, kv, kvp, tab, tabp, sinks):
        first = i == 0
        outs = []
        for h in range(ATT_KV_HEADS):
            qq, kc, vc = _swa_prep(q, kv, kvp, tab, tabp, h)
            o = _swa_core(qq, kc, vc, sinks, h, first)
            outs += [o[CHUNK * r:CHUNK * (r + 1)] for r in range(ATT_REP)]
        return (jnp.concatenate(outs, axis=1),), ()
    return _rows(fn, [_row(q), _row(kv), _row(kv, shift=-1), _row(tab), _row(tab, shift=-1)], [sinks],
                 [(q.shape[1], BF16)], tm=CHUNK, name=name)[0]


def _swa_bwd(q, kv, tab, sinks, do, name):
    w = HEAD_PAD

    def fn(i, j, q, kv, kvp, tab, tabp, do, sinks):
        first = i == 0
        dqs, dkc, dkp, dvc, dvp = [], [], [], [], []
        dsink = jnp.zeros_like(sinks)
        for h in range(ATT_KV_HEADS):
            qq, kc, vc = _swa_prep(q, kv, kvp, tab, tabp, h)
            dout = jnp.concatenate([do[:, w * (h * ATT_REP + r):w * (h * ATT_REP + r + 1)] for r in range(ATT_REP)], axis=0)
            _, vjp = jax.vjp(lambda a, b, c, s: _swa_core(a, b, c, s, h, first), qq, kc, vc, sinks)
            dqq, dk, dv, ds = vjp(dout.astype(F32))
            dsink = dsink + ds
            dqs += [_rope(dqq[CHUNK * r:CHUNK * (r + 1)], tab, -1.0) for r in range(ATT_REP)]
            dkp.append(_rope(dk[:CHUNK], tabp, -1.0))
            dkc.append(_rope(dk[CHUNK:], tab, -1.0))
            dvp.append(dv[:CHUNK])
            dvc.append(dv[CHUNK:])
        return (jnp.concatenate(dqs, axis=1), jnp.concatenate(dkc + dvc, axis=1), jnp.concatenate(dkp + dvp, axis=1)), (dsink,)
    return _rows(fn, [_row(q), _row(kv), _row(kv, shift=-1), _row(tab), _row(tab, shift=-1), _row(do)], [sinks],
                 [(q.shape[1], BF16), (kv.shape[1], F32), (kv.shape[1], F32)], [sinks.shape], tm=CHUNK, name=name)


def _swa_combine(dkv_cur, dkv_prev, dq, name):
    nb = dq.shape[0] // CHUNK

    def fn(i, j, cur, nxt, dq):
        dkv = (cur + jnp.where(i < nb - 1, nxt, 0.0)).astype(BF16)
        return (dkv,), (jnp.sum(dq.astype(F32), axis=0, keepdims=True), jnp.sum(dkv.astype(F32), axis=0, keepdims=True))
    return _rows(fn, [_row(dkv_cur), _row(dkv_prev, shift=1), _row(dq)], [], [(dkv_cur.shape[1], BF16)],
                 [(1, dq.shape[1]), (1, dkv_cur.shape[1])], tm=CHUNK, name=name)


def _xattn_fn(q, k, v):
    outs = []
    for h in range(X_HEADS):
        sl = slice(X_HEAD_DIM * h, X_HEAD_DIM * (h + 1))
        s = _bdot_nt(q[:, sl], k[:, sl]) * X_SCALE
        p = jnp.exp(s - jnp.max(s, -1, keepdims=True))
        outs.append(_bdot(p / jnp.sum(p, -1, keepdims=True), v[:, sl]))
    return jnp.concatenate(outs, axis=1)


def _xattn_fwd(q, k, v, name):
    def fn(i, j, q, k, v):
        return (_xattn_fn(q, k, v),), ()
    return _rows(fn, [_row(q)], [k, v], [(q.shape[1], BF16)], tm=_pick(q.shape[0], (512, 256)), name=name)[0]


def _xattn_bwd(q, k, v, do, name):
    def fn(i, j, q, do, k, v):
        _, vjp = jax.vjp(_xattn_fn, q, k, v)
        dq, dk, dv = vjp(do)
        return (dq,), (dk, dv)
    return _rows(fn, [_row(q), _row(do)], [k, v], [(q.shape[1], BF16)], [k.shape, v.shape],
                 tm=_pick(q.shape[0], (512, 256)), name=name)


def _loss_fn(x, t, g):
    return 0.5 * jnp.sum(jnp.mean(jnp.square(_rms(x, g) - t), axis=-1))


def _loss_fwd_bwd(x, t, g, name):
    def fn(i, j, x, t, g):
        loss, vjp = jax.vjp(_loss_fn, x, t, g)
        dx, _, dg = vjp(jnp.ones((), F32))
        return (dx, dx), (jnp.broadcast_to(loss, (8, 128)), dg)
    dx, dx16, loss, dg = _rows(fn, [_row(x), _row(t)], [g], [(x.shape[1], F32), (x.shape[1], BF16)], [(8, 128), g.shape],
                               tm=_pick(x.shape[0], (256,)), name=name)
    return loss, (dx, dx16), dg


def _adamw(parts, w, m, v, name):
    plist = list(parts) if isinstance(parts, (list, tuple)) else [parts]
    nl = len(plist)
    r, c = w.shape
    tm = _pick(r // nl, (128, 64, 32, 16, 8))
    nbl = r // nl // tm

    def fn(i, j, *blocks):
        parts, (w, m, v) = blocks[0], blocks[nl:]
        for l in range(1, nl):
            parts = jnp.where(i >= l * nbl, blocks[l], parts)
        g = parts[0].astype(F32)
        for k in range(1, N_DEV):
            g = g + parts[k].astype(F32)
        m2 = ADAM_B1 * m + (1.0 - ADAM_B1) * g
        v2 = ADAM_B2 * v + (1.0 - ADAM_B2) * jnp.square(g)
        m_hat = m2 / (1.0 - ADAM_B1 ** ADAM_STEP)
        v_hat = v2 / (1.0 - ADAM_B2 ** ADAM_STEP)
        delta = -ADAM_LR * (m_hat / (jnp.sqrt(v_hat) + ADAM_EPS) + ADAM_WD * w)
        return (g, delta, m2, v2), ()
    prow = [_row(p, shift=(lambda i, l=l: jnp.clip(i - l * nbl, 0, nbl - 1))) for l, p in enumerate(plist)]
    return _rows(fn, prow + [_row(w), _row(m), _row(v)], [], [(c, F32)] * 4, tm=tm, n_rows=r, name=name)


def _peer(k):
    return (k // 4, (k // 2) % 2, k % 2)


GATHER, SCATTER = "gather", "scatter"


def _ex_scratch():
    return [pltpu.SemaphoreType.DMA((N_DEV - 1,)), pltpu.SemaphoreType.DMA((N_DEV - 1,)), pltpu.SemaphoreType.DMA]


def _ex_out_shape(x, kind):
    return jax.ShapeDtypeStruct((N_DEV,) + x.shape[-2:], x.dtype)


def _gather_plan():
    x, y, c = lax.axis_index("x"), lax.axis_index("y"), lax.axis_index("c")
    return (x, y, c), (x, y, 1 - c), [(1 - x, y), (x, 1 - y), (1 - x, 1 - y)], c


def _gather_copy(x_ref, o_ref, send, recv, k, block, to, from_input=False):
    slot = o_ref.at[4 * block[0] + 2 * block[1] + block[2]]
    return pltpu.make_async_remote_copy(src_ref=x_ref if from_input else slot, dst_ref=slot, send_sem=send.at[k],
                                        recv_sem=recv.at[k], device_id=to, device_id_type=MESH_ID)


def _gather_start(x_ref, o_ref, send, recv, local):
    me, sib, chips, c = _gather_plan()
    pltpu.make_async_copy(x_ref, o_ref.at[4 * me[0] + 2 * me[1] + me[2]], local).start()
    _gather_copy(x_ref, o_ref, send, recv, 0, me, sib, True).start()
    for j, chip in enumerate(chips):
        _gather_copy(x_ref, o_ref, send, recv, 1 + j, me, (*chip, c), True).start()


def _gather_finish(x_ref, o_ref, send, recv, local):
    me, sib, chips, c = _gather_plan()
    for j, chip in enumerate(chips):
        _gather_copy(x_ref, o_ref, send, recv, 1 + j, (*chip, c), me).wait_recv()
        _gather_copy(x_ref, o_ref, send, recv, 4 + j, (*chip, c), sib).start()
    _gather_copy(x_ref, o_ref, send, recv, 0, sib, me).wait_recv()
    for j, chip in enumerate(chips):
        _gather_copy(x_ref, o_ref, send, recv, 4 + j, (*chip, 1 - c), me).wait_recv()
    _gather_copy(x_ref, o_ref, send, recv, 0, me, sib, True).wait_send()
    for j, chip in enumerate(chips):
        _gather_copy(x_ref, o_ref, send, recv, 1 + j, me, (*chip, c), True).wait_send()
        _gather_copy(x_ref, o_ref, send, recv, 4 + j, (*chip, c), sib).wait_send()
    pltpu.make_async_copy(x_ref, o_ref.at[4 * me[0] + 2 * me[1] + me[2]], local).wait()


def _scatter_copy(x_ref, o_ref, send, recv, d, frm, to):
    return pltpu.make_async_remote_copy(src_ref=x_ref.at[to], dst_ref=o_ref.at[frm], send_sem=send.at[d - 1],
                                        recv_sem=recv.at[d - 1], device_id=_peer(to), device_id_type=MESH_ID)


def _scatter_start(x_ref, o_ref, send, recv, local):
    me = 4 * lax.axis_index("x") + 2 * lax.axis_index("y") + lax.axis_index("c")
    pltpu.make_async_copy(x_ref.at[me], o_ref.at[me], local).start()
    for d in range(1, N_DEV):
        _scatter_copy(x_ref, o_ref, send, recv, d, me, (me + d) % N_DEV).start()


def _scatter_finish(x_ref, o_ref, send, recv, local):
    me = 4 * lax.axis_index("x") + 2 * lax.axis_index("y") + lax.axis_index("c")
    for d in range(1, N_DEV):
        _scatter_copy(x_ref, o_ref, send, recv, d, (me + N_DEV - d) % N_DEV, me).wait_recv()
    for d in range(1, N_DEV):
        _scatter_copy(x_ref, o_ref, send, recv, d, me, (me + d) % N_DEV).wait_send()
    pltpu.make_async_copy(x_ref.at[me], o_ref.at[me], local).wait()


_EX_START = {GATHER: _gather_start, SCATTER: _scatter_start}
_EX_FINISH = {GATHER: _gather_finish, SCATTER: _scatter_finish}


def _exchange(x, kind, name):
    def body(x_ref, o_ref, send, recv, local):
        _EX_START[kind](x_ref, o_ref, send, recv, local)
        _EX_FINISH[kind](x_ref, o_ref, send, recv, local)

    return pl.pallas_call(
        body, name=name, in_specs=[pl.BlockSpec(memory_space=pl.ANY)], out_specs=pl.BlockSpec(memory_space=pl.ANY),
        out_shape=_ex_out_shape(x, kind), scratch_shapes=_ex_scratch(),
    )(x)


class _Freight:
    def __init__(self):
        self.load = {}
        self.landed = {}

    def put(self, carrier, key, arr, kind):
        self.load.setdefault(carrier, []).append((key, arr, kind))

    def take(self, carrier):
        return self.load.pop(carrier, [])


def _all_gather(x, name):
    return _exchange(x.reshape(-1, x.shape[-1]), GATHER, name).reshape((N_DEV,) + x.shape)


def _all_to_all(x, name):
    return _exchange(x.reshape(N_DEV, -1, x.shape[-1]), SCATTER, name).reshape(x.shape)


def _pack(arrs):
    flat = []
    for a in arrs:
        a = a.reshape(-1).astype(F32)
        flat.append(jnp.pad(a, (0, (-a.shape[0]) % 1024)))
    return jnp.concatenate(flat).reshape(-1, 128)


def _unpack(p, shapes):
    p = p.reshape(-1)
    out, off = [], 0
    for s in shapes:
        n = 1
        for d in s:
            n *= d
        out.append(p[off:off + n].reshape(s))
        off += n + (-n) % 1024
    return out


def _cols_to_full(g):
    return g.transpose(1, 0, 2).reshape(g.shape[1], -1)


def _full_to_cols(w, shards=N_DEV):
    return w.reshape(w.shape[0], shards, -1).transpose(1, 0, 2)


SMALL = ("norm_ffn1", "norm_mix", "gm_ln_g", "gm_ln_b", "gm_ws", "gm_bs", "conv_b", "dt_bias", "a_log", "d_skip", "ssd_norm",
         "sinks", "norm_xq", "norm_mem", "norm_ffn2", "final_norm")
SHARDED = ("w_ffn1_gu", "w_ffn1_down", "w_in_even", "w_out_even", "w_qkv", "w_o_odd", "w_xq", "w_xkv", "w_xo", "w_ffn2_gu",
           "w_ffn2_down")
SMALL_SHARDED = ("conv_w", "b_qkv")
ORDER = ("norm_ffn1", "w_ffn1_gu", "w_ffn1_down", "norm_mix", "w_in_even", "gm_ln_g", "gm_ln_b", "gm_ws", "gm_bs", "conv_w",
         "conv_b", "dt_bias", "a_log", "d_skip", "ssd_norm", "w_out_even", "w_qkv", "b_qkv", "sinks", "w_o_odd", "norm_xq",
         "norm_mem", "w_xq", "w_xkv", "w_xo", "norm_ffn2", "w_ffn2_gu", "w_ffn2_down", "final_norm")


def _ffn_fwd(x, gain, wg, wu, wd, tag, freight=None):
    mm = functools.partial(_matmul, freight=freight)
    h = _rms_fwd(x, gain, f"{tag}_norm")
    g = mm(h, wg, out_dtype=BF16, name=f"{tag}_gate")
    u = mm(h, wu, out_dtype=BF16, name=f"{tag}_up")
    act = _swiglu_fwd(g, u, f"{tag}_act")
    wd = wd() if callable(wd) else wd
    y = mm(act, wd, alpha=0.5, res=x, name=f"{tag}_down")
    return y, (x, h, g, u, act), wd


def _ffn_bwd(dy, saved, gain, wg, wu, wd, tag, freight=None, ship=None):
    mm = functools.partial(_matmul, freight=freight)
    x, h, g, u, act = saved
    dy, dy16 = dy
    dact = mm(dy16, wd, tb=True, alpha=0.5, out_dtype=BF16, name=f"{tag}_dact")
    dwd = mm(act, dy16, ta=True, alpha=0.5, out_dtype=BF16, name=f"{tag}_dwd")
    if ship is not None:
        ship(f"{tag}_dwg", "dn", dwd.reshape(N_DEV, -1, dwd.shape[1]))
    dg, du = _swiglu_bwd(g, u, dact, f"{tag}_dgu")
    dwg = mm(h, dg, ta=True, out_dtype=BF16, name=f"{tag}_dwg")
    dwu = mm(h, du, ta=True, out_dtype=BF16, name=f"{tag}_dwu")
    if ship is not None:
        half = N_DEV // 2
        parts = jnp.concatenate([_full_to_cols(dwg, half), _full_to_cols(dwu, half)], axis=0)
        rows = parts.shape[1] // 2
        ship(f"{tag}_dh_g", "gu_a", parts[:, :rows])
        ship(f"{tag}_dh_u", "gu_b", parts[:, rows:])
    dh = mm(dg, wg, tb=True, name=f"{tag}_dh_g")
    dh = mm(du, wu, tb=True, res=dh, name=f"{tag}_dh_u")
    dx, dgain = _rms_bwd(x, dh, dy, gain, f"{tag}_dnorm")
    return dx, dgain, dwg, dwu, dwd


def _xattn_layer_fwd(x, mem, gq, gm, wq, wkv, wo, tag, freight=None):
    mm = functools.partial(_matmul, freight=freight)
    hq = _rms_fwd(x, gq, f"{tag}_normq")
    mn = _rms_fwd(mem, gm, f"{tag}_normm")
    q = mm(hq, wq, name=f"{tag}_q")
    kv = mm(mn, wkv, name=f"{tag}_kv")
    k, v = kv[:, :X_HEADS * X_HEAD_DIM], kv[:, X_HEADS * X_HEAD_DIM:]
    o = _xattn_fwd(q, k, v, f"{tag}_attn")
    y = mm(o, wo, res=x, name=f"{tag}_o")
    return y, (x, hq, mn, q, k, v, o)


def _xattn_layer_bwd(dy, saved, mem, gq, gm, wq, wkv, wo, tag, freight=None):
    mm = functools.partial(_matmul, freight=freight)
    x, hq, mn, q, k, v, o = saved
    dy, dy16 = dy
    do = mm(dy16, wo, tb=True, name=f"{tag}_do")
    dwo = mm(o, dy16, ta=True, out_dtype=BF16, name=f"{tag}_dwo")
    dq, dk, dv = _xattn_bwd(q, k, v, do, f"{tag}_dattn")
    dkv = jnp.concatenate([dk, dv], axis=1)
    dwq = mm(hq, dq, ta=True, out_dtype=BF16, name=f"{tag}_dwq")
    dwkv = mm(mn, dkv, ta=True, out_dtype=BF16, name=f"{tag}_dwkv")
    dhq = mm(dq, wq, tb=True, name=f"{tag}_dhq")
    dmn = mm(dkv, wkv, tb=True, name=f"{tag}_dmn")
    dx, dgq = _rms_bwd(x, dhq, dy, gq, f"{tag}_dnormq")
    dgm = _rms_bwd_gain(mem, dmn, gm, f"{tag}_dnormm")
    return dx, dgq, dgm, dwq, dwkv, dwo


def _even_fwd(x, weights, params, freight=None):
    mm = functools.partial(_matmul, freight=freight)
    w_main, w_dt, w_out_a, w_out_b = weights
    gain, lng, lnb, ws, bst, conv_w, conv_b, dtb, alog, dsk, ssd_norm, expand = params
    hm = _rms_fwd(x, gain, "l0_normmix")
    proj = mm(hm, w_main, name="l0_proj")
    dtr = mm(hm, w_dt, name="l0_dt")
    a_out = _gmlp_fwd(proj, lng, lnb, ws, bst, "l0_gmlp")
    xbc = _conv_fwd(proj, conv_w, conv_b, "l0_conv")
    y_ssd, hp_all = _ssd_fwd(xbc, dtr, dtb, alog, dsk, expand, "l0_ssd")
    b_out = _gate_fwd(y_ssd, proj, ssd_norm, "l0_gate")
    y = mm(a_out, w_out_a, res=x, name="l0_out_a")
    y = mm(b_out, w_out_b, res=y, name="l0_out_b")
    return y, (x, hm, proj, dtr, a_out, xbc, y_ssd, hp_all, b_out)


def _even_bwd(dx, saved, weights, params, freight=None):
    mm = functools.partial(_matmul, freight=freight)
    w_main, w_dt, w_out_a, w_out_b = weights
    w_uv, w_z, w_xbc = w_main[:, :4096], w_main[:, 4096:6144], w_main[:, 6144:]
    gain, lng, lnb, ws, bst, conv_w, conv_b, dtb, alog, dsk, ssd_norm, expand = params
    x, hm, proj, dtr, a_out, xbc, y_ssd, hp_all, b_out = saved
    uv = zz = xbc_raw = proj
    dx, dx16 = dx
    da_out = mm(dx16, w_out_a, tb=True, name="l0_da")
    db_out = mm(dx16, w_out_b, tb=True, name="l0_db")
    dw_out_a = mm(a_out, dx16, ta=True, out_dtype=BF16, name="l0_dwout_a")
    dw_out_b = mm(b_out, dx16, ta=True, out_dtype=BF16, name="l0_dwout_b")
    dy_ssd, dzz, d_ssd_norm = _gate_bwd(y_ssd, zz, db_out, ssd_norm, "l0_dgate")
    dxbc, ddtr, d_dtb, d_alog, d_dsk = _ssd_bwd(xbc, dtr, dtb, alog, dsk, expand, hp_all, dy_ssd, "l0_dssd")
    dpre, d_conv_w, d_conv_b = _conv_bwd_pre(xbc_raw, dxbc, conv_w, conv_b, "l0_dconv_pre")
    dxbc_raw = _conv_bwd_x(dpre, conv_w, "l0_dconv_x")
    duv, d_lng, d_lnb, d_ws, d_bst = _gmlp_bwd(uv, da_out, lng, lnb, ws, bst, "l0_dgmlp")
    ddtr16 = ddtr.astype(BF16)
    dhm = mm(duv, w_uv, tb=True, name="l0_dh_uv")
    dhm = mm(dzz, w_z, tb=True, res=dhm, name="l0_dh_z")
    dhm = mm(dxbc_raw, w_xbc, tb=True, res=dhm, name="l0_dh_xbc")
    dhm = mm(ddtr16, w_dt, tb=True, res=dhm, name="l0_dh_dt")
    dw_uv = mm(hm, duv, ta=True, out_dtype=BF16, name="l0_dwuv")
    dw_z = mm(hm, dzz, ta=True, out_dtype=BF16, name="l0_dwz")
    dw_xbc = mm(hm, dxbc_raw, ta=True, out_dtype=BF16, name="l0_dwxbc")
    dw_dt = mm(hm, ddtr16, ta=True, out_dtype=BF16, name="l0_dwdt")
    dx, d_gain = _rms_bwd(x, dhm, dx, gain, "l0_dnormmix")
    small = (d_gain, d_lng, d_lnb, d_ws, d_bst, d_conv_w, d_conv_b, d_dtb, d_alog, d_dsk, d_ssd_norm)
    return dx, small, (dw_uv, dw_z, dw_xbc, dw_dt, dw_out_a, dw_out_b)


def _odd_fwd(x, weights, params, freight=None):
    mm = functools.partial(_matmul, freight=freight)
    w_q, w_kv, w_o = weights
    gain, b_q, b_kv, tab, snk = params
    hm = _rms_fwd(x, gain, "l1_normmix")
    q = mm(hm, w_q, bias=b_q, name="l1_q")
    kv = mm(hm, w_kv, bias=b_kv, name="l1_kv")
    o = _swa_fwd(q, kv, tab, snk, "l1_swa")
    y = mm(o, w_o, res=x, name="l1_o")
    return y, (x, hm, q, kv, o)


def _odd_bwd(dx, saved, weights, params, freight=None):
    mm = functools.partial(_matmul, freight=freight)
    w_q, w_kv, w_o = weights
    gain, b_q, b_kv, tab, snk = params
    x, hm, q, kv, o = saved
    dx, dx16 = dx
    do = mm(dx16, w_o, tb=True, out_dtype=BF16, name="l1_do")
    dw_o = mm(o, dx16, ta=True, out_dtype=BF16, name="l1_dwo")
    dq, dkv_cur, dkv_prev, d_snk = _swa_bwd(q, kv, tab, snk, do, "l1_dswa")
    dkv, db_q, db_kv = _swa_combine(dkv_cur, dkv_prev, dq, "l1_dkv")
    dhm = mm(dq, w_q, tb=True, name="l1_dh_q")
    dhm = mm(dkv, w_kv, tb=True, res=dhm, name="l1_dh_kv")
    dw_q = mm(hm, dq, ta=True, out_dtype=BF16, name="l1_dwq")
    dw_kv = mm(hm, dkv, ta=True, out_dtype=BF16, name="l1_dwkv")
    dx, d_gain = _rms_bwd(x, dhm, dx, gain, "l1_dnormmix")
    return dx, (d_gain, db_q, db_kv, d_snk), (dw_q, dw_kv, dw_o)


def _rope_table(positions):
    seq = positions.size
    inv_freq = ROPE_THETA ** (-jnp.arange(0, 2 * ROT_HALF, 2, dtype=F32) / (2 * ROT_HALF))
    ang = positions.reshape(seq, 1).astype(F32) * inv_freq
    cos, sin, zero = jnp.cos(ang), jnp.sin(ang), jnp.zeros((seq, 128 - 2 * ROT_HALF), F32)
    z8 = jnp.zeros((seq, ROT_HALF), F32)
    return jnp.concatenate([cos, cos, zero + 1.0, -sin, z8, zero, z8, sin, zero], axis=1)


def _pad_heads_cols(w, heads):
    k = w.shape[0]
    return jnp.pad(w.reshape(k, heads, ATT_HEAD_DIM), ((0, 0), (0, 0), (0, HEAD_PAD - ATT_HEAD_DIM))).reshape(k, heads * HEAD_PAD)


def _unpad_heads_cols(w, heads):
    k = w.shape[0]
    return w.reshape(k, heads, HEAD_PAD)[:, :, :ATT_HEAD_DIM].reshape(k, heads * ATT_HEAD_DIM)


def kernel(x, mem, positions, norm_ffn1, w_ffn1_gu, w_ffn1_down, norm_mix, w_in_even, gm_ln_g, gm_ln_b, gm_ws, gm_bs, conv_w, conv_b, dt_bias, a_log, d_skip, ssd_norm, w_out_even, w_qkv, b_qkv, sinks, w_o_odd, norm_xq, norm_mem, w_xq, w_xkv, w_xo, norm_ffn2, w_ffn2_gu, w_ffn2_down, final_norm, loss_target, m_norm_ffn1, m_w_ffn1_gu, m_w_ffn1_down, m_norm_mix, m_w_in_even, m_gm_ln_g, m_gm_ln_b, m_gm_ws, m_gm_bs, m_conv_w, m_conv_b, m_dt_bias, m_a_log, m_d_skip, m_ssd_norm, m_w_out_even, m_w_qkv, m_b_qkv, m_sinks, m_w_o_odd, m_norm_xq, m_norm_mem, m_w_xq, m_w_xkv, m_w_xo, m_norm_ffn2, m_w_ffn2_gu, m_w_ffn2_down, m_final_norm, v_norm_ffn1, v_w_ffn1_gu, v_w_ffn1_down, v_norm_mix, v_w_in_even, v_gm_ln_g, v_gm_ln_b, v_gm_ws, v_gm_bs, v_conv_w, v_conv_b, v_dt_bias, v_a_log, v_d_skip, v_ssd_norm, v_w_out_even, v_w_qkv, v_b_qkv, v_sinks, v_w_o_odd, v_norm_xq, v_norm_mem, v_w_xq, v_w_xkv, v_w_xo, v_norm_ffn2, v_w_ffn2_gu, v_w_ffn2_down, v_final_norm):
    env = dict(locals())
    W = {n: env[n] for n in ORDER}
    M = {n: env["m_" + n] for n in ORDER}
    V = {n: env["v_" + n] for n in ORDER}
    seq = x.shape[1]
    x0 = x.reshape(seq, D_MODEL)
    mem2 = mem.reshape(-1, D_MODEL)
    target = loss_target.reshape(seq, D_MODEL)

    fr = _Freight()
    b16 = lambda a: a.astype(BF16)
    grp_a = {"out": w_out_even[0], "xq0": w_xq[0], "xkv0": w_xkv[0], "xo0": w_xo[0]}
    grp_b = {"qkv": w_qkv[0], "o": w_o_odd[0], "xq1": w_xq[1], "xkv1": w_xkv[1], "xo1": w_xo[1]}
    fr.put("l0f1_gate", "dn1_0", b16(w_ffn1_down[0]), GATHER)
    fr.put("l0f1_up", "in", b16(w_in_even[0]), GATHER)
    for key, w in grp_a.items():
        fr.put("l0f1_down", key, b16(w), GATHER)
    fr.put("l0f1_down", "dn2_0", b16(w_ffn2_down[0]), GATHER)
    fr.put("l0_proj", "gu2_0", b16(w_ffn2_gu[0]), GATHER)
    fr.put("l0f2_gate", "gu1_1", b16(w_ffn1_gu[1]), GATHER)
    fr.put("l0f2_up", "dn1_1", b16(w_ffn1_down[1]), GATHER)
    for key, w in grp_b.items():
        fr.put("l0f2_down", key, b16(w), GATHER)
    fr.put("l0f2_down", "dn2_1", b16(w_ffn2_down[1]), GATHER)
    fr.put("l1f1_gate", "gu2_1", b16(w_ffn2_gu[1]), GATHER)
    gu_first = _exchange(b16(w_ffn1_gu[0]), GATHER, "ag_l0f1_gu")
    gs = _all_gather(_pack([conv_w, b_qkv]), "ag_small")
    gs = [_unpack(gs[k], [conv_w.shape, b_qkv.shape]) for k in range(N_DEV)]
    conv_w_full = jnp.concatenate([g[0][0] for g in gs], axis=1)
    b_qkv_full = jnp.concatenate([g[1][0] for g in gs], axis=0)

    def gate_up(g):
        return _cols_to_full(g[:N_DEV // 2]), _cols_to_full(g[N_DEV // 2:])

    down = lambda key: fr.landed[key].reshape(D_FF, D_MODEL)

    row = lambda a: a.reshape(1, -1)
    pad128 = lambda a: jnp.pad(a.reshape(1, -1), ((0, 0), (0, 128 - a.size)))
    bst = jnp.pad(gm_bs[0].T, ((0, 0), (0, 128 - GM_GROUPS)))
    ws = gm_ws[0]
    dtb, alog, dsk, snk = pad128(dt_bias), pad128(a_log), pad128(d_skip), pad128(sinks)
    expand = (jnp.arange(128)[:, None] == (jnp.arange(D_MODEL) // SSD_HEAD_DIM)[None, :]).astype(F32)
    tab = _rope_table(positions)

    wg_f1a, wu_f1a = gate_up(gu_first)
    xa, s_f1a, wd_f1a = _ffn_fwd(x0, row(norm_ffn1[0]), wg_f1a, wu_f1a, lambda: down("dn1_0"), "l0f1", fr)
    w_in = _cols_to_full(fr.landed["in"])
    n_main = 2 * GM_GROUPS * GM_GDIM + D_MODEL + XBC_WIDTH
    w_main = w_in[:, :n_main]
    w_dt = jnp.pad(w_in[:, n_main:], ((0, 0), (0, 128 - SSD_HEADS)))
    a_xq, a_xkv, a_xo = fr.landed["xq0"], fr.landed["xkv0"], fr.landed["xo0"]
    w_out = fr.landed["out"].reshape(2 * D_MODEL, D_MODEL)
    even_w = (w_main, w_dt, w_out[:D_MODEL], w_out[D_MODEL:])
    even_p = (row(norm_mix[0]), gm_ln_g, gm_ln_b, ws, bst, conv_w_full, conv_b, dtb, alog, dsk, ssd_norm, expand)
    xb, s_even = _even_fwd(xa, even_w, even_p, fr)
    x0_w = (a_xq.reshape(D_MODEL, -1), a_xkv.reshape(D_MODEL, -1), _cols_to_full(a_xo))
    xc, s_x0 = _xattn_layer_fwd(xb, mem2, row(norm_xq[0]), row(norm_mem[0]), *x0_w, "l0x", fr)
    wg_f2a, wu_f2a = gate_up(fr.landed["gu2_0"])
    xd, s_f2a, wd_f2a = _ffn_fwd(xc, row(norm_ffn2[0]), wg_f2a, wu_f2a, down("dn2_0"), "l0f2", fr)
    wg_f1b, wu_f1b = gate_up(fr.landed["gu1_1"])
    xe, s_f1b, wd_f1b = _ffn_fwd(xd, row(norm_ffn1[1]), wg_f1b, wu_f1b, down("dn1_1"), "l1f1", fr)
    b_xq, b_xkv, b_xo = fr.landed["xq1"], fr.landed["xkv1"], fr.landed["xo1"]
    nq = ATT_HEADS * ATT_HEAD_DIM
    wqkv = _cols_to_full(fr.landed["qkv"])
    w_o = _pad_heads_cols(fr.landed["o"].reshape(D_MODEL, D_MODEL).T, ATT_HEADS).T
    odd_w = (_pad_heads_cols(wqkv[:, :nq], ATT_HEADS), _pad_heads_cols(wqkv[:, nq:], 2 * ATT_KV_HEADS), w_o)
    odd_p = (row(norm_mix[1]), _pad_heads_cols(b_qkv_full[None, :nq], ATT_HEADS),
             _pad_heads_cols(b_qkv_full[None, nq:], 2 * ATT_KV_HEADS), tab, snk)
    xf, s_odd = _odd_fwd(xe, odd_w, odd_p, fr)
    x1_w = (b_xq.reshape(D_MODEL, -1), b_xkv.reshape(D_MODEL, -1), _cols_to_full(b_xo))
    xg, s_x1 = _xattn_layer_fwd(xf, mem2, row(norm_xq[1]), row(norm_mem[1]), *x1_w, "l1x", fr)
    wg_f2b, wu_f2b = gate_up(fr.landed["gu2_1"])
    xh, s_f2b, wd_f2b = _ffn_fwd(xg, row(norm_ffn2[1]), wg_f2b, wu_f2b, down("dn2_1"), "l1f2", fr)
    loss8, dx, d_final = _loss_fwd_bwd(xh, target, row(final_norm), "loss")
    loss = lax.psum(loss8[0, 0], ("x", "y", "c"))
    assert not fr.load, sorted(fr.load)

    def shipper(tag):
        return lambda carrier, key, parts: fr.put(carrier, f"{tag}_{key}", parts, SCATTER)

    dx, dn_f2b, _, _, _ = _ffn_bwd(dx, s_f2b, row(norm_ffn2[1]), wg_f2b, wu_f2b, wd_f2b, "l1f2", fr, shipper("l1f2"))
    dx, dn_xq1, dn_mem1, dwxq1, dwxkv1, dwxo1 = _xattn_layer_bwd(dx, s_x1, mem2, row(norm_xq[1]), row(norm_mem[1]), *x1_w, "l1x", fr)
    dx, (dn_mix1, db_q, db_kv, d_snk), (dw_q, dw_kv, dw_o) = _odd_bwd(dx, s_odd, odd_w, odd_p, fr)
    rows_parts = lambda g: g.reshape(N_DEV, -1, g.shape[1])
    dwqkv = jnp.concatenate([_unpad_heads_cols(dw_q, ATT_HEADS), _unpad_heads_cols(dw_kv, 2 * ATT_KV_HEADS)], axis=1)
    parts_b = {"d_qkv": _full_to_cols(dwqkv), "d_o": rows_parts(_unpad_heads_cols(dw_o.T, ATT_HEADS).T), "d_xq1": rows_parts(dwxq1),
               "d_xkv1": rows_parts(dwxkv1), "d_xo1": _full_to_cols(dwxo1)}
    for key, parts in parts_b.items():
        fr.put("l1f1_dact", key, parts, SCATTER)
    dx, dn_f1b, _, _, _ = _ffn_bwd(dx, s_f1b, row(norm_ffn1[1]), wg_f1b, wu_f1b, wd_f1b, "l1f1", fr, shipper("l1f1"))
    dx, dn_f2a, _, _, _ = _ffn_bwd(dx, s_f2a, row(norm_ffn2[0]), wg_f2a, wu_f2a, wd_f2a, "l0f2", fr, shipper("l0f2"))
    dx, dn_xq0, dn_mem0, dwxq0, dwxkv0, dwxo0 = _xattn_layer_bwd(dx, s_x0, mem2, row(norm_xq[0]), row(norm_mem[0]), *x0_w, "l0x", fr)
    dx, small_even, (dw_uv, dw_z, dw_xbc, dw_dt, dw_out_a, dw_out_b) = _even_bwd(dx, s_even, even_w, even_p, fr)
    dn_mix0, d_lng, d_lnb, d_ws, d_bst, d_conv_w, d_conv_b, d_dtb, d_alog, d_dsk, d_ssd_norm = small_even
    fr.put("l0f1_dact", "d_in", _full_to_cols(jnp.concatenate([dw_uv, dw_z, dw_xbc, dw_dt[:, :SSD_HEADS]], axis=1)), SCATTER)
    parts_a = {"d_out": rows_parts(jnp.concatenate([dw_out_a, dw_out_b], axis=0)), "d_xq0": rows_parts(dwxq0),
               "d_xkv0": rows_parts(dwxkv0), "d_xo0": _full_to_cols(dwxo0)}
    for key, parts in parts_a.items():
        fr.put("l0f1_dwd", key, parts, SCATTER)
    dx, dn_f1a, _, _, _ = _ffn_bwd(dx, s_f1a, row(norm_ffn1[0]), wg_f1a, wu_f1a, wd_f1a, "l0f1", fr, shipper("l0f1"))
    assert not fr.load, sorted(fr.load)
    grad_x = dx[0].reshape(x.shape)

    got = fr.landed
    received = {
        "w_ffn1_gu": [got["l0f1_gu_a"], got["l0f1_gu_b"], got["l1f1_gu_a"], got["l1f1_gu_b"]],
        "w_ffn2_gu": [got["l0f2_gu_a"], got["l0f2_gu_b"], got["l1f2_gu_a"], got["l1f2_gu_b"]],
        "w_ffn1_down": [got["l0f1_dn"], got["l1f1_dn"]], "w_ffn2_down": [got["l0f2_dn"], got["l1f2_dn"]],
        "w_in_even": [got["d_in"]], "w_out_even": [got["d_out"]], "w_qkv": [got["d_qkv"]], "w_o_odd": [got["d_o"]],
        "w_xq": [got["d_xq0"], got["d_xq1"]], "w_xkv": [got["d_xkv0"], got["d_xkv1"]], "w_xo": [got["d_xo0"], got["d_xo1"]],
    }
    out = {}
    for n in SHARDED:
        shp = W[n].shape
        two = lambda a: a.reshape(-1, shp[-1])
        res = _adamw(received[n], two(W[n]), two(M[n]), two(V[n]), f"adam_{n}")
        out[n] = [r.reshape(shp) for r in res]

    db_qkv = jnp.concatenate([_unpad_heads_cols(db_q, ATT_HEADS), _unpad_heads_cols(db_kv, 2 * ATT_KV_HEADS)], axis=1).reshape(-1)
    cw_parts = _full_to_cols(d_conv_w)
    bq_parts = db_qkv.reshape(N_DEV, -1)
    ss_parts = jnp.stack([_pack([cw_parts[k], bq_parts[k]]) for k in range(N_DEV)])
    recv = _all_to_all(ss_parts, "a2a_small")
    res = _adamw(recv, _pack([conv_w, b_qkv]), _pack([m_conv_w, m_b_qkv]), _pack([v_conv_w, v_b_qkv]), "adam_small_sharded")
    res = [_unpack(r, [conv_w.shape, b_qkv.shape]) for r in res]
    out["conv_w"] = [r[0] for r in res]
    out["b_qkv"] = [r[1] for r in res]

    small_grads = {
        "norm_ffn1": jnp.concatenate([dn_f1a, dn_f1b]), "norm_mix": jnp.concatenate([dn_mix0, dn_mix1]),
        "gm_ln_g": d_lng, "gm_ln_b": d_lnb, "gm_ws": d_ws[None], "gm_bs": d_bst[:, :GM_GROUPS].T[None],
        "conv_b": d_conv_b, "dt_bias": d_dtb[:, :SSD_HEADS], "a_log": d_alog[:, :SSD_HEADS], "d_skip": d_dsk[:, :SSD_HEADS],
        "ssd_norm": d_ssd_norm, "sinks": d_snk[:, :ATT_HEADS], "norm_xq": jnp.concatenate([dn_xq0, dn_xq1]),
        "norm_mem": jnp.concatenate([dn_mem0, dn_mem1]), "norm_ffn2": jnp.concatenate([dn_f2a, dn_f2b]),
        "final_norm": d_final.reshape(-1),
    }
    shapes = [W[n].shape for n in SMALL]
    recv = _all_gather(_pack([small_grads[n] for n in SMALL]), "ag_small_grads")
    res = _adamw(recv, _pack([W[n] for n in SMALL]), _pack([M[n] for n in SMALL]), _pack([V[n] for n in SMALL]), "adam_small")
    res = [_unpack(r, shapes) for r in res]
    for i, n in enumerate(SMALL):
        out[n] = [r[i] for r in res]

    return (loss, grad_x, *[out[n][0] for n in ORDER], *[out[n][1] for n in ORDER], *[out[n][2] for n in ORDER],
            *[out[n][3] for n in ORDER])
```

```python
import functools

import jax
import jax.numpy as jnp
from jax import lax
from jax.experimental import pallas as pl
from jax.experimental.pallas import tpu as pltpu

F32, BF16 = jnp.float32, jnp.bfloat16
HIGHEST = lax.Precision.HIGHEST

N_DEV = 8
D_MODEL = 2048
D_FF = 5632
EPS = 1e-5
CHUNK = 128
GM_GROUPS, GM_GDIM = 4, 512
SSD_HEADS, SSD_HEAD_DIM, SSD_GROUPS, SSD_STATE = 32, 64, 4, 128
XBC_WIDTH = D_MODEL + 2 * SSD_GROUPS * SSD_STATE
Z_COL = 2 * GM_GROUPS * GM_GDIM // D_MODEL
XBC_COL = (2 * GM_GROUPS * GM_GDIM + D_MODEL) // XBC_WIDTH
ATT_HEADS, ATT_KV_HEADS, ATT_HEAD_DIM, ATT_REP = 32, 4, 64, 8
HEAD_PAD = 128
ROT_HALF = 8
ROPE_THETA = 500000.0
ATT_SCALE = ATT_HEAD_DIM ** -0.5
X_HEADS, X_HEAD_DIM = 4, 128
X_SCALE = X_HEAD_DIM ** -0.5
ADAM_LR, ADAM_B1, ADAM_B2, ADAM_EPS, ADAM_WD, ADAM_STEP = 0.001, 0.9, 0.999, 1e-08, 0.01, 10

VMEM_LIMIT_BYTES = 56 * 1024 * 1024
MESH_ID = pl.DeviceIdType.MESH


def _params(*sem):
    return pltpu.CompilerParams(dimension_semantics=sem, vmem_limit_bytes=VMEM_LIMIT_BYTES)


def _pick(n, cands):
    for c in cands:
        if n % c == 0:
            return c
    return n


def _dg(a, b, ca, cb, precision=None):
    return lax.dot_general(a, b, (((ca,), (cb,)), ((), ())), precision=precision, preferred_element_type=F32)


@jax.custom_vjp
def _bdot(a, b):
    return _dg(a.astype(BF16), b.astype(BF16), 1, 0)


def _bdot_fwd(a, b):
    return _bdot(a, b), (a, b)


def _bdot_bwd(r, g):
    a, b = r
    g = g.astype(BF16)
    return _dg(g, b.astype(BF16), 1, 1), _dg(a.astype(BF16), g, 0, 0)


_bdot.defvjp(_bdot_fwd, _bdot_bwd)


@jax.custom_vjp
def _bdot_nt(a, b):
    return _dg(a.astype(BF16), b.astype(BF16), 1, 1)


def _bdot_nt_fwd(a, b):
    return _bdot_nt(a, b), (a, b)


def _bdot_nt_bwd(r, g):
    a, b = r
    g = g.astype(BF16)
    return _dg(g, b.astype(BF16), 1, 0), _dg(g, a.astype(BF16), 0, 0)


_bdot_nt.defvjp(_bdot_nt_fwd, _bdot_nt_bwd)


@jax.custom_vjp
def _bdot_tn(a, b):
    return _dg(a.astype(BF16), b.astype(BF16), 0, 0)


def _bdot_tn_fwd(a, b):
    return _bdot_tn(a, b), (a, b)


def _bdot_tn_bwd(r, g):
    a, b = r
    g = g.astype(BF16)
    return _dg(b.astype(BF16), g, 1, 1), _dg(a.astype(BF16), g, 1, 0)


_bdot_tn.defvjp(_bdot_tn_fwd, _bdot_tn_bwd)


@jax.custom_vjp
def _hdot(a, b):
    return _dg(a, b, 1, 0, HIGHEST)


def _hdot_fwd(a, b):
    return _hdot(a, b), (a, b)


def _hdot_bwd(r, g):
    a, b = r
    return _dg(g, b, 1, 1, HIGHEST), _dg(a, g, 0, 0, HIGHEST)


_hdot.defvjp(_hdot_fwd, _hdot_bwd)


def _sigmoid(x):
    return 1.0 / (1.0 + jnp.exp(-x))


def _silu(x):
    return x * _sigmoid(x)


def _gelu(x):
    return 0.5 * x * (1.0 + lax.erf(x * 0.7071067811865476))


def _softplus(x):
    return jnp.maximum(x, 0.0) + jnp.log1p(jnp.exp(-jnp.abs(x)))


def _rms(x, g):
    return x * lax.rsqrt(jnp.mean(x * x, -1, keepdims=True) + EPS) * g


def _iota(shape, dim):
    return lax.broadcasted_iota(jnp.int32, shape, dim)


MXU_DIM = 256
MATMUL_VMEM_BYTES = 44 * 1024 * 1024
MXU_FLOPS = 9.0e14
HBM_BYTES_PER_S = 3.0e12
STEP_SECONDS = 0.35e-6


def _matmul_tiles(m, n, kk, a_item, b_item, o_item, has_res):
    def divisors(d, cands):
        return [c for c in cands if d % c == 0] or [d]

    def pad(d):
        return -(-d // MXU_DIM) * MXU_DIM

    best = None
    for tm in divisors(m, (1024, 512, 256, 128)):
        for tn in divisors(n, (2816, 2048, 1408, 1024, 512, 256, 128)):
            for tk in divisors(kk, (2816, 2048, 1408, 1024, 512, 256, 128)):
                o_bytes = tm * tn * (o_item + (4 if has_res else 0))
                if 2 * (tm * tk * a_item + tk * tn * b_item + o_bytes) + tm * tn * 4 > MATMUL_VMEM_BYTES:
                    continue
                nk = kk // tk
                a_bytes = tm * tk * a_item / (1 if nk > 1 else n // tn)
                seconds = max(2 * tm * pad(tn) * pad(tk) / MXU_FLOPS, (a_bytes + tk * tn * b_item + o_bytes / nk) / HBM_BYTES_PER_S)
                total = (m // tm) * (n // tn) * nk * (seconds + STEP_SECONDS)
                if best is None or total < best[0]:
                    best = (total, tm, tn, tk)
    return best[1:]


def _matmul(a, b, *, ta=False, tb=False, out_dtype=F32, alpha=1.0, bias=None, res=None, name, freight=None):
    cargo = freight.take(name) if freight is not None else []
    if ta:
        kk, m = a.shape
    else:
        m, kk = a.shape
    if tb:
        n, k2 = b.shape
    else:
        k2, n = b.shape
    assert kk == k2, (a.shape, b.shape, ta, tb)
    tm, tn, tk = _matmul_tiles(m, n, kk, a.dtype.itemsize, b.dtype.itemsize, jnp.dtype(out_dtype).itemsize, res is not None)
    nk = kk // tk
    nc = len(cargo)
    has_bias, has_res = bias is not None, res is not None

    def body(*refs):
        a_ref, b_ref = refs[0], refs[1]
        pos = 2
        bias_ref = res_ref = None
        if has_bias:
            bias_ref = refs[pos]
            pos += 1
        if has_res:
            res_ref = refs[pos]
            pos += 1
        cargo_in = refs[pos:pos + nc]
        pos += nc
        o_ref = refs[pos]
        cargo_out = refs[pos + 1:pos + 1 + nc]
        acc_ref = refs[pos + 1 + nc]
        sems = refs[pos + 2 + nc:]
        i, j, k = pl.program_id(0), pl.program_id(1), pl.program_id(2)

        if nc:
            @pl.when((i == 0) & (j == 0) & (k == 0))
            def _():
                for c, (_, _, kind) in enumerate(cargo):
                    _EX_START[kind](cargo_in[c], cargo_out[c], *sems[3 * c:3 * c + 3])

        @pl.when(k == 0)
        def _():
            acc_ref[...] = jnp.zeros_like(acc_ref)

        acc_ref[...] += _dg(a_ref[...].astype(BF16), b_ref[...].astype(BF16), 0 if ta else 1, 1 if tb else 0)

        @pl.when(k == nk - 1)
        def _():
            r = acc_ref[...]
            if alpha != 1.0:
                r = r * alpha
            if has_bias:
                r = r + bias_ref[...]
            if has_res:
                r = r + res_ref[...]
            o_ref[...] = r.astype(out_dtype)

        if nc:
            @pl.when((i == m // tm - 1) & (j == n // tn - 1) & (k == nk - 1))
            def _():
                for c, (_, _, kind) in enumerate(cargo):
                    _EX_FINISH[kind](cargo_in[c], cargo_out[c], *sems[3 * c:3 * c + 3])

    in_specs = [
        pl.BlockSpec((tk, tm), lambda i, j, k: (k, i)) if ta else pl.BlockSpec((tm, tk), lambda i, j, k: (i, k)),
        pl.BlockSpec((tn, tk), lambda i, j, k: (j, k)) if tb else pl.BlockSpec((tk, tn), lambda i, j, k: (k, j)),
    ]
    args = [a, b]
    if has_bias:
        in_specs.append(pl.BlockSpec((1, tn), lambda i, j, k: (0, j)))
        args.append(bias)
    if has_res:
        in_specs.append(pl.BlockSpec((tm, tn), lambda i, j, k: (i, j)))
        args.append(res)
    anyspec = pl.BlockSpec(memory_space=pl.ANY)
    out_specs = [pl.BlockSpec((tm, tn), lambda i, j, k: (i, j))] + [anyspec] * nc
    out_shape = [jax.ShapeDtypeStruct((m, n), out_dtype)] + [_ex_out_shape(arr, kind) for _, arr, kind in cargo]
    scratch = [pltpu.VMEM((tm, tn), F32)]
    for _ in cargo:
        scratch += _ex_scratch()
    res_all = pl.pallas_call(
        body, name=name, grid=(m // tm, n // tn, nk), in_specs=in_specs + [anyspec] * nc,
        out_specs=out_specs, out_shape=out_shape, scratch_shapes=scratch,
        compiler_params=_params("arbitrary", "arbitrary", "arbitrary") if nc else _params("parallel", "parallel", "arbitrary"),
    )(*args, *[arr for _, arr, _ in cargo])
    for (key, _, _), landed in zip(cargo, res_all[1:]):
        freight.landed[key] = landed
    return res_all[0]


def _row(arr, width=None, cidx=0, shift=0):
    return (arr, arr.shape[-1] if width is None else width, cidx, shift)


def _rows(fn, rows, consts, outs, accs=(), *, tm, ncol=1, n_rows=None, name):
    n = rows[0][0].shape[-2] if n_rows is None else n_rows
    assert n % tm == 0, (n, tm, name)
    nb = n // tm
    n_in = len(rows) + len(consts)
    n_out = len(outs)

    def cfun(cidx):
        return cidx if callable(cidx) else (lambda j, c=cidx: c)

    in_specs = []
    for arr, width, cidx, shift in rows:
        cf = cfun(cidx)
        if callable(shift):
            rf = shift
        elif shift:
            rf = lambda i, s=shift: jnp.clip(i + s, 0, nb - 1)
        else:
            rf = lambda i: i
        if arr.ndim == 3:
            in_specs.append(pl.BlockSpec((arr.shape[0], tm, width), lambda i, j, rf=rf, cf=cf: (0, rf(i), cf(j))))
        else:
            in_specs.append(pl.BlockSpec((tm, width), lambda i, j, rf=rf, cf=cf: (rf(i), cf(j))))
    for c in consts:
        in_specs.append(pl.BlockSpec(c.shape, lambda i, j, nd=c.ndim: (0,) * nd))
    out_shape, out_specs = [], []
    for o in outs:
        width, dt = o[0], o[1]
        total = o[2] if len(o) > 2 else width * ncol
        out_shape.append(jax.ShapeDtypeStruct((n, total), dt))
        out_specs.append(pl.BlockSpec((tm, width), lambda i, j: (i, j)))
    for shp in accs:
        out_shape.append(jax.ShapeDtypeStruct(shp, F32))
        out_specs.append(pl.BlockSpec(shp, lambda i, j, nd=len(shp): (0,) * nd))

    def body(*refs):
        i, j = pl.program_id(0), pl.program_id(1)
        ins = [r[...] for r in refs[:n_in]]
        ro, ao = fn(i, j, *ins)
        for r, val in zip(refs[n_in:n_in + n_out], ro):
            r[...] = val.astype(r.dtype)
        if accs:
            acc_refs = refs[n_in + n_out:]

            @pl.when((i == 0) & (j == 0))
            def _():
                for r in acc_refs:
                    r[...] = jnp.zeros_like(r)

            for r, val in zip(acc_refs, ao):
                r[...] += val

    res = pl.pallas_call(
        body, name=name, grid=(nb, ncol), in_specs=in_specs, out_specs=out_specs, out_shape=out_shape,
        compiler_params=_params("arbitrary", "arbitrary"),
    )(*[r[0] for r in rows], *consts)
    return res


def _rms_fwd(x, g, name):
    def fn(i, j, x, g):
        return (_rms(x, g),), ()
    return _rows(fn, [_row(x)], [g], [(x.shape[1], BF16)], tm=_pick(x.shape[0], (512, 256)), name=name)[0]


def _rms_bwd(x, dh, dres, g, name):
    def fn(i, j, x, dh, dres, g):
        _, vjp = jax.vjp(_rms, x, g)
        dx, dg = vjp(dh)
        return (dx + dres, dx + dres), (dg,)
    dx, dx16, dg = _rows(fn, [_row(x), _row(dh), _row(dres)], [g], [(x.shape[1], F32), (x.shape[1], BF16)], [g.shape],
                         tm=_pick(x.shape[0], (256,)), name=name)
    return (dx, dx16), dg


def _rms_bwd_gain(x, dh, g, name):
    def fn(i, j, x, dh, g):
        _, vjp = jax.vjp(_rms, x, g)
        return (), (vjp(dh)[1],)
    return _rows(fn, [_row(x), _row(dh)], [g], [], [g.shape], tm=_pick(x.shape[0], (256,)), name=name)[0]


def _swiglu_fwd(g, u, name):
    def fn(i, j, g, u):
        return (_silu(g.astype(F32)) * u.astype(F32),), ()
    n, w = g.shape
    return _rows(fn, [_row(g, 512, lambda j: j), _row(u, 512, lambda j: j)], [], [(512, BF16)],
                 tm=_pick(n, (1024, 512, 256)), ncol=w // 512, name=name)[0]


def _swiglu_bwd(g, u, dact, name):
    def fn(i, j, g, u, da):
        g, u, da = g.astype(F32), u.astype(F32), da.astype(F32)
        s = _sigmoid(g)
        return (da * u * (s * (1.0 + g * (1.0 - s))), da * (g * s)), ()
    n, w = g.shape
    return _rows(fn, [_row(g, 512, lambda j: j), _row(u, 512, lambda j: j), _row(dact, 512, lambda j: j)], [],
                 [(512, BF16), (512, BF16)], tm=_pick(n, (1024, 512, 256)), ncol=w // 512, name=name)


def _gmlp_fn(uv, lng, lnb, ws0, ws1, ws2, ws3, bst):
    ws = (ws0, ws1, ws2, ws3)
    q = uv.shape[0]
    u = _gelu(uv[:, :D_MODEL])
    v = _gelu(uv[:, D_MODEL:])
    tril = _iota((q, q), 0) >= _iota((q, q), 1)
    outs = []
    for g in range(GM_GROUPS):
        sl = slice(GM_GDIM * g, GM_GDIM * (g + 1))
        vg = v[:, sl]
        mu = jnp.mean(vg, -1, keepdims=True)
        var = jnp.mean(jnp.square(vg - mu), -1, keepdims=True)
        vn = (vg - mu) * lax.rsqrt(var + EPS) * lng[:, sl] + lnb[:, sl]
        w = jnp.where(tril, ws[g], 0.0)
        bcol = jnp.sum(bst * (_iota((1, 128), 1) == g).astype(F32), axis=1, keepdims=True)
        outs.append(u[:, sl] * (_bdot(w, vn) + bcol))
    return jnp.concatenate(outs, axis=1)


def _gmlp_fwd(uv, lng, lnb, ws, bst, name):
    def fn(i, j, uv, lng, lnb, ws, bst):
        return (_gmlp_fn(uv, lng, lnb, ws[0], ws[1], ws[2], ws[3], bst),), ()
    return _rows(fn, [_row(uv, 2 * D_MODEL, 0)], [lng, lnb, ws, bst], [(D_MODEL, BF16)], tm=CHUNK, name=name)[0]


def _gmlp_bwd(uv, da, lng, lnb, ws, bst, name):
    def fn(i, j, uv, da, lng, lnb, ws, bst):
        _, vjp = jax.vjp(_gmlp_fn, uv, lng, lnb, ws[0], ws[1], ws[2], ws[3], bst)
        duv, dlng, dlnb, d0, d1, d2, d3, dbst = vjp(da)
        return (duv,), (dlng, dlnb, jnp.stack([d0, d1, d2, d3]), dbst)
    return _rows(fn, [_row(uv, 2 * D_MODEL, 0), _row(da)], [lng, lnb, ws, bst], [(2 * D_MODEL, BF16)],
                 [lng.shape, lnb.shape, ws.shape, bst.shape], tm=CHUNK, name=name)


def _conv_taps(scr, x, halo, first_row):
    q = x.shape[0]
    scr[pl.ds(0, 8), :] = halo
    scr[pl.ds(8, q), :] = x
    return [scr[pl.ds(first_row + k, q), :] for k in range(4)]


def _halo_before(q, c, col):
    return pl.BlockSpec((8, c), lambda i: (jnp.maximum(i * (q // 8) - 1, 0), col))


def _conv_fwd(proj, w, b, name):
    n, c = proj.shape[0], w.shape[1]
    xbc = proj
    q = CHUNK

    def body(x_ref, xp_ref, w_ref, b_ref, o_ref, scr):
        i = pl.program_id(0)
        halo = jnp.where(i > 0, xp_ref[...], 0.0)
        taps = _conv_taps(scr, x_ref[...], halo, 5)
        pre = b_ref[...] + sum(taps[k] * w_ref[pl.ds(k, 1), :] for k in range(4))
        o_ref[...] = _silu(pre)

    return pl.pallas_call(
        body, name=name, grid=(n // q,),
        in_specs=[pl.BlockSpec((q, c), lambda i: (i, XBC_COL)), _halo_before(q, c, XBC_COL),
                  pl.BlockSpec(w.shape, lambda i: (0, 0)), pl.BlockSpec(b.shape, lambda i: (0, 0))],
        out_specs=pl.BlockSpec((q, c), lambda i: (i, 0)), out_shape=jax.ShapeDtypeStruct((n, c), F32),
        scratch_shapes=[pltpu.VMEM((q + 8, c), F32)], compiler_params=_params("arbitrary"),
    )(xbc, xbc, w, b)


def _conv_bwd_pre(xbc, dy, w, b, name):
    n, c = xbc.shape[0], w.shape[1]
    q = CHUNK

    def body(x_ref, xp_ref, dy_ref, w_ref, b_ref, dp_ref, dw_ref, db_ref, scr):
        i = pl.program_id(0)
        halo = jnp.where(i > 0, xp_ref[...], 0.0)
        taps = _conv_taps(scr, x_ref[...], halo, 5)
        pre = b_ref[...] + sum(taps[k] * w_ref[pl.ds(k, 1), :] for k in range(4))
        s = _sigmoid(pre)
        dp = dy_ref[...] * (s * (1.0 + pre * (1.0 - s)))
        dp_ref[...] = dp

        @pl.when(i == 0)
        def _():
            dw_ref[...] = jnp.zeros_like(dw_ref)
            db_ref[...] = jnp.zeros_like(db_ref)

        db_ref[...] += jnp.sum(dp, axis=0, keepdims=True)
        for k in range(4):
            dw_ref[pl.ds(k, 1), :] += jnp.sum(dp * taps[k], axis=0, keepdims=True)

    return pl.pallas_call(
        body, name=name, grid=(n // q,),
        in_specs=[pl.BlockSpec((q, c), lambda i: (i, XBC_COL)), _halo_before(q, c, XBC_COL),
                  pl.BlockSpec((q, c), lambda i: (i, 0)),
                  pl.BlockSpec(w.shape, lambda i: (0, 0)), pl.BlockSpec(b.shape, lambda i: (0, 0))],
        out_specs=[pl.BlockSpec((q, c), lambda i: (i, 0)), pl.BlockSpec(w.shape, lambda i: (0, 0)),
                   pl.BlockSpec(b.shape, lambda i: (0, 0))],
        out_shape=[jax.ShapeDtypeStruct((n, c), F32), jax.ShapeDtypeStruct(w.shape, F32), jax.ShapeDtypeStruct(b.shape, F32)],
        scratch_shapes=[pltpu.VMEM((q + 8, c), F32)], compiler_params=_params("arbitrary"),
    )(xbc, xbc, dy, w, b)


def _conv_bwd_x(dpre, w, name):
    n, c = dpre.shape
    q = CHUNK
    nb = n // q

    def body(d_ref, dn_ref, w_ref, o_ref, scr):
        i = pl.program_id(0)
        scr[pl.ds(0, q), :] = d_ref[...]
        scr[pl.ds(q, 8), :] = jnp.where(i < nb - 1, dn_ref[...], 0.0)
        o_ref[...] = sum(scr[pl.ds(3 - k, q), :] * w_ref[pl.ds(k, 1), :] for k in range(4)).astype(o_ref.dtype)

    return pl.pallas_call(
        body, name=name, grid=(nb,),
        in_specs=[pl.BlockSpec((q, c), lambda i: (i, 0)),
                  pl.BlockSpec((8, c), lambda i: (jnp.minimum((i + 1) * (q // 8), nb * (q // 8) - 1), 0)),
                  pl.BlockSpec(w.shape, lambda i: (0, 0))],
        out_specs=pl.BlockSpec((q, c), lambda i: (i, 0)), out_shape=jax.ShapeDtypeStruct((n, c), BF16),
        scratch_shapes=[pltpu.VMEM((q + 8, c), F32)], compiler_params=_params("arbitrary"),
    )(dpre, dpre, w)


def _ssd_chunk(xs, bc, dtr, dtb, alog, dsk, expand, hp):
    q = xs.shape[0]
    tril = _iota((q, q), 0) >= _iota((q, q), 1)
    dt = _softplus(dtr + dtb)
    a = dt * (-jnp.exp(alog))
    cs = _hdot(tril.astype(F32), a)
    cs_t = cs.T
    last = (_iota((q, 1), 0) == q - 1).astype(F32)
    gw = SSD_HEADS // SSD_GROUPS * SSD_HEAD_DIM
    lane = _iota((1, gw), 1)
    ys, hs = [], []
    for g in range(SSD_GROUPS):
        sl = slice(gw * g, gw * (g + 1))
        eg = expand[:, sl]
        dt_e, cs_e = _hdot(dt, eg), _hdot(cs, eg)
        cl_e = jnp.sum(cs_e * last, axis=0, keepdims=True)
        d_e = jnp.sum(_hdot(jnp.broadcast_to(dsk, (8, 128)), eg), axis=0, keepdims=True) * 0.125
        xg = xs[:, sl]
        xdt = xg * dt_e
        bg = bc[:, SSD_STATE * g:SSD_STATE * (g + 1)]
        cg = bc[:, SSD_GROUPS * SSD_STATE + SSD_STATE * g:SSD_GROUPS * SSD_STATE + SSD_STATE * (g + 1)]
        cb = _bdot_nt(cg, bg)
        ms, xm = [], []
        for r in range(SSD_HEADS // SSD_GROUPS):
            h = g * (SSD_HEADS // SSD_GROUPS) + r
            col = jnp.sum(cs * (_iota((1, 128), 1) == h).astype(F32), axis=1, keepdims=True)
            row = jnp.sum(cs_t * (_iota((128, 1), 0) == h).astype(F32), axis=0, keepdims=True)
            decay = jnp.where(tril, jnp.exp(jnp.where(tril, col - row, 0.0)), 0.0)
            ms.append(cb * decay)
            xm.append(xdt * ((lane >= SSD_HEAD_DIM * r) & (lane < SSD_HEAD_DIM * (r + 1))).astype(F32))
        y_diag = _bdot(jnp.concatenate(ms, axis=1), jnp.concatenate(xm, axis=0))
        hg = hp[:, sl]
        y_off = _bdot(cg, hg) * jnp.exp(cs_e)
        states = _bdot_tn(bg, xdt * jnp.exp(cl_e - cs_e))
        hs.append(hg * jnp.exp(cl_e) + states)
        ys.append(y_diag + y_off + xg * d_e)
    return jnp.concatenate(ys, axis=1), jnp.concatenate(hs, axis=1)


def _ssd_fwd(xbc, dtr, dtb, alog, dsk, expand, name):
    n = xbc.shape[0]
    q, w = CHUNK, D_MODEL
    nc = n // q

    def body(xs_ref, bc_ref, dtr_ref, dtb_ref, alog_ref, dsk_ref, e_ref, y_ref, hp_ref, h_scr):
        @pl.when(pl.program_id(0) == 0)
        def _():
            h_scr[...] = jnp.zeros_like(h_scr)

        hp = h_scr[...]
        y, hn = _ssd_chunk(xs_ref[...], bc_ref[...], dtr_ref[...], dtb_ref[...], alog_ref[...], dsk_ref[...], e_ref[...], hp)
        y_ref[...] = y
        hp_ref[...] = hp
        h_scr[...] = hn

    small = pl.BlockSpec((1, 128), lambda c: (0, 0))
    return pl.pallas_call(
        body, name=name, grid=(nc,),
        in_specs=[pl.BlockSpec((q, w), lambda c: (c, 0)), pl.BlockSpec((q, 1024), lambda c: (c, 2)),
                  pl.BlockSpec((q, 128), lambda c: (c, 0)), small, small, small, pl.BlockSpec((128, w), lambda c: (0, 0))],
        out_specs=[pl.BlockSpec((q, w), lambda c: (c, 0)), pl.BlockSpec((SSD_STATE, w), lambda c: (c, 0))],
        out_shape=[jax.ShapeDtypeStruct((n, w), F32), jax.ShapeDtypeStruct((nc * SSD_STATE, w), F32)],
        scratch_shapes=[pltpu.VMEM((SSD_STATE, w), F32)], compiler_params=_params("arbitrary"),
    )(xbc, xbc, dtr, dtb, alog, dsk, expand)


def _ssd_bwd(xbc, dtr, dtb, alog, dsk, expand, hp_all, dy, name):
    n = xbc.shape[0]
    q, w = CHUNK, D_MODEL
    nc = n // q

    def body(xs_ref, bc_ref, dtr_ref, dtb_ref, alog_ref, dsk_ref, e_ref, hp_ref, dy_ref,
             dxbc_ref, ddtr_ref, ddtb_ref, dalog_ref, ddsk_ref, dh_scr):
        @pl.when(pl.program_id(0) == 0)
        def _():
            dh_scr[...] = jnp.zeros_like(dh_scr)
            ddtb_ref[...] = jnp.zeros_like(ddtb_ref)
            dalog_ref[...] = jnp.zeros_like(dalog_ref)
            ddsk_ref[...] = jnp.zeros_like(ddsk_ref)

        e = e_ref[...]
        _, vjp = jax.vjp(lambda xs, bc, dtr, dtb, alog, dsk, hp: _ssd_chunk(xs, bc, dtr, dtb, alog, dsk, e, hp),
                         xs_ref[...], bc_ref[...], dtr_ref[...], dtb_ref[...], alog_ref[...], dsk_ref[...], hp_ref[...])
        dxs, dbc, ddtr, ddtb, dalog, ddsk, dhp = vjp((dy_ref[...], dh_scr[...]))
        dxbc_ref[...] = jnp.concatenate([dxs, dbc], axis=1)
        ddtr_ref[...] = ddtr
        ddtb_ref[...] += ddtb
        dalog_ref[...] += dalog
        ddsk_ref[...] += ddsk
        dh_scr[...] = dhp

    rev = lambda c: nc - 1 - c
    small = pl.BlockSpec((1, 128), lambda c: (0, 0))
    return pl.pallas_call(
        body, name=name, grid=(nc,),
        in_specs=[pl.BlockSpec((q, w), lambda c: (rev(c), 0)), pl.BlockSpec((q, 1024), lambda c: (rev(c), 2)),
                  pl.BlockSpec((q, 128), lambda c: (rev(c), 0)), small, small, small,
                  pl.BlockSpec((128, w), lambda c: (0, 0)), pl.BlockSpec((SSD_STATE, w), lambda c: (rev(c), 0)),
                  pl.BlockSpec((q, w), lambda c: (rev(c), 0))],
        out_specs=[pl.BlockSpec((q, w + 1024), lambda c: (rev(c), 0)),
                   pl.BlockSpec((q, 128), lambda c: (rev(c), 0)), small, small, small],
        out_shape=[jax.ShapeDtypeStruct((n, w + 1024), F32), jax.ShapeDtypeStruct((n, 128), F32),
                   jax.ShapeDtypeStruct((1, 128), F32), jax.ShapeDtypeStruct((1, 128), F32), jax.ShapeDtypeStruct((1, 128), F32)],
        scratch_shapes=[pltpu.VMEM((SSD_STATE, w), F32)], compiler_params=_params("arbitrary"),
    )(xbc, xbc, dtr, dtb, alog, dsk, expand, hp_all, dy)


def _gate_fn(y, z, g):
    outs = []
    for k in range(SSD_GROUPS):
        sl = slice(512 * k, 512 * (k + 1))
        yg = y[:, sl] * _silu(z[:, sl])
        outs.append(yg * lax.rsqrt(jnp.mean(yg * yg, -1, keepdims=True) + EPS) * g[:, sl])
    return jnp.concatenate(outs, axis=1)


def _gate_fwd(y, z, g, name):
    def fn(i, j, y, z, g):
        return (_gate_fn(y, z, g),), ()
    return _rows(fn, [_row(y), _row(z, D_MODEL, Z_COL)], [g], [(D_MODEL, BF16)], tm=_pick(y.shape[0], (256, 128)), name=name)[0]


def _gate_bwd(y, z, db, g, name):
    def fn(i, j, y, z, db, g):
        _, vjp = jax.vjp(_gate_fn, y, z, g)
        dy, dz, dg = vjp(db)
        return (dy, dz), (dg,)
    return _rows(fn, [_row(y), _row(z, D_MODEL, Z_COL), _row(db)], [g], [(D_MODEL, F32), (D_MODEL, BF16)], [g.shape],
                 tm=_pick(y.shape[0], (256, 128)), name=name)


def _rope(x, tab, sign=1.0):
    return x * tab[:, 0:128] + sign * (pltpu.roll(x, 128 - ROT_HALF, 1) * tab[:, 128:256] + pltpu.roll(x, ROT_HALF, 1) * tab[:, 256:384])


def _sink_softmax_parts(s, sink):
    m = jnp.maximum(jnp.max(s, -1, keepdims=True), sink)
    p = jnp.exp(s - m)
    e_sink = jnp.exp(sink - m)
    inv = 1.0 / (jnp.sum(p, -1, keepdims=True) + e_sink)
    return p * inv, e_sink * inv


@jax.custom_vjp
def _sink_softmax(s, sink):
    return _sink_softmax_parts(s, sink)[0]


def _sink_softmax_fwd(s, sink):
    pr, pr_sink = _sink_softmax_parts(s, sink)
    return pr, (pr, pr_sink)


def _sink_softmax_bwd(r, g):
    pr, pr_sink = r
    t = jnp.sum(g * pr, -1, keepdims=True)
    return pr * (g - t), -pr_sink * t


_sink_softmax.defvjp(_sink_softmax_fwd, _sink_softmax_bwd)


def _swa_core(qq, kc, vc, sinks, kv_head, first):
    q = kc.shape[0] // 2
    s = _bdot_nt(qq, kc) * ATT_SCALE
    rows = ATT_REP * q
    iq = _iota((rows, 2 * q), 0) & (q - 1)
    js = _iota((rows, 2 * q), 1)
    rel = iq + q - js
    mask = (rel >= 0) & (rel < q) & ((js >= q) | jnp.logical_not(first))
    s = jnp.where(mask, s, -jnp.inf)
    rep = lax.shift_right_logical(_iota((rows, 1), 0), q.bit_length() - 1)
    sink = jnp.zeros((rows, 1), F32)
    for r in range(ATT_REP):
        s_r = jnp.sum(sinks * (_iota((1, 128), 1) == kv_head * ATT_REP + r).astype(F32), axis=1, keepdims=True)
        sink = sink + jnp.where(rep == r, s_r, 0.0)
    return _bdot(_sink_softmax(s, sink), vc)


def _swa_prep(q, kv, kvp, tab, tabp, kv_head):
    w = HEAD_PAD
    kc = jnp.concatenate([_rope(kvp[:, w * kv_head:w * (kv_head + 1)], tabp), _rope(kv[:, w * kv_head:w * (kv_head + 1)], tab)], axis=0)
    o = ATT_KV_HEADS * w
    vc = jnp.concatenate([kvp[:, o + w * kv_head:o + w * (kv_head + 1)], kv[:, o + w * kv_head:o + w * (kv_head + 1)]], axis=0)
    qq = jnp.concatenate([_rope(q[:, w * (kv_head * ATT_REP + r):w * (kv_head * ATT_REP + r + 1)], tab) for r in range(ATT_REP)], axis=0)
    return qq, kc, vc


def _swa_fwd(q, kv, tab, sinks, name):
    def fn(i, j, q, kv, kvp, tab, tabp, sinks):
        first = i == 0
        outs = []
        for h in range(ATT_KV_HEADS):
            qq, kc, vc = _swa_prep(q, kv, kvp, tab, tabp, h)
            o = _swa_core(qq, kc, vc, sinks, h, first)
            outs += [o[CHUNK * r:CHUNK * (r + 1)] for r in range(ATT_REP)]
        return (jnp.concatenate(outs, axis=1),), ()
    return _rows(fn, [_row(q), _row(kv), _row(kv, shift=-1), _row(tab), _row(tab, shift=-1)], [sinks],
                 [(q.shape[1], BF16)], tm=CHUNK, name=name)[0]


def _swa_bwd(q, kv, tab, sinks, do, name):
    w = HEAD_PAD

    def fn(i, j, q, kv, kvp, tab, tabp, do, sinks):
        first = i == 0
        dqs, dkc, dkp, dvc, dvp = [], [], [], [], []
        dsink = jnp.zeros_like(sinks)
        for h in range(ATT_KV_HEADS):
            qq, kc, vc = _swa_prep(q, kv, kvp, tab, tabp, h)
            dout = jnp.concatenate([do[:, w * (h * ATT_REP + r):w * (h * ATT_REP + r + 1)] for r in range(ATT_REP)], axis=0)
            _, vjp = jax.vjp(lambda a, b, c, s: _swa_core(a, b, c, s, h, first), qq, kc, vc, sinks)
            dqq, dk, dv, ds = vjp(dout.astype(F32))
            dsink = dsink + ds
            dqs += [_rope(dqq[CHUNK * r:CHUNK * (r + 1)], tab, -1.0) for r in range(ATT_REP)]
            dkp.append(_rope(dk[:CHUNK], tabp, -1.0))
            dkc.append(_rope(dk[CHUNK:], tab, -1.0))
            dvp.append(dv[:CHUNK])
            dvc.append(dv[CHUNK:])
        return (jnp.concatenate(dqs, axis=1), jnp.concatenate(dkc + dvc, axis=1), jnp.concatenate(dkp + dvp, axis=1)), (dsink,)
    return _rows(fn, [_row(q), _row(kv), _row(kv, shift=-1), _row(tab), _row(tab, shift=-1), _row(do)], [sinks],
                 [(q.shape[1], BF16), (kv.shape[1], F32), (kv.shape[1], F32)], [sinks.shape], tm=CHUNK, name=name)


def _swa_combine(dkv_cur, dkv_prev, dq, name):
    nb = dq.shape[0] // CHUNK

    def fn(i, j, cur, nxt, dq):
        dkv = (cur + jnp.where(i < nb - 1, nxt, 0.0)).astype(BF16)
        return (dkv,), (jnp.sum(dq.astype(F32), axis=0, keepdims=True), jnp.sum(dkv.astype(F32), axis=0, keepdims=True))
    return _rows(fn, [_row(dkv_cur), _row(dkv_prev, shift=1), _row(dq)], [], [(dkv_cur.shape[1], BF16)],
                 [(1, dq.shape[1]), (1, dkv_cur.shape[1])], tm=CHUNK, name=name)


def _xattn_fn(q, k, v):
    outs = []
    for h in range(X_HEADS):
        sl = slice(X_HEAD_DIM * h, X_HEAD_DIM * (h + 1))
        s = _bdot_nt(q[:, sl], k[:, sl]) * X_SCALE
        p = jnp.exp(s - jnp.max(s, -1, keepdims=True))
        outs.append(_bdot(p / jnp.sum(p, -1, keepdims=True), v[:, sl]))
    return jnp.concatenate(outs, axis=1)


def _xattn_fwd(q, k, v, name):
    def fn(i, j, q, k, v):
        return (_xattn_fn(q, k, v),), ()
    return _rows(fn, [_row(q)], [k, v], [(q.shape[1], BF16)], tm=_pick(q.shape[0], (512, 256)), name=name)[0]


def _xattn_bwd(q, k, v, do, name):
    def fn(i, j, q, do, k, v):
        _, vjp = jax.vjp(_xattn_fn, q, k, v)
        dq, dk, dv = vjp(do)
        return (dq,), (dk, dv)
    return _rows(fn, [_row(q), _row(do)], [k, v], [(q.shape[1], BF16)], [k.shape, v.shape],
                 tm=_pick(q.shape[0], (512, 256)), name=name)


def _loss_fn(x, t, g):
    return 0.5 * jnp.sum(jnp.mean(jnp.square(_rms(x, g) - t), axis=-1))


def _loss_fwd_bwd(x, t, g, name):
    def fn(i, j, x, t, g):
        loss, vjp = jax.vjp(_loss_fn, x, t, g)
        dx, _, dg = vjp(jnp.ones((), F32))
        return (dx, dx), (jnp.broadcast_to(loss, (8, 128)), dg)
    dx, dx16, loss, dg = _rows(fn, [_row(x), _row(t)], [g], [(x.shape[1], F32), (x.shape[1], BF16)], [(8, 128), g.shape],
                               tm=_pick(x.shape[0], (256,)), name=name)
    return loss, (dx, dx16), dg


def _adamw(parts, w, m, v, name):
    plist = list(parts) if isinstance(parts, (list, tuple)) else [parts]
    nl = len(plist)
    r, c = w.shape
    tm = _pick(r // nl, (128, 64, 32, 16, 8))
    nbl = r // nl // tm

    def fn(i, j, *blocks):
        parts, (w, m, v) = blocks[0], blocks[nl:]
        for l in range(1, nl):
            parts = jnp.where(i >= l * nbl, blocks[l], parts)
        g = parts[0].astype(F32)
        for k in range(1, N_DEV):
            g = g + parts[k].astype(F32)
        m2 = ADAM_B1 * m + (1.0 - ADAM_B1) * g
        v2 = ADAM_B2 * v + (1.0 - ADAM_B2) * jnp.square(g)
        m_hat = m2 / (1.0 - ADAM_B1 ** ADAM_STEP)
        v_hat = v2 / (1.0 - ADAM_B2 ** ADAM_STEP)
        delta = -ADAM_LR * (m_hat / (jnp.sqrt(v_hat) + ADAM_EPS) + ADAM_WD * w)
        return (g, delta, m2, v2), ()
    prow = [_row(p, shift=(lambda i, l=l: jnp.clip(i - l * nbl, 0, nbl - 1))) for l, p in enumerate(plist)]
    return _rows(fn, prow + [_row(w), _row(m), _row(v)], [], [(c, F32)] * 4, tm=tm, n_rows=r, name=name)


def _peer(k):
    return (k // 4, (k // 2) % 2, k % 2)


GATHER, SCATTER = "gather", "scatter"


def _ex_scratch():
    return [pltpu.SemaphoreType.DMA((N_DEV - 1,)), pltpu.SemaphoreType.DMA((N_DEV - 1,)), pltpu.SemaphoreType.DMA]


def _ex_out_shape(x, kind):
    return jax.ShapeDtypeStruct((N_DEV,) + x.shape[-2:], x.dtype)


def _gather_plan():
    x, y, c = lax.axis_index("x"), lax.axis_index("y"), lax.axis_index("c")
    return (x, y, c), (x, y, 1 - c), [(1 - x, y), (x, 1 - y), (1 - x, 1 - y)], c


def _gather_copy(x_ref, o_ref, send, recv, k, block, to, from_input=False):
    slot = o_ref.at[4 * block[0] + 2 * block[1] + block[2]]
    return pltpu.make_async_remote_copy(src_ref=x_ref if from_input else slot, dst_ref=slot, send_sem=send.at[k],
                                        recv_sem=recv.at[k], device_id=to, device_id_type=MESH_ID)


def _gather_start(x_ref, o_ref, send, recv, local):
    me, sib, chips, c = _gather_plan()
    pltpu.make_async_copy(x_ref, o_ref.at[4 * me[0] + 2 * me[1] + me[2]], local).start()
    _gather_copy(x_ref, o_ref, send, recv, 0, me, sib, True).start()
    for j, chip in enumerate(chips):
        _gather_copy(x_ref, o_ref, send, recv, 1 + j, me, (*chip, c), True).start()


def _gather_finish(x_ref, o_ref, send, recv, local):
    me, sib, chips, c = _gather_plan()
    for j, chip in enumerate(chips):
        _gather_copy(x_ref, o_ref, send, recv, 1 + j, (*chip, c), me).wait_recv()
        _gather_copy(x_ref, o_ref, send, recv, 4 + j, (*chip, c), sib).start()
    _gather_copy(x_ref, o_ref, send, recv, 0, sib, me).wait_recv()
    for j, chip in enumerate(chips):
        _gather_copy(x_ref, o_ref, send, recv, 4 + j, (*chip, 1 - c), me).wait_recv()
    _gather_copy(x_ref, o_ref, send, recv, 0, me, sib, True).wait_send()
    for j, chip in enumerate(chips):
        _gather_copy(x_ref, o_ref, send, recv, 1 + j, me, (*chip, c), True).wait_send()
        _gather_copy(x_ref, o_ref, send, recv, 4 + j, (*chip, c), sib).wait_send()
    pltpu.make_async_copy(x_ref, o_ref.at[4 * me[0] + 2 * me[1] + me[2]], local).wait()


def _scatter_copy(x_ref, o_ref, send, recv, d, frm, to):
    return pltpu.make_async_remote_copy(src_ref=x_ref.at[to], dst_ref=o_ref.at[frm], send_sem=send.at[d - 1],
                                        recv_sem=recv.at[d - 1], device_id=_peer(to), device_id_type=MESH_ID)


def _scatter_start(x_ref, o_ref, send, recv, local):
    me = 4 * lax.axis_index("x") + 2 * lax.axis_index("y") + lax.axis_index("c")
    pltpu.make_async_copy(x_ref.at[me], o_ref.at[me], local).start()
    for d in range(1, N_DEV):
        _scatter_copy(x_ref, o_ref, send, recv, d, me, (me + d) % N_DEV).start()


def _scatter_finish(x_ref, o_ref, send, recv, local):
    me = 4 * lax.axis_index("x") + 2 * lax.axis_index("y") + lax.axis_index("c")
    for d in range(1, N_DEV):
        _scatter_copy(x_ref, o_ref, send, recv, d, (me + N_DEV - d) % N_DEV, me).wait_recv()
    for d in range(1, N_DEV):
        _scatter_copy(x_ref, o_ref, send, recv, d, me, (me + d) % N_DEV).wait_send()
    pltpu.make_async_copy(x_ref.at[me], o_ref.at[me], local).wait()


_EX_START = {GATHER: _gather_start, SCATTER: _scatter_start}
_EX_FINISH = {GATHER: _gather_finish, SCATTER: _scatter_finish}


def _exchange(x, kind, name):
    def body(x_ref, o_ref, send, recv, local):
        _EX_START[kind](x_ref, o_ref, send, recv, local)
        _EX_FINISH[kind](x_ref, o_ref, send, recv, local)

    return pl.pallas_call(
        body, name=name, in_specs=[pl.BlockSpec(memory_space=pl.ANY)], out_specs=pl.BlockSpec(memory_space=pl.ANY),
        out_shape=_ex_out_shape(x, kind), scratch_shapes=_ex_scratch(),
    )(x)


class _Freight:
    def __init__(self):
        self.load = {}
        self.landed = {}

    def put(self, carrier, key, arr, kind):
        self.load.setdefault(carrier, []).append((key, arr, kind))

    def take(self, carrier):
        return self.load.pop(carrier, [])


def _all_gather(x, name):
    return _exchange(x.reshape(-1, x.shape[-1]), GATHER, name).reshape((N_DEV,) + x.shape)


def _all_to_all(x, name):
    return _exchange(x.reshape(N_DEV, -1, x.shape[-1]), SCATTER, name).reshape(x.shape)


def _pack(arrs):
    flat = []
    for a in arrs:
        a = a.reshape(-1).astype(F32)
        flat.append(jnp.pad(a, (0, (-a.shape[0]) % 1024)))
    return jnp.concatenate(flat).reshape(-1, 128)


def _unpack(p, shapes):
    p = p.reshape(-1)
    out, off = [], 0
    for s in shapes:
        n = 1
        for d in s:
            n *= d
        out.append(p[off:off + n].reshape(s))
        off += n + (-n) % 1024
    return out


def _cols_to_full(g):
    return g.transpose(1, 0, 2).reshape(g.shape[1], -1)


def _full_to_cols(w, shards=N_DEV):
    return w.reshape(w.shape[0], shards, -1).transpose(1, 0, 2)


SMALL = ("norm_ffn1", "norm_mix", "gm_ln_g", "gm_ln_b", "gm_ws", "gm_bs", "conv_b", "dt_bias", "a_log", "d_skip", "ssd_norm",
         "sinks", "norm_xq", "norm_mem", "norm_ffn2", "final_norm")
SHARDED = ("w_ffn1_gu", "w_ffn1_down", "w_in_even", "w_out_even", "w_qkv", "w_o_odd", "w_xq", "w_xkv", "w_xo", "w_ffn2_gu",
           "w_ffn2_down")
SMALL_SHARDED = ("conv_w", "b_qkv")
ORDER = ("norm_ffn1", "w_ffn1_gu", "w_ffn1_down", "norm_mix", "w_in_even", "gm_ln_g", "gm_ln_b", "gm_ws", "gm_bs", "conv_w",
         "conv_b", "dt_bias", "a_log", "d_skip", "ssd_norm", "w_out_even", "w_qkv", "b_qkv", "sinks", "w_o_odd", "norm_xq",
         "norm_mem", "w_xq", "w_xkv", "w_xo", "norm_ffn2", "w_ffn2_gu", "w_ffn2_down", "final_norm")


def _ffn_fwd(x, gain, wg, wu, wd, tag, freight=None):
    mm = functools.partial(_matmul, freight=freight)
    h = _rms_fwd(x, gain, f"{tag}_norm")
    g = mm(h, wg, out_dtype=BF16, name=f"{tag}_gate")
    u = mm(h, wu, out_dtype=BF16, name=f"{tag}_up")
    act = _swiglu_fwd(g, u, f"{tag}_act")
    wd = wd() if callable(wd) else wd
    y = mm(act, wd, alpha=0.5, res=x, name=f"{tag}_down")
    return y, (x, h, g, u, act), wd


def _ffn_bwd(dy, saved, gain, wg, wu, wd, tag, freight=None, ship=None):
    mm = functools.partial(_matmul, freight=freight)
    x, h, g, u, act = saved
    dy, dy16 = dy
    dact = mm(dy16, wd, tb=True, alpha=0.5, out_dtype=BF16, name=f"{tag}_dact")
    dwd = mm(act, dy16, ta=True, alpha=0.5, out_dtype=BF16, name=f"{tag}_dwd")
    if ship is not None:
        ship(f"{tag}_dwg", "dn", dwd.reshape(N_DEV, -1, dwd.shape[1]))
    dg, du = _swiglu_bwd(g, u, dact, f"{tag}_dgu")
    dwg = mm(h, dg, ta=True, out_dtype=BF16, name=f"{tag}_dwg")
    dwu = mm(h, du, ta=True, out_dtype=BF16, name=f"{tag}_dwu")
    if ship is not None:
        half = N_DEV // 2
        parts = jnp.concatenate([_full_to_cols(dwg, half), _full_to_cols(dwu, half)], axis=0)
        rows = parts.shape[1] // 2
        ship(f"{tag}_dh_g", "gu_a", parts[:, :rows])
        ship(f"{tag}_dh_u", "gu_b", parts[:, rows:])
    dh = mm(dg, wg, tb=True, name=f"{tag}_dh_g")
    dh = mm(du, wu, tb=True, res=dh, name=f"{tag}_dh_u")
    dx, dgain = _rms_bwd(x, dh, dy, gain, f"{tag}_dnorm")
    return dx, dgain, dwg, dwu, dwd


def _xattn_layer_fwd(x, mem, gq, gm, wq, wkv, wo, tag, freight=None):
    mm = functools.partial(_matmul, freight=freight)
    hq = _rms_fwd(x, gq, f"{tag}_normq")
    mn = _rms_fwd(mem, gm, f"{tag}_normm")
    q = mm(hq, wq, name=f"{tag}_q")
    kv = mm(mn, wkv, name=f"{tag}_kv")
    k, v = kv[:, :X_HEADS * X_HEAD_DIM], kv[:, X_HEADS * X_HEAD_DIM:]
    o = _xattn_fwd(q, k, v, f"{tag}_attn")
    y = mm(o, wo, res=x, name=f"{tag}_o")
    return y, (x, hq, mn, q, k, v, o)


def _xattn_layer_bwd(dy, saved, mem, gq, gm, wq, wkv, wo, tag, freight=None):
    mm = functools.partial(_matmul, freight=freight)
    x, hq, mn, q, k, v, o = saved
    dy, dy16 = dy
    do = mm(dy16, wo, tb=True, name=f"{tag}_do")
    dwo = mm(o, dy16, ta=True, out_dtype=BF16, name=f"{tag}_dwo")
    dq, dk, dv = _xattn_bwd(q, k, v, do, f"{tag}_dattn")
    dkv = jnp.concatenate([dk, dv], axis=1)
    dwq = mm(hq, dq, ta=True, out_dtype=BF16, name=f"{tag}_dwq")
    dwkv = mm(mn, dkv, ta=True, out_dtype=BF16, name=f"{tag}_dwkv")
    dhq = mm(dq, wq, tb=True, name=f"{tag}_dhq")
    dmn = mm(dkv, wkv, tb=True, name=f"{tag}_dmn")
    dx, dgq = _rms_bwd(x, dhq, dy, gq, f"{tag}_dnormq")
    dgm = _rms_bwd_gain(mem, dmn, gm, f"{tag}_dnormm")
    return dx, dgq, dgm, dwq, dwkv, dwo


def _even_fwd(x, weights, params, freight=None):
    mm = functools.partial(_matmul, freight=freight)
    w_main, w_dt, w_out_a, w_out_b = weights
    gain, lng, lnb, ws, bst, conv_w, conv_b, dtb, alog, dsk, ssd_norm, expand = params
    hm = _rms_fwd(x, gain, "l0_normmix")
    proj = mm(hm, w_main, name="l0_proj")
    dtr = mm(hm, w_dt, name="l0_dt")
    a_out = _gmlp_fwd(proj, lng, lnb, ws, bst, "l0_gmlp")
    xbc = _conv_fwd(proj, conv_w, conv_b, "l0_conv")
    y_ssd, hp_all = _ssd_fwd(xbc, dtr, dtb, alog, dsk, expand, "l0_ssd")
    b_out = _gate_fwd(y_ssd, proj, ssd_norm, "l0_gate")
    y = mm(a_out, w_out_a, res=x, name="l0_out_a")
    y = mm(b_out, w_out_b, res=y, name="l0_out_b")
    return y, (x, hm, proj, dtr, a_out, xbc, y_ssd, hp_all, b_out)


def _even_bwd(dx, saved, weights, params, freight=None, ship_out=None):
    mm = functools.partial(_matmul, freight=freight)
    w_main, w_dt, w_out_a, w_out_b = weights
    w_uv, w_z, w_xbc = w_main[:, :4096], w_main[:, 4096:6144], w_main[:, 6144:]
    gain, lng, lnb, ws, bst, conv_w, conv_b, dtb, alog, dsk, ssd_norm, expand = params
    x, hm, proj, dtr, a_out, xbc, y_ssd, hp_all, b_out = saved
    uv = zz = xbc_raw = proj
    dx, dx16 = dx
    da_out = mm(dx16, w_out_a, tb=True, name="l0_da")
    db_out = mm(dx16, w_out_b, tb=True, name="l0_db")
    dw_out_a = mm(a_out, dx16, ta=True, out_dtype=BF16, name="l0_dwout_a")
    dw_out_b = mm(b_out, dx16, ta=True, out_dtype=BF16, name="l0_dwout_b")
    dw_out = jnp.concatenate([dw_out_a, dw_out_b], axis=0)
    if ship_out is not None:
        ship_out(dw_out)
    dy_ssd, dzz, d_ssd_norm = _gate_bwd(y_ssd, zz, db_out, ssd_norm, "l0_dgate")
    dxbc, ddtr, d_dtb, d_alog, d_dsk = _ssd_bwd(xbc, dtr, dtb, alog, dsk, expand, hp_all, dy_ssd, "l0_dssd")
    dpre, d_conv_w, d_conv_b = _conv_bwd_pre(xbc_raw, dxbc, conv_w, conv_b, "l0_dconv_pre")
    dxbc_raw = _conv_bwd_x(dpre, conv_w, "l0_dconv_x")
    duv, d_lng, d_lnb, d_ws, d_bst = _gmlp_bwd(uv, da_out, lng, lnb, ws, bst, "l0_dgmlp")
    ddtr16 = ddtr.astype(BF16)
    dhm = mm(duv, w_uv, tb=True, name="l0_dh_uv")
    dhm = mm(dzz, w_z, tb=True, res=dhm, name="l0_dh_z")
    dhm = mm(dxbc_raw, w_xbc, tb=True, res=dhm, name="l0_dh_xbc")
    dhm = mm(ddtr16, w_dt, tb=True, res=dhm, name="l0_dh_dt")
    dw_uv = mm(hm, duv, ta=True, out_dtype=BF16, name="l0_dwuv")
    dw_z = mm(hm, dzz, ta=True, out_dtype=BF16, name="l0_dwz")
    dw_xbc = mm(hm, dxbc_raw, ta=True, out_dtype=BF16, name="l0_dwxbc")
    dw_dt = mm(hm, ddtr16, ta=True, out_dtype=BF16, name="l0_dwdt")
    dx, d_gain = _rms_bwd(x, dhm, dx, gain, "l0_dnormmix")
    small = (d_gain, d_lng, d_lnb, d_ws, d_bst, d_conv_w, d_conv_b, d_dtb, d_alog, d_dsk, d_ssd_norm)
    return dx, small, (dw_uv, dw_z, dw_xbc, dw_dt, dw_out)


def _odd_fwd(x, weights, params, freight=None):
    mm = functools.partial(_matmul, freight=freight)
    w_q, w_kv, w_o = weights
    gain, b_q, b_kv, tab, snk = params
    hm = _rms_fwd(x, gain, "l1_normmix")
    q = mm(hm, w_q, bias=b_q, name="l1_q")
    kv = mm(hm, w_kv, bias=b_kv, name="l1_kv")
    o = _swa_fwd(q, kv, tab, snk, "l1_swa")
    y = mm(o, w_o, res=x, name="l1_o")
    return y, (x, hm, q, kv, o)


def _odd_bwd(dx, saved, weights, params, freight=None):
    mm = functools.partial(_matmul, freight=freight)
    w_q, w_kv, w_o = weights
    gain, b_q, b_kv, tab, snk = params
    x, hm, q, kv, o = saved
    dx, dx16 = dx
    do = mm(dx16, w_o, tb=True, out_dtype=BF16, name="l1_do")
    dw_o = mm(o, dx16, ta=True, out_dtype=BF16, name="l1_dwo")
    dq, dkv_cur, dkv_prev, d_snk = _swa_bwd(q, kv, tab, snk, do, "l1_dswa")
    dkv, db_q, db_kv = _swa_combine(dkv_cur, dkv_prev, dq, "l1_dkv")
    dhm = mm(dq, w_q, tb=True, name="l1_dh_q")
    dhm = mm(dkv, w_kv, tb=True, res=dhm, name="l1_dh_kv")
    dw_q = mm(hm, dq, ta=True, out_dtype=BF16, name="l1_dwq")
    dw_kv = mm(hm, dkv, ta=True, out_dtype=BF16, name="l1_dwkv")
    dx, d_gain = _rms_bwd(x, dhm, dx, gain, "l1_dnormmix")
    return dx, (d_gain, db_q, db_kv, d_snk), (dw_q, dw_kv, dw_o)


def _rope_table(positions):
    seq = positions.size
    inv_freq = ROPE_THETA ** (-jnp.arange(0, 2 * ROT_HALF, 2, dtype=F32) / (2 * ROT_HALF))
    ang = positions.reshape(seq, 1).astype(F32) * inv_freq
    cos, sin, zero = jnp.cos(ang), jnp.sin(ang), jnp.zeros((seq, 128 - 2 * ROT_HALF), F32)
    z8 = jnp.zeros((seq, ROT_HALF), F32)
    return jnp.concatenate([cos, cos, zero + 1.0, -sin, z8, zero, z8, sin, zero], axis=1)


def _pad_heads_cols(w, heads):
    k = w.shape[0]
    return jnp.pad(w.reshape(k, heads, ATT_HEAD_DIM), ((0, 0), (0, 0), (0, HEAD_PAD - ATT_HEAD_DIM))).reshape(k, heads * HEAD_PAD)


def _unpad_heads_cols(w, heads):
    k = w.shape[0]
    return w.reshape(k, heads, HEAD_PAD)[:, :, :ATT_HEAD_DIM].reshape(k, heads * ATT_HEAD_DIM)


def kernel(x, mem, positions, norm_ffn1, w_ffn1_gu, w_ffn1_down, norm_mix, w_in_even, gm_ln_g, gm_ln_b, gm_ws, gm_bs, conv_w, conv_b, dt_bias, a_log, d_skip, ssd_norm, w_out_even, w_qkv, b_qkv, sinks, w_o_odd, norm_xq, norm_mem, w_xq, w_xkv, w_xo, norm_ffn2, w_ffn2_gu, w_ffn2_down, final_norm, loss_target, m_norm_ffn1, m_w_ffn1_gu, m_w_ffn1_down, m_norm_mix, m_w_in_even, m_gm_ln_g, m_gm_ln_b, m_gm_ws, m_gm_bs, m_conv_w, m_conv_b, m_dt_bias, m_a_log, m_d_skip, m_ssd_norm, m_w_out_even, m_w_qkv, m_b_qkv, m_sinks, m_w_o_odd, m_norm_xq, m_norm_mem, m_w_xq, m_w_xkv, m_w_xo, m_norm_ffn2, m_w_ffn2_gu, m_w_ffn2_down, m_final_norm, v_norm_ffn1, v_w_ffn1_gu, v_w_ffn1_down, v_norm_mix, v_w_in_even, v_gm_ln_g, v_gm_ln_b, v_gm_ws, v_gm_bs, v_conv_w, v_conv_b, v_dt_bias, v_a_log, v_d_skip, v_ssd_norm, v_w_out_even, v_w_qkv, v_b_qkv, v_sinks, v_w_o_odd, v_norm_xq, v_norm_mem, v_w_xq, v_w_xkv, v_w_xo, v_norm_ffn2, v_w_ffn2_gu, v_w_ffn2_down, v_final_norm):
    env = dict(locals())
    W = {n: env[n] for n in ORDER}
    M = {n: env["m_" + n] for n in ORDER}
    V = {n: env["v_" + n] for n in ORDER}
    seq = x.shape[1]
    x0 = x.reshape(seq, D_MODEL)
    mem2 = mem.reshape(-1, D_MODEL)
    target = loss_target.reshape(seq, D_MODEL)

    fr = _Freight()
    b16 = lambda a: a.astype(BF16)
    grp_a = {"out": w_out_even[0], "xq0": w_xq[0], "xkv0": w_xkv[0], "xo0": w_xo[0]}
    grp_b = {"qkv": w_qkv[0], "o": w_o_odd[0], "xq1": w_xq[1], "xkv1": w_xkv[1], "xo1": w_xo[1]}
    half_rows = D_MODEL // 2
    fr.put("l0f1_gate", "dn1_0", b16(w_ffn1_down[0]), GATHER)
    fr.put("l0f1_gate", "dn2_0", b16(w_ffn2_down[0]), GATHER)
    fr.put("l0f1_up", "in", b16(w_in_even[0]), GATHER)
    for key, w in grp_a.items():
        fr.put("l0f1_down", key, b16(w), GATHER)
    fr.put("l0_proj", "gu2_0", b16(w_ffn2_gu[0]), GATHER)
    fr.put("l0f2_gate", "gu1_1a", b16(w_ffn1_gu[1, :half_rows]), GATHER)
    fr.put("l0f2_gate", "dn1_1", b16(w_ffn1_down[1]), GATHER)
    fr.put("l0f2_up", "gu1_1b", b16(w_ffn1_gu[1, half_rows:]), GATHER)
    for key, w in grp_b.items():
        fr.put("l0f2_down", key, b16(w), GATHER)
    fr.put("l1f1_gate", "gu2_1a", b16(w_ffn2_gu[1, :half_rows]), GATHER)
    fr.put("l1f1_gate", "dn2_1", b16(w_ffn2_down[1]), GATHER)
    fr.put("l1f1_up", "gu2_1b", b16(w_ffn2_gu[1, half_rows:]), GATHER)
    gu_first = _exchange(b16(w_ffn1_gu[0]), GATHER, "ag_l0f1_gu")
    gs = _all_gather(_pack([conv_w, b_qkv]), "ag_small")
    gs = [_unpack(gs[k], [conv_w.shape, b_qkv.shape]) for k in range(N_DEV)]
    conv_w_full = jnp.concatenate([g[0][0] for g in gs], axis=1)
    b_qkv_full = jnp.concatenate([g[1][0] for g in gs], axis=0)

    def gate_up(g):
        return _cols_to_full(g[:N_DEV // 2]), _cols_to_full(g[N_DEV // 2:])

    down = lambda key: fr.landed[key].reshape(D_FF, D_MODEL)

    row = lambda a: a.reshape(1, -1)
    pad128 = lambda a: jnp.pad(a.reshape(1, -1), ((0, 0), (0, 128 - a.size)))
    bst = jnp.pad(gm_bs[0].T, ((0, 0), (0, 128 - GM_GROUPS)))
    ws = gm_ws[0]
    dtb, alog, dsk, snk = pad128(dt_bias), pad128(a_log), pad128(d_skip), pad128(sinks)
    expand = (jnp.arange(128)[:, None] == (jnp.arange(D_MODEL) // SSD_HEAD_DIM)[None, :]).astype(F32)
    tab = _rope_table(positions)

    wg_f1a, wu_f1a = gate_up(gu_first)
    xa, s_f1a, wd_f1a = _ffn_fwd(x0, row(norm_ffn1[0]), wg_f1a, wu_f1a, lambda: down("dn1_0"), "l0f1", fr)
    w_in = _cols_to_full(fr.landed["in"])
    n_main = 2 * GM_GROUPS * GM_GDIM + D_MODEL + XBC_WIDTH
    w_main = w_in[:, :n_main]
    w_dt = jnp.pad(w_in[:, n_main:], ((0, 0), (0, 128 - SSD_HEADS)))
    a_xq, a_xkv, a_xo = fr.landed["xq0"], fr.landed["xkv0"], fr.landed["xo0"]
    w_out = fr.landed["out"].reshape(2 * D_MODEL, D_MODEL)
    even_w = (w_main, w_dt, w_out[:D_MODEL], w_out[D_MODEL:])
    even_p = (row(norm_mix[0]), gm_ln_g, gm_ln_b, ws, bst, conv_w_full, conv_b, dtb, alog, dsk, ssd_norm, expand)
    xb, s_even = _even_fwd(xa, even_w, even_p, fr)
    x0_w = (a_xq.reshape(D_MODEL, -1), a_xkv.reshape(D_MODEL, -1), _cols_to_full(a_xo))
    xc, s_x0 = _xattn_layer_fwd(xb, mem2, row(norm_xq[0]), row(norm_mem[0]), *x0_w, "l0x", fr)
    wg_f2a, wu_f2a = gate_up(fr.landed["gu2_0"])
    xd, s_f2a, wd_f2a = _ffn_fwd(xc, row(norm_ffn2[0]), wg_f2a, wu_f2a, down("dn2_0"), "l0f2", fr)
    wg_f1b, wu_f1b = gate_up(jnp.concatenate([fr.landed["gu1_1a"], fr.landed["gu1_1b"]], axis=1))
    xe, s_f1b, wd_f1b = _ffn_fwd(xd, row(norm_ffn1[1]), wg_f1b, wu_f1b, down("dn1_1"), "l1f1", fr)
    b_xq, b_xkv, b_xo = fr.landed["xq1"], fr.landed["xkv1"], fr.landed["xo1"]
    nq = ATT_HEADS * ATT_HEAD_DIM
    wqkv = _cols_to_full(fr.landed["qkv"])
    w_o = _pad_heads_cols(fr.landed["o"].reshape(D_MODEL, D_MODEL).T, ATT_HEADS).T
    odd_w = (_pad_heads_cols(wqkv[:, :nq], ATT_HEADS), _pad_heads_cols(wqkv[:, nq:], 2 * ATT_KV_HEADS), w_o)
    odd_p = (row(norm_mix[1]), _pad_heads_cols(b_qkv_full[None, :nq], ATT_HEADS),
             _pad_heads_cols(b_qkv_full[None, nq:], 2 * ATT_KV_HEADS), tab, snk)
    xf, s_odd = _odd_fwd(xe, odd_w, odd_p, fr)
    x1_w = (b_xq.reshape(D_MODEL, -1), b_xkv.reshape(D_MODEL, -1), _cols_to_full(b_xo))
    xg, s_x1 = _xattn_layer_fwd(xf, mem2, row(norm_xq[1]), row(norm_mem[1]), *x1_w, "l1x", fr)
    wg_f2b, wu_f2b = gate_up(jnp.concatenate([fr.landed["gu2_1a"], fr.landed["gu2_1b"]], axis=1))
    xh, s_f2b, wd_f2b = _ffn_fwd(xg, row(norm_ffn2[1]), wg_f2b, wu_f2b, down("dn2_1"), "l1f2", fr)
    loss8, dx, d_final = _loss_fwd_bwd(xh, target, row(final_norm), "loss")
    loss = lax.psum(loss8[0, 0], ("x", "y", "c"))
    assert not fr.load, sorted(fr.load)

    def shipper(tag):
        return lambda carrier, key, parts: fr.put(carrier, f"{tag}_{key}", parts, SCATTER)

    dx, dn_f2b, _, _, _ = _ffn_bwd(dx, s_f2b, row(norm_ffn2[1]), wg_f2b, wu_f2b, wd_f2b, "l1f2", fr, shipper("l1f2"))
    dx, dn_xq1, dn_mem1, dwxq1, dwxkv1, dwxo1 = _xattn_layer_bwd(dx, s_x1, mem2, row(norm_xq[1]), row(norm_mem[1]), *x1_w, "l1x", fr)
    dx, (dn_mix1, db_q, db_kv, d_snk), (dw_q, dw_kv, dw_o) = _odd_bwd(dx, s_odd, odd_w, odd_p, fr)
    rows_parts = lambda g: g.reshape(N_DEV, -1, g.shape[1])
    dwqkv = jnp.concatenate([_unpad_heads_cols(dw_q, ATT_HEADS), _unpad_heads_cols(dw_kv, 2 * ATT_KV_HEADS)], axis=1)
    parts_b = {"d_qkv": _full_to_cols(dwqkv), "d_o": rows_parts(_unpad_heads_cols(dw_o.T, ATT_HEADS).T), "d_xq1": rows_parts(dwxq1),
               "d_xkv1": rows_parts(dwxkv1), "d_xo1": _full_to_cols(dwxo1)}
    for key, parts in parts_b.items():
        fr.put("l1f1_dact" if key in ("d_qkv", "d_o") else "l1f1_dwd", key, parts, SCATTER)
    dx, dn_f1b, _, _, _ = _ffn_bwd(dx, s_f1b, row(norm_ffn1[1]), wg_f1b, wu_f1b, wd_f1b, "l1f1", fr, shipper("l1f1"))
    dx, dn_f2a, _, _, _ = _ffn_bwd(dx, s_f2a, row(norm_ffn2[0]), wg_f2a, wu_f2a, wd_f2a, "l0f2", fr, shipper("l0f2"))
    dx, dn_xq0, dn_mem0, dwxq0, dwxkv0, dwxo0 = _xattn_layer_bwd(dx, s_x0, mem2, row(norm_xq[0]), row(norm_mem[0]), *x0_w, "l0x", fr)
    for key, parts in {"d_xq0": rows_parts(dwxq0), "d_xkv0": rows_parts(dwxkv0), "d_xo0": _full_to_cols(dwxo0)}.items():
        fr.put("l0_da", key, parts, SCATTER)
    ship_out = lambda dw_out: fr.put("l0_dh_uv", "d_out", rows_parts(dw_out), SCATTER)
    dx, small_even, (dw_uv, dw_z, dw_xbc, dw_dt, _) = _even_bwd(dx, s_even, even_w, even_p, fr, ship_out)
    dn_mix0, d_lng, d_lnb, d_ws, d_bst, d_conv_w, d_conv_b, d_dtb, d_alog, d_dsk, d_ssd_norm = small_even
    d_in = _full_to_cols(jnp.concatenate([dw_uv, dw_z, dw_xbc, dw_dt[:, :SSD_HEADS]], axis=1))
    fr.put("l0f1_dact", "d_in_a", d_in[:, :half_rows], SCATTER)
    fr.put("l0f1_dwd", "d_in_b", d_in[:, half_rows:], SCATTER)
    dx, dn_f1a, _, _, _ = _ffn_bwd(dx, s_f1a, row(norm_ffn1[0]), wg_f1a, wu_f1a, wd_f1a, "l0f1", fr, shipper("l0f1"))
    assert not fr.load, sorted(fr.load)
    grad_x = dx[0].reshape(x.shape)

    got = fr.landed
    received = {
        "w_ffn1_gu": [got["l0f1_gu_a"], got["l0f1_gu_b"], got["l1f1_gu_a"], got["l1f1_gu_b"]],
        "w_ffn2_gu": [got["l0f2_gu_a"], got["l0f2_gu_b"], got["l1f2_gu_a"], got["l1f2_gu_b"]],
        "w_ffn1_down": [got["l0f1_dn"], got["l1f1_dn"]], "w_ffn2_down": [got["l0f2_dn"], got["l1f2_dn"]],
        "w_in_even": [got["d_in_a"], got["d_in_b"]], "w_out_even": [got["d_out"]], "w_qkv": [got["d_qkv"]], "w_o_odd": [got["d_o"]],
        "w_xq": [got["d_xq0"], got["d_xq1"]], "w_xkv": [got["d_xkv0"], got["d_xkv1"]], "w_xo": [got["d_xo0"], got["d_xo1"]],
    }
    out = {}
    for n in SHARDED:
        shp = W[n].shape
        two = lambda a: a.reshape(-1, shp[-1])
        res = _adamw(received[n], two(W[n]), two(M[n]), two(V[n]), f"adam_{n}")
        out[n] = [r.reshape(shp) for r in res]

    db_qkv = jnp.concatenate([_unpad_heads_cols(db_q, ATT_HEADS), _unpad_heads_cols(db_kv, 2 * ATT_KV_HEADS)], axis=1).reshape(-1)
    cw_parts = _full_to_cols(d_conv_w)
    bq_parts = db_qkv.reshape(N_DEV, -1)
    ss_parts = jnp.stack([_pack([cw_parts[k], bq_parts[k]]) for k in range(N_DEV)])
    recv = _all_to_all(ss_parts, "a2a_small")
    res = _adamw(recv, _pack([conv_w, b_qkv]), _pack([m_conv_w, m_b_qkv]), _pack([v_conv_w, v_b_qkv]), "adam_small_sharded")
    res = [_unpack(r, [conv_w.shape, b_qkv.shape]) for r in res]
    out["conv_w"] = [r[0] for r in res]
    out["b_qkv"] = [r[1] for r in res]

    small_grads = {
        "norm_ffn1": jnp.concatenate([dn_f1a, dn_f1b]), "norm_mix": jnp.concatenate([dn_mix0, dn_mix1]),
        "gm_ln_g": d_lng, "gm_ln_b": d_lnb, "gm_ws": d_ws[None], "gm_bs": d_bst[:, :GM_GROUPS].T[None],
        "conv_b": d_conv_b, "dt_bias": d_dtb[:, :SSD_HEADS], "a_log": d_alog[:, :SSD_HEADS], "d_skip": d_dsk[:, :SSD_HEADS],
        "ssd_norm": d_ssd_norm, "sinks": d_snk[:, :ATT_HEADS], "norm_xq": jnp.concatenate([dn_xq0, dn_xq1]),
        "norm_mem": jnp.concatenate([dn_mem0, dn_mem1]), "norm_ffn2": jnp.concatenate([dn_f2a, dn_f2b]),
        "final_norm": d_final.reshape(-1),
    }
    shapes = [W[n].shape for n in SMALL]
    recv = _all_gather(_pack([small_grads[n] for n in SMALL]), "ag_small_grads")
    res = _adamw(recv, _pack([W[n] for n in SMALL]), _pack([M[n] for n in SMALL]), _pack([V[n] for n in SMALL]), "adam_small")
    res = [_unpack(r, shapes) for r in res]
    for i, n in enumerate(SMALL):
        out[n] = [r[i] for r in res]

    return (loss, grad_x, *[out[n][0] for n in ORDER], *[out[n][1] for n in ORDER], *[out[n][2] for n in ORDER],
            *[out[n][3] for n in ORDER])
```

```python
import functools

import jax
import jax.numpy as jnp
from jax import lax
from jax.experimental import pallas as pl
from jax.experimental.pallas import tpu as pltpu

F32, BF16 = jnp.float32, jnp.bfloat16

N_DEV = 8
D_MODEL = 2048
D_FF = 5632
EPS = 1e-5
CHUNK = 128
GM_GROUPS, GM_GDIM = 4, 512
SSD_HEADS, SSD_HEAD_DIM, SSD_GROUPS, SSD_STATE = 32, 64, 4, 128
XBC_WIDTH = D_MODEL + 2 * SSD_GROUPS * SSD_STATE
Z_COL = 2 * GM_GROUPS * GM_GDIM // D_MODEL
XBC_COL = (2 * GM_GROUPS * GM_GDIM + D_MODEL) // XBC_WIDTH
ATT_HEADS, ATT_KV_HEADS, ATT_HEAD_DIM, ATT_REP = 32, 4, 64, 8
HEAD_PAD = 128
ROT_HALF = 8
ROPE_THETA = 500000.0
ATT_SCALE = ATT_HEAD_DIM ** -0.5
X_HEADS, X_HEAD_DIM = 4, 128
X_SCALE = X_HEAD_DIM ** -0.5
ADAM_LR, ADAM_B1, ADAM_B2, ADAM_EPS, ADAM_WD, ADAM_STEP = 0.001, 0.9, 0.999, 1e-08, 0.01, 10

VMEM_LIMIT_BYTES = 56 * 1024 * 1024
MESH_ID = pl.DeviceIdType.MESH


def _params(*sem):
    return pltpu.CompilerParams(dimension_semantics=sem, vmem_limit_bytes=VMEM_LIMIT_BYTES)


def _pick(n, cands):
    for c in cands:
        if n % c == 0:
            return c
    return n


def _dg(a, b, ca, cb, precision=None):
    return lax.dot_general(a, b, (((ca,), (cb,)), ((), ())), precision=precision, preferred_element_type=F32)


@jax.custom_vjp
def _bdot(a, b):
    return _dg(a.astype(BF16), b.astype(BF16), 1, 0)


def _bdot_fwd(a, b):
    return _bdot(a, b), (a, b)


def _bdot_bwd(r, g):
    a, b = r
    g = g.astype(BF16)
    return _dg(g, b.astype(BF16), 1, 1), _dg(a.astype(BF16), g, 0, 0)


_bdot.defvjp(_bdot_fwd, _bdot_bwd)


@jax.custom_vjp
def _bdot_nt(a, b):
    return _dg(a.astype(BF16), b.astype(BF16), 1, 1)


def _bdot_nt_fwd(a, b):
    return _bdot_nt(a, b), (a, b)


def _bdot_nt_bwd(r, g):
    a, b = r
    g = g.astype(BF16)
    return _dg(g, b.astype(BF16), 1, 0), _dg(g, a.astype(BF16), 0, 0)


_bdot_nt.defvjp(_bdot_nt_fwd, _bdot_nt_bwd)


@jax.custom_vjp
def _bdot_tn(a, b):
    return _dg(a.astype(BF16), b.astype(BF16), 0, 0)


def _bdot_tn_fwd(a, b):
    return _bdot_tn(a, b), (a, b)


def _bdot_tn_bwd(r, g):
    a, b = r
    g = g.astype(BF16)
    return _dg(b.astype(BF16), g, 1, 1), _dg(a.astype(BF16), g, 1, 0)


_bdot_tn.defvjp(_bdot_tn_fwd, _bdot_tn_bwd)


def _split3(x):
    hi = x.astype(BF16)
    r1 = x - hi.astype(F32)
    mid = r1.astype(BF16)
    lo = (r1 - mid.astype(F32)).astype(BF16)
    return hi, mid, lo


def _dot_exact(x, one, cx, co, x_is_lhs):
    one = one.astype(BF16)
    out = None
    for piece in _split3(x):
        term = _dg(piece, one, cx, co) if x_is_lhs else _dg(one, piece, co, cx)
        out = term if out is None else out + term
    return out


@jax.custom_vjp
def _spread(a, e):
    return _dot_exact(a, e, 1, 0, True)


def _spread_fwd(a, e):
    return _spread(a, e), e


def _spread_bwd(e, g):
    return _dot_exact(g, e, 1, 1, True), jnp.zeros_like(e)


_spread.defvjp(_spread_fwd, _spread_bwd)


@jax.custom_vjp
def _running_sum(t, a):
    return _dot_exact(a, t, 0, 1, False)


def _running_sum_fwd(t, a):
    return _running_sum(t, a), t


def _running_sum_bwd(t, g):
    return jnp.zeros_like(t), _dot_exact(g, t, 0, 0, False)


_running_sum.defvjp(_running_sum_fwd, _running_sum_bwd)


def _sigmoid(x):
    return 1.0 / (1.0 + jnp.exp(-x))


def _silu(x):
    return x * _sigmoid(x)


def _gelu(x):
    return 0.5 * x * (1.0 + lax.erf(x * 0.7071067811865476))


def _softplus(x):
    return jnp.maximum(x, 0.0) + jnp.log1p(jnp.exp(-jnp.abs(x)))


def _rms(x, g):
    return x * lax.rsqrt(jnp.mean(x * x, -1, keepdims=True) + EPS) * g


def _iota(shape, dim):
    return lax.broadcasted_iota(jnp.int32, shape, dim)


MXU_DIM = 256
MATMUL_VMEM_BYTES = 44 * 1024 * 1024
MXU_FLOPS = 9.0e14
HBM_BYTES_PER_S = 3.0e12
STEP_SECONDS = 0.35e-6


def _matmul_tiles(m, n, kk, a_item, b_item, o_item, has_res):
    def divisors(d, cands):
        return [c for c in cands if d % c == 0] or [d]

    def pad(d):
        return -(-d // MXU_DIM) * MXU_DIM

    best = None
    for tm in divisors(m, (1024, 512, 256, 128)):
        for tn in divisors(n, (2816, 2048, 1408, 1024, 512, 256, 128)):
            for tk in divisors(kk, (2816, 2048, 1408, 1024, 512, 256, 128)):
                o_bytes = tm * tn * (o_item + (4 if has_res else 0))
                if 2 * (tm * tk * a_item + tk * tn * b_item + o_bytes) + tm * tn * 4 > MATMUL_VMEM_BYTES:
                    continue
                nk = kk // tk
                a_bytes = tm * tk * a_item / (1 if nk > 1 else n // tn)
                seconds = max(2 * tm * pad(tn) * pad(tk) / MXU_FLOPS, (a_bytes + tk * tn * b_item + o_bytes / nk) / HBM_BYTES_PER_S)
                total = (m // tm) * (n // tn) * nk * (seconds + STEP_SECONDS)
                if best is None or total < best[0]:
                    best = (total, tm, tn, tk)
    return best[1:]


def _matmul(a, b, *, ta=False, tb=False, out_dtype=F32, alpha=1.0, bias=None, res=None, name, freight=None,
            extras=(), epilogue=None):
    cargo = freight.take(name) if freight is not None else []
    out_dtypes = tuple(out_dtype) if epilogue is not None else (out_dtype,)
    n_out, n_ex = len(out_dtypes), len(extras)
    if ta:
        kk, m = a.shape
    else:
        m, kk = a.shape
    if tb:
        n, k2 = b.shape
    else:
        k2, n = b.shape
    assert kk == k2, (a.shape, b.shape, ta, tb)
    tile_item = sum(jnp.dtype(d).itemsize for d in out_dtypes) + sum(e.dtype.itemsize for e in extras)
    tm, tn, tk = _matmul_tiles(m, n, kk, a.dtype.itemsize, b.dtype.itemsize, tile_item, res is not None)
    nk = kk // tk
    nc = len(cargo)
    has_bias, has_res = bias is not None, res is not None

    def body(*refs):
        a_ref, b_ref = refs[0], refs[1]
        pos = 2
        bias_ref = res_ref = None
        if has_bias:
            bias_ref = refs[pos]
            pos += 1
        if has_res:
            res_ref = refs[pos]
            pos += 1
        extra_refs = refs[pos:pos + n_ex]
        pos += n_ex
        cargo_in = refs[pos:pos + nc]
        pos += nc
        o_refs = refs[pos:pos + n_out]
        pos += n_out
        cargo_out = refs[pos:pos + nc]
        acc_ref = refs[pos + nc]
        sems = refs[pos + nc + 1:]
        i, j, k = pl.program_id(0), pl.program_id(1), pl.program_id(2)

        if nc:
            @pl.when((i == 0) & (j == 0) & (k == 0))
            def _():
                for c, (_, _, kind) in enumerate(cargo):
                    _EX_START[kind](cargo_in[c], cargo_out[c], *sems[3 * c:3 * c + 3])

        @pl.when(k == 0)
        def _():
            acc_ref[...] = jnp.zeros_like(acc_ref)

        acc_ref[...] += _dg(a_ref[...].astype(BF16), b_ref[...].astype(BF16), 0 if ta else 1, 1 if tb else 0)

        @pl.when(k == nk - 1)
        def _():
            r = acc_ref[...]
            if alpha != 1.0:
                r = r * alpha
            if has_bias:
                r = r + bias_ref[...]
            if has_res:
                r = r + res_ref[...]
            vals = epilogue(r, *[e[...] for e in extra_refs]) if epilogue is not None else (r,)
            for o_ref, val in zip(o_refs, vals, strict=True):
                o_ref[...] = val.astype(o_ref.dtype)

        if nc:
            @pl.when((i == m // tm - 1) & (j == n // tn - 1) & (k == nk - 1))
            def _():
                for c, (_, _, kind) in enumerate(cargo):
                    _EX_FINISH[kind](cargo_in[c], cargo_out[c], *sems[3 * c:3 * c + 3])

    in_specs = [
        pl.BlockSpec((tk, tm), lambda i, j, k: (k, i)) if ta else pl.BlockSpec((tm, tk), lambda i, j, k: (i, k)),
        pl.BlockSpec((tn, tk), lambda i, j, k: (j, k)) if tb else pl.BlockSpec((tk, tn), lambda i, j, k: (k, j)),
    ]
    args = [a, b]
    if has_bias:
        in_specs.append(pl.BlockSpec((1, tn), lambda i, j, k: (0, j)))
        args.append(bias)
    if has_res:
        in_specs.append(pl.BlockSpec((tm, tn), lambda i, j, k: (i, j)))
        args.append(res)
    anyspec = pl.BlockSpec(memory_space=pl.ANY)
    tile = pl.BlockSpec((tm, tn), lambda i, j, k: (i, j))
    for e in extras:
        assert e.shape == (m, n), (e.shape, m, n, name)
    out_specs = [tile] * n_out + [anyspec] * nc
    out_shape = [jax.ShapeDtypeStruct((m, n), d) for d in out_dtypes] + [_ex_out_shape(arr, kind) for _, arr, kind in cargo]
    scratch = [pltpu.VMEM((tm, tn), F32)]
    for _ in cargo:
        scratch += _ex_scratch()
    res_all = pl.pallas_call(
        body, name=name, grid=(m // tm, n // tn, nk), in_specs=in_specs + [tile] * n_ex + [anyspec] * nc,
        out_specs=out_specs, out_shape=out_shape, scratch_shapes=scratch,
        compiler_params=_params("arbitrary", "arbitrary", "arbitrary") if nc else _params("parallel", "parallel", "arbitrary"),
    )(*args, *extras, *[arr for _, arr, _ in cargo])
    for (key, _, _), landed in zip(cargo, res_all[n_out:]):
        freight.landed[key] = landed
    return tuple(res_all[:n_out]) if epilogue is not None else res_all[0]


def _row(arr, width=None, cidx=0, shift=0):
    return (arr, arr.shape[-1] if width is None else width, cidx, shift)


def _rows(fn, rows, consts, outs, accs=(), *, tm, ncol=1, n_rows=None, name):
    n = rows[0][0].shape[-2] if n_rows is None else n_rows
    assert n % tm == 0, (n, tm, name)
    nb = n // tm
    n_in = len(rows) + len(consts)
    n_out = len(outs)

    def cfun(cidx):
        return cidx if callable(cidx) else (lambda j, c=cidx: c)

    in_specs = []
    for arr, width, cidx, shift in rows:
        cf = cfun(cidx)
        if callable(shift):
            rf = shift
        elif shift:
            rf = lambda i, s=shift: jnp.clip(i + s, 0, nb - 1)
        else:
            rf = lambda i: i
        if arr.ndim == 3:
            in_specs.append(pl.BlockSpec((arr.shape[0], tm, width), lambda i, j, rf=rf, cf=cf: (0, rf(i), cf(j))))
        else:
            in_specs.append(pl.BlockSpec((tm, width), lambda i, j, rf=rf, cf=cf: (rf(i), cf(j))))
    for c in consts:
        in_specs.append(pl.BlockSpec(c.shape, lambda i, j, nd=c.ndim: (0,) * nd))
    out_shape, out_specs = [], []
    for o in outs:
        width, dt = o[0], o[1]
        total = o[2] if len(o) > 2 else width * ncol
        out_shape.append(jax.ShapeDtypeStruct((n, total), dt))
        out_specs.append(pl.BlockSpec((tm, width), lambda i, j: (i, j)))
    for shp in accs:
        out_shape.append(jax.ShapeDtypeStruct(shp, F32))
        out_specs.append(pl.BlockSpec(shp, lambda i, j, nd=len(shp): (0,) * nd))

    def body(*refs):
        i, j = pl.program_id(0), pl.program_id(1)
        ins = [r[...] for r in refs[:n_in]]
        ro, ao = fn(i, j, *ins)
        for r, val in zip(refs[n_in:n_in + n_out], ro):
            r[...] = val.astype(r.dtype)
        if accs:
            acc_refs = refs[n_in + n_out:]

            @pl.when((i == 0) & (j == 0))
            def _():
                for r in acc_refs:
                    r[...] = jnp.zeros_like(r)

            for r, val in zip(acc_refs, ao):
                r[...] += val

    res = pl.pallas_call(
        body, name=name, grid=(nb, ncol), in_specs=in_specs, out_specs=out_specs, out_shape=out_shape,
        compiler_params=_params("arbitrary", "arbitrary"),
    )(*[r[0] for r in rows], *consts)
    return res


def _rms_fwd(x, g, name):
    def fn(i, j, x, g):
        return (_rms(x, g),), ()
    return _rows(fn, [_row(x)], [g], [(x.shape[1], BF16)], tm=_pick(x.shape[0], (512, 256)), name=name)[0]


def _rms_bwd(x, dh, dres, g, name):
    def fn(i, j, x, dh, dres, g):
        _, vjp = jax.vjp(_rms, x, g)
        dx, dg = vjp(dh)
        return (dx + dres, dx + dres), (dg,)
    dx, dx16, dg = _rows(fn, [_row(x), _row(dh), _row(dres)], [g], [(x.shape[1], F32), (x.shape[1], BF16)], [g.shape],
                         tm=_pick(x.shape[0], (256,)), name=name)
    return (dx, dx16), dg


def _rms_bwd_gain(x, dh, g, name):
    def fn(i, j, x, dh, g):
        _, vjp = jax.vjp(_rms, x, g)
        return (), (vjp(dh)[1],)
    return _rows(fn, [_row(x), _row(dh)], [g], [], [g.shape], tm=_pick(x.shape[0], (256,)), name=name)[0]


def _swiglu_tile(u, g):
    return u, _silu(g.astype(F32)) * u


def _swiglu_bwd_tile(da, g, u):
    g, u = g.astype(F32), u.astype(F32)
    s = _sigmoid(g)
    return da * u * (s * (1.0 + g * (1.0 - s))), da * (g * s)


def _gmlp_fn(uv, lng, lnb, ws0, ws1, ws2, ws3, bst):
    ws = (ws0, ws1, ws2, ws3)
    q = uv.shape[0]
    u = _gelu(uv[:, :D_MODEL])
    v = _gelu(uv[:, D_MODEL:])
    tril = _iota((q, q), 0) >= _iota((q, q), 1)
    outs = []
    for g in range(GM_GROUPS):
        sl = slice(GM_GDIM * g, GM_GDIM * (g + 1))
        vg = v[:, sl]
        mu = jnp.mean(vg, -1, keepdims=True)
        var = jnp.mean(jnp.square(vg - mu), -1, keepdims=True)
        vn = (vg - mu) * lax.rsqrt(var + EPS) * lng[:, sl] + lnb[:, sl]
        w = jnp.where(tril, ws[g], 0.0)
        bcol = jnp.sum(bst * (_iota((1, 128), 1) == g).astype(F32), axis=1, keepdims=True)
        outs.append(u[:, sl] * (_bdot(w, vn) + bcol))
    return jnp.concatenate(outs, axis=1)


def _gmlp_fwd(uv, lng, lnb, ws, bst, name):
    def fn(i, j, uv, lng, lnb, ws, bst):
        return (_gmlp_fn(uv, lng, lnb, ws[0], ws[1], ws[2], ws[3], bst),), ()
    return _rows(fn, [_row(uv, 2 * D_MODEL, 0)], [lng, lnb, ws, bst], [(D_MODEL, BF16)], tm=CHUNK, name=name)[0]


def _gmlp_bwd(uv, da, lng, lnb, ws, bst, name):
    def fn(i, j, uv, da, lng, lnb, ws, bst):
        _, vjp = jax.vjp(_gmlp_fn, uv, lng, lnb, ws[0], ws[1], ws[2], ws[3], bst)
        duv, dlng, dlnb, d0, d1, d2, d3, dbst = vjp(da)
        return (duv,), (dlng, dlnb, jnp.stack([d0, d1, d2, d3]), dbst)
    return _rows(fn, [_row(uv, 2 * D_MODEL, 0), _row(da)], [lng, lnb, ws, bst], [(2 * D_MODEL, BF16)],
                 [lng.shape, lnb.shape, ws.shape, bst.shape], tm=CHUNK, name=name)


def _conv_taps(scr, x, halo, first_row):
    q = x.shape[0]
    scr[pl.ds(0, 8), :] = halo
    scr[pl.ds(8, q), :] = x
    return [scr[pl.ds(first_row + k, q), :] for k in range(4)]


def _halo_before(q, c, col):
    return pl.BlockSpec((8, c), lambda i: (jnp.maximum(i * (q // 8) - 1, 0), col))


def _conv_fwd(proj, w, b, name):
    n, c = proj.shape[0], w.shape[1]
    xbc = proj
    q = CHUNK

    def body(x_ref, xp_ref, w_ref, b_ref, o_ref, scr):
        i = pl.program_id(0)
        halo = jnp.where(i > 0, xp_ref[...], 0.0)
        taps = _conv_taps(scr, x_ref[...], halo, 5)
        pre = b_ref[...] + sum(taps[k] * w_ref[pl.ds(k, 1), :] for k in range(4))
        o_ref[...] = _silu(pre)

    return pl.pallas_call(
        body, name=name, grid=(n // q,),
        in_specs=[pl.BlockSpec((q, c), lambda i: (i, XBC_COL)), _halo_before(q, c, XBC_COL),
                  pl.BlockSpec(w.shape, lambda i: (0, 0)), pl.BlockSpec(b.shape, lambda i: (0, 0))],
        out_specs=pl.BlockSpec((q, c), lambda i: (i, 0)), out_shape=jax.ShapeDtypeStruct((n, c), F32),
        scratch_shapes=[pltpu.VMEM((q + 8, c), F32)], compiler_params=_params("arbitrary"),
    )(xbc, xbc, w, b)


def _conv_bwd_pre(xbc, dy, w, b, name):
    n, c = xbc.shape[0], w.shape[1]
    q = CHUNK

    def body(x_ref, xp_ref, dy_ref, w_ref, b_ref, dp_ref, dw_ref, db_ref, scr):
        i = pl.program_id(0)
        halo = jnp.where(i > 0, xp_ref[...], 0.0)
        taps = _conv_taps(scr, x_ref[...], halo, 5)
        pre = b_ref[...] + sum(taps[k] * w_ref[pl.ds(k, 1), :] for k in range(4))
        s = _sigmoid(pre)
        dp = dy_ref[...] * (s * (1.0 + pre * (1.0 - s)))
        dp_ref[...] = dp

        @pl.when(i == 0)
        def _():
            dw_ref[...] = jnp.zeros_like(dw_ref)
            db_ref[...] = jnp.zeros_like(db_ref)

        db_ref[...] += jnp.sum(dp, axis=0, keepdims=True)
        for k in range(4):
            dw_ref[pl.ds(k, 1), :] += jnp.sum(dp * taps[k], axis=0, keepdims=True)

    return pl.pallas_call(
        body, name=name, grid=(n // q,),
        in_specs=[pl.BlockSpec((q, c), lambda i: (i, XBC_COL)), _halo_before(q, c, XBC_COL),
                  pl.BlockSpec((q, c), lambda i: (i, 0)),
                  pl.BlockSpec(w.shape, lambda i: (0, 0)), pl.BlockSpec(b.shape, lambda i: (0, 0))],
        out_specs=[pl.BlockSpec((q, c), lambda i: (i, 0)), pl.BlockSpec(w.shape, lambda i: (0, 0)),
                   pl.BlockSpec(b.shape, lambda i: (0, 0))],
        out_shape=[jax.ShapeDtypeStruct((n, c), F32), jax.ShapeDtypeStruct(w.shape, F32), jax.ShapeDtypeStruct(b.shape, F32)],
        scratch_shapes=[pltpu.VMEM((q + 8, c), F32)], compiler_params=_params("arbitrary"),
    )(xbc, xbc, dy, w, b)


def _conv_bwd_x(dpre, w, name):
    n, c = dpre.shape
    q = CHUNK
    nb = n // q

    def body(d_ref, dn_ref, w_ref, o_ref, scr):
        i = pl.program_id(0)
        scr[pl.ds(0, q), :] = d_ref[...]
        scr[pl.ds(q, 8), :] = jnp.where(i < nb - 1, dn_ref[...], 0.0)
        o_ref[...] = sum(scr[pl.ds(3 - k, q), :] * w_ref[pl.ds(k, 1), :] for k in range(4)).astype(o_ref.dtype)

    return pl.pallas_call(
        body, name=name, grid=(nb,),
        in_specs=[pl.BlockSpec((q, c), lambda i: (i, 0)),
                  pl.BlockSpec((8, c), lambda i: (jnp.minimum((i + 1) * (q // 8), nb * (q // 8) - 1), 0)),
                  pl.BlockSpec(w.shape, lambda i: (0, 0))],
        out_specs=pl.BlockSpec((q, c), lambda i: (i, 0)), out_shape=jax.ShapeDtypeStruct((n, c), BF16),
        scratch_shapes=[pltpu.VMEM((q + 8, c), F32)], compiler_params=_params("arbitrary"),
    )(dpre, dpre, w)


def _ssd_chunk(xs, bc, dtr, dtb, alog, dsk, expand, hp):
    q = xs.shape[0]
    tril = _iota((q, q), 0) >= _iota((q, q), 1)
    dt = _softplus(dtr + dtb)
    a = dt * (-jnp.exp(alog))
    cs = _running_sum(tril.astype(F32), a)
    cs_t = cs.T
    last = (_iota((q, 1), 0) == q - 1).astype(F32)
    gw = SSD_HEADS // SSD_GROUPS * SSD_HEAD_DIM
    lane = _iota((1, gw), 1)
    ys, hs = [], []
    for g in range(SSD_GROUPS):
        sl = slice(gw * g, gw * (g + 1))
        eg = expand[:, sl]
        dt_e, cs_e = _spread(dt, eg), _spread(cs, eg)
        cl_e = jnp.sum(cs_e * last, axis=0, keepdims=True)
        d_e = jnp.sum(_spread(jnp.broadcast_to(dsk, (8, 128)), eg), axis=0, keepdims=True) * 0.125
        xg = xs[:, sl]
        xdt = xg * dt_e
        bg = bc[:, SSD_STATE * g:SSD_STATE * (g + 1)]
        cg = bc[:, SSD_GROUPS * SSD_STATE + SSD_STATE * g:SSD_GROUPS * SSD_STATE + SSD_STATE * (g + 1)]
        cb = _bdot_nt(cg, bg)
        ms, xm = [], []
        for r in range(SSD_HEADS // SSD_GROUPS):
            h = g * (SSD_HEADS // SSD_GROUPS) + r
            col = jnp.sum(cs * (_iota((1, 128), 1) == h).astype(F32), axis=1, keepdims=True)
            row = jnp.sum(cs_t * (_iota((128, 1), 0) == h).astype(F32), axis=0, keepdims=True)
            decay = jnp.where(tril, jnp.exp(jnp.where(tril, col - row, 0.0)), 0.0)
            ms.append(cb * decay)
            xm.append(xdt * ((lane >= SSD_HEAD_DIM * r) & (lane < SSD_HEAD_DIM * (r + 1))).astype(F32))
        y_diag = _bdot(jnp.concatenate(ms, axis=1), jnp.concatenate(xm, axis=0))
        hg = hp[:, sl]
        y_off = _bdot(cg, hg) * jnp.exp(cs_e)
        states = _bdot_tn(bg, xdt * jnp.exp(cl_e - cs_e))
        hs.append(hg * jnp.exp(cl_e) + states)
        ys.append(y_diag + y_off + xg * d_e)
    return jnp.concatenate(ys, axis=1), jnp.concatenate(hs, axis=1)


def _ssd_fwd(xbc, dtr, dtb, alog, dsk, expand, name):
    n = xbc.shape[0]
    q, w = CHUNK, D_MODEL
    nc = n // q

    def body(xs_ref, bc_ref, dtr_ref, dtb_ref, alog_ref, dsk_ref, e_ref, y_ref, hp_ref, h_scr):
        @pl.when(pl.program_id(0) == 0)
        def _():
            h_scr[...] = jnp.zeros_like(h_scr)

        hp = h_scr[...]
        y, hn = _ssd_chunk(xs_ref[...], bc_ref[...], dtr_ref[...], dtb_ref[...], alog_ref[...], dsk_ref[...], e_ref[...], hp)
        y_ref[...] = y
        hp_ref[...] = hp
        h_scr[...] = hn

    small = pl.BlockSpec((1, 128), lambda c: (0, 0))
    return pl.pallas_call(
        body, name=name, grid=(nc,),
        in_specs=[pl.BlockSpec((q, w), lambda c: (c, 0)), pl.BlockSpec((q, 1024), lambda c: (c, 2)),
                  pl.BlockSpec((q, 128), lambda c: (c, 0)), small, small, small, pl.BlockSpec((128, w), lambda c: (0, 0))],
        out_specs=[pl.BlockSpec((q, w), lambda c: (c, 0)), pl.BlockSpec((SSD_STATE, w), lambda c: (c, 0))],
        out_shape=[jax.ShapeDtypeStruct((n, w), F32), jax.ShapeDtypeStruct((nc * SSD_STATE, w), F32)],
        scratch_shapes=[pltpu.VMEM((SSD_STATE, w), F32)], compiler_params=_params("arbitrary"),
    )(xbc, xbc, dtr, dtb, alog, dsk, expand)


def _ssd_bwd(xbc, dtr, dtb, alog, dsk, expand, hp_all, dy, name):
    n = xbc.shape[0]
    q, w = CHUNK, D_MODEL
    nc = n // q

    def body(xs_ref, bc_ref, dtr_ref, dtb_ref, alog_ref, dsk_ref, e_ref, hp_ref, dy_ref,
             dxbc_ref, ddtr_ref, ddtb_ref, dalog_ref, ddsk_ref, dh_scr):
        @pl.when(pl.program_id(0) == 0)
        def _():
            dh_scr[...] = jnp.zeros_like(dh_scr)
            ddtb_ref[...] = jnp.zeros_like(ddtb_ref)
            dalog_ref[...] = jnp.zeros_like(dalog_ref)
            ddsk_ref[...] = jnp.zeros_like(ddsk_ref)

        e = e_ref[...]
        _, vjp = jax.vjp(lambda xs, bc, dtr, dtb, alog, dsk, hp: _ssd_chunk(xs, bc, dtr, dtb, alog, dsk, e, hp),
                         xs_ref[...], bc_ref[...], dtr_ref[...], dtb_ref[...], alog_ref[...], dsk_ref[...], hp_ref[...])
        dxs, dbc, ddtr, ddtb, dalog, ddsk, dhp = vjp((dy_ref[...], dh_scr[...]))
        dxbc_ref[...] = jnp.concatenate([dxs, dbc], axis=1)
        ddtr_ref[...] = ddtr
        ddtb_ref[...] += ddtb
        dalog_ref[...] += dalog
        ddsk_ref[...] += ddsk
        dh_scr[...] = dhp

    rev = lambda c: nc - 1 - c
    small = pl.BlockSpec((1, 128), lambda c: (0, 0))
    return pl.pallas_call(
        body, name=name, grid=(nc,),
        in_specs=[pl.BlockSpec((q, w), lambda c: (rev(c), 0)), pl.BlockSpec((q, 1024), lambda c: (rev(c), 2)),
                  pl.BlockSpec((q, 128), lambda c: (rev(c), 0)), small, small, small,
                  pl.BlockSpec((128, w), lambda c: (0, 0)), pl.BlockSpec((SSD_STATE, w), lambda c: (rev(c), 0)),
                  pl.BlockSpec((q, w), lambda c: (rev(c), 0))],
        out_specs=[pl.BlockSpec((q, w + 1024), lambda c: (rev(c), 0)),
                   pl.BlockSpec((q, 128), lambda c: (rev(c), 0)), small, small, small],
        out_shape=[jax.ShapeDtypeStruct((n, w + 1024), F32), jax.ShapeDtypeStruct((n, 128), F32),
                   jax.ShapeDtypeStruct((1, 128), F32), jax.ShapeDtypeStruct((1, 128), F32), jax.ShapeDtypeStruct((1, 128), F32)],
        scratch_shapes=[pltpu.VMEM((SSD_STATE, w), F32)], compiler_params=_params("arbitrary"),
    )(xbc, xbc, dtr, dtb, alog, dsk, expand, hp_all, dy)


def _gate_fn(y, z, g):
    outs = []
    for k in range(SSD_GROUPS):
        sl = slice(512 * k, 512 * (k + 1))
        yg = y[:, sl] * _silu(z[:, sl])
        outs.append(yg * lax.rsqrt(jnp.mean(yg * yg, -1, keepdims=True) + EPS) * g[:, sl])
    return jnp.concatenate(outs, axis=1)


def _gate_fwd(y, z, g, name):
    def fn(i, j, y, z, g):
        return (_gate_fn(y, z, g),), ()
    return _rows(fn, [_row(y), _row(z, D_MODEL, Z_COL)], [g], [(D_MODEL, BF16)], tm=_pick(y.shape[0], (256, 128)), name=name)[0]


def _gate_bwd(y, z, db, g, name):
    def fn(i, j, y, z, db, g):
        _, vjp = jax.vjp(_gate_fn, y, z, g)
        dy, dz, dg = vjp(db)
        return (dy, dz), (dg,)
    return _rows(fn, [_row(y), _row(z, D_MODEL, Z_COL), _row(db)], [g], [(D_MODEL, F32), (D_MODEL, BF16)], [g.shape],
                 tm=_pick(y.shape[0], (256, 128)), name=name)


def _rope(x, tab, sign=1.0):
    return x * tab[:, 0:128] + sign * (pltpu.roll(x, 128 - ROT_HALF, 1) * tab[:, 128:256] + pltpu.roll(x, ROT_HALF, 1) * tab[:, 256:384])


def _sink_softmax_parts(s, sink):
    m = jnp.maximum(jnp.max(s, -1, keepdims=True), sink)
    p = jnp.exp(s - m)
    e_sink = jnp.exp(sink - m)
    inv = 1.0 / (jnp.sum(p, -1, keepdims=True) + e_sink)
    return p * inv, e_sink * inv


@jax.custom_vjp
def _sink_softmax(s, sink):
    return _sink_softmax_parts(s, sink)[0]


def _sink_softmax_fwd(s, sink):
    pr, pr_sink = _sink_softmax_parts(s, sink)
    return pr, (pr, pr_sink)


def _sink_softmax_bwd(r, g):
    pr, pr_sink = r
    t = jnp.sum(g * pr, -1, keepdims=True)
    return pr * (g - t), -pr_sink * t


_sink_softmax.defvjp(_sink_softmax_fwd, _sink_softmax_bwd)


def _swa_core(qq, kc, vc, sinks, kv_head, first):
    q = kc.shape[0] // 2
    s = _bdot_nt(qq, kc) * ATT_SCALE
    rows = ATT_REP * q
    iq = _iota((rows, 2 * q), 0) & (q - 1)
    js = _iota((rows, 2 * q), 1)
    rel = iq + q - js
    mask = (rel >= 0) & (rel < q) & ((js >= q) | jnp.logical_not(first))
    s = jnp.where(mask, s, -jnp.inf)
    rep = lax.shift_right_logical(_iota((rows, 1), 0), q.bit_length() - 1)
    sink = jnp.zeros((rows, 1), F32)
    for r in range(ATT_REP):
        s_r = jnp.sum(sinks * (_iota((1, 128), 1) == kv_head * ATT_REP + r).astype(F32), axis=1, keepdims=True)
        sink = sink + jnp.where(rep == r, s_r, 0.0)
    return _bdot(_sink_softmax(s, sink), vc)


def _swa_prep(q, kv, kvp, tab, tabp, kv_head):
    w = HEAD_PAD
    kc = jnp.concatenate([_rope(kvp[:, w * kv_head:w * (kv_head + 1)], tabp), _rope(kv[:, w * kv_head:w * (kv_head + 1)], tab)], axis=0)
    o = ATT_KV_HEADS * w
    vc = jnp.concatenate([kvp[:, o + w * kv_head:o + w * (kv_head + 1)], kv[:, o + w * kv_head:o + w * (kv_head + 1)]], axis=0)
    qq = jnp.concatenate([_rope(q[:, w * (kv_head * ATT_REP + r):w * (kv_head * ATT_REP + r + 1)], tab) for r in range(ATT_REP)], axis=0)
    return qq, kc, vc


def _swa_fwd(q, kv, tab, sinks, name):
    def fn(i, j, q, kv, kvp, tab, tabp, sinks):
        first = i == 0
        outs = []
        for h in range(ATT_KV_HEADS):
            qq, kc, vc = _swa_prep(q, kv, kvp, tab, tabp, h)
            o = _swa_core(qq, kc, vc, sinks, h, first)
            outs += [o[CHUNK * r:CHUNK * (r + 1)] for r in range(ATT_REP)]
        return (jnp.concatenate(outs, axis=1),), ()
    return _rows(fn, [_row(q), _row(kv), _row(kv, shift=-1), _row(tab), _row(tab, shift=-1)], [sinks],
                 [(q.shape[1], BF16)], tm=CHUNK, name=name)[0]


def _swa_bwd(q, kv, tab, sinks, do, name):
    w = HEAD_PAD

    def fn(i, j, q, kv, kvp, tab, tabp, do, sinks):
        first = i == 0
        dqs, dkc, dkp, dvc, dvp = [], [], [], [], []
        dsink = jnp.zeros_like(sinks)
        for h in range(ATT_KV_HEADS):
            qq, kc, vc = _swa_prep(q, kv, kvp, tab, tabp, h)
            dout = jnp.concatenate([do[:, w * (h * ATT_REP + r):w * (h * ATT_REP + r + 1)] for r in range(ATT_REP)], axis=0)
            _, vjp = jax.vjp(lambda a, b, c, s: _swa_core(a, b, c, s, h, first), qq, kc, vc, sinks)
            dqq, dk, dv, ds = vjp(dout.astype(F32))
            dsink = dsink + ds
            dqs += [_rope(dqq[CHUNK * r:CHUNK * (r + 1)], tab, -1.0) for r in range(ATT_REP)]
            dkp.append(_rope(dk[:CHUNK], tabp, -1.0))
            dkc.append(_rope(dk[CHUNK:], tab, -1.0))
            dvp.append(dv[:CHUNK])
            dvc.append(dv[CHUNK:])
        return (jnp.concatenate(dqs, axis=1), jnp.concatenate(dkc + dvc, axis=1), jnp.concatenate(dkp + dvp, axis=1)), (dsink,)
    return _rows(fn, [_row(q), _row(kv), _row(kv, shift=-1), _row(tab), _row(tab, shift=-1), _row(do)], [sinks],
                 [(q.shape[1], BF16), (kv.shape[1], F32), (kv.shape[1], F32)], [sinks.shape], tm=CHUNK, name=name)


def _swa_combine(dkv_cur, dkv_prev, dq, name):
    nb = dq.shape[0] // CHUNK

    def fn(i, j, cur, nxt, dq):
        dkv = (cur + jnp.where(i < nb - 1, nxt, 0.0)).astype(BF16)
        return (dkv,), (jnp.sum(dq.astype(F32), axis=0, keepdims=True), jnp.sum(dkv.astype(F32), axis=0, keepdims=True))
    return _rows(fn, [_row(dkv_cur), _row(dkv_prev, shift=1), _row(dq)], [], [(dkv_cur.shape[1], BF16)],
                 [(1, dq.shape[1]), (1, dkv_cur.shape[1])], tm=CHUNK, name=name)


def _xattn_fn(q, k, v):
    outs = []
    for h in range(X_HEADS):
        sl = slice(X_HEAD_DIM * h, X_HEAD_DIM * (h + 1))
        s = _bdot_nt(q[:, sl], k[:, sl]) * X_SCALE
        p = jnp.exp(s - jnp.max(s, -1, keepdims=True))
        outs.append(_bdot(p / jnp.sum(p, -1, keepdims=True), v[:, sl]))
    return jnp.concatenate(outs, axis=1)


def _xattn_fwd(q, k, v, name):
    def fn(i, j, q, k, v):
        return (_xattn_fn(q, k, v),), ()
    return _rows(fn, [_row(q)], [k, v], [(q.shape[1], BF16)], tm=_pick(q.shape[0], (512, 256)), name=name)[0]


def _xattn_bwd(q, k, v, do, name):
    def fn(i, j, q, do, k, v):
        _, vjp = jax.vjp(_xattn_fn, q, k, v)
        dq, dk, dv = vjp(do)
        return (dq,), (dk, dv)
    return _rows(fn, [_row(q), _row(do)], [k, v], [(q.shape[1], BF16)], [k.shape, v.shape],
                 tm=_pick(q.shape[0], (512, 256)), name=name)


def _loss_fn(x, t, g):
    return 0.5 * jnp.sum(jnp.mean(jnp.square(_rms(x, g) - t), axis=-1))


def _loss_fwd_bwd(x, t, g, name):
    def fn(i, j, x, t, g):
        loss, vjp = jax.vjp(_loss_fn, x, t, g)
        dx, _, dg = vjp(jnp.ones((), F32))
        return (dx, dx), (jnp.broadcast_to(loss, (8, 128)), dg)
    dx, dx16, loss, dg = _rows(fn, [_row(x), _row(t)], [g], [(x.shape[1], F32), (x.shape[1], BF16)], [(8, 128), g.shape],
                               tm=_pick(x.shape[0], (256,)), name=name)
    return loss, (dx, dx16), dg


def _adamw(parts, w, m, v, name):
    plist = list(parts) if isinstance(parts, (list, tuple)) else [parts]
    nl = len(plist)
    r, c = w.shape
    tm = _pick(r // nl, (128, 64, 32, 16, 8))
    nbl = r // nl // tm

    def fn(i, j, *blocks):
        parts, (w, m, v) = blocks[0], blocks[nl:]
        for l in range(1, nl):
            parts = jnp.where(i >= l * nbl, blocks[l], parts)
        g = parts[0].astype(F32)
        for k in range(1, N_DEV):
            g = g + parts[k].astype(F32)
        m2 = ADAM_B1 * m + (1.0 - ADAM_B1) * g
        v2 = ADAM_B2 * v + (1.0 - ADAM_B2) * jnp.square(g)
        m_hat = m2 / (1.0 - ADAM_B1 ** ADAM_STEP)
        v_hat = v2 / (1.0 - ADAM_B2 ** ADAM_STEP)
        delta = -ADAM_LR * (m_hat / (jnp.sqrt(v_hat) + ADAM_EPS) + ADAM_WD * w)
        return (g, delta, m2, v2), ()
    prow = [_row(p, shift=(lambda i, l=l: jnp.clip(i - l * nbl, 0, nbl - 1))) for l, p in enumerate(plist)]
    return _rows(fn, prow + [_row(w), _row(m), _row(v)], [], [(c, F32)] * 4, tm=tm, n_rows=r, name=name)


def _peer(k):
    return (k // 4, (k // 2) % 2, k % 2)


GATHER, SCATTER = "gather", "scatter"


def _ex_scratch():
    return [pltpu.SemaphoreType.DMA((N_DEV - 1,)), pltpu.SemaphoreType.DMA((N_DEV - 1,)), pltpu.SemaphoreType.DMA]


def _ex_out_shape(x, kind):
    return jax.ShapeDtypeStruct((N_DEV,) + x.shape[-2:], x.dtype)


def _gather_plan():
    x, y, c = lax.axis_index("x"), lax.axis_index("y"), lax.axis_index("c")
    return (x, y, c), (x, y, 1 - c), [(1 - x, y), (x, 1 - y), (1 - x, 1 - y)], c


def _gather_copy(x_ref, o_ref, send, recv, k, block, to, from_input=False):
    slot = o_ref.at[4 * block[0] + 2 * block[1] + block[2]]
    return pltpu.make_async_remote_copy(src_ref=x_ref if from_input else slot, dst_ref=slot, send_sem=send.at[k],
                                        recv_sem=recv.at[k], device_id=to, device_id_type=MESH_ID)


def _gather_start(x_ref, o_ref, send, recv, local):
    me, sib, chips, c = _gather_plan()
    pltpu.make_async_copy(x_ref, o_ref.at[4 * me[0] + 2 * me[1] + me[2]], local).start()
    _gather_copy(x_ref, o_ref, send, recv, 0, me, sib, True).start()
    for j, chip in enumerate(chips):
        _gather_copy(x_ref, o_ref, send, recv, 1 + j, me, (*chip, c), True).start()


def _gather_finish(x_ref, o_ref, send, recv, local):
    me, sib, chips, c = _gather_plan()
    for j, chip in enumerate(chips):
        _gather_copy(x_ref, o_ref, send, recv, 1 + j, (*chip, c), me).wait_recv()
        _gather_copy(x_ref, o_ref, send, recv, 4 + j, (*chip, c), sib).start()
    _gather_copy(x_ref, o_ref, send, recv, 0, sib, me).wait_recv()
    for j, chip in enumerate(chips):
        _gather_copy(x_ref, o_ref, send, recv, 4 + j, (*chip, 1 - c), me).wait_recv()
    _gather_copy(x_ref, o_ref, send, recv, 0, me, sib, True).wait_send()
    for j, chip in enumerate(chips):
        _gather_copy(x_ref, o_ref, send, recv, 1 + j, me, (*chip, c), True).wait_send()
        _gather_copy(x_ref, o_ref, send, recv, 4 + j, (*chip, c), sib).wait_send()
    pltpu.make_async_copy(x_ref, o_ref.at[4 * me[0] + 2 * me[1] + me[2]], local).wait()


def _scatter_copy(x_ref, o_ref, send, recv, d, frm, to):
    return pltpu.make_async_remote_copy(src_ref=x_ref.at[to], dst_ref=o_ref.at[frm], send_sem=send.at[d - 1],
                                        recv_sem=recv.at[d - 1], device_id=_peer(to), device_id_type=MESH_ID)


def _scatter_start(x_ref, o_ref, send, recv, local):
    me = 4 * lax.axis_index("x") + 2 * lax.axis_index("y") + lax.axis_index("c")
    pltpu.make_async_copy(x_ref.at[me], o_ref.at[me], local).start()
    for d in range(1, N_DEV):
        _scatter_copy(x_ref, o_ref, send, recv, d, me, (me + d) % N_DEV).start()


def _scatter_finish(x_ref, o_ref, send, recv, local):
    me = 4 * lax.axis_index("x") + 2 * lax.axis_index("y") + lax.axis_index("c")
    for d in range(1, N_DEV):
        _scatter_copy(x_ref, o_ref, send, recv, d, (me + N_DEV - d) % N_DEV, me).wait_recv()
    for d in range(1, N_DEV):
        _scatter_copy(x_ref, o_ref, send, recv, d, me, (me + d) % N_DEV).wait_send()
    pltpu.make_async_copy(x_ref.at[me], o_ref.at[me], local).wait()


_EX_START = {GATHER: _gather_start, SCATTER: _scatter_start}
_EX_FINISH = {GATHER: _gather_finish, SCATTER: _scatter_finish}


def _exchange(x, kind, name):
    def body(x_ref, o_ref, send, recv, local):
        _EX_START[kind](x_ref, o_ref, send, recv, local)
        _EX_FINISH[kind](x_ref, o_ref, send, recv, local)

    return pl.pallas_call(
        body, name=name, in_specs=[pl.BlockSpec(memory_space=pl.ANY)], out_specs=pl.BlockSpec(memory_space=pl.ANY),
        out_shape=_ex_out_shape(x, kind), scratch_shapes=_ex_scratch(),
    )(x)


class _Freight:
    def __init__(self):
        self.load = {}
        self.landed = {}

    def put(self, carrier, key, arr, kind):
        self.load.setdefault(carrier, []).append((key, arr, kind))

    def take(self, carrier):
        return self.load.pop(carrier, [])


def _all_gather(x, name):
    return _exchange(x.reshape(-1, x.shape[-1]), GATHER, name).reshape((N_DEV,) + x.shape)


def _all_to_all(x, name):
    return _exchange(x.reshape(N_DEV, -1, x.shape[-1]), SCATTER, name).reshape(x.shape)


def _pack(arrs):
    flat = []
    for a in arrs:
        a = a.reshape(-1).astype(F32)
        flat.append(jnp.pad(a, (0, (-a.shape[0]) % 1024)))
    return jnp.concatenate(flat).reshape(-1, 128)


def _unpack(p, shapes):
    p = p.reshape(-1)
    out, off = [], 0
    for s in shapes:
        n = 1
        for d in s:
            n *= d
        out.append(p[off:off + n].reshape(s))
        off += n + (-n) % 1024
    return out


def _cols_to_full(g):
    return g.transpose(1, 0, 2).reshape(g.shape[1], -1)


def _full_to_cols(w, shards=N_DEV):
    return w.reshape(w.shape[0], shards, -1).transpose(1, 0, 2)


SMALL = ("norm_ffn1", "norm_mix", "gm_ln_g", "gm_ln_b", "gm_ws", "gm_bs", "conv_b", "dt_bias", "a_log", "d_skip", "ssd_norm",
         "sinks", "norm_xq", "norm_mem", "norm_ffn2", "final_norm")
SHARDED = ("w_ffn1_gu", "w_ffn1_down", "w_in_even", "w_out_even", "w_qkv", "w_o_odd", "w_xq", "w_xkv", "w_xo", "w_ffn2_gu",
           "w_ffn2_down")
SMALL_SHARDED = ("conv_w", "b_qkv")
ORDER = ("norm_ffn1", "w_ffn1_gu", "w_ffn1_down", "norm_mix", "w_in_even", "gm_ln_g", "gm_ln_b", "gm_ws", "gm_bs", "conv_w",
         "conv_b", "dt_bias", "a_log", "d_skip", "ssd_norm", "w_out_even", "w_qkv", "b_qkv", "sinks", "w_o_odd", "norm_xq",
         "norm_mem", "w_xq", "w_xkv", "w_xo", "norm_ffn2", "w_ffn2_gu", "w_ffn2_down", "final_norm")


def _ffn_fwd(x, gain, wg, wu, wd, tag, freight=None):
    mm = functools.partial(_matmul, freight=freight)
    h = _rms_fwd(x, gain, f"{tag}_norm")
    g = mm(h, wg, out_dtype=BF16, name=f"{tag}_gate")
    u, act = mm(h, wu, out_dtype=(BF16, BF16), extras=[g], epilogue=_swiglu_tile, name=f"{tag}_up")
    wd = wd() if callable(wd) else wd
    y = mm(act, wd, alpha=0.5, res=x, name=f"{tag}_down")
    return y, (x, h, g, u, act), wd


def _ffn_bwd(dy, saved, gain, wg, wu, wd, tag, freight=None, ship=None):
    mm = functools.partial(_matmul, freight=freight)
    x, h, g, u, act = saved
    dy, dy16 = dy
    dg, du = mm(dy16, wd, tb=True, alpha=0.5, out_dtype=(BF16, BF16), extras=[g, u], epilogue=_swiglu_bwd_tile, name=f"{tag}_dact")
    dwd = mm(act, dy16, ta=True, alpha=0.5, out_dtype=BF16, name=f"{tag}_dwd")
    if ship is not None:
        ship(f"{tag}_dwg", "dn", dwd.reshape(N_DEV, -1, dwd.shape[1]))
    dwg = mm(h, dg, ta=True, out_dtype=BF16, name=f"{tag}_dwg")
    dwu = mm(h, du, ta=True, out_dtype=BF16, name=f"{tag}_dwu")
    if ship is not None:
        half = N_DEV // 2
        parts = jnp.concatenate([_full_to_cols(dwg, half), _full_to_cols(dwu, half)], axis=0)
        rows = parts.shape[1] // 2
        ship(f"{tag}_dh_g", "gu_a", parts[:, :rows])
        ship(f"{tag}_dh_u", "gu_b", parts[:, rows:])
    dh = mm(dg, wg, tb=True, name=f"{tag}_dh_g")
    dh = mm(du, wu, tb=True, res=dh, name=f"{tag}_dh_u")
    dx, dgain = _rms_bwd(x, dh, dy, gain, f"{tag}_dnorm")
    return dx, dgain, dwg, dwu, dwd


def _xattn_layer_fwd(x, mem, gq, gm, wq, wkv, wo, tag, freight=None):
    mm = functools.partial(_matmul, freight=freight)
    hq = _rms_fwd(x, gq, f"{tag}_normq")
    mn = _rms_fwd(mem, gm, f"{tag}_normm")
    q = mm(hq, wq, name=f"{tag}_q")
    kv = mm(mn, wkv, name=f"{tag}_kv")
    k, v = kv[:, :X_HEADS * X_HEAD_DIM], kv[:, X_HEADS * X_HEAD_DIM:]
    o = _xattn_fwd(q, k, v, f"{tag}_attn")
    y = mm(o, wo, res=x, name=f"{tag}_o")
    return y, (x, hq, mn, q, k, v, o)


def _xattn_layer_bwd(dy, saved, mem, gq, gm, wq, wkv, wo, tag, freight=None):
    mm = functools.partial(_matmul, freight=freight)
    x, hq, mn, q, k, v, o = saved
    dy, dy16 = dy
    do = mm(dy16, wo, tb=True, name=f"{tag}_do")
    dwo = mm(o, dy16, ta=True, out_dtype=BF16, name=f"{tag}_dwo")
    dq, dk, dv = _xattn_bwd(q, k, v, do, f"{tag}_dattn")
    dkv = jnp.concatenate([dk, dv], axis=1)
    dwq = mm(hq, dq, ta=True, out_dtype=BF16, name=f"{tag}_dwq")
    dwkv = mm(mn, dkv, ta=True, out_dtype=BF16, name=f"{tag}_dwkv")
    dhq = mm(dq, wq, tb=True, name=f"{tag}_dhq")
    dmn = mm(dkv, wkv, tb=True, name=f"{tag}_dmn")
    dx, dgq = _rms_bwd(x, dhq, dy, gq, f"{tag}_dnormq")
    dgm = _rms_bwd_gain(mem, dmn, gm, f"{tag}_dnormm")
    return dx, dgq, dgm, dwq, dwkv, dwo


def _even_fwd(x, weights, params, freight=None):
    mm = functools.partial(_matmul, freight=freight)
    w_main, w_dt, w_out_a, w_out_b = weights
    gain, lng, lnb, ws, bst, conv_w, conv_b, dtb, alog, dsk, ssd_norm, expand = params
    hm = _rms_fwd(x, gain, "l0_normmix")
    proj = mm(hm, w_main, name="l0_proj")
    dtr = mm(hm, w_dt, name="l0_dt")
    a_out = _gmlp_fwd(proj, lng, lnb, ws, bst, "l0_gmlp")
    xbc = _conv_fwd(proj, conv_w, conv_b, "l0_conv")
    y_ssd, hp_all = _ssd_fwd(xbc, dtr, dtb, alog, dsk, expand, "l0_ssd")
    b_out = _gate_fwd(y_ssd, proj, ssd_norm, "l0_gate")
    y = mm(a_out, w_out_a, res=x, name="l0_out_a")
    y = mm(b_out, w_out_b, res=y, name="l0_out_b")
    return y, (x, hm, proj, dtr, a_out, xbc, y_ssd, hp_all, b_out)


def _even_bwd(dx, saved, weights, params, freight=None, ship_out=None):
    mm = functools.partial(_matmul, freight=freight)
    w_main, w_dt, w_out_a, w_out_b = weights
    w_uv, w_z, w_xbc = w_main[:, :4096], w_main[:, 4096:6144], w_main[:, 6144:]
    gain, lng, lnb, ws, bst, conv_w, conv_b, dtb, alog, dsk, ssd_norm, expand = params
    x, hm, proj, dtr, a_out, xbc, y_ssd, hp_all, b_out = saved
    uv = zz = xbc_raw = proj
    dx, dx16 = dx
    da_out = mm(dx16, w_out_a, tb=True, name="l0_da")
    db_out = mm(dx16, w_out_b, tb=True, name="l0_db")
    dw_out_a = mm(a_out, dx16, ta=True, out_dtype=BF16, name="l0_dwout_a")
    dw_out_b = mm(b_out, dx16, ta=True, out_dtype=BF16, name="l0_dwout_b")
    dw_out = jnp.concatenate([dw_out_a, dw_out_b], axis=0)
    if ship_out is not None:
        ship_out(dw_out)
    dy_ssd, dzz, d_ssd_norm = _gate_bwd(y_ssd, zz, db_out, ssd_norm, "l0_dgate")
    dxbc, ddtr, d_dtb, d_alog, d_dsk = _ssd_bwd(xbc, dtr, dtb, alog, dsk, expand, hp_all, dy_ssd, "l0_dssd")
    dpre, d_conv_w, d_conv_b = _conv_bwd_pre(xbc_raw, dxbc, conv_w, conv_b, "l0_dconv_pre")
    dxbc_raw = _conv_bwd_x(dpre, conv_w, "l0_dconv_x")
    duv, d_lng, d_lnb, d_ws, d_bst = _gmlp_bwd(uv, da_out, lng, lnb, ws, bst, "l0_dgmlp")
    ddtr16 = ddtr.astype(BF16)
    dhm = mm(duv, w_uv, tb=True, name="l0_dh_uv")
    dhm = mm(dzz, w_z, tb=True, res=dhm, name="l0_dh_z")
    dhm = mm(dxbc_raw, w_xbc, tb=True, res=dhm, name="l0_dh_xbc")
    dhm = mm(ddtr16, w_dt, tb=True, res=dhm, name="l0_dh_dt")
    dw_uv = mm(hm, duv, ta=True, out_dtype=BF16, name="l0_dwuv")
    dw_z = mm(hm, dzz, ta=True, out_dtype=BF16, name="l0_dwz")
    dw_xbc = mm(hm, dxbc_raw, ta=True, out_dtype=BF16, name="l0_dwxbc")
    dw_dt = mm(hm, ddtr16, ta=True, out_dtype=BF16, name="l0_dwdt")
    dx, d_gain = _rms_bwd(x, dhm, dx, gain, "l0_dnormmix")
    small = (d_gain, d_lng, d_lnb, d_ws, d_bst, d_conv_w, d_conv_b, d_dtb, d_alog, d_dsk, d_ssd_norm)
    return dx, small, (dw_uv, dw_z, dw_xbc, dw_dt, dw_out)


def _odd_fwd(x, weights, params, freight=None):
    mm = functools.partial(_matmul, freight=freight)
    w_q, w_kv, w_o = weights
    gain, b_q, b_kv, tab, snk = params
    hm = _rms_fwd(x, gain, "l1_normmix")
    q = mm(hm, w_q, bias=b_q, name="l1_q")
    kv = mm(hm, w_kv, bias=b_kv, name="l1_kv")
    o = _swa_fwd(q, kv, tab, snk, "l1_swa")
    y = mm(o, w_o, res=x, name="l1_o")
    return y, (x, hm, q, kv, o)


def _odd_bwd(dx, saved, weights, params, freight=None):
    mm = functools.partial(_matmul, freight=freight)
    w_q, w_kv, w_o = weights
    gain, b_q, b_kv, tab, snk = params
    x, hm, q, kv, o = saved
    dx, dx16 = dx
    do = mm(dx16, w_o, tb=True, out_dtype=BF16, name="l1_do")
    dw_o = mm(o, dx16, ta=True, out_dtype=BF16, name="l1_dwo")
    dq, dkv_cur, dkv_prev, d_snk = _swa_bwd(q, kv, tab, snk, do, "l1_dswa")
    dkv, db_q, db_kv = _swa_combine(dkv_cur, dkv_prev, dq, "l1_dkv")
    dhm = mm(dq, w_q, tb=True, name="l1_dh_q")
    dhm = mm(dkv, w_kv, tb=True, res=dhm, name="l1_dh_kv")
    dw_q = mm(hm, dq, ta=True, out_dtype=BF16, name="l1_dwq")
    dw_kv = mm(hm, dkv, ta=True, out_dtype=BF16, name="l1_dwkv")
    dx, d_gain = _rms_bwd(x, dhm, dx, gain, "l1_dnormmix")
    return dx, (d_gain, db_q, db_kv, d_snk), (dw_q, dw_kv, dw_o)


def _rope_table(positions):
    seq = positions.size
    inv_freq = ROPE_THETA ** (-jnp.arange(0, 2 * ROT_HALF, 2, dtype=F32) / (2 * ROT_HALF))
    ang = positions.reshape(seq, 1).astype(F32) * inv_freq
    cos, sin, zero = jnp.cos(ang), jnp.sin(ang), jnp.zeros((seq, 128 - 2 * ROT_HALF), F32)
    z8 = jnp.zeros((seq, ROT_HALF), F32)
    return jnp.concatenate([cos, cos, zero + 1.0, -sin, z8, zero, z8, sin, zero], axis=1)


def _pad_heads_cols(w, heads):
    k = w.shape[0]
    return jnp.pad(w.reshape(k, heads, ATT_HEAD_DIM), ((0, 0), (0, 0), (0, HEAD_PAD - ATT_HEAD_DIM))).reshape(k, heads * HEAD_PAD)


def _unpad_heads_cols(w, heads):
    k = w.shape[0]
    return w.reshape(k, heads, HEAD_PAD)[:, :, :ATT_HEAD_DIM].reshape(k, heads * ATT_HEAD_DIM)


def kernel(x, mem, positions, norm_ffn1, w_ffn1_gu, w_ffn1_down, norm_mix, w_in_even, gm_ln_g, gm_ln_b, gm_ws, gm_bs, conv_w, conv_b, dt_bias, a_log, d_skip, ssd_norm, w_out_even, w_qkv, b_qkv, sinks, w_o_odd, norm_xq, norm_mem, w_xq, w_xkv, w_xo, norm_ffn2, w_ffn2_gu, w_ffn2_down, final_norm, loss_target, m_norm_ffn1, m_w_ffn1_gu, m_w_ffn1_down, m_norm_mix, m_w_in_even, m_gm_ln_g, m_gm_ln_b, m_gm_ws, m_gm_bs, m_conv_w, m_conv_b, m_dt_bias, m_a_log, m_d_skip, m_ssd_norm, m_w_out_even, m_w_qkv, m_b_qkv, m_sinks, m_w_o_odd, m_norm_xq, m_norm_mem, m_w_xq, m_w_xkv, m_w_xo, m_norm_ffn2, m_w_ffn2_gu, m_w_ffn2_down, m_final_norm, v_norm_ffn1, v_w_ffn1_gu, v_w_ffn1_down, v_norm_mix, v_w_in_even, v_gm_ln_g, v_gm_ln_b, v_gm_ws, v_gm_bs, v_conv_w, v_conv_b, v_dt_bias, v_a_log, v_d_skip, v_ssd_norm, v_w_out_even, v_w_qkv, v_b_qkv, v_sinks, v_w_o_odd, v_norm_xq, v_norm_mem, v_w_xq, v_w_xkv, v_w_xo, v_norm_ffn2, v_w_ffn2_gu, v_w_ffn2_down, v_final_norm):
    env = dict(locals())
    W = {n: env[n] for n in ORDER}
    M = {n: env["m_" + n] for n in ORDER}
    V = {n: env["v_" + n] for n in ORDER}
    seq = x.shape[1]
    x0 = x.reshape(seq, D_MODEL)
    mem2 = mem.reshape(-1, D_MODEL)
    target = loss_target.reshape(seq, D_MODEL)

    fr = _Freight()
    b16 = lambda a: a.astype(BF16)
    grp_a = {"out": w_out_even[0], "xq0": w_xq[0], "xkv0": w_xkv[0], "xo0": w_xo[0]}
    grp_b = {"qkv": w_qkv[0], "o": w_o_odd[0], "xq1": w_xq[1], "xkv1": w_xkv[1], "xo1": w_xo[1]}
    half_rows = D_MODEL // 2
    fr.put("l0f1_gate", "dn1_0", b16(w_ffn1_down[0]), GATHER)
    fr.put("l0f1_gate", "dn2_0", b16(w_ffn2_down[0]), GATHER)
    fr.put("l0f1_up", "in", b16(w_in_even[0]), GATHER)
    for key, w in grp_a.items():
        fr.put("l0f1_down", key, b16(w), GATHER)
    fr.put("l0_proj", "gu2_0", b16(w_ffn2_gu[0]), GATHER)
    fr.put("l0f2_gate", "gu1_1a", b16(w_ffn1_gu[1, :half_rows]), GATHER)
    fr.put("l0f2_gate", "dn1_1", b16(w_ffn1_down[1]), GATHER)
    fr.put("l0f2_up", "gu1_1b", b16(w_ffn1_gu[1, half_rows:]), GATHER)
    for key, w in grp_b.items():
        fr.put("l0f2_down", key, b16(w), GATHER)
    fr.put("l1f1_gate", "gu2_1a", b16(w_ffn2_gu[1, :half_rows]), GATHER)
    fr.put("l1f1_gate", "dn2_1", b16(w_ffn2_down[1]), GATHER)
    fr.put("l1f1_up", "gu2_1b", b16(w_ffn2_gu[1, half_rows:]), GATHER)
    gu_first = _exchange(b16(w_ffn1_gu[0]), GATHER, "ag_l0f1_gu")
    gs = _all_gather(_pack([conv_w, b_qkv]), "ag_small")
    gs = [_unpack(gs[k], [conv_w.shape, b_qkv.shape]) for k in range(N_DEV)]
    conv_w_full = jnp.concatenate([g[0][0] for g in gs], axis=1)
    b_qkv_full = jnp.concatenate([g[1][0] for g in gs], axis=0)

    def gate_up(g):
        return _cols_to_full(g[:N_DEV // 2]), _cols_to_full(g[N_DEV // 2:])

    down = lambda key: fr.landed[key].reshape(D_FF, D_MODEL)

    row = lambda a: a.reshape(1, -1)
    pad128 = lambda a: jnp.pad(a.reshape(1, -1), ((0, 0), (0, 128 - a.size)))
    bst = jnp.pad(gm_bs[0].T, ((0, 0), (0, 128 - GM_GROUPS)))
    ws = gm_ws[0]
    dtb, alog, dsk, snk = pad128(dt_bias), pad128(a_log), pad128(d_skip), pad128(sinks)
    expand = (jnp.arange(128)[:, None] == (jnp.arange(D_MODEL) // SSD_HEAD_DIM)[None, :]).astype(F32)
    tab = _rope_table(positions)

    wg_f1a, wu_f1a = gate_up(gu_first)
    xa, s_f1a, wd_f1a = _ffn_fwd(x0, row(norm_ffn1[0]), wg_f1a, wu_f1a, lambda: down("dn1_0"), "l0f1", fr)
    w_in = _cols_to_full(fr.landed["in"])
    n_main = 2 * GM_GROUPS * GM_GDIM + D_MODEL + XBC_WIDTH
    w_main = w_in[:, :n_main]
    w_dt = jnp.pad(w_in[:, n_main:], ((0, 0), (0, 128 - SSD_HEADS)))
    a_xq, a_xkv, a_xo = fr.landed["xq0"], fr.landed["xkv0"], fr.landed["xo0"]
    w_out = fr.landed["out"].reshape(2 * D_MODEL, D_MODEL)
    even_w = (w_main, w_dt, w_out[:D_MODEL], w_out[D_MODEL:])
    even_p = (row(norm_mix[0]), gm_ln_g, gm_ln_b, ws, bst, conv_w_full, conv_b, dtb, alog, dsk, ssd_norm, expand)
    xb, s_even = _even_fwd(xa, even_w, even_p, fr)
    x0_w = (a_xq.reshape(D_MODEL, -1), a_xkv.reshape(D_MODEL, -1), _cols_to_full(a_xo))
    xc, s_x0 = _xattn_layer_fwd(xb, mem2, row(norm_xq[0]), row(norm_mem[0]), *x0_w, "l0x", fr)
    wg_f2a, wu_f2a = gate_up(fr.landed["gu2_0"])
    xd, s_f2a, wd_f2a = _ffn_fwd(xc, row(norm_ffn2[0]), wg_f2a, wu_f2a, down("dn2_0"), "l0f2", fr)
    wg_f1b, wu_f1b = gate_up(jnp.concatenate([fr.landed["gu1_1a"], fr.landed["gu1_1b"]], axis=1))
    xe, s_f1b, wd_f1b = _ffn_fwd(xd, row(norm_ffn1[1]), wg_f1b, wu_f1b, down("dn1_1"), "l1f1", fr)
    b_xq, b_xkv, b_xo = fr.landed["xq1"], fr.landed["xkv1"], fr.landed["xo1"]
    nq = ATT_HEADS * ATT_HEAD_DIM
    wqkv = _cols_to_full(fr.landed["qkv"])
    w_o = _pad_heads_cols(fr.landed["o"].reshape(D_MODEL, D_MODEL).T, ATT_HEADS).T
    odd_w = (_pad_heads_cols(wqkv[:, :nq], ATT_HEADS), _pad_heads_cols(wqkv[:, nq:], 2 * ATT_KV_HEADS), w_o)
    odd_p = (row(norm_mix[1]), _pad_heads_cols(b_qkv_full[None, :nq], ATT_HEADS),
             _pad_heads_cols(b_qkv_full[None, nq:], 2 * ATT_KV_HEADS), tab, snk)
    xf, s_odd = _odd_fwd(xe, odd_w, odd_p, fr)
    x1_w = (b_xq.reshape(D_MODEL, -1), b_xkv.reshape(D_MODEL, -1), _cols_to_full(b_xo))
    xg, s_x1 = _xattn_layer_fwd(xf, mem2, row(norm_xq[1]), row(norm_mem[1]), *x1_w, "l1x", fr)
    wg_f2b, wu_f2b = gate_up(jnp.concatenate([fr.landed["gu2_1a"], fr.landed["gu2_1b"]], axis=1))
    xh, s_f2b, wd_f2b = _ffn_fwd(xg, row(norm_ffn2[1]), wg_f2b, wu_f2b, down("dn2_1"), "l1f2", fr)
    loss8, dx, d_final = _loss_fwd_bwd(xh, target, row(final_norm), "loss")
    loss = lax.psum(loss8[0, 0], ("x", "y", "c"))
    assert not fr.load, sorted(fr.load)

    def shipper(tag):
        return lambda carrier, key, parts: fr.put(carrier, f"{tag}_{key}", parts, SCATTER)

    dx, dn_f2b, _, _, _ = _ffn_bwd(dx, s_f2b, row(norm_ffn2[1]), wg_f2b, wu_f2b, wd_f2b, "l1f2", fr, shipper("l1f2"))
    dx, dn_xq1, dn_mem1, dwxq1, dwxkv1, dwxo1 = _xattn_layer_bwd(dx, s_x1, mem2, row(norm_xq[1]), row(norm_mem[1]), *x1_w, "l1x", fr)
    dx, (dn_mix1, db_q, db_kv, d_snk), (dw_q, dw_kv, dw_o) = _odd_bwd(dx, s_odd, odd_w, odd_p, fr)
    rows_parts = lambda g: g.reshape(N_DEV, -1, g.shape[1])
    dwqkv = jnp.concatenate([_unpad_heads_cols(dw_q, ATT_HEADS), _unpad_heads_cols(dw_kv, 2 * ATT_KV_HEADS)], axis=1)
    parts_b = {"d_qkv": _full_to_cols(dwqkv), "d_o": rows_parts(_unpad_heads_cols(dw_o.T, ATT_HEADS).T), "d_xq1": rows_parts(dwxq1),
               "d_xkv1": rows_parts(dwxkv1), "d_xo1": _full_to_cols(dwxo1)}
    for key, parts in parts_b.items():
        fr.put("l1f1_dact" if key in ("d_qkv", "d_o") else "l1f1_dwd", key, parts, SCATTER)
    dx, dn_f1b, _, _, _ = _ffn_bwd(dx, s_f1b, row(norm_ffn1[1]), wg_f1b, wu_f1b, wd_f1b, "l1f1", fr, shipper("l1f1"))
    dx, dn_f2a, _, _, _ = _ffn_bwd(dx, s_f2a, row(norm_ffn2[0]), wg_f2a, wu_f2a, wd_f2a, "l0f2", fr, shipper("l0f2"))
    dx, dn_xq0, dn_mem0, dwxq0, dwxkv0, dwxo0 = _xattn_layer_bwd(dx, s_x0, mem2, row(norm_xq[0]), row(norm_mem[0]), *x0_w, "l0x", fr)
    for key, parts in {"d_xq0": rows_parts(dwxq0), "d_xkv0": rows_parts(dwxkv0), "d_xo0": _full_to_cols(dwxo0)}.items():
        fr.put("l0_da", key, parts, SCATTER)
    ship_out = lambda dw_out: fr.put("l0_dh_uv", "d_out", rows_parts(dw_out), SCATTER)
    dx, small_even, (dw_uv, dw_z, dw_xbc, dw_dt, _) = _even_bwd(dx, s_even, even_w, even_p, fr, ship_out)
    dn_mix0, d_lng, d_lnb, d_ws, d_bst, d_conv_w, d_conv_b, d_dtb, d_alog, d_dsk, d_ssd_norm = small_even
    d_in = _full_to_cols(jnp.concatenate([dw_uv, dw_z, dw_xbc, dw_dt[:, :SSD_HEADS]], axis=1))
    fr.put("l0f1_dact", "d_in_a", d_in[:, :half_rows], SCATTER)
    fr.put("l0f1_dwd", "d_in_b", d_in[:, half_rows:], SCATTER)
    dx, dn_f1a, _, _, _ = _ffn_bwd(dx, s_f1a, row(norm_ffn1[0]), wg_f1a, wu_f1a, wd_f1a, "l0f1", fr, shipper("l0f1"))
    assert not fr.load, sorted(fr.load)
    grad_x = dx[0].reshape(x.shape)

    got = fr.landed
    received = {
        "w_ffn1_gu": [got["l0f1_gu_a"], got["l0f1_gu_b"], got["l1f1_gu_a"], got["l1f1_gu_b"]],
        "w_ffn2_gu": [got["l0f2_gu_a"], got["l0f2_gu_b"], got["l1f2_gu_a"], got["l1f2_gu_b"]],
        "w_ffn1_down": [got["l0f1_dn"], got["l1f1_dn"]], "w_ffn2_down": [got["l0f2_dn"], got["l1f2_dn"]],
        "w_in_even": [got["d_in_a"], got["d_in_b"]], "w_out_even": [got["d_out"]], "w_qkv": [got["d_qkv"]], "w_o_odd": [got["d_o"]],
        "w_xq": [got["d_xq0"], got["d_xq1"]], "w_xkv": [got["d_xkv0"], got["d_xkv1"]], "w_xo": [got["d_xo0"], got["d_xo1"]],
    }
    out = {}
    for n in SHARDED:
        shp = W[n].shape
        two = lambda a: a.reshape(-1, shp[-1])
        res = _adamw(received[n], two(W[n]), two(M[n]), two(V[n]), f"adam_{n}")
        out[n] = [r.reshape(shp) for r in res]

    db_qkv = jnp.concatenate([_unpad_heads_cols(db_q, ATT_HEADS), _unpad_heads_cols(db_kv, 2 * ATT_KV_HEADS)], axis=1).reshape(-1)
    cw_parts = _full_to_cols(d_conv_w)
    bq_parts = db_qkv.reshape(N_DEV, -1)
    ss_parts = jnp.stack([_pack([cw_parts[k], bq_parts[k]]) for k in range(N_DEV)])
    recv = _all_to_all(ss_parts, "a2a_small")
    res = _adamw(recv, _pack([conv_w, b_qkv]), _pack([m_conv_w, m_b_qkv]), _pack([v_conv_w, v_b_qkv]), "adam_small_sharded")
    res = [_unpack(r, [conv_w.shape, b_qkv.shape]) for r in res]
    out["conv_w"] = [r[0] for r in res]
    out["b_qkv"] = [r[1] for r in res]

    small_grads = {
        "norm_ffn1": jnp.concatenate([dn_f1a, dn_f1b]), "norm_mix": jnp.concatenate([dn_mix0, dn_mix1]),
        "gm_ln_g": d_lng, "gm_ln_b": d_lnb, "gm_ws": d_ws[None], "gm_bs": d_bst[:, :GM_GROUPS].T[None],
        "conv_b": d_conv_b, "dt_bias": d_dtb[:, :SSD_HEADS], "a_log": d_alog[:, :SSD_HEADS], "d_skip": d_dsk[:, :SSD_HEADS],
        "ssd_norm": d_ssd_norm, "sinks": d_snk[:, :ATT_HEADS], "norm_xq": jnp.concatenate([dn_xq0, dn_xq1]),
        "norm_mem": jnp.concatenate([dn_mem0, dn_mem1]), "norm_ffn2": jnp.concatenate([dn_f2a, dn_f2b]),
        "final_norm": d_final.reshape(-1),
    }
    shapes = [W[n].shape for n in SMALL]
    recv = _all_gather(_pack([small_grads[n] for n in SMALL]), "ag_small_grads")
    res = _adamw(recv, _pack([W[n] for n in SMALL]), _pack([M[n] for n in SMALL]), _pack([V[n] for n in SMALL]), "adam_small")
    res = [_unpack(r, shapes) for r in res]
    for i, n in enumerate(SMALL):
        out[n] = [r[i] for r in res]

    return (loss, grad_x, *[out[n][0] for n in ORDER], *[out[n][1] for n in ORDER], *[out[n][2] for n in ORDER],
            *[out[n][3] for n in ORDER])
```

```python
import functools

import jax
import jax.numpy as jnp
from jax import lax
from jax.experimental import pallas as pl
from jax.experimental.pallas import tpu as pltpu

F32, BF16 = jnp.float32, jnp.bfloat16

N_DEV = 8
D_MODEL = 2048
D_FF = 5632
EPS = 1e-5
CHUNK = 128
GM_GROUPS, GM_GDIM = 4, 512
SSD_HEADS, SSD_HEAD_DIM, SSD_GROUPS, SSD_STATE = 32, 64, 4, 128
XBC_WIDTH = D_MODEL + 2 * SSD_GROUPS * SSD_STATE
Z_COL = 2 * GM_GROUPS * GM_GDIM // D_MODEL
XBC_COL = (2 * GM_GROUPS * GM_GDIM + D_MODEL) // XBC_WIDTH
ATT_HEADS, ATT_KV_HEADS, ATT_HEAD_DIM, ATT_REP = 32, 4, 64, 8
HEAD_PAD = 128
ROT_HALF = 8
ROPE_THETA = 500000.0
ATT_SCALE = ATT_HEAD_DIM ** -0.5
X_HEADS, X_HEAD_DIM = 4, 128
X_SCALE = X_HEAD_DIM ** -0.5
ADAM_LR, ADAM_B1, ADAM_B2, ADAM_EPS, ADAM_WD, ADAM_STEP = 0.001, 0.9, 0.999, 1e-08, 0.01, 10

VMEM_LIMIT_BYTES = 56 * 1024 * 1024
MESH_ID = pl.DeviceIdType.MESH


def _params(*sem):
    return pltpu.CompilerParams(dimension_semantics=sem, vmem_limit_bytes=VMEM_LIMIT_BYTES)


def _pick(n, cands):
    for c in cands:
        if n % c == 0:
            return c
    return n


def _dg(a, b, ca, cb, precision=None):
    return lax.dot_general(a, b, (((ca,), (cb,)), ((), ())), precision=precision, preferred_element_type=F32)


@jax.custom_vjp
def _bdot(a, b):
    return _dg(a.astype(BF16), b.astype(BF16), 1, 0)


def _bdot_fwd(a, b):
    return _bdot(a, b), (a, b)


def _bdot_bwd(r, g):
    a, b = r
    g = g.astype(BF16)
    return _dg(g, b.astype(BF16), 1, 1), _dg(a.astype(BF16), g, 0, 0)


_bdot.defvjp(_bdot_fwd, _bdot_bwd)


@jax.custom_vjp
def _bdot_nt(a, b):
    return _dg(a.astype(BF16), b.astype(BF16), 1, 1)


def _bdot_nt_fwd(a, b):
    return _bdot_nt(a, b), (a, b)


def _bdot_nt_bwd(r, g):
    a, b = r
    g = g.astype(BF16)
    return _dg(g, b.astype(BF16), 1, 0), _dg(g, a.astype(BF16), 0, 0)


_bdot_nt.defvjp(_bdot_nt_fwd, _bdot_nt_bwd)


@jax.custom_vjp
def _bdot_tn(a, b):
    return _dg(a.astype(BF16), b.astype(BF16), 0, 0)


def _bdot_tn_fwd(a, b):
    return _bdot_tn(a, b), (a, b)


def _bdot_tn_bwd(r, g):
    a, b = r
    g = g.astype(BF16)
    return _dg(b.astype(BF16), g, 1, 1), _dg(a.astype(BF16), g, 1, 0)


_bdot_tn.defvjp(_bdot_tn_fwd, _bdot_tn_bwd)


def _split3(x):
    hi = x.astype(BF16)
    r1 = x - hi.astype(F32)
    mid = r1.astype(BF16)
    lo = (r1 - mid.astype(F32)).astype(BF16)
    return hi, mid, lo


def _dot_exact(x, one, cx, co, x_is_lhs):
    one = one.astype(BF16)
    out = None
    for piece in _split3(x):
        term = _dg(piece, one, cx, co) if x_is_lhs else _dg(one, piece, co, cx)
        out = term if out is None else out + term
    return out


@jax.custom_vjp
def _spread(a, e):
    return _dot_exact(a, e, 1, 0, True)


def _spread_fwd(a, e):
    return _spread(a, e), e


def _spread_bwd(e, g):
    return _dot_exact(g, e, 1, 1, True), jnp.zeros_like(e)


_spread.defvjp(_spread_fwd, _spread_bwd)


@jax.custom_vjp
def _running_sum(t, a):
    return _dot_exact(a, t, 0, 1, False)


def _running_sum_fwd(t, a):
    return _running_sum(t, a), t


def _running_sum_bwd(t, g):
    return jnp.zeros_like(t), _dot_exact(g, t, 0, 0, False)


_running_sum.defvjp(_running_sum_fwd, _running_sum_bwd)


def _sigmoid(x):
    return 1.0 / (1.0 + jnp.exp(-x))


def _silu(x):
    return x * _sigmoid(x)


def _gelu(x):
    return 0.5 * x * (1.0 + lax.erf(x * 0.7071067811865476))


def _softplus(x):
    return jnp.maximum(x, 0.0) + jnp.log1p(jnp.exp(-jnp.abs(x)))


def _rms(x, g):
    return x * lax.rsqrt(jnp.mean(x * x, -1, keepdims=True) + EPS) * g


def _iota(shape, dim):
    return lax.broadcasted_iota(jnp.int32, shape, dim)


MXU_DIM = 256
MATMUL_VMEM_BYTES = 44 * 1024 * 1024
MXU_FLOPS = 9.0e14
HBM_BYTES_PER_S = 3.0e12
STEP_SECONDS = 0.35e-6


def _matmul_tiles(m, n, kk, a_item, b_item, o_item, has_res):
    def divisors(d, cands):
        return [c for c in cands if d % c == 0] or [d]

    def pad(d):
        return -(-d // MXU_DIM) * MXU_DIM

    best = None
    for tm in divisors(m, (1024, 512, 256, 128)):
        for tn in divisors(n, (2816, 2048, 1408, 1024, 512, 256, 128)):
            for tk in divisors(kk, (2816, 2048, 1408, 1024, 512, 256, 128)):
                o_bytes = tm * tn * (o_item + (4 if has_res else 0))
                if 2 * (tm * tk * a_item + tk * tn * b_item + o_bytes) + tm * tn * 4 > MATMUL_VMEM_BYTES:
                    continue
                nk = kk // tk
                a_bytes = tm * tk * a_item / (1 if nk > 1 else n // tn)
                seconds = max(2 * tm * pad(tn) * pad(tk) / MXU_FLOPS, (a_bytes + tk * tn * b_item + o_bytes / nk) / HBM_BYTES_PER_S)
                total = (m // tm) * (n // tn) * nk * (seconds + STEP_SECONDS)
                if best is None or total < best[0]:
                    best = (total, tm, tn, tk)
    return best[1:]


def _matmul(a, b, *, ta=False, tb=False, out_dtype=F32, alpha=1.0, bias=None, res=None, name, freight=None,
            extras=(), epilogue=None):
    cargo = freight.take(name) if freight is not None else []
    out_dtypes = tuple(out_dtype) if epilogue is not None else (out_dtype,)
    n_out, n_ex = len(out_dtypes), len(extras)
    if ta:
        kk, m = a.shape
    else:
        m, kk = a.shape
    if tb:
        n, k2 = b.shape
    else:
        k2, n = b.shape
    assert kk == k2, (a.shape, b.shape, ta, tb)
    tile_item = sum(jnp.dtype(d).itemsize for d in out_dtypes) + sum(e.dtype.itemsize for e in extras)
    tm, tn, tk = _matmul_tiles(m, n, kk, a.dtype.itemsize, b.dtype.itemsize, tile_item, res is not None)
    nk = kk // tk
    nc = len(cargo)
    has_bias, has_res = bias is not None, res is not None

    def body(*refs):
        a_ref, b_ref = refs[0], refs[1]
        pos = 2
        bias_ref = res_ref = None
        if has_bias:
            bias_ref = refs[pos]
            pos += 1
        if has_res:
            res_ref = refs[pos]
            pos += 1
        extra_refs = refs[pos:pos + n_ex]
        pos += n_ex
        cargo_in = refs[pos:pos + nc]
        pos += nc
        o_refs = refs[pos:pos + n_out]
        pos += n_out
        cargo_out = refs[pos:pos + nc]
        acc_ref = refs[pos + nc]
        sems = refs[pos + nc + 1:]
        i, j, k = pl.program_id(0), pl.program_id(1), pl.program_id(2)

        if nc:
            @pl.when((i == 0) & (j == 0) & (k == 0))
            def _():
                for c, (_, _, kind) in enumerate(cargo):
                    _EX_START[kind](cargo_in[c], cargo_out[c], *sems[3 * c:3 * c + 3])

        part = _dg(a_ref[...].astype(BF16), b_ref[...].astype(BF16), 0 if ta else 1, 1 if tb else 0)
        if nk > 1:
            @pl.when(k == 0)
            def _():
                acc_ref[...] = part

            @pl.when((k > 0) & (k < nk - 1))
            def _():
                acc_ref[...] += part

        @pl.when(k == nk - 1)
        def _():
            r = acc_ref[...] + part if nk > 1 else part
            if alpha != 1.0:
                r = r * alpha
            if has_bias:
                r = r + bias_ref[...]
            if has_res:
                r = r + res_ref[...]
            vals = epilogue(r, *[e[...] for e in extra_refs]) if epilogue is not None else (r,)
            for o_ref, val in zip(o_refs, vals, strict=True):
                o_ref[...] = val.astype(o_ref.dtype)

        if nc:
            @pl.when((i == m // tm - 1) & (j == n // tn - 1) & (k == nk - 1))
            def _():
                for c, (_, _, kind) in enumerate(cargo):
                    _EX_FINISH[kind](cargo_in[c], cargo_out[c], *sems[3 * c:3 * c + 3])

    in_specs = [
        pl.BlockSpec((tk, tm), lambda i, j, k: (k, i)) if ta else pl.BlockSpec((tm, tk), lambda i, j, k: (i, k)),
        pl.BlockSpec((tn, tk), lambda i, j, k: (j, k)) if tb else pl.BlockSpec((tk, tn), lambda i, j, k: (k, j)),
    ]
    args = [a, b]
    if has_bias:
        in_specs.append(pl.BlockSpec((1, tn), lambda i, j, k: (0, j)))
        args.append(bias)
    if has_res:
        in_specs.append(pl.BlockSpec((tm, tn), lambda i, j, k: (i, j)))
        args.append(res)
    anyspec = pl.BlockSpec(memory_space=pl.ANY)
    tile = pl.BlockSpec((tm, tn), lambda i, j, k: (i, j))
    for e in extras:
        assert e.shape == (m, n), (e.shape, m, n, name)
    out_specs = [tile] * n_out + [anyspec] * nc
    out_shape = [jax.ShapeDtypeStruct((m, n), d) for d in out_dtypes] + [_ex_out_shape(arr, kind) for _, arr, kind in cargo]
    scratch = [pltpu.VMEM((tm, tn), F32)]
    for _ in cargo:
        scratch += _ex_scratch()
    res_all = pl.pallas_call(
        body, name=name, grid=(m // tm, n // tn, nk), in_specs=in_specs + [tile] * n_ex + [anyspec] * nc,
        out_specs=out_specs, out_shape=out_shape, scratch_shapes=scratch,
        compiler_params=_params("arbitrary", "arbitrary", "arbitrary") if nc else _params("parallel", "parallel", "arbitrary"),
    )(*args, *extras, *[arr for _, arr, _ in cargo])
    for (key, _, _), landed in zip(cargo, res_all[n_out:]):
        freight.landed[key] = landed
    return tuple(res_all[:n_out]) if epilogue is not None else res_all[0]


def _row(arr, width=None, cidx=0, shift=0):
    return (arr, arr.shape[-1] if width is None else width, cidx, shift)


def _rows(fn, rows, consts, outs, accs=(), *, tm, ncol=1, n_rows=None, name):
    n = rows[0][0].shape[-2] if n_rows is None else n_rows
    assert n % tm == 0, (n, tm, name)
    nb = n // tm
    n_in = len(rows) + len(consts)
    n_out = len(outs)

    def cfun(cidx):
        return cidx if callable(cidx) else (lambda j, c=cidx: c)

    in_specs = []
    for arr, width, cidx, shift in rows:
        cf = cfun(cidx)
        if callable(shift):
            rf = shift
        elif shift:
            rf = lambda i, s=shift: jnp.clip(i + s, 0, nb - 1)
        else:
            rf = lambda i: i
        if arr.ndim == 3:
            in_specs.append(pl.BlockSpec((arr.shape[0], tm, width), lambda i, j, rf=rf, cf=cf: (0, rf(i), cf(j))))
        else:
            in_specs.append(pl.BlockSpec((tm, width), lambda i, j, rf=rf, cf=cf: (rf(i), cf(j))))
    for c in consts:
        in_specs.append(pl.BlockSpec(c.shape, lambda i, j, nd=c.ndim: (0,) * nd))
    out_shape, out_specs = [], []
    for o in outs:
        width, dt = o[0], o[1]
        total = o[2] if len(o) > 2 else width * ncol
        out_shape.append(jax.ShapeDtypeStruct((n, total), dt))
        out_specs.append(pl.BlockSpec((tm, width), lambda i, j: (i, j)))
    for shp in accs:
        out_shape.append(jax.ShapeDtypeStruct(shp, F32))
        out_specs.append(pl.BlockSpec(shp, lambda i, j, nd=len(shp): (0,) * nd))

    def body(*refs):
        i, j = pl.program_id(0), pl.program_id(1)
        ins = [r[...] for r in refs[:n_in]]
        ro, ao = fn(i, j, *ins)
        for r, val in zip(refs[n_in:n_in + n_out], ro):
            r[...] = val.astype(r.dtype)
        if accs:
            acc_refs = refs[n_in + n_out:]

            @pl.when((i == 0) & (j == 0))
            def _():
                for r in acc_refs:
                    r[...] = jnp.zeros_like(r)

            for r, val in zip(acc_refs, ao):
                r[...] += val

    res = pl.pallas_call(
        body, name=name, grid=(nb, ncol), in_specs=in_specs, out_specs=out_specs, out_shape=out_shape,
        compiler_params=_params("arbitrary", "arbitrary"),
    )(*[r[0] for r in rows], *consts)
    return res


def _rms_fwd(x, g, name):
    def fn(i, j, x, g):
        return (_rms(x, g),), ()
    return _rows(fn, [_row(x)], [g], [(x.shape[1], BF16)], tm=_pick(x.shape[0], (512, 256)), name=name)[0]


def _rms_bwd(x, dh, dres, g, name):
    def fn(i, j, x, dh, dres, g):
        _, vjp = jax.vjp(_rms, x, g)
        dx, dg = vjp(dh)
        return (dx + dres, dx + dres), (dg,)
    dx, dx16, dg = _rows(fn, [_row(x), _row(dh), _row(dres)], [g], [(x.shape[1], F32), (x.shape[1], BF16)], [g.shape],
                         tm=_pick(x.shape[0], (256,)), name=name)
    return (dx, dx16), dg


def _rms_bwd_gain(x, dh, g, name):
    def fn(i, j, x, dh, g):
        _, vjp = jax.vjp(_rms, x, g)
        return (), (vjp(dh)[1],)
    return _rows(fn, [_row(x), _row(dh)], [g], [], [g.shape], tm=_pick(x.shape[0], (256,)), name=name)[0]


def _swiglu_tile(u, g):
    return u, _silu(g.astype(F32)) * u


def _swiglu_bwd_tile(da, g, u):
    g, u = g.astype(F32), u.astype(F32)
    s = _sigmoid(g)
    return da * u * (s * (1.0 + g * (1.0 - s))), da * (g * s)


def _gmlp_fn(uv, lng, lnb, ws0, ws1, ws2, ws3, bst):
    ws = (ws0, ws1, ws2, ws3)
    q = uv.shape[0]
    u = _gelu(uv[:, :D_MODEL])
    v = _gelu(uv[:, D_MODEL:])
    tril = _iota((q, q), 0) >= _iota((q, q), 1)
    outs = []
    for g in range(GM_GROUPS):
        sl = slice(GM_GDIM * g, GM_GDIM * (g + 1))
        vg = v[:, sl]
        mu = jnp.mean(vg, -1, keepdims=True)
        var = jnp.mean(jnp.square(vg - mu), -1, keepdims=True)
        vn = (vg - mu) * lax.rsqrt(var + EPS) * lng[:, sl] + lnb[:, sl]
        w = jnp.where(tril, ws[g], 0.0)
        bcol = jnp.sum(bst * (_iota((1, 128), 1) == g).astype(F32), axis=1, keepdims=True)
        outs.append(u[:, sl] * (_bdot(w, vn) + bcol))
    return jnp.concatenate(outs, axis=1)


def _gmlp_fwd(uv, lng, lnb, ws, bst, name):
    def fn(i, j, uv, lng, lnb, ws, bst):
        return (_gmlp_fn(uv, lng, lnb, ws[0], ws[1], ws[2], ws[3], bst),), ()
    return _rows(fn, [_row(uv, 2 * D_MODEL, 0)], [lng, lnb, ws, bst], [(D_MODEL, BF16)], tm=CHUNK, name=name)[0]


def _gmlp_bwd(uv, da, lng, lnb, ws, bst, name):
    def fn(i, j, uv, da, lng, lnb, ws, bst):
        _, vjp = jax.vjp(_gmlp_fn, uv, lng, lnb, ws[0], ws[1], ws[2], ws[3], bst)
        duv, dlng, dlnb, d0, d1, d2, d3, dbst = vjp(da)
        return (duv,), (dlng, dlnb, jnp.stack([d0, d1, d2, d3]), dbst)
    return _rows(fn, [_row(uv, 2 * D_MODEL, 0), _row(da)], [lng, lnb, ws, bst], [(2 * D_MODEL, BF16)],
                 [lng.shape, lnb.shape, ws.shape, bst.shape], tm=CHUNK, name=name)


def _conv_taps(scr, x, halo, first_row):
    q = x.shape[0]
    scr[pl.ds(0, 8), :] = halo
    scr[pl.ds(8, q), :] = x
    return [scr[pl.ds(first_row + k, q), :] for k in range(4)]


def _halo_before(q, c, col):
    return pl.BlockSpec((8, c), lambda i: (jnp.maximum(i * (q // 8) - 1, 0), col))


def _conv_fwd(proj, w, b, name):
    n, c = proj.shape[0], w.shape[1]
    xbc = proj
    q = CHUNK

    def body(x_ref, xp_ref, w_ref, b_ref, o_ref, scr):
        i = pl.program_id(0)
        halo = jnp.where(i > 0, xp_ref[...], 0.0)
        taps = _conv_taps(scr, x_ref[...], halo, 5)
        pre = b_ref[...] + sum(taps[k] * w_ref[pl.ds(k, 1), :] for k in range(4))
        o_ref[...] = _silu(pre)

    return pl.pallas_call(
        body, name=name, grid=(n // q,),
        in_specs=[pl.BlockSpec((q, c), lambda i: (i, XBC_COL)), _halo_before(q, c, XBC_COL),
                  pl.BlockSpec(w.shape, lambda i: (0, 0)), pl.BlockSpec(b.shape, lambda i: (0, 0))],
        out_specs=pl.BlockSpec((q, c), lambda i: (i, 0)), out_shape=jax.ShapeDtypeStruct((n, c), F32),
        scratch_shapes=[pltpu.VMEM((q + 8, c), F32)], compiler_params=_params("arbitrary"),
    )(xbc, xbc, w, b)


def _conv_bwd(proj, dy, w, b, name):
    n, c = proj.shape[0], w.shape[1]
    q = CHUNK
    nb = n // q

    def body(x_ref, xp_ref, dy_ref, w_ref, b_ref, dx_ref, dw_ref, db_ref, scr_x, scr_d):
        i = pl.program_id(0)

        @pl.when(i == 0)
        def _():
            dw_ref[...] = jnp.zeros_like(dw_ref)
            db_ref[...] = jnp.zeros_like(db_ref)
            scr_d[pl.ds(q, 8), :] = jnp.zeros((8, c), F32)

        halo = jnp.where(i < nb - 1, xp_ref[...], 0.0)
        taps = _conv_taps(scr_x, x_ref[...], halo, 5)
        pre = b_ref[...] + sum(taps[k] * w_ref[pl.ds(k, 1), :] for k in range(4))
        s = _sigmoid(pre)
        dp = dy_ref[...] * (s * (1.0 + pre * (1.0 - s)))
        db_ref[...] += jnp.sum(dp, axis=0, keepdims=True)
        for k in range(4):
            dw_ref[pl.ds(k, 1), :] += jnp.sum(dp * taps[k], axis=0, keepdims=True)
        scr_d[pl.ds(0, q), :] = dp
        dx_ref[...] = sum(scr_d[pl.ds(3 - k, q), :] * w_ref[pl.ds(k, 1), :] for k in range(4)).astype(dx_ref.dtype)
        scr_d[pl.ds(q, 8), :] = dp[0:8]

    rev = lambda i: nb - 1 - i
    return pl.pallas_call(
        body, name=name, grid=(nb,),
        in_specs=[pl.BlockSpec((q, c), lambda i: (rev(i), XBC_COL)),
                  pl.BlockSpec((8, c), lambda i: (jnp.maximum(rev(i) * (q // 8) - 1, 0), XBC_COL)),
                  pl.BlockSpec((q, c), lambda i: (rev(i), 0)),
                  pl.BlockSpec(w.shape, lambda i: (0, 0)), pl.BlockSpec(b.shape, lambda i: (0, 0))],
        out_specs=[pl.BlockSpec((q, c), lambda i: (rev(i), 0)), pl.BlockSpec(w.shape, lambda i: (0, 0)),
                   pl.BlockSpec(b.shape, lambda i: (0, 0))],
        out_shape=[jax.ShapeDtypeStruct((n, c), BF16), jax.ShapeDtypeStruct(w.shape, F32), jax.ShapeDtypeStruct(b.shape, F32)],
        scratch_shapes=[pltpu.VMEM((q + 8, c), F32), pltpu.VMEM((q + 8, c), F32)], compiler_params=_params("arbitrary"),
    )(proj, proj, dy, w, b)


def _ssd_chunk(xs, bc, dtr, dtb, alog, dsk, expand, hp):
    q = xs.shape[0]
    tril = _iota((q, q), 0) >= _iota((q, q), 1)
    dt = _softplus(dtr + dtb)
    a = dt * (-jnp.exp(alog))
    cs = _running_sum(tril.astype(F32), a)
    cs_t = cs.T
    last = (_iota((q, 1), 0) == q - 1).astype(F32)
    gw = SSD_HEADS // SSD_GROUPS * SSD_HEAD_DIM
    lane = _iota((1, gw), 1)
    ys, hs = [], []
    for g in range(SSD_GROUPS):
        sl = slice(gw * g, gw * (g + 1))
        eg = expand[:, sl]
        dt_e, cs_e = _spread(dt, eg), _spread(cs, eg)
        cl_e = jnp.sum(cs_e * last, axis=0, keepdims=True)
        d_e = jnp.sum(_spread(jnp.broadcast_to(dsk, (8, 128)), eg), axis=0, keepdims=True) * 0.125
        xg = xs[:, sl]
        xdt = xg * dt_e
        bg = bc[:, SSD_STATE * g:SSD_STATE * (g + 1)]
        cg = bc[:, SSD_GROUPS * SSD_STATE + SSD_STATE * g:SSD_GROUPS * SSD_STATE + SSD_STATE * (g + 1)]
        cb = _bdot_nt(cg, bg)
        ms, xm = [], []
        for r in range(SSD_HEADS // SSD_GROUPS):
            h = g * (SSD_HEADS // SSD_GROUPS) + r
            col = jnp.sum(cs * (_iota((1, 128), 1) == h).astype(F32), axis=1, keepdims=True)
            row = jnp.sum(cs_t * (_iota((128, 1), 0) == h).astype(F32), axis=0, keepdims=True)
            decay = jnp.where(tril, jnp.exp(jnp.where(tril, col - row, 0.0)), 0.0)
            ms.append(cb * decay)
            xm.append(xdt * ((lane >= SSD_HEAD_DIM * r) & (lane < SSD_HEAD_DIM * (r + 1))).astype(F32))
        y_diag = _bdot(jnp.concatenate(ms, axis=1), jnp.concatenate(xm, axis=0))
        hg = hp[:, sl]
        y_off = _bdot(cg, hg) * jnp.exp(cs_e)
        states = _bdot_tn(bg, xdt * jnp.exp(cl_e - cs_e))
        hs.append(hg * jnp.exp(cl_e) + states)
        ys.append(y_diag + y_off + xg * d_e)
    return jnp.concatenate(ys, axis=1), jnp.concatenate(hs, axis=1)


def _ssd_fwd(xbc, dtr, dtb, alog, dsk, expand, name):
    n = xbc.shape[0]
    q, w = CHUNK, D_MODEL
    nc = n // q

    def body(xs_ref, bc_ref, dtr_ref, dtb_ref, alog_ref, dsk_ref, e_ref, y_ref, hp_ref, h_scr):
        @pl.when(pl.program_id(0) == 0)
        def _():
            h_scr[...] = jnp.zeros_like(h_scr)

        hp = h_scr[...]
        y, hn = _ssd_chunk(xs_ref[...], bc_ref[...], dtr_ref[...], dtb_ref[...], alog_ref[...], dsk_ref[...], e_ref[...], hp)
        y_ref[...] = y
        hp_ref[...] = hp
        h_scr[...] = hn

    small = pl.BlockSpec((1, 128), lambda c: (0, 0))
    return pl.pallas_call(
        body, name=name, grid=(nc,),
        in_specs=[pl.BlockSpec((q, w), lambda c: (c, 0)), pl.BlockSpec((q, 1024), lambda c: (c, 2)),
                  pl.BlockSpec((q, 128), lambda c: (c, 0)), small, small, small, pl.BlockSpec((128, w), lambda c: (0, 0))],
        out_specs=[pl.BlockSpec((q, w), lambda c: (c, 0)), pl.BlockSpec((SSD_STATE, w), lambda c: (c, 0))],
        out_shape=[jax.ShapeDtypeStruct((n, w), F32), jax.ShapeDtypeStruct((nc * SSD_STATE, w), F32)],
        scratch_shapes=[pltpu.VMEM((SSD_STATE, w), F32)], compiler_params=_params("arbitrary"),
    )(xbc, xbc, dtr, dtb, alog, dsk, expand)


def _ssd_bwd(xbc, dtr, dtb, alog, dsk, expand, hp_all, dy, name):
    n = xbc.shape[0]
    q, w = CHUNK, D_MODEL
    nc = n // q

    def body(xs_ref, bc_ref, dtr_ref, dtb_ref, alog_ref, dsk_ref, e_ref, hp_ref, dy_ref,
             dxbc_ref, ddtr_ref, ddtb_ref, dalog_ref, ddsk_ref, dh_scr):
        @pl.when(pl.program_id(0) == 0)
        def _():
            dh_scr[...] = jnp.zeros_like(dh_scr)
            ddtb_ref[...] = jnp.zeros_like(ddtb_ref)
            dalog_ref[...] = jnp.zeros_like(dalog_ref)
            ddsk_ref[...] = jnp.zeros_like(ddsk_ref)

        e = e_ref[...]
        _, vjp = jax.vjp(lambda xs, bc, dtr, dtb, alog, dsk, hp: _ssd_chunk(xs, bc, dtr, dtb, alog, dsk, e, hp),
                         xs_ref[...], bc_ref[...], dtr_ref[...], dtb_ref[...], alog_ref[...], dsk_ref[...], hp_ref[...])
        dxs, dbc, ddtr, ddtb, dalog, ddsk, dhp = vjp((dy_ref[...], dh_scr[...]))
        dxbc_ref[...] = jnp.concatenate([dxs, dbc], axis=1)
        ddtr_ref[...] = ddtr
        ddtb_ref[...] += ddtb
        dalog_ref[...] += dalog
        ddsk_ref[...] += ddsk
        dh_scr[...] = dhp

    rev = lambda c: nc - 1 - c
    small = pl.BlockSpec((1, 128), lambda c: (0, 0))
    return pl.pallas_call(
        body, name=name, grid=(nc,),
        in_specs=[pl.BlockSpec((q, w), lambda c: (rev(c), 0)), pl.BlockSpec((q, 1024), lambda c: (rev(c), 2)),
                  pl.BlockSpec((q, 128), lambda c: (rev(c), 0)), small, small, small,
                  pl.BlockSpec((128, w), lambda c: (0, 0)), pl.BlockSpec((SSD_STATE, w), lambda c: (rev(c), 0)),
                  pl.BlockSpec((q, w), lambda c: (rev(c), 0))],
        out_specs=[pl.BlockSpec((q, w + 1024), lambda c: (rev(c), 0)),
                   pl.BlockSpec((q, 128), lambda c: (rev(c), 0)), small, small, small],
        out_shape=[jax.ShapeDtypeStruct((n, w + 1024), F32), jax.ShapeDtypeStruct((n, 128), F32),
                   jax.ShapeDtypeStruct((1, 128), F32), jax.ShapeDtypeStruct((1, 128), F32), jax.ShapeDtypeStruct((1, 128), F32)],
        scratch_shapes=[pltpu.VMEM((SSD_STATE, w), F32)], compiler_params=_params("arbitrary"),
    )(xbc, xbc, dtr, dtb, alog, dsk, expand, hp_all, dy)


def _gate_fn(y, z, g):
    outs = []
    for k in range(SSD_GROUPS):
        sl = slice(512 * k, 512 * (k + 1))
        yg = y[:, sl] * _silu(z[:, sl])
        outs.append(yg * lax.rsqrt(jnp.mean(yg * yg, -1, keepdims=True) + EPS) * g[:, sl])
    return jnp.concatenate(outs, axis=1)


def _gate_fwd(y, z, g, name):
    def fn(i, j, y, z, g):
        return (_gate_fn(y, z, g),), ()
    return _rows(fn, [_row(y), _row(z, D_MODEL, Z_COL)], [g], [(D_MODEL, BF16)], tm=_pick(y.shape[0], (256, 128)), name=name)[0]


def _gate_bwd(y, z, db, g, name):
    def fn(i, j, y, z, db, g):
        _, vjp = jax.vjp(_gate_fn, y, z, g)
        dy, dz, dg = vjp(db)
        return (dy, dz), (dg,)
    return _rows(fn, [_row(y), _row(z, D_MODEL, Z_COL), _row(db)], [g], [(D_MODEL, F32), (D_MODEL, BF16)], [g.shape],
                 tm=_pick(y.shape[0], (256, 128)), name=name)


def _rope(x, tab, sign=1.0):
    return x * tab[:, 0:128] + sign * (pltpu.roll(x, 128 - ROT_HALF, 1) * tab[:, 128:256] + pltpu.roll(x, ROT_HALF, 1) * tab[:, 256:384])


def _sink_softmax_parts(s, sink):
    m = jnp.maximum(jnp.max(s, -1, keepdims=True), sink)
    p = jnp.exp(s - m)
    e_sink = jnp.exp(sink - m)
    inv = 1.0 / (jnp.sum(p, -1, keepdims=True) + e_sink)
    return p * inv, e_sink * inv


@jax.custom_vjp
def _sink_softmax(s, sink):
    return _sink_softmax_parts(s, sink)[0]


def _sink_softmax_fwd(s, sink):
    pr, pr_sink = _sink_softmax_parts(s, sink)
    return pr, (pr, pr_sink)


def _sink_softmax_bwd(r, g):
    pr, pr_sink = r
    t = jnp.sum(g * pr, -1, keepdims=True)
    return pr * (g - t), -pr_sink * t


_sink_softmax.defvjp(_sink_softmax_fwd, _sink_softmax_bwd)


def _swa_core(qq, kc, vc, sinks, kv_head, first):
    q = kc.shape[0] // 2
    s = _bdot_nt(qq, kc) * ATT_SCALE
    rows = ATT_REP * q
    iq = _iota((rows, 2 * q), 0) & (q - 1)
    js = _iota((rows, 2 * q), 1)
    rel = iq + q - js
    mask = (rel >= 0) & (rel < q) & ((js >= q) | jnp.logical_not(first))
    s = jnp.where(mask, s, -jnp.inf)
    rep = lax.shift_right_logical(_iota((rows, 1), 0), q.bit_length() - 1)
    sink = jnp.zeros((rows, 1), F32)
    for r in range(ATT_REP):
        s_r = jnp.sum(sinks * (_iota((1, 128), 1) == kv_head * ATT_REP + r).astype(F32), axis=1, keepdims=True)
        sink = sink + jnp.where(rep == r, s_r, 0.0)
    return _bdot(_sink_softmax(s, sink), vc)


def _swa_prep(q, kv, kvp, tab, tabp, kv_head):
    w = HEAD_PAD
    kc = jnp.concatenate([_rope(kvp[:, w * kv_head:w * (kv_head + 1)], tabp), _rope(kv[:, w * kv_head:w * (kv_head + 1)], tab)], axis=0)
    o = ATT_KV_HEADS * w
    vc = jnp.concatenate([kvp[:, o + w * kv_head:o + w * (kv_head + 1)], kv[:, o + w * kv_head:o + w * (kv_head + 1)]], axis=0)
    qq = jnp.concatenate([_rope(q[:, w * (kv_head * ATT_REP + r):w * (kv_head * ATT_REP + r + 1)], tab) for r in range(ATT_REP)], axis=0)
    return qq, kc, vc


def _swa_fwd(q, kv, tab, sinks, name):
    def fn(i, j, q, kv, kvp, tab, tabp, sinks):
        first = i == 0
        outs = []
        for h in range(ATT_KV_HEADS):
            qq, kc, vc = _swa_prep(q, kv, kvp, tab, tabp, h)
            o = _swa_core(qq, kc, vc, sinks, h, first)
            outs += [o[CHUNK * r:CHUNK * (r + 1)] for r in range(ATT_REP)]
        return (jnp.concatenate(outs, axis=1),), ()
    return _rows(fn, [_row(q), _row(kv), _row(kv, shift=-1), _row(tab), _row(tab, shift=-1)], [sinks],
                 [(q.shape[1], BF16)], tm=CHUNK, name=name)[0]


def _swa_bwd(q, kv, tab, sinks, do, name):
    w = HEAD_PAD

    def fn(i, j, q, kv, kvp, tab, tabp, do, sinks):
        first = i == 0
        dqs, dkc, dkp, dvc, dvp = [], [], [], [], []
        dsink = jnp.zeros_like(sinks)
        for h in range(ATT_KV_HEADS):
            qq, kc, vc = _swa_prep(q, kv, kvp, tab, tabp, h)
            dout = jnp.concatenate([do[:, w * (h * ATT_REP + r):w * (h * ATT_REP + r + 1)] for r in range(ATT_REP)], axis=0)
            _, vjp = jax.vjp(lambda a, b, c, s: _swa_core(a, b, c, s, h, first), qq, kc, vc, sinks)
            dqq, dk, dv, ds = vjp(dout.astype(F32))
            dsink = dsink + ds
            dqs += [_rope(dqq[CHUNK * r:CHUNK * (r + 1)], tab, -1.0) for r in range(ATT_REP)]
            dkp.append(_rope(dk[:CHUNK], tabp, -1.0))
            dkc.append(_rope(dk[CHUNK:], tab, -1.0))
            dvp.append(dv[:CHUNK])
            dvc.append(dv[CHUNK:])
        return (jnp.concatenate(dqs, axis=1), jnp.concatenate(dkc + dvc, axis=1), jnp.concatenate(dkp + dvp, axis=1)), (dsink,)
    return _rows(fn, [_row(q), _row(kv), _row(kv, shift=-1), _row(tab), _row(tab, shift=-1), _row(do)], [sinks],
                 [(q.shape[1], BF16), (kv.shape[1], F32), (kv.shape[1], F32)], [sinks.shape], tm=CHUNK, name=name)


def _swa_combine(dkv_cur, dkv_prev, dq, name):
    nb = dq.shape[0] // CHUNK

    def fn(i, j, cur, nxt, dq):
        dkv = (cur + jnp.where(i < nb - 1, nxt, 0.0)).astype(BF16)
        return (dkv,), (jnp.sum(dq.astype(F32), axis=0, keepdims=True), jnp.sum(dkv.astype(F32), axis=0, keepdims=True))
    return _rows(fn, [_row(dkv_cur), _row(dkv_prev, shift=1), _row(dq)], [], [(dkv_cur.shape[1], BF16)],
                 [(1, dq.shape[1]), (1, dkv_cur.shape[1])], tm=CHUNK, name=name)


def _xattn_fn(q, k, v):
    outs = []
    for h in range(X_HEADS):
        sl = slice(X_HEAD_DIM * h, X_HEAD_DIM * (h + 1))
        s = _bdot_nt(q[:, sl], k[:, sl]) * X_SCALE
        p = jnp.exp(s - jnp.max(s, -1, keepdims=True))
        outs.append(_bdot(p / jnp.sum(p, -1, keepdims=True), v[:, sl]))
    return jnp.concatenate(outs, axis=1)


def _xattn_fwd(q, k, v, name):
    def fn(i, j, q, k, v):
        return (_xattn_fn(q, k, v),), ()
    return _rows(fn, [_row(q)], [k, v], [(q.shape[1], BF16)], tm=_pick(q.shape[0], (512, 256)), name=name)[0]


def _xattn_bwd(q, k, v, do, name):
    def fn(i, j, q, do, k, v):
        _, vjp = jax.vjp(_xattn_fn, q, k, v)
        dq, dk, dv = vjp(do)
        return (dq,), (dk, dv)
    return _rows(fn, [_row(q), _row(do)], [k, v], [(q.shape[1], BF16)], [k.shape, v.shape],
                 tm=_pick(q.shape[0], (512, 256)), name=name)


def _loss_fn(x, t, g):
    return 0.5 * jnp.sum(jnp.mean(jnp.square(_rms(x, g) - t), axis=-1))


def _loss_fwd_bwd(x, t, g, name):
    def fn(i, j, x, t, g):
        loss, vjp = jax.vjp(_loss_fn, x, t, g)
        dx, _, dg = vjp(jnp.ones((), F32))
        return (dx, dx), (jnp.broadcast_to(loss, (8, 128)), dg)
    dx, dx16, loss, dg = _rows(fn, [_row(x), _row(t)], [g], [(x.shape[1], F32), (x.shape[1], BF16)], [(8, 128), g.shape],
                               tm=_pick(x.shape[0], (256,)), name=name)
    return loss, (dx, dx16), dg


def _adamw(parts, w, m, v, name):
    plist = list(parts) if isinstance(parts, (list, tuple)) else [parts]
    nl = len(plist)
    r, c = w.shape
    tm = _pick(r // nl, (128, 64, 32, 16, 8))
    nbl = r // nl // tm

    def fn(i, j, *blocks):
        parts, (w, m, v) = blocks[0], blocks[nl:]
        for l in range(1, nl):
            parts = jnp.where(i >= l * nbl, blocks[l], parts)
        g = parts[0].astype(F32)
        for k in range(1, N_DEV):
            g = g + parts[k].astype(F32)
        m2 = ADAM_B1 * m + (1.0 - ADAM_B1) * g
        v2 = ADAM_B2 * v + (1.0 - ADAM_B2) * jnp.square(g)
        m_hat = m2 / (1.0 - ADAM_B1 ** ADAM_STEP)
        v_hat = v2 / (1.0 - ADAM_B2 ** ADAM_STEP)
        delta = -ADAM_LR * (m_hat / (jnp.sqrt(v_hat) + ADAM_EPS) + ADAM_WD * w)
        return (g, delta, m2, v2), ()
    prow = [_row(p, shift=(lambda i, l=l: jnp.clip(i - l * nbl, 0, nbl - 1))) for l, p in enumerate(plist)]
    return _rows(fn, prow + [_row(w), _row(m), _row(v)], [], [(c, F32)] * 4, tm=tm, n_rows=r, name=name)


def _peer(k):
    return (k // 4, (k // 2) % 2, k % 2)


GATHER, SCATTER = "gather", "scatter"


def _ex_scratch():
    return [pltpu.SemaphoreType.DMA((N_DEV - 1,)), pltpu.SemaphoreType.DMA((N_DEV - 1,)), pltpu.SemaphoreType.DMA]


def _ex_out_shape(x, kind):
    return jax.ShapeDtypeStruct((N_DEV,) + x.shape[-2:], x.dtype)


def _gather_plan():
    x, y, c = lax.axis_index("x"), lax.axis_index("y"), lax.axis_index("c")
    return (x, y, c), (x, y, 1 - c), [(1 - x, y), (x, 1 - y), (1 - x, 1 - y)], c


def _gather_copy(x_ref, o_ref, send, recv, k, block, to, from_input=False):
    slot = o_ref.at[4 * block[0] + 2 * block[1] + block[2]]
    return pltpu.make_async_remote_copy(src_ref=x_ref if from_input else slot, dst_ref=slot, send_sem=send.at[k],
                                        recv_sem=recv.at[k], device_id=to, device_id_type=MESH_ID)


def _gather_start(x_ref, o_ref, send, recv, local):
    me, sib, chips, c = _gather_plan()
    pltpu.make_async_copy(x_ref, o_ref.at[4 * me[0] + 2 * me[1] + me[2]], local).start()
    _gather_copy(x_ref, o_ref, send, recv, 0, me, sib, True).start()
    for j, chip in enumerate(chips):
        _gather_copy(x_ref, o_ref, send, recv, 1 + j, me, (*chip, c), True).start()


def _gather_finish(x_ref, o_ref, send, recv, local):
    me, sib, chips, c = _gather_plan()
    for j, chip in enumerate(chips):
        _gather_copy(x_ref, o_ref, send, recv, 1 + j, (*chip, c), me).wait_recv()
        _gather_copy(x_ref, o_ref, send, recv, 4 + j, (*chip, c), sib).start()
    _gather_copy(x_ref, o_ref, send, recv, 0, sib, me).wait_recv()
    for j, chip in enumerate(chips):
        _gather_copy(x_ref, o_ref, send, recv, 4 + j, (*chip, 1 - c), me).wait_recv()
    _gather_copy(x_ref, o_ref, send, recv, 0, me, sib, True).wait_send()
    for j, chip in enumerate(chips):
        _gather_copy(x_ref, o_ref, send, recv, 1 + j, me, (*chip, c), True).wait_send()
        _gather_copy(x_ref, o_ref, send, recv, 4 + j, (*chip, c), sib).wait_send()
    pltpu.make_async_copy(x_ref, o_ref.at[4 * me[0] + 2 * me[1] + me[2]], local).wait()


def _scatter_copy(x_ref, o_ref, send, recv, d, frm, to):
    return pltpu.make_async_remote_copy(src_ref=x_ref.at[to], dst_ref=o_ref.at[frm], send_sem=send.at[d - 1],
                                        recv_sem=recv.at[d - 1], device_id=_peer(to), device_id_type=MESH_ID)


def _scatter_start(x_ref, o_ref, send, recv, local):
    me = 4 * lax.axis_index("x") + 2 * lax.axis_index("y") + lax.axis_index("c")
    pltpu.make_async_copy(x_ref.at[me], o_ref.at[me], local).start()
    for d in range(1, N_DEV):
        _scatter_copy(x_ref, o_ref, send, recv, d, me, (me + d) % N_DEV).start()


def _scatter_finish(x_ref, o_ref, send, recv, local):
    me = 4 * lax.axis_index("x") + 2 * lax.axis_index("y") + lax.axis_index("c")
    for d in range(1, N_DEV):
        _scatter_copy(x_ref, o_ref, send, recv, d, (me + N_DEV - d) % N_DEV, me).wait_recv()
    for d in range(1, N_DEV):
        _scatter_copy(x_ref, o_ref, send, recv, d, me, (me + d) % N_DEV).wait_send()
    pltpu.make_async_copy(x_ref.at[me], o_ref.at[me], local).wait()


_EX_START = {GATHER: _gather_start, SCATTER: _scatter_start}
_EX_FINISH = {GATHER: _gather_finish, SCATTER: _scatter_finish}


def _exchange(x, kind, name):
    def body(x_ref, o_ref, send, recv, local):
        _EX_START[kind](x_ref, o_ref, send, recv, local)
        _EX_FINISH[kind](x_ref, o_ref, send, recv, local)

    return pl.pallas_call(
        body, name=name, in_specs=[pl.BlockSpec(memory_space=pl.ANY)], out_specs=pl.BlockSpec(memory_space=pl.ANY),
        out_shape=_ex_out_shape(x, kind), scratch_shapes=_ex_scratch(),
    )(x)


class _Freight:
    def __init__(self):
        self.load = {}
        self.landed = {}

    def put(self, carrier, key, arr, kind):
        self.load.setdefault(carrier, []).append((key, arr, kind))

    def take(self, carrier):
        return self.load.pop(carrier, [])


def _all_gather(x, name):
    return _exchange(x.reshape(-1, x.shape[-1]), GATHER, name).reshape((N_DEV,) + x.shape)


def _all_to_all(x, name):
    return _exchange(x.reshape(N_DEV, -1, x.shape[-1]), SCATTER, name).reshape(x.shape)


def _pack(arrs):
    flat = []
    for a in arrs:
        a = a.reshape(-1).astype(F32)
        flat.append(jnp.pad(a, (0, (-a.shape[0]) % 1024)))
    return jnp.concatenate(flat).reshape(-1, 128)


def _unpack(p, shapes):
    p = p.reshape(-1)
    out, off = [], 0
    for s in shapes:
        n = 1
        for d in s:
            n *= d
        out.append(p[off:off + n].reshape(s))
        off += n + (-n) % 1024
    return out


def _cols_to_full(g):
    return g.transpose(1, 0, 2).reshape(g.shape[1], -1)


def _full_to_cols(w, shards=N_DEV):
    return w.reshape(w.shape[0], shards, -1).transpose(1, 0, 2)


SMALL = ("norm_ffn1", "norm_mix", "gm_ln_g", "gm_ln_b", "gm_ws", "gm_bs", "conv_b", "dt_bias", "a_log", "d_skip", "ssd_norm",
         "sinks", "norm_xq", "norm_mem", "norm_ffn2", "final_norm")
SHARDED = ("w_ffn1_gu", "w_ffn1_down", "w_in_even", "w_out_even", "w_qkv", "w_o_odd", "w_xq", "w_xkv", "w_xo", "w_ffn2_gu",
           "w_ffn2_down")
SMALL_SHARDED = ("conv_w", "b_qkv")
ORDER = ("norm_ffn1", "w_ffn1_gu", "w_ffn1_down", "norm_mix", "w_in_even", "gm_ln_g", "gm_ln_b", "gm_ws", "gm_bs", "conv_w",
         "conv_b", "dt_bias", "a_log", "d_skip", "ssd_norm", "w_out_even", "w_qkv", "b_qkv", "sinks", "w_o_odd", "norm_xq",
         "norm_mem", "w_xq", "w_xkv", "w_xo", "norm_ffn2", "w_ffn2_gu", "w_ffn2_down", "final_norm")


def _ffn_fwd(x, gain, wg, wu, wd, tag, freight=None):
    mm = functools.partial(_matmul, freight=freight)
    h = _rms_fwd(x, gain, f"{tag}_norm")
    g = mm(h, wg, out_dtype=BF16, name=f"{tag}_gate")
    u, act = mm(h, wu, out_dtype=(BF16, BF16), extras=[g], epilogue=_swiglu_tile, name=f"{tag}_up")
    wd = wd() if callable(wd) else wd
    y = mm(act, wd, alpha=0.5, res=x, name=f"{tag}_down")
    return y, (x, h, g, u, act), wd


def _ffn_bwd(dy, saved, gain, wg, wu, wd, tag, freight=None, ship=None):
    mm = functools.partial(_matmul, freight=freight)
    x, h, g, u, act = saved
    dy, dy16 = dy
    dg, du = mm(dy16, wd, tb=True, alpha=0.5, out_dtype=(BF16, BF16), extras=[g, u], epilogue=_swiglu_bwd_tile, name=f"{tag}_dact")
    dwd = mm(act, dy16, ta=True, alpha=0.5, out_dtype=BF16, name=f"{tag}_dwd")
    if ship is not None:
        ship(f"{tag}_dwg", "dn", dwd.reshape(N_DEV, -1, dwd.shape[1]))
    dwg = mm(h, dg, ta=True, out_dtype=BF16, name=f"{tag}_dwg")
    dwu = mm(h, du, ta=True, out_dtype=BF16, name=f"{tag}_dwu")
    if ship is not None:
        half = N_DEV // 2
        parts = jnp.concatenate([_full_to_cols(dwg, half), _full_to_cols(dwu, half)], axis=0)
        rows = parts.shape[1] // 2
        ship(f"{tag}_dh_g", "gu_a", parts[:, :rows])
        ship(f"{tag}_dh_u", "gu_b", parts[:, rows:])
    dh = mm(dg, wg, tb=True, name=f"{tag}_dh_g")
    dh = mm(du, wu, tb=True, res=dh, name=f"{tag}_dh_u")
    dx, dgain = _rms_bwd(x, dh, dy, gain, f"{tag}_dnorm")
    return dx, dgain, dwg, dwu, dwd


def _xattn_layer_fwd(x, mem, gq, gm, wq, wkv, wo, tag, freight=None):
    mm = functools.partial(_matmul, freight=freight)
    hq = _rms_fwd(x, gq, f"{tag}_normq")
    mn = _rms_fwd(mem, gm, f"{tag}_normm")
    q = mm(hq, wq, name=f"{tag}_q")
    kv = mm(mn, wkv, name=f"{tag}_kv")
    k, v = kv[:, :X_HEADS * X_HEAD_DIM], kv[:, X_HEADS * X_HEAD_DIM:]
    o = _xattn_fwd(q, k, v, f"{tag}_attn")
    y = mm(o, wo, res=x, name=f"{tag}_o")
    return y, (x, hq, mn, q, k, v, o)


def _xattn_layer_bwd(dy, saved, mem, gq, gm, wq, wkv, wo, tag, freight=None):
    mm = functools.partial(_matmul, freight=freight)
    x, hq, mn, q, k, v, o = saved
    dy, dy16 = dy
    do = mm(dy16, wo, tb=True, name=f"{tag}_do")
    dwo = mm(o, dy16, ta=True, out_dtype=BF16, name=f"{tag}_dwo")
    dq, dk, dv = _xattn_bwd(q, k, v, do, f"{tag}_dattn")
    dkv = jnp.concatenate([dk, dv], axis=1)
    dwq = mm(hq, dq, ta=True, out_dtype=BF16, name=f"{tag}_dwq")
    dwkv = mm(mn, dkv, ta=True, out_dtype=BF16, name=f"{tag}_dwkv")
    dhq = mm(dq, wq, tb=True, name=f"{tag}_dhq")
    dmn = mm(dkv, wkv, tb=True, name=f"{tag}_dmn")
    dx, dgq = _rms_bwd(x, dhq, dy, gq, f"{tag}_dnormq")
    dgm = _rms_bwd_gain(mem, dmn, gm, f"{tag}_dnormm")
    return dx, dgq, dgm, dwq, dwkv, dwo


def _even_fwd(x, weights, params, freight=None):
    mm = functools.partial(_matmul, freight=freight)
    w_main, w_dt, w_out_a, w_out_b = weights
    gain, lng, lnb, ws, bst, conv_w, conv_b, dtb, alog, dsk, ssd_norm, expand = params
    hm = _rms_fwd(x, gain, "l0_normmix")
    proj = mm(hm, w_main, name="l0_proj")
    dtr = mm(hm, w_dt, name="l0_dt")
    a_out = _gmlp_fwd(proj, lng, lnb, ws, bst, "l0_gmlp")
    xbc = _conv_fwd(proj, conv_w, conv_b, "l0_conv")
    y_ssd, hp_all = _ssd_fwd(xbc, dtr, dtb, alog, dsk, expand, "l0_ssd")
    b_out = _gate_fwd(y_ssd, proj, ssd_norm, "l0_gate")
    y = mm(a_out, w_out_a, res=x, name="l0_out_a")
    y = mm(b_out, w_out_b, res=y, name="l0_out_b")
    return y, (x, hm, proj, dtr, a_out, xbc, y_ssd, hp_all, b_out)


def _even_bwd(dx, saved, weights, params, freight=None, ship_out=None):
    mm = functools.partial(_matmul, freight=freight)
    w_main, w_dt, w_out_a, w_out_b = weights
    w_uv, w_z, w_xbc = w_main[:, :4096], w_main[:, 4096:6144], w_main[:, 6144:]
    gain, lng, lnb, ws, bst, conv_w, conv_b, dtb, alog, dsk, ssd_norm, expand = params
    x, hm, proj, dtr, a_out, xbc, y_ssd, hp_all, b_out = saved
    uv = zz = xbc_raw = proj
    dx, dx16 = dx
    da_out = mm(dx16, w_out_a, tb=True, name="l0_da")
    db_out = mm(dx16, w_out_b, tb=True, name="l0_db")
    dw_out_a = mm(a_out, dx16, ta=True, out_dtype=BF16, name="l0_dwout_a")
    dw_out_b = mm(b_out, dx16, ta=True, out_dtype=BF16, name="l0_dwout_b")
    dw_out = jnp.concatenate([dw_out_a, dw_out_b], axis=0)
    if ship_out is not None:
        ship_out(dw_out)
    dy_ssd, dzz, d_ssd_norm = _gate_bwd(y_ssd, zz, db_out, ssd_norm, "l0_dgate")
    dxbc, ddtr, d_dtb, d_alog, d_dsk = _ssd_bwd(xbc, dtr, dtb, alog, dsk, expand, hp_all, dy_ssd, "l0_dssd")
    dxbc_raw, d_conv_w, d_conv_b = _conv_bwd(xbc_raw, dxbc, conv_w, conv_b, "l0_dconv")
    duv, d_lng, d_lnb, d_ws, d_bst = _gmlp_bwd(uv, da_out, lng, lnb, ws, bst, "l0_dgmlp")
    ddtr16 = ddtr.astype(BF16)
    dhm = mm(duv, w_uv, tb=True, name="l0_dh_uv")
    dhm = mm(dzz, w_z, tb=True, res=dhm, name="l0_dh_z")
    dhm = mm(dxbc_raw, w_xbc, tb=True, res=dhm, name="l0_dh_xbc")
    dhm = mm(ddtr16, w_dt, tb=True, res=dhm, name="l0_dh_dt")
    dw_uv = mm(hm, duv, ta=True, out_dtype=BF16, name="l0_dwuv")
    dw_z = mm(hm, dzz, ta=True, out_dtype=BF16, name="l0_dwz")
    dw_xbc = mm(hm, dxbc_raw, ta=True, out_dtype=BF16, name="l0_dwxbc")
    dw_dt = mm(hm, ddtr16, ta=True, out_dtype=BF16, name="l0_dwdt")
    dx, d_gain = _rms_bwd(x, dhm, dx, gain, "l0_dnormmix")
    small = (d_gain, d_lng, d_lnb, d_ws, d_bst, d_conv_w, d_conv_b, d_dtb, d_alog, d_dsk, d_ssd_norm)
    return dx, small, (dw_uv, dw_z, dw_xbc, dw_dt, dw_out)


def _odd_fwd(x, weights, params, freight=None):
    mm = functools.partial(_matmul, freight=freight)
    w_q, w_kv, w_o = weights
    gain, b_q, b_kv, tab, snk = params
    hm = _rms_fwd(x, gain, "l1_normmix")
    q = mm(hm, w_q, bias=b_q, name="l1_q")
    kv = mm(hm, w_kv, bias=b_kv, name="l1_kv")
    o = _swa_fwd(q, kv, tab, snk, "l1_swa")
    y = mm(o, w_o, res=x, name="l1_o")
    return y, (x, hm, q, kv, o)


def _odd_bwd(dx, saved, weights, params, freight=None):
    mm = functools.partial(_matmul, freight=freight)
    w_q, w_kv, w_o = weights
    gain, b_q, b_kv, tab, snk = params
    x, hm, q, kv, o = saved
    dx, dx16 = dx
    do = mm(dx16, w_o, tb=True, out_dtype=BF16, name="l1_do")
    dw_o = mm(o, dx16, ta=True, out_dtype=BF16, name="l1_dwo")
    dq, dkv_cur, dkv_prev, d_snk = _swa_bwd(q, kv, tab, snk, do, "l1_dswa")
    dkv, db_q, db_kv = _swa_combine(dkv_cur, dkv_prev, dq, "l1_dkv")
    dhm = mm(dq, w_q, tb=True, name="l1_dh_q")
    dhm = mm(dkv, w_kv, tb=True, res=dhm, name="l1_dh_kv")
    dw_q = mm(hm, dq, ta=True, out_dtype=BF16, name="l1_dwq")
    dw_kv = mm(hm, dkv, ta=True, out_dtype=BF16, name="l1_dwkv")
    dx, d_gain = _rms_bwd(x, dhm, dx, gain, "l1_dnormmix")
    return dx, (d_gain, db_q, db_kv, d_snk), (dw_q, dw_kv, dw_o)


def _rope_table(positions):
    seq = positions.size
    inv_freq = ROPE_THETA ** (-jnp.arange(0, 2 * ROT_HALF, 2, dtype=F32) / (2 * ROT_HALF))
    ang = positions.reshape(seq, 1).astype(F32) * inv_freq
    cos, sin, zero = jnp.cos(ang), jnp.sin(ang), jnp.zeros((seq, 128 - 2 * ROT_HALF), F32)
    z8 = jnp.zeros((seq, ROT_HALF), F32)
    return jnp.concatenate([cos, cos, zero + 1.0, -sin, z8, zero, z8, sin, zero], axis=1)


def _pad_heads_cols(w, heads):
    k = w.shape[0]
    return jnp.pad(w.reshape(k, heads, ATT_HEAD_DIM), ((0, 0), (0, 0), (0, HEAD_PAD - ATT_HEAD_DIM))).reshape(k, heads * HEAD_PAD)


def _unpad_heads_cols(w, heads):
    k = w.shape[0]
    return w.reshape(k, heads, HEAD_PAD)[:, :, :ATT_HEAD_DIM].reshape(k, heads * ATT_HEAD_DIM)


def kernel(x, mem, positions, norm_ffn1, w_ffn1_gu, w_ffn1_down, norm_mix, w_in_even, gm_ln_g, gm_ln_b, gm_ws, gm_bs, conv_w, conv_b, dt_bias, a_log, d_skip, ssd_norm, w_out_even, w_qkv, b_qkv, sinks, w_o_odd, norm_xq, norm_mem, w_xq, w_xkv, w_xo, norm_ffn2, w_ffn2_gu, w_ffn2_down, final_norm, loss_target, m_norm_ffn1, m_w_ffn1_gu, m_w_ffn1_down, m_norm_mix, m_w_in_even, m_gm_ln_g, m_gm_ln_b, m_gm_ws, m_gm_bs, m_conv_w, m_conv_b, m_dt_bias, m_a_log, m_d_skip, m_ssd_norm, m_w_out_even, m_w_qkv, m_b_qkv, m_sinks, m_w_o_odd, m_norm_xq, m_norm_mem, m_w_xq, m_w_xkv, m_w_xo, m_norm_ffn2, m_w_ffn2_gu, m_w_ffn2_down, m_final_norm, v_norm_ffn1, v_w_ffn1_gu, v_w_ffn1_down, v_norm_mix, v_w_in_even, v_gm_ln_g, v_gm_ln_b, v_gm_ws, v_gm_bs, v_conv_w, v_conv_b, v_dt_bias, v_a_log, v_d_skip, v_ssd_norm, v_w_out_even, v_w_qkv, v_b_qkv, v_sinks, v_w_o_odd, v_norm_xq, v_norm_mem, v_w_xq, v_w_xkv, v_w_xo, v_norm_ffn2, v_w_ffn2_gu, v_w_ffn2_down, v_final_norm):
    env = dict(locals())
    W = {n: env[n] for n in ORDER}
    M = {n: env["m_" + n] for n in ORDER}
    V = {n: env["v_" + n] for n in ORDER}
    seq = x.shape[1]
    x0 = x.reshape(seq, D_MODEL)
    mem2 = mem.reshape(-1, D_MODEL)
    target = loss_target.reshape(seq, D_MODEL)

    fr = _Freight()
    b16 = lambda a: a.astype(BF16)
    grp_a = {"out": w_out_even[0], "xq0": w_xq[0], "xkv0": w_xkv[0], "xo0": w_xo[0]}
    grp_b = {"qkv": w_qkv[0], "o": w_o_odd[0], "xq1": w_xq[1], "xkv1": w_xkv[1], "xo1": w_xo[1]}
    half_rows = D_MODEL // 2
    fr.put("l0f1_gate", "dn1_0", b16(w_ffn1_down[0]), GATHER)
    fr.put("l0f1_gate", "dn2_0", b16(w_ffn2_down[0]), GATHER)
    fr.put("l0f1_up", "in", b16(w_in_even[0]), GATHER)
    for key, w in grp_a.items():
        fr.put("l0f1_down", key, b16(w), GATHER)
    fr.put("l0_proj", "gu2_0", b16(w_ffn2_gu[0]), GATHER)
    fr.put("l0f2_gate", "gu1_1a", b16(w_ffn1_gu[1, :half_rows]), GATHER)
    fr.put("l0f2_gate", "dn1_1", b16(w_ffn1_down[1]), GATHER)
    fr.put("l0f2_up", "gu1_1b", b16(w_ffn1_gu[1, half_rows:]), GATHER)
    for key, w in grp_b.items():
        fr.put("l0f2_down", key, b16(w), GATHER)
    fr.put("l1f1_gate", "gu2_1a", b16(w_ffn2_gu[1, :half_rows]), GATHER)
    fr.put("l1f1_gate", "dn2_1", b16(w_ffn2_down[1]), GATHER)
    fr.put("l1f1_up", "gu2_1b", b16(w_ffn2_gu[1, half_rows:]), GATHER)
    gu_first = _exchange(b16(w_ffn1_gu[0]), GATHER, "ag_l0f1_gu")
    gs = _all_gather(_pack([conv_w, b_qkv]), "ag_small")
    gs = [_unpack(gs[k], [conv_w.shape, b_qkv.shape]) for k in range(N_DEV)]
    conv_w_full = jnp.concatenate([g[0][0] for g in gs], axis=1)
    b_qkv_full = jnp.concatenate([g[1][0] for g in gs], axis=0)

    def gate_up(g):
        return _cols_to_full(g[:N_DEV // 2]), _cols_to_full(g[N_DEV // 2:])

    down = lambda key: fr.landed[key].reshape(D_FF, D_MODEL)

    row = lambda a: a.reshape(1, -1)
    pad128 = lambda a: jnp.pad(a.reshape(1, -1), ((0, 0), (0, 128 - a.size)))
    bst = jnp.pad(gm_bs[0].T, ((0, 0), (0, 128 - GM_GROUPS)))
    ws = gm_ws[0]
    dtb, alog, dsk, snk = pad128(dt_bias), pad128(a_log), pad128(d_skip), pad128(sinks)
    expand = (jnp.arange(128)[:, None] == (jnp.arange(D_MODEL) // SSD_HEAD_DIM)[None, :]).astype(F32)
    tab = _rope_table(positions)

    wg_f1a, wu_f1a = gate_up(gu_first)
    xa, s_f1a, wd_f1a = _ffn_fwd(x0, row(norm_ffn1[0]), wg_f1a, wu_f1a, lambda: down("dn1_0"), "l0f1", fr)
    w_in = _cols_to_full(fr.landed["in"])
    n_main = 2 * GM_GROUPS * GM_GDIM + D_MODEL + XBC_WIDTH
    w_main = w_in[:, :n_main]
    w_dt = jnp.pad(w_in[:, n_main:], ((0, 0), (0, 128 - SSD_HEADS)))
    a_xq, a_xkv, a_xo = fr.landed["xq0"], fr.landed["xkv0"], fr.landed["xo0"]
    w_out = fr.landed["out"].reshape(2 * D_MODEL, D_MODEL)
    even_w = (w_main, w_dt, w_out[:D_MODEL], w_out[D_MODEL:])
    even_p = (row(norm_mix[0]), gm_ln_g, gm_ln_b, ws, bst, conv_w_full, conv_b, dtb, alog, dsk, ssd_norm, expand)
    xb, s_even = _even_fwd(xa, even_w, even_p, fr)
    x0_w = (a_xq.reshape(D_MODEL, -1), a_xkv.reshape(D_MODEL, -1), _cols_to_full(a_xo))
    xc, s_x0 = _xattn_layer_fwd(xb, mem2, row(norm_xq[0]), row(norm_mem[0]), *x0_w, "l0x", fr)
    wg_f2a, wu_f2a = gate_up(fr.landed["gu2_0"])
    xd, s_f2a, wd_f2a = _ffn_fwd(xc, row(norm_ffn2[0]), wg_f2a, wu_f2a, down("dn2_0"), "l0f2", fr)
    wg_f1b, wu_f1b = gate_up(jnp.concatenate([fr.landed["gu1_1a"], fr.landed["gu1_1b"]], axis=1))
    xe, s_f1b, wd_f1b = _ffn_fwd(xd, row(norm_ffn1[1]), wg_f1b, wu_f1b, down("dn1_1"), "l1f1", fr)
    b_xq, b_xkv, b_xo = fr.landed["xq1"], fr.landed["xkv1"], fr.landed["xo1"]
    nq = ATT_HEADS * ATT_HEAD_DIM
    wqkv = _cols_to_full(fr.landed["qkv"])
    w_o = _pad_heads_cols(fr.landed["o"].reshape(D_MODEL, D_MODEL).T, ATT_HEADS).T
    odd_w = (_pad_heads_cols(wqkv[:, :nq], ATT_HEADS), _pad_heads_cols(wqkv[:, nq:], 2 * ATT_KV_HEADS), w_o)
    odd_p = (row(norm_mix[1]), _pad_heads_cols(b_qkv_full[None, :nq], ATT_HEADS),
             _pad_heads_cols(b_qkv_full[None, nq:], 2 * ATT_KV_HEADS), tab, snk)
    xf, s_odd = _odd_fwd(xe, odd_w, odd_p, fr)
    x1_w = (b_xq.reshape(D_MODEL, -1), b_xkv.reshape(D_MODEL, -1), _cols_to_full(b_xo))
    xg, s_x1 = _xattn_layer_fwd(xf, mem2, row(norm_xq[1]), row(norm_mem[1]), *x1_w, "l1x", fr)
    wg_f2b, wu_f2b = gate_up(jnp.concatenate([fr.landed["gu2_1a"], fr.landed["gu2_1b"]], axis=1))
    xh, s_f2b, wd_f2b = _ffn_fwd(xg, row(norm_ffn2[1]), wg_f2b, wu_f2b, down("dn2_1"), "l1f2", fr)
    loss8, dx, d_final = _loss_fwd_bwd(xh, target, row(final_norm), "loss")
    loss = lax.psum(loss8[0, 0], ("x", "y", "c"))
    assert not fr.load, sorted(fr.load)

    def shipper(tag):
        return lambda carrier, key, parts: fr.put(carrier, f"{tag}_{key}", parts, SCATTER)

    dx, dn_f2b, _, _, _ = _ffn_bwd(dx, s_f2b, row(norm_ffn2[1]), wg_f2b, wu_f2b, wd_f2b, "l1f2", fr, shipper("l1f2"))
    dx, dn_xq1, dn_mem1, dwxq1, dwxkv1, dwxo1 = _xattn_layer_bwd(dx, s_x1, mem2, row(norm_xq[1]), row(norm_mem[1]), *x1_w, "l1x", fr)
    dx, (dn_mix1, db_q, db_kv, d_snk), (dw_q, dw_kv, dw_o) = _odd_bwd(dx, s_odd, odd_w, odd_p, fr)
    rows_parts = lambda g: g.reshape(N_DEV, -1, g.shape[1])
    dwqkv = jnp.concatenate([_unpad_heads_cols(dw_q, ATT_HEADS), _unpad_heads_cols(dw_kv, 2 * ATT_KV_HEADS)], axis=1)
    parts_b = {"d_qkv": _full_to_cols(dwqkv), "d_o": rows_parts(_unpad_heads_cols(dw_o.T, ATT_HEADS).T), "d_xq1": rows_parts(dwxq1),
               "d_xkv1": rows_parts(dwxkv1), "d_xo1": _full_to_cols(dwxo1)}
    for key, parts in parts_b.items():
        fr.put("l1f1_dact" if key in ("d_qkv", "d_o") else "l1f1_dwd", key, parts, SCATTER)
    dx, dn_f1b, _, _, _ = _ffn_bwd(dx, s_f1b, row(norm_ffn1[1]), wg_f1b, wu_f1b, wd_f1b, "l1f1", fr, shipper("l1f1"))
    dx, dn_f2a, _, _, _ = _ffn_bwd(dx, s_f2a, row(norm_ffn2[0]), wg_f2a, wu_f2a, wd_f2a, "l0f2", fr, shipper("l0f2"))
    dx, dn_xq0, dn_mem0, dwxq0, dwxkv0, dwxo0 = _xattn_layer_bwd(dx, s_x0, mem2, row(norm_xq[0]), row(norm_mem[0]), *x0_w, "l0x", fr)
    for key, parts in {"d_xq0": rows_parts(dwxq0), "d_xkv0": rows_parts(dwxkv0), "d_xo0": _full_to_cols(dwxo0)}.items():
        fr.put("l0_da", key, parts, SCATTER)
    ship_out = lambda dw_out: fr.put("l0_dh_uv", "d_out", rows_parts(dw_out), SCATTER)
    dx, small_even, (dw_uv, dw_z, dw_xbc, dw_dt, _) = _even_bwd(dx, s_even, even_w, even_p, fr, ship_out)
    dn_mix0, d_lng, d_lnb, d_ws, d_bst, d_conv_w, d_conv_b, d_dtb, d_alog, d_dsk, d_ssd_norm = small_even
    d_in = _full_to_cols(jnp.concatenate([dw_uv, dw_z, dw_xbc, dw_dt[:, :SSD_HEADS]], axis=1))
    fr.put("l0f1_dact", "d_in_a", d_in[:, :half_rows], SCATTER)
    fr.put("l0f1_dwd", "d_in_b", d_in[:, half_rows:], SCATTER)
    dx, dn_f1a, _, _, _ = _ffn_bwd(dx, s_f1a, row(norm_ffn1[0]), wg_f1a, wu_f1a, wd_f1a, "l0f1", fr, shipper("l0f1"))
    assert not fr.load, sorted(fr.load)
    grad_x = dx[0].reshape(x.shape)

    got = fr.landed
    received = {
        "w_ffn1_gu": [got["l0f1_gu_a"], got["l0f1_gu_b"], got["l1f1_gu_a"], got["l1f1_gu_b"]],
        "w_ffn2_gu": [got["l0f2_gu_a"], got["l0f2_gu_b"], got["l1f2_gu_a"], got["l1f2_gu_b"]],
        "w_ffn1_down": [got["l0f1_dn"], got["l1f1_dn"]], "w_ffn2_down": [got["l0f2_dn"], got["l1f2_dn"]],
        "w_in_even": [got["d_in_a"], got["d_in_b"]], "w_out_even": [got["d_out"]], "w_qkv": [got["d_qkv"]], "w_o_odd": [got["d_o"]],
        "w_xq": [got["d_xq0"], got["d_xq1"]], "w_xkv": [got["d_xkv0"], got["d_xkv1"]], "w_xo": [got["d_xo0"], got["d_xo1"]],
    }
    out = {}
    for n in SHARDED:
        shp = W[n].shape
        two = lambda a: a.reshape(-1, shp[-1])
        res = _adamw(received[n], two(W[n]), two(M[n]), two(V[n]), f"adam_{n}")
        out[n] = [r.reshape(shp) for r in res]

    db_qkv = jnp.concatenate([_unpad_heads_cols(db_q, ATT_HEADS), _unpad_heads_cols(db_kv, 2 * ATT_KV_HEADS)], axis=1).reshape(-1)
    cw_parts = _full_to_cols(d_conv_w)
    bq_parts = db_qkv.reshape(N_DEV, -1)
    ss_parts = jnp.stack([_pack([cw_parts[k], bq_parts[k]]) for k in range(N_DEV)])
    recv = _all_to_all(ss_parts, "a2a_small")
    res = _adamw(recv, _pack([conv_w, b_qkv]), _pack([m_conv_w, m_b_qkv]), _pack([v_conv_w, v_b_qkv]), "adam_small_sharded")
    res = [_unpack(r, [conv_w.shape, b_qkv.shape]) for r in res]
    out["conv_w"] = [r[0] for r in res]
    out["b_qkv"] = [r[1] for r in res]

    small_grads = {
        "norm_ffn1": jnp.concatenate([dn_f1a, dn_f1b]), "norm_mix": jnp.concatenate([dn_mix0, dn_mix1]),
        "gm_ln_g": d_lng, "gm_ln_b": d_lnb, "gm_ws": d_ws[None], "gm_bs": d_bst[:, :GM_GROUPS].T[None],
        "conv_b": d_conv_b, "dt_bias": d_dtb[:, :SSD_HEADS], "a_log": d_alog[:, :SSD_HEADS], "d_skip": d_dsk[:, :SSD_HEADS],
        "ssd_norm": d_ssd_norm, "sinks": d_snk[:, :ATT_HEADS], "norm_xq": jnp.concatenate([dn_xq0, dn_xq1]),
        "norm_mem": jnp.concatenate([dn_mem0, dn_mem1]), "norm_ffn2": jnp.concatenate([dn_f2a, dn_f2b]),
        "final_norm": d_final.reshape(-1),
    }
    shapes = [W[n].shape for n in SMALL]
    recv = _all_gather(_pack([small_grads[n] for n in SMALL]), "ag_small_grads")
    res = _adamw(recv, _pack([W[n] for n in SMALL]), _pack([M[n] for n in SMALL]), _pack([V[n] for n in SMALL]), "adam_small")
    res = [_unpack(r, shapes) for r in res]
    for i, n in enumerate(SMALL):
        out[n] = [r[i] for r in res]

    return (loss, grad_x, *[out[n][0] for n in ORDER], *[out[n][1] for n in ORDER], *[out[n][2] for n in ORDER],
            *[out[n][3] for n in ORDER])
```

```python
import functools

import jax
import jax.numpy as jnp
from jax import lax
from jax.experimental import pallas as pl
from jax.experimental.pallas import tpu as pltpu

F32, BF16 = jnp.float32, jnp.bfloat16

N_DEV = 8
D_MODEL = 2048
D_FF = 5632
EPS = 1e-5
CHUNK = 128
GM_GROUPS, GM_GDIM = 4, 512
SSD_HEADS, SSD_HEAD_DIM, SSD_GROUPS, SSD_STATE = 32, 64, 4, 128
XBC_WIDTH = D_MODEL + 2 * SSD_GROUPS * SSD_STATE
Z_COL = 2 * GM_GROUPS * GM_GDIM // D_MODEL
XBC_COL = (2 * GM_GROUPS * GM_GDIM + D_MODEL) // XBC_WIDTH
ATT_HEADS, ATT_KV_HEADS, ATT_HEAD_DIM, ATT_REP = 32, 4, 64, 8
HEAD_PAD = 128
ROT_HALF = 8
ROPE_THETA = 500000.0
ATT_SCALE = ATT_HEAD_DIM ** -0.5
X_HEADS, X_HEAD_DIM = 4, 128
X_SCALE = X_HEAD_DIM ** -0.5
ADAM_LR, ADAM_B1, ADAM_B2, ADAM_EPS, ADAM_WD, ADAM_STEP = 0.001, 0.9, 0.999, 1e-08, 0.01, 10

VMEM_LIMIT_BYTES = 56 * 1024 * 1024
MESH_ID = pl.DeviceIdType.MESH


def _params(*sem):
    return pltpu.CompilerParams(dimension_semantics=sem, vmem_limit_bytes=VMEM_LIMIT_BYTES)


def _pick(n, cands):
    for c in cands:
        if n % c == 0:
            return c
    return n


def _dg(a, b, ca, cb, precision=None):
    return lax.dot_general(a, b, (((ca,), (cb,)), ((), ())), precision=precision, preferred_element_type=F32)


@jax.custom_vjp
def _bdot(a, b):
    return _dg(a.astype(BF16), b.astype(BF16), 1, 0)


def _bdot_fwd(a, b):
    return _bdot(a, b), (a, b)


def _bdot_bwd(r, g):
    a, b = r
    g = g.astype(BF16)
    return _dg(g, b.astype(BF16), 1, 1), _dg(a.astype(BF16), g, 0, 0)


_bdot.defvjp(_bdot_fwd, _bdot_bwd)


@jax.custom_vjp
def _bdot_nt(a, b):
    return _dg(a.astype(BF16), b.astype(BF16), 1, 1)


def _bdot_nt_fwd(a, b):
    return _bdot_nt(a, b), (a, b)


def _bdot_nt_bwd(r, g):
    a, b = r
    g = g.astype(BF16)
    return _dg(g, b.astype(BF16), 1, 0), _dg(g, a.astype(BF16), 0, 0)


_bdot_nt.defvjp(_bdot_nt_fwd, _bdot_nt_bwd)


@jax.custom_vjp
def _bdot_tn(a, b):
    return _dg(a.astype(BF16), b.astype(BF16), 0, 0)


def _bdot_tn_fwd(a, b):
    return _bdot_tn(a, b), (a, b)


def _bdot_tn_bwd(r, g):
    a, b = r
    g = g.astype(BF16)
    return _dg(b.astype(BF16), g, 1, 1), _dg(a.astype(BF16), g, 1, 0)


_bdot_tn.defvjp(_bdot_tn_fwd, _bdot_tn_bwd)


def _split3(x):
    hi = x.astype(BF16)
    r1 = x - hi.astype(F32)
    mid = r1.astype(BF16)
    lo = (r1 - mid.astype(F32)).astype(BF16)
    return hi, mid, lo


def _dot_exact(x, one, cx, co, x_is_lhs):
    one = one.astype(BF16)
    out = None
    for piece in _split3(x):
        term = _dg(piece, one, cx, co) if x_is_lhs else _dg(one, piece, co, cx)
        out = term if out is None else out + term
    return out


@jax.custom_vjp
def _spread(a, e):
    return _dot_exact(a, e, 1, 0, True)


def _spread_fwd(a, e):
    return _spread(a, e), e


def _spread_bwd(e, g):
    return _dot_exact(g, e, 1, 1, True), jnp.zeros_like(e)


_spread.defvjp(_spread_fwd, _spread_bwd)


@jax.custom_vjp
def _running_sum(t, a):
    return _dot_exact(a, t, 0, 1, False)


def _running_sum_fwd(t, a):
    return _running_sum(t, a), t


def _running_sum_bwd(t, g):
    return jnp.zeros_like(t), _dot_exact(g, t, 0, 0, False)


_running_sum.defvjp(_running_sum_fwd, _running_sum_bwd)


def _sigmoid(x):
    return 1.0 / (1.0 + jnp.exp(-x))


def _silu(x):
    return x * _sigmoid(x)


def _gelu(x):
    return 0.5 * x * (1.0 + lax.erf(x * 0.7071067811865476))


def _softplus(x):
    return jnp.maximum(x, 0.0) + jnp.log1p(jnp.exp(-jnp.abs(x)))


def _rms(x, g):
    return x * lax.rsqrt(jnp.mean(x * x, -1, keepdims=True) + EPS) * g


def _iota(shape, dim):
    return lax.broadcasted_iota(jnp.int32, shape, dim)


MXU_DIM = 256
MATMUL_VMEM_BYTES = 44 * 1024 * 1024
MXU_FLOPS = 9.0e14
HBM_BYTES_PER_S = 3.0e12
STEP_SECONDS = 0.35e-6


def _matmul_tiles(m, n, kk, a_item, b_item, o_item, has_res):
    def divisors(d, cands):
        return [c for c in cands if d % c == 0] or [d]

    def pad(d):
        return -(-d // MXU_DIM) * MXU_DIM

    best = None
    for tm in divisors(m, (1024, 512, 256, 128)):
        for tn in divisors(n, (2816, 2048, 1408, 1024, 512, 256, 128)):
            for tk in divisors(kk, (2816, 2048, 1408, 1024, 512, 256, 128)):
                o_bytes = tm * tn * (o_item + (4 if has_res else 0))
                if 2 * (tm * tk * a_item + tk * tn * b_item + o_bytes) + tm * tn * 4 > MATMUL_VMEM_BYTES:
                    continue
                nk = kk // tk
                a_bytes = tm * tk * a_item / (1 if nk > 1 else n // tn)
                seconds = max(2 * tm * pad(tn) * pad(tk) / MXU_FLOPS, (a_bytes + tk * tn * b_item + o_bytes / nk) / HBM_BYTES_PER_S)
                total = (m // tm) * (n // tn) * nk * (seconds + STEP_SECONDS)
                if best is None or total < best[0]:
                    best = (total, tm, tn, tk)
    return best[1:]


def _matmul(a, b, *, ta=False, tb=False, out_dtype=F32, alpha=1.0, bias=None, res=None, name, freight=None,
            extras=(), epilogue=None):
    cargo = freight.take(name) if freight is not None else []
    out_dtypes = tuple(out_dtype) if epilogue is not None else (out_dtype,)
    n_out, n_ex = len(out_dtypes), len(extras)
    if ta:
        kk, m = a.shape
    else:
        m, kk = a.shape
    if tb:
        n, k2 = b.shape
    else:
        k2, n = b.shape
    assert kk == k2, (a.shape, b.shape, ta, tb)
    tile_item = sum(jnp.dtype(d).itemsize for d in out_dtypes) + sum(e.dtype.itemsize for e in extras)
    tm, tn, tk = _matmul_tiles(m, n, kk, a.dtype.itemsize, b.dtype.itemsize, tile_item, res is not None)
    nk = kk // tk
    nc = len(cargo)
    has_bias, has_res = bias is not None, res is not None

    def body(*refs):
        a_ref, b_ref = refs[0], refs[1]
        pos = 2
        bias_ref = res_ref = None
        if has_bias:
            bias_ref = refs[pos]
            pos += 1
        if has_res:
            res_ref = refs[pos]
            pos += 1
        extra_refs = refs[pos:pos + n_ex]
        pos += n_ex
        cargo_in = refs[pos:pos + nc]
        pos += nc
        o_refs = refs[pos:pos + n_out]
        pos += n_out
        cargo_out = refs[pos:pos + nc]
        acc_ref = refs[pos + nc]
        sems = refs[pos + nc + 1:]
        i, j, k = pl.program_id(0), pl.program_id(1), pl.program_id(2)

        if nc:
            @pl.when((i == 0) & (j == 0) & (k == 0))
            def _():
                for c, (_, _, kind) in enumerate(cargo):
                    _EX_START[kind](cargo_in[c], cargo_out[c], *sems[3 * c:3 * c + 3])

        @pl.when(k == 0)
        def _():
            acc_ref[...] = jnp.zeros_like(acc_ref)

        acc_ref[...] += _dg(a_ref[...].astype(BF16), b_ref[...].astype(BF16), 0 if ta else 1, 1 if tb else 0)

        @pl.when(k == nk - 1)
        def _():
            r = acc_ref[...]
            if alpha != 1.0:
                r = r * alpha
            if has_bias:
                r = r + bias_ref[...]
            if has_res:
                r = r + res_ref[...]
            vals = epilogue(r, *[e[...] for e in extra_refs]) if epilogue is not None else (r,)
            for o_ref, val in zip(o_refs, vals, strict=True):
                o_ref[...] = val.astype(o_ref.dtype)

        if nc:
            @pl.when((i == m // tm - 1) & (j == n // tn - 1) & (k == nk - 1))
            def _():
                for c, (_, _, kind) in enumerate(cargo):
                    _EX_FINISH[kind](cargo_in[c], cargo_out[c], *sems[3 * c:3 * c + 3])

    in_specs = [
        pl.BlockSpec((tk, tm), lambda i, j, k: (k, i)) if ta else pl.BlockSpec((tm, tk), lambda i, j, k: (i, k)),
        pl.BlockSpec((tn, tk), lambda i, j, k: (j, k)) if tb else pl.BlockSpec((tk, tn), lambda i, j, k: (k, j)),
    ]
    args = [a, b]
    if has_bias:
        in_specs.append(pl.BlockSpec((1, tn), lambda i, j, k: (0, j)))
        args.append(bias)
    if has_res:
        in_specs.append(pl.BlockSpec((tm, tn), lambda i, j, k: (i, j)))
        args.append(res)
    anyspec = pl.BlockSpec(memory_space=pl.ANY)
    tile = pl.BlockSpec((tm, tn), lambda i, j, k: (i, j))
    for e in extras:
        assert e.shape == (m, n), (e.shape, m, n, name)
    out_specs = [tile] * n_out + [anyspec] * nc
    out_shape = [jax.ShapeDtypeStruct((m, n), d) for d in out_dtypes] + [_ex_out_shape(arr, kind) for _, arr, kind in cargo]
    scratch = [pltpu.VMEM((tm, tn), F32)]
    for _ in cargo:
        scratch += _ex_scratch()
    res_all = pl.pallas_call(
        body, name=name, grid=(m // tm, n // tn, nk), in_specs=in_specs + [tile] * n_ex + [anyspec] * nc,
        out_specs=out_specs, out_shape=out_shape, scratch_shapes=scratch,
        compiler_params=_params("arbitrary", "arbitrary", "arbitrary") if nc else _params("parallel", "parallel", "arbitrary"),
    )(*args, *extras, *[arr for _, arr, _ in cargo])
    for (key, _, _), landed in zip(cargo, res_all[n_out:]):
        freight.landed[key] = landed
    return tuple(res_all[:n_out]) if epilogue is not None else res_all[0]


def _row(arr, width=None, cidx=0, shift=0):
    return (arr, arr.shape[-1] if width is None else width, cidx, shift)


def _rows(fn, rows, consts, outs, accs=(), *, tm, ncol=1, n_rows=None, name):
    n = rows[0][0].shape[-2] if n_rows is None else n_rows
    assert n % tm == 0, (n, tm, name)
    nb = n // tm
    n_in = len(rows) + len(consts)
    n_out = len(outs)

    def cfun(cidx):
        return cidx if callable(cidx) else (lambda j, c=cidx: c)

    in_specs = []
    for arr, width, cidx, shift in rows:
        cf = cfun(cidx)
        if callable(shift):
            rf = shift
        elif shift:
            rf = lambda i, s=shift: jnp.clip(i + s, 0, nb - 1)
        else:
            rf = lambda i: i
        if arr.ndim == 3:
            in_specs.append(pl.BlockSpec((arr.shape[0], tm, width), lambda i, j, rf=rf, cf=cf: (0, rf(i), cf(j))))
        else:
            in_specs.append(pl.BlockSpec((tm, width), lambda i, j, rf=rf, cf=cf: (rf(i), cf(j))))
    for c in consts:
        in_specs.append(pl.BlockSpec(c.shape, lambda i, j, nd=c.ndim: (0,) * nd))
    out_shape, out_specs = [], []
    for o in outs:
        width, dt = o[0], o[1]
        total = o[2] if len(o) > 2 else width * ncol
        out_shape.append(jax.ShapeDtypeStruct((n, total), dt))
        out_specs.append(pl.BlockSpec((tm, width), lambda i, j: (i, j)))
    for shp in accs:
        out_shape.append(jax.ShapeDtypeStruct(shp, F32))
        out_specs.append(pl.BlockSpec(shp, lambda i, j, nd=len(shp): (0,) * nd))

    def body(*refs):
        i, j = pl.program_id(0), pl.program_id(1)
        ins = [r[...] for r in refs[:n_in]]
        ro, ao = fn(i, j, *ins)
        for r, val in zip(refs[n_in:n_in + n_out], ro):
            r[...] = val.astype(r.dtype)
        if accs:
            acc_refs = refs[n_in + n_out:]

            @pl.when((i == 0) & (j == 0))
            def _():
                for r in acc_refs:
                    r[...] = jnp.zeros_like(r)

            for r, val in zip(acc_refs, ao):
                r[...] += val

    res = pl.pallas_call(
        body, name=name, grid=(nb, ncol), in_specs=in_specs, out_specs=out_specs, out_shape=out_shape,
        compiler_params=_params("arbitrary", "arbitrary"),
    )(*[r[0] for r in rows], *consts)
    return res


def _rms_fwd(x, g, name):
    def fn(i, j, x, g):
        return (_rms(x, g),), ()
    return _rows(fn, [_row(x)], [g], [(x.shape[1], BF16)], tm=_pick(x.shape[0], (512, 256)), name=name)[0]


def _rms_bwd(x, dh, dres, g, name):
    def fn(i, j, x, dh, dres, g):
        _, vjp = jax.vjp(_rms, x, g)
        dx, dg = vjp(dh)
        return (dx + dres, dx + dres), (dg,)
    dx, dx16, dg = _rows(fn, [_row(x), _row(dh), _row(dres)], [g], [(x.shape[1], F32), (x.shape[1], BF16)], [g.shape],
                         tm=_pick(x.shape[0], (256,)), name=name)
    return (dx, dx16), dg


def _rms_bwd_gain(x, dh, g, name):
    def fn(i, j, x, dh, g):
        _, vjp = jax.vjp(_rms, x, g)
        return (), (vjp(dh)[1],)
    return _rows(fn, [_row(x), _row(dh)], [g], [], [g.shape], tm=_pick(x.shape[0], (256,)), name=name)[0]


def _swiglu_tile(u, g):
    return u, _silu(g.astype(F32)) * u


def _swiglu_bwd_tile(da, g, u):
    g, u = g.astype(F32), u.astype(F32)
    s = _sigmoid(g)
    return da * u * (s * (1.0 + g * (1.0 - s))), da * (g * s)


def _gmlp_fn(uv, lng, lnb, ws0, ws1, ws2, ws3, bst):
    ws = (ws0, ws1, ws2, ws3)
    q = uv.shape[0]
    u = _gelu(uv[:, :D_MODEL])
    v = _gelu(uv[:, D_MODEL:])
    tril = _iota((q, q), 0) >= _iota((q, q), 1)
    outs = []
    for g in range(GM_GROUPS):
        sl = slice(GM_GDIM * g, GM_GDIM * (g + 1))
        vg = v[:, sl]
        mu = jnp.mean(vg, -1, keepdims=True)
        var = jnp.mean(jnp.square(vg - mu), -1, keepdims=True)
        vn = (vg - mu) * lax.rsqrt(var + EPS) * lng[:, sl] + lnb[:, sl]
        w = jnp.where(tril, ws[g], 0.0)
        bcol = jnp.sum(bst * (_iota((1, 128), 1) == g).astype(F32), axis=1, keepdims=True)
        outs.append(u[:, sl] * (_bdot(w, vn) + bcol))
    return jnp.concatenate(outs, axis=1)


def _gmlp_fwd(uv, lng, lnb, ws, bst, name):
    def fn(i, j, uv, lng, lnb, ws, bst):
        return (_gmlp_fn(uv, lng, lnb, ws[0], ws[1], ws[2], ws[3], bst),), ()
    return _rows(fn, [_row(uv, 2 * D_MODEL, 0)], [lng, lnb, ws, bst], [(D_MODEL, BF16)], tm=CHUNK, name=name)[0]


def _gmlp_bwd(uv, da, lng, lnb, ws, bst, name):
    def fn(i, j, uv, da, lng, lnb, ws, bst):
        _, vjp = jax.vjp(_gmlp_fn, uv, lng, lnb, ws[0], ws[1], ws[2], ws[3], bst)
        duv, dlng, dlnb, d0, d1, d2, d3, dbst = vjp(da)
        return (duv,), (dlng, dlnb, jnp.stack([d0, d1, d2, d3]), dbst)
    return _rows(fn, [_row(uv, 2 * D_MODEL, 0), _row(da)], [lng, lnb, ws, bst], [(2 * D_MODEL, BF16)],
                 [lng.shape, lnb.shape, ws.shape, bst.shape], tm=CHUNK, name=name)


def _conv_taps(scr, x, halo, first_row):
    q = x.shape[0]
    scr[pl.ds(0, 8), :] = halo
    scr[pl.ds(8, q), :] = x
    return [scr[pl.ds(first_row + k, q), :] for k in range(4)]


def _halo_before(q, c, col):
    return pl.BlockSpec((8, c), lambda i: (jnp.maximum(i * (q // 8) - 1, 0), col))


def _conv_fwd(proj, w, b, name):
    n, c = proj.shape[0], w.shape[1]
    xbc = proj
    q = CHUNK

    def body(x_ref, xp_ref, w_ref, b_ref, o_ref, scr):
        i = pl.program_id(0)
        halo = jnp.where(i > 0, xp_ref[...], 0.0)
        taps = _conv_taps(scr, x_ref[...], halo, 5)
        pre = b_ref[...] + sum(taps[k] * w_ref[pl.ds(k, 1), :] for k in range(4))
        o_ref[...] = _silu(pre)

    return pl.pallas_call(
        body, name=name, grid=(n // q,),
        in_specs=[pl.BlockSpec((q, c), lambda i: (i, XBC_COL)), _halo_before(q, c, XBC_COL),
                  pl.BlockSpec(w.shape, lambda i: (0, 0)), pl.BlockSpec(b.shape, lambda i: (0, 0))],
        out_specs=pl.BlockSpec((q, c), lambda i: (i, 0)), out_shape=jax.ShapeDtypeStruct((n, c), F32),
        scratch_shapes=[pltpu.VMEM((q + 8, c), F32)], compiler_params=_params("arbitrary"),
    )(xbc, xbc, w, b)


def _conv_bwd(proj, dy, w, b, name):
    n, c = proj.shape[0], w.shape[1]
    q = CHUNK
    nb = n // q

    def body(x_ref, xp_ref, dy_ref, w_ref, b_ref, dx_ref, dw_ref, db_ref, scr_x, scr_d):
        i = pl.program_id(0)

        @pl.when(i == 0)
        def _():
            dw_ref[...] = jnp.zeros_like(dw_ref)
            db_ref[...] = jnp.zeros_like(db_ref)
            scr_d[pl.ds(q, 8), :] = jnp.zeros((8, c), F32)

        halo = jnp.where(i < nb - 1, xp_ref[...], 0.0)
        taps = _conv_taps(scr_x, x_ref[...], halo, 5)
        pre = b_ref[...] + sum(taps[k] * w_ref[pl.ds(k, 1), :] for k in range(4))
        s = _sigmoid(pre)
        dp = dy_ref[...] * (s * (1.0 + pre * (1.0 - s)))
        db_ref[...] += jnp.sum(dp, axis=0, keepdims=True)
        for k in range(4):
            dw_ref[pl.ds(k, 1), :] += jnp.sum(dp * taps[k], axis=0, keepdims=True)
        scr_d[pl.ds(0, q), :] = dp
        dx_ref[...] = sum(scr_d[pl.ds(3 - k, q), :] * w_ref[pl.ds(k, 1), :] for k in range(4)).astype(dx_ref.dtype)
        scr_d[pl.ds(q, 8), :] = dp[0:8]

    rev = lambda i: nb - 1 - i
    return pl.pallas_call(
        body, name=name, grid=(nb,),
        in_specs=[pl.BlockSpec((q, c), lambda i: (rev(i), XBC_COL)),
                  pl.BlockSpec((8, c), lambda i: (jnp.maximum(rev(i) * (q // 8) - 1, 0), XBC_COL)),
                  pl.BlockSpec((q, c), lambda i: (rev(i), 0)),
                  pl.BlockSpec(w.shape, lambda i: (0, 0)), pl.BlockSpec(b.shape, lambda i: (0, 0))],
        out_specs=[pl.BlockSpec((q, c), lambda i: (rev(i), 0)), pl.BlockSpec(w.shape, lambda i: (0, 0)),
                   pl.BlockSpec(b.shape, lambda i: (0, 0))],
        out_shape=[jax.ShapeDtypeStruct((n, c), BF16), jax.ShapeDtypeStruct(w.shape, F32), jax.ShapeDtypeStruct(b.shape, F32)],
        scratch_shapes=[pltpu.VMEM((q + 8, c), F32), pltpu.VMEM((q + 8, c), F32)], compiler_params=_params("arbitrary"),
    )(proj, proj, dy, w, b)


def _ssd_chunk(xs, bc, dtr, dtb, alog, dsk, expand, hp):
    q = xs.shape[0]
    tril = _iota((q, q), 0) >= _iota((q, q), 1)
    dt = _softplus(dtr + dtb)
    a = dt * (-jnp.exp(alog))
    cs = _running_sum(tril.astype(F32), a)
    cs_t = cs.T
    last = (_iota((q, 1), 0) == q - 1).astype(F32)
    gw = SSD_HEADS // SSD_GROUPS * SSD_HEAD_DIM
    lane = _iota((1, gw), 1)
    ys, hs = [], []
    for g in range(SSD_GROUPS):
        sl = slice(gw * g, gw * (g + 1))
        eg = expand[:, sl]
        dt_e, cs_e = _spread(dt, eg), _spread(cs, eg)
        cl_e = jnp.sum(cs_e * last, axis=0, keepdims=True)
        d_e = jnp.sum(_spread(jnp.broadcast_to(dsk, (8, 128)), eg), axis=0, keepdims=True) * 0.125
        xg = xs[:, sl]
        xdt = xg * dt_e
        bg = bc[:, SSD_STATE * g:SSD_STATE * (g + 1)]
        cg = bc[:, SSD_GROUPS * SSD_STATE + SSD_STATE * g:SSD_GROUPS * SSD_STATE + SSD_STATE * (g + 1)]
        cb = _bdot_nt(cg, bg)
        ms, xm = [], []
        for r in range(SSD_HEADS // SSD_GROUPS):
            h = g * (SSD_HEADS // SSD_GROUPS) + r
            col = jnp.sum(cs * (_iota((1, 128), 1) == h).astype(F32), axis=1, keepdims=True)
            row = jnp.sum(cs_t * (_iota((128, 1), 0) == h).astype(F32), axis=0, keepdims=True)
            decay = jnp.where(tril, jnp.exp(jnp.where(tril, col - row, 0.0)), 0.0)
            ms.append(cb * decay)
            xm.append(xdt * ((lane >= SSD_HEAD_DIM * r) & (lane < SSD_HEAD_DIM * (r + 1))).astype(F32))
        y_diag = _bdot(jnp.concatenate(ms, axis=1), jnp.concatenate(xm, axis=0))
        hg = hp[:, sl]
        y_off = _bdot(cg, hg) * jnp.exp(cs_e)
        states = _bdot_tn(bg, xdt * jnp.exp(cl_e - cs_e))
        hs.append(hg * jnp.exp(cl_e) + states)
        ys.append(y_diag + y_off + xg * d_e)
    return jnp.concatenate(ys, axis=1), jnp.concatenate(hs, axis=1)


def _ssd_fwd(xbc, dtr, dtb, alog, dsk, expand, name):
    n = xbc.shape[0]
    q, w = CHUNK, D_MODEL
    nc = n // q

    def body(xs_ref, bc_ref, dtr_ref, dtb_ref, alog_ref, dsk_ref, e_ref, y_ref, hp_ref, h_scr):
        @pl.when(pl.program_id(0) == 0)
        def _():
            h_scr[...] = jnp.zeros_like(h_scr)

        hp = h_scr[...]
        y, hn = _ssd_chunk(xs_ref[...], bc_ref[...], dtr_ref[...], dtb_ref[...], alog_ref[...], dsk_ref[...], e_ref[...], hp)
        y_ref[...] = y
        hp_ref[...] = hp
        h_scr[...] = hn

    small = pl.BlockSpec((1, 128), lambda c: (0, 0))
    return pl.pallas_call(
        body, name=name, grid=(nc,),
        in_specs=[pl.BlockSpec((q, w), lambda c: (c, 0)), pl.BlockSpec((q, 1024), lambda c: (c, 2)),
                  pl.BlockSpec((q, 128), lambda c: (c, 0)), small, small, small, pl.BlockSpec((128, w), lambda c: (0, 0))],
        out_specs=[pl.BlockSpec((q, w), lambda c: (c, 0)), pl.BlockSpec((SSD_STATE, w), lambda c: (c, 0))],
        out_shape=[jax.ShapeDtypeStruct((n, w), F32), jax.ShapeDtypeStruct((nc * SSD_STATE, w), F32)],
        scratch_shapes=[pltpu.VMEM((SSD_STATE, w), F32)], compiler_params=_params("arbitrary"),
    )(xbc, xbc, dtr, dtb, alog, dsk, expand)


def _ssd_bwd(xbc, dtr, dtb, alog, dsk, expand, hp_all, dy, name):
    n = xbc.shape[0]
    q, w = CHUNK, D_MODEL
    nc = n // q

    def body(xs_ref, bc_ref, dtr_ref, dtb_ref, alog_ref, dsk_ref, e_ref, hp_ref, dy_ref,
             dxbc_ref, ddtr_ref, ddtb_ref, dalog_ref, ddsk_ref, dh_scr):
        @pl.when(pl.program_id(0) == 0)
        def _():
            dh_scr[...] = jnp.zeros_like(dh_scr)
            ddtb_ref[...] = jnp.zeros_like(ddtb_ref)
            dalog_ref[...] = jnp.zeros_like(dalog_ref)
            ddsk_ref[...] = jnp.zeros_like(ddsk_ref)

        e = e_ref[...]
        _, vjp = jax.vjp(lambda xs, bc, dtr, dtb, alog, dsk, hp: _ssd_chunk(xs, bc, dtr, dtb, alog, dsk, e, hp),
                         xs_ref[...], bc_ref[...], dtr_ref[...], dtb_ref[...], alog_ref[...], dsk_ref[...], hp_ref[...])
        dxs, dbc, ddtr, ddtb, dalog, ddsk, dhp = vjp((dy_ref[...], dh_scr[...]))
        dxbc_ref[...] = jnp.concatenate([dxs, dbc], axis=1)
        ddtr_ref[...] = ddtr
        ddtb_ref[...] += ddtb
        dalog_ref[...] += dalog
        ddsk_ref[...] += ddsk
        dh_scr[...] = dhp

    rev = lambda c: nc - 1 - c
    small = pl.BlockSpec((1, 128), lambda c: (0, 0))
    return pl.pallas_call(
        body, name=name, grid=(nc,),
        in_specs=[pl.BlockSpec((q, w), lambda c: (rev(c), 0)), pl.BlockSpec((q, 1024), lambda c: (rev(c), 2)),
                  pl.BlockSpec((q, 128), lambda c: (rev(c), 0)), small, small, small,
                  pl.BlockSpec((128, w), lambda c: (0, 0)), pl.BlockSpec((SSD_STATE, w), lambda c: (rev(c), 0)),
                  pl.BlockSpec((q, w), lambda c: (rev(c), 0))],
        out_specs=[pl.BlockSpec((q, w + 1024), lambda c: (rev(c), 0)),
                   pl.BlockSpec((q, 128), lambda c: (rev(c), 0)), small, small, small],
        out_shape=[jax.ShapeDtypeStruct((n, w + 1024), F32), jax.ShapeDtypeStruct((n, 128), F32),
                   jax.ShapeDtypeStruct((1, 128), F32), jax.ShapeDtypeStruct((1, 128), F32), jax.ShapeDtypeStruct((1, 128), F32)],
        scratch_shapes=[pltpu.VMEM((SSD_STATE, w), F32)], compiler_params=_params("arbitrary"),
    )(xbc, xbc, dtr, dtb, alog, dsk, expand, hp_all, dy)


def _gate_fn(y, z, g):
    outs = []
    for k in range(SSD_GROUPS):
        sl = slice(512 * k, 512 * (k + 1))
        yg = y[:, sl] * _silu(z[:, sl])
        outs.append(yg * lax.rsqrt(jnp.mean(yg * yg, -1, keepdims=True) + EPS) * g[:, sl])
    return jnp.concatenate(outs, axis=1)


def _gate_fwd(y, z, g, name):
    def fn(i, j, y, z, g):
        return (_gate_fn(y, z, g),), ()
    return _rows(fn, [_row(y), _row(z, D_MODEL, Z_COL)], [g], [(D_MODEL, BF16)], tm=_pick(y.shape[0], (256, 128)), name=name)[0]


def _gate_bwd(y, z, db, g, name):
    def fn(i, j, y, z, db, g):
        _, vjp = jax.vjp(_gate_fn, y, z, g)
        dy, dz, dg = vjp(db)
        return (dy, dz), (dg,)
    return _rows(fn, [_row(y), _row(z, D_MODEL, Z_COL), _row(db)], [g], [(D_MODEL, F32), (D_MODEL, BF16)], [g.shape],
                 tm=_pick(y.shape[0], (256, 128)), name=name)


def _rope(x, tab, sign=1.0):
    return x * tab[:, 0:128] + sign * (pltpu.roll(x, 128 - ROT_HALF, 1) * tab[:, 128:256] + pltpu.roll(x, ROT_HALF, 1) * tab[:, 256:384])


def _sink_softmax_parts(s, sink):
    m = jnp.maximum(jnp.max(s, -1, keepdims=True), sink)
    p = jnp.exp(s - m)
    e_sink = jnp.exp(sink - m)
    inv = 1.0 / (jnp.sum(p, -1, keepdims=True) + e_sink)
    return p * inv, e_sink * inv


@jax.custom_vjp
def _sink_softmax(s, sink):
    return _sink_softmax_parts(s, sink)[0]


def _sink_softmax_fwd(s, sink):
    pr, pr_sink = _sink_softmax_parts(s, sink)
    return pr, (pr, pr_sink)


def _sink_softmax_bwd(r, g):
    pr, pr_sink = r
    t = jnp.sum(g * pr, -1, keepdims=True)
    return pr * (g - t), -pr_sink * t


_sink_softmax.defvjp(_sink_softmax_fwd, _sink_softmax_bwd)


def _swa_mask(first, q=CHUNK):
    rows = ATT_REP * q
    iq = _iota((rows, 2 * q), 0) & (q - 1)
    js = _iota((rows, 2 * q), 1)
    rel = iq + q - js
    mask = (rel >= 0) & (rel < q) & ((js >= q) | jnp.logical_not(first))
    return mask, lax.shift_right_logical(_iota((rows, 1), 0), q.bit_length() - 1)


def _swa_core(qq, kc, vc, sinks, kv_head, mask_rep):
    mask, rep = mask_rep
    s = jnp.where(mask, _bdot_nt(qq, kc) * ATT_SCALE, -jnp.inf)
    sink = jnp.zeros((qq.shape[0], 1), F32)
    for r in range(ATT_REP):
        s_r = jnp.sum(sinks * (_iota((1, 128), 1) == kv_head * ATT_REP + r).astype(F32), axis=1, keepdims=True)
        sink = sink + jnp.where(rep == r, s_r, 0.0)
    return _bdot(_sink_softmax(s, sink), vc)


def _swa_prep(q, kv, kvp, tab, tabp, kv_head):
    w = HEAD_PAD
    kc = jnp.concatenate([_rope(kvp[:, w * kv_head:w * (kv_head + 1)], tabp), _rope(kv[:, w * kv_head:w * (kv_head + 1)], tab)], axis=0)
    o = ATT_KV_HEADS * w
    vc = jnp.concatenate([kvp[:, o + w * kv_head:o + w * (kv_head + 1)], kv[:, o + w * kv_head:o + w * (kv_head + 1)]], axis=0)
    qq = jnp.concatenate([_rope(q[:, w * (kv_head * ATT_REP + r):w * (kv_head * ATT_REP + r + 1)], tab) for r in range(ATT_REP)], axis=0)
    return qq, kc, vc


def _swa_fwd(q, kv, tab, sinks, name):
    def fn(i, j, q, kv, kvp, tab, tabp, sinks):
        mask_rep = _swa_mask(i == 0)
        outs = []
        for h in range(ATT_KV_HEADS):
            qq, kc, vc = _swa_prep(q, kv, kvp, tab, tabp, h)
            o = _swa_core(qq, kc, vc, sinks, h, mask_rep)
            outs += [o[CHUNK * r:CHUNK * (r + 1)] for r in range(ATT_REP)]
        return (jnp.concatenate(outs, axis=1),), ()
    return _rows(fn, [_row(q), _row(kv), _row(kv, shift=-1), _row(tab), _row(tab, shift=-1)], [sinks],
                 [(q.shape[1], BF16)], tm=CHUNK, name=name)[0]


def _swa_bwd(q, kv, tab, sinks, do, name):
    w = HEAD_PAD

    def fn(i, j, q, kv, kvp, tab, tabp, do, sinks):
        mask_rep = _swa_mask(i == 0)
        dqs, dkc, dkp, dvc, dvp = [], [], [], [], []
        dsink = jnp.zeros_like(sinks)
        for h in range(ATT_KV_HEADS):
            qq, kc, vc = _swa_prep(q, kv, kvp, tab, tabp, h)
            dout = jnp.concatenate([do[:, w * (h * ATT_REP + r):w * (h * ATT_REP + r + 1)] for r in range(ATT_REP)], axis=0)
            _, vjp = jax.vjp(lambda a, b, c, s: _swa_core(a, b, c, s, h, mask_rep), qq, kc, vc, sinks)
            dqq, dk, dv, ds = vjp(dout.astype(F32))
            dsink = dsink + ds
            dqs += [_rope(dqq[CHUNK * r:CHUNK * (r + 1)], tab, -1.0) for r in range(ATT_REP)]
            dkp.append(_rope(dk[:CHUNK], tabp, -1.0))
            dkc.append(_rope(dk[CHUNK:], tab, -1.0))
            dvp.append(dv[:CHUNK])
            dvc.append(dv[CHUNK:])
        return (jnp.concatenate(dqs, axis=1), jnp.concatenate(dkc + dvc, axis=1), jnp.concatenate(dkp + dvp, axis=1)), (dsink,)
    return _rows(fn, [_row(q), _row(kv), _row(kv, shift=-1), _row(tab), _row(tab, shift=-1), _row(do)], [sinks],
                 [(q.shape[1], BF16), (kv.shape[1], F32), (kv.shape[1], F32)], [sinks.shape], tm=CHUNK, name=name)


def _swa_combine(dkv_cur, dkv_prev, dq, name):
    nb = dq.shape[0] // CHUNK

    def fn(i, j, cur, nxt, dq):
        dkv = (cur + jnp.where(i < nb - 1, nxt, 0.0)).astype(BF16)
        return (dkv,), (jnp.sum(dq.astype(F32), axis=0, keepdims=True), jnp.sum(dkv.astype(F32), axis=0, keepdims=True))
    return _rows(fn, [_row(dkv_cur), _row(dkv_prev, shift=1), _row(dq)], [], [(dkv_cur.shape[1], BF16)],
                 [(1, dq.shape[1]), (1, dkv_cur.shape[1])], tm=CHUNK, name=name)


def _xattn_fn(q, k, v):
    outs = []
    for h in range(X_HEADS):
        sl = slice(X_HEAD_DIM * h, X_HEAD_DIM * (h + 1))
        s = _bdot_nt(q[:, sl], k[:, sl]) * X_SCALE
        p = jnp.exp(s - jnp.max(s, -1, keepdims=True))
        outs.append(_bdot(p / jnp.sum(p, -1, keepdims=True), v[:, sl]))
    return jnp.concatenate(outs, axis=1)


def _xattn_fwd(q, k, v, name):
    def fn(i, j, q, k, v):
        return (_xattn_fn(q, k, v),), ()
    return _rows(fn, [_row(q)], [k, v], [(q.shape[1], BF16)], tm=_pick(q.shape[0], (512, 256)), name=name)[0]


def _xattn_bwd(q, k, v, do, name):
    def fn(i, j, q, do, k, v):
        _, vjp = jax.vjp(_xattn_fn, q, k, v)
        dq, dk, dv = vjp(do)
        return (dq,), (dk, dv)
    return _rows(fn, [_row(q), _row(do)], [k, v], [(q.shape[1], BF16)], [k.shape, v.shape],
                 tm=_pick(q.shape[0], (512, 256)), name=name)


def _loss_fn(x, t, g):
    return 0.5 * jnp.sum(jnp.mean(jnp.square(_rms(x, g) - t), axis=-1))


def _loss_fwd_bwd(x, t, g, name):
    def fn(i, j, x, t, g):
        loss, vjp = jax.vjp(_loss_fn, x, t, g)
        dx, _, dg = vjp(jnp.ones((), F32))
        return (dx, dx), (jnp.broadcast_to(loss, (8, 128)), dg)
    dx, dx16, loss, dg = _rows(fn, [_row(x), _row(t)], [g], [(x.shape[1], F32), (x.shape[1], BF16)], [(8, 128), g.shape],
                               tm=_pick(x.shape[0], (256,)), name=name)
    return loss, (dx, dx16), dg


def _adamw(parts, w, m, v, name):
    plist = list(parts) if isinstance(parts, (list, tuple)) else [parts]
    nl = len(plist)
    r, c = w.shape
    tm = _pick(r // nl, (128, 64, 32, 16, 8))
    nbl = r // nl // tm

    def fn(i, j, *blocks):
        parts, (w, m, v) = blocks[0], blocks[nl:]
        for l in range(1, nl):
            parts = jnp.where(i >= l * nbl, blocks[l], parts)
        g = parts[0].astype(F32)
        for k in range(1, N_DEV):
            g = g + parts[k].astype(F32)
        m2 = ADAM_B1 * m + (1.0 - ADAM_B1) * g
        v2 = ADAM_B2 * v + (1.0 - ADAM_B2) * jnp.square(g)
        m_hat = m2 / (1.0 - ADAM_B1 ** ADAM_STEP)
        v_hat = v2 / (1.0 - ADAM_B2 ** ADAM_STEP)
        delta = -ADAM_LR * (m_hat / (jnp.sqrt(v_hat) + ADAM_EPS) + ADAM_WD * w)
        return (g, delta, m2, v2), ()
    prow = [_row(p, shift=(lambda i, l=l: jnp.clip(i - l * nbl, 0, nbl - 1))) for l, p in enumerate(plist)]
    return _rows(fn, prow + [_row(w), _row(m), _row(v)], [], [(c, F32)] * 4, tm=tm, n_rows=r, name=name)


def _peer(k):
    return (k // 4, (k // 2) % 2, k % 2)


GATHER, SCATTER = "gather", "scatter"


def _ex_scratch():
    return [pltpu.SemaphoreType.DMA((N_DEV - 1,)), pltpu.SemaphoreType.DMA((N_DEV - 1,)), pltpu.SemaphoreType.DMA]


def _ex_out_shape(x, kind):
    return jax.ShapeDtypeStruct((N_DEV,) + x.shape[-2:], x.dtype)


def _gather_plan():
    x, y, c = lax.axis_index("x"), lax.axis_index("y"), lax.axis_index("c")
    return (x, y, c), (x, y, 1 - c), [(1 - x, y), (x, 1 - y), (1 - x, 1 - y)], c


def _gather_copy(x_ref, o_ref, send, recv, k, block, to, from_input=False):
    slot = o_ref.at[4 * block[0] + 2 * block[1] + block[2]]
    return pltpu.make_async_remote_copy(src_ref=x_ref if from_input else slot, dst_ref=slot, send_sem=send.at[k],
                                        recv_sem=recv.at[k], device_id=to, device_id_type=MESH_ID)


def _gather_start(x_ref, o_ref, send, recv, local):
    me, sib, chips, c = _gather_plan()
    pltpu.make_async_copy(x_ref, o_ref.at[4 * me[0] + 2 * me[1] + me[2]], local).start()
    _gather_copy(x_ref, o_ref, send, recv, 0, me, sib, True).start()
    for j, chip in enumerate(chips):
        _gather_copy(x_ref, o_ref, send, recv, 1 + j, me, (*chip, c), True).start()


def _gather_finish(x_ref, o_ref, send, recv, local):
    me, sib, chips, c = _gather_plan()
    for j, chip in enumerate(chips):
        _gather_copy(x_ref, o_ref, send, recv, 1 + j, (*chip, c), me).wait_recv()
        _gather_copy(x_ref, o_ref, send, recv, 4 + j, (*chip, c), sib).start()
    _gather_copy(x_ref, o_ref, send, recv, 0, sib, me).wait_recv()
    for j, chip in enumerate(chips):
        _gather_copy(x_ref, o_ref, send, recv, 4 + j, (*chip, 1 - c), me).wait_recv()
    _gather_copy(x_ref, o_ref, send, recv, 0, me, sib, True).wait_send()
    for j, chip in enumerate(chips):
        _gather_copy(x_ref, o_ref, send, recv, 1 + j, me, (*chip, c), True).wait_send()
        _gather_copy(x_ref, o_ref, send, recv, 4 + j, (*chip, c), sib).wait_send()
    pltpu.make_async_copy(x_ref, o_ref.at[4 * me[0] + 2 * me[1] + me[2]], local).wait()


def _scatter_copy(x_ref, o_ref, send, recv, d, frm, to):
    return pltpu.make_async_remote_copy(src_ref=x_ref.at[to], dst_ref=o_ref.at[frm], send_sem=send.at[d - 1],
                                        recv_sem=recv.at[d - 1], device_id=_peer(to), device_id_type=MESH_ID)


def _scatter_start(x_ref, o_ref, send, recv, local):
    me = 4 * lax.axis_index("x") + 2 * lax.axis_index("y") + lax.axis_index("c")
    pltpu.make_async_copy(x_ref.at[me], o_ref.at[me], local).start()
    for d in range(1, N_DEV):
        _scatter_copy(x_ref, o_ref, send, recv, d, me, (me + d) % N_DEV).start()


def _scatter_finish(x_ref, o_ref, send, recv, local):
    me = 4 * lax.axis_index("x") + 2 * lax.axis_index("y") + lax.axis_index("c")
    for d in range(1, N_DEV):
        _scatter_copy(x_ref, o_ref, send, recv, d, (me + N_DEV - d) % N_DEV, me).wait_recv()
    for d in range(1, N_DEV):
        _scatter_copy(x_ref, o_ref, send, recv, d, me, (me + d) % N_DEV).wait_send()
    pltpu.make_async_copy(x_ref.at[me], o_ref.at[me], local).wait()


_EX_START = {GATHER: _gather_start, SCATTER: _scatter_start}
_EX_FINISH = {GATHER: _gather_finish, SCATTER: _scatter_finish}


def _exchange(x, kind, name):
    def body(x_ref, o_ref, send, recv, local):
        _EX_START[kind](x_ref, o_ref, send, recv, local)
        _EX_FINISH[kind](x_ref, o_ref, send, recv, local)

    return pl.pallas_call(
        body, name=name, in_specs=[pl.BlockSpec(memory_space=pl.ANY)], out_specs=pl.BlockSpec(memory_space=pl.ANY),
        out_shape=_ex_out_shape(x, kind), scratch_shapes=_ex_scratch(),
    )(x)


class _Freight:
    def __init__(self):
        self.load = {}
        self.landed = {}

    def put(self, carrier, key, arr, kind):
        self.load.setdefault(carrier, []).append((key, arr, kind))

    def take(self, carrier):
        return self.load.pop(carrier, [])


def _all_gather(x, name):
    return _exchange(x.reshape(-1, x.shape[-1]), GATHER, name).reshape((N_DEV,) + x.shape)


def _all_to_all(x, name):
    return _exchange(x.reshape(N_DEV, -1, x.shape[-1]), SCATTER, name).reshape(x.shape)


def _pack(arrs):
    flat = []
    for a in arrs:
        a = a.reshape(-1).astype(F32)
        flat.append(jnp.pad(a, (0, (-a.shape[0]) % 1024)))
    return jnp.concatenate(flat).reshape(-1, 128)


def _unpack(p, shapes):
    p = p.reshape(-1)
    out, off = [], 0
    for s in shapes:
        n = 1
        for d in s:
            n *= d
        out.append(p[off:off + n].reshape(s))
        off += n + (-n) % 1024
    return out


def _cols_to_full(g):
    return g.transpose(1, 0, 2).reshape(g.shape[1], -1)


def _full_to_cols(w, shards=N_DEV):
    return w.reshape(w.shape[0], shards, -1).transpose(1, 0, 2)


SMALL = ("norm_ffn1", "norm_mix", "gm_ln_g", "gm_ln_b", "gm_ws", "gm_bs", "conv_b", "dt_bias", "a_log", "d_skip", "ssd_norm",
         "sinks", "norm_xq", "norm_mem", "norm_ffn2", "final_norm")
SHARDED = ("w_ffn1_gu", "w_ffn1_down", "w_in_even", "w_out_even", "w_qkv", "w_o_odd", "w_xq", "w_xkv", "w_xo", "w_ffn2_gu",
           "w_ffn2_down")
SMALL_SHARDED = ("conv_w", "b_qkv")
ORDER = ("norm_ffn1", "w_ffn1_gu", "w_ffn1_down", "norm_mix", "w_in_even", "gm_ln_g", "gm_ln_b", "gm_ws", "gm_bs", "conv_w",
         "conv_b", "dt_bias", "a_log", "d_skip", "ssd_norm", "w_out_even", "w_qkv", "b_qkv", "sinks", "w_o_odd", "norm_xq",
         "norm_mem", "w_xq", "w_xkv", "w_xo", "norm_ffn2", "w_ffn2_gu", "w_ffn2_down", "final_norm")


def _ffn_fwd(x, gain, wg, wu, wd, tag, freight=None):
    mm = functools.partial(_matmul, freight=freight)
    h = _rms_fwd(x, gain, f"{tag}_norm")
    g = mm(h, wg, out_dtype=BF16, name=f"{tag}_gate")
    u, act = mm(h, wu, out_dtype=(BF16, BF16), extras=[g], epilogue=_swiglu_tile, name=f"{tag}_up")
    wd = wd() if callable(wd) else wd
    y = mm(act, wd, alpha=0.5, res=x, name=f"{tag}_down")
    return y, (x, h, g, u, act), wd


def _ffn_bwd(dy, saved, gain, wg, wu, wd, tag, freight=None, ship=None):
    mm = functools.partial(_matmul, freight=freight)
    x, h, g, u, act = saved
    dy, dy16 = dy
    dg, du = mm(dy16, wd, tb=True, alpha=0.5, out_dtype=(BF16, BF16), extras=[g, u], epilogue=_swiglu_bwd_tile, name=f"{tag}_dact")
    dwd = mm(act, dy16, ta=True, alpha=0.5, out_dtype=BF16, name=f"{tag}_dwd")
    if ship is not None:
        ship(f"{tag}_dwg", "dn", dwd.reshape(N_DEV, -1, dwd.shape[1]))
    dwg = mm(h, dg, ta=True, out_dtype=BF16, name=f"{tag}_dwg")
    dwu = mm(h, du, ta=True, out_dtype=BF16, name=f"{tag}_dwu")
    if ship is not None:
        half = N_DEV // 2
        parts = jnp.concatenate([_full_to_cols(dwg, half), _full_to_cols(dwu, half)], axis=0)
        rows = parts.shape[1] // 2
        ship(f"{tag}_dh_g", "gu_a", parts[:, :rows])
        ship(f"{tag}_dh_u", "gu_b", parts[:, rows:])
    dh = mm(dg, wg, tb=True, name=f"{tag}_dh_g")
    dh = mm(du, wu, tb=True, res=dh, name=f"{tag}_dh_u")
    dx, dgain = _rms_bwd(x, dh, dy, gain, f"{tag}_dnorm")
    return dx, dgain, dwg, dwu, dwd


def _xattn_layer_fwd(x, mem, gq, gm, wq, wkv, wo, tag, freight=None):
    mm = functools.partial(_matmul, freight=freight)
    hq = _rms_fwd(x, gq, f"{tag}_normq")
    mn = _rms_fwd(mem, gm, f"{tag}_normm")
    q = mm(hq, wq, name=f"{tag}_q")
    kv = mm(mn, wkv, name=f"{tag}_kv")
    k, v = kv[:, :X_HEADS * X_HEAD_DIM], kv[:, X_HEADS * X_HEAD_DIM:]
    o = _xattn_fwd(q, k, v, f"{tag}_attn")
    y = mm(o, wo, res=x, name=f"{tag}_o")
    return y, (x, hq, mn, q, k, v, o)


def _xattn_layer_bwd(dy, saved, mem, gq, gm, wq, wkv, wo, tag, freight=None):
    mm = functools.partial(_matmul, freight=freight)
    x, hq, mn, q, k, v, o = saved
    dy, dy16 = dy
    do = mm(dy16, wo, tb=True, name=f"{tag}_do")
    dwo = mm(o, dy16, ta=True, out_dtype=BF16, name=f"{tag}_dwo")
    dq, dk, dv = _xattn_bwd(q, k, v, do, f"{tag}_dattn")
    dkv = jnp.concatenate([dk, dv], axis=1)
    dwq = mm(hq, dq, ta=True, out_dtype=BF16, name=f"{tag}_dwq")
    dwkv = mm(mn, dkv, ta=True, out_dtype=BF16, name=f"{tag}_dwkv")
    dhq = mm(dq, wq, tb=True, name=f"{tag}_dhq")
    dmn = mm(dkv, wkv, tb=True, name=f"{tag}_dmn")
    dx, dgq = _rms_bwd(x, dhq, dy, gq, f"{tag}_dnormq")
    dgm = _rms_bwd_gain(mem, dmn, gm, f"{tag}_dnormm")
    return dx, dgq, dgm, dwq, dwkv, dwo


def _even_fwd(x, weights, params, freight=None):
    mm = functools.partial(_matmul, freight=freight)
    w_main, w_dt, w_out_a, w_out_b = weights
    gain, lng, lnb, ws, bst, conv_w, conv_b, dtb, alog, dsk, ssd_norm, expand = params
    hm = _rms_fwd(x, gain, "l0_normmix")
    proj = mm(hm, w_main, name="l0_proj")
    dtr = mm(hm, w_dt, name="l0_dt")
    a_out = _gmlp_fwd(proj, lng, lnb, ws, bst, "l0_gmlp")
    xbc = _conv_fwd(proj, conv_w, conv_b, "l0_conv")
    y_ssd, hp_all = _ssd_fwd(xbc, dtr, dtb, alog, dsk, expand, "l0_ssd")
    b_out = _gate_fwd(y_ssd, proj, ssd_norm, "l0_gate")
    y = mm(a_out, w_out_a, res=x, name="l0_out_a")
    y = mm(b_out, w_out_b, res=y, name="l0_out_b")
    return y, (x, hm, proj, dtr, a_out, xbc, y_ssd, hp_all, b_out)


def _even_bwd(dx, saved, weights, params, freight=None, ship_out=None):
    mm = functools.partial(_matmul, freight=freight)
    w_main, w_dt, w_out_a, w_out_b = weights
    w_uv, w_z, w_xbc = w_main[:, :4096], w_main[:, 4096:6144], w_main[:, 6144:]
    gain, lng, lnb, ws, bst, conv_w, conv_b, dtb, alog, dsk, ssd_norm, expand = params
    x, hm, proj, dtr, a_out, xbc, y_ssd, hp_all, b_out = saved
    uv = zz = xbc_raw = proj
    dx, dx16 = dx
    da_out = mm(dx16, w_out_a, tb=True, name="l0_da")
    db_out = mm(dx16, w_out_b, tb=True, name="l0_db")
    dw_out_a = mm(a_out, dx16, ta=True, out_dtype=BF16, name="l0_dwout_a")
    dw_out_b = mm(b_out, dx16, ta=True, out_dtype=BF16, name="l0_dwout_b")
    dw_out = jnp.concatenate([dw_out_a, dw_out_b], axis=0)
    if ship_out is not None:
        ship_out(dw_out)
    dy_ssd, dzz, d_ssd_norm = _gate_bwd(y_ssd, zz, db_out, ssd_norm, "l0_dgate")
    dxbc, ddtr, d_dtb, d_alog, d_dsk = _ssd_bwd(xbc, dtr, dtb, alog, dsk, expand, hp_all, dy_ssd, "l0_dssd")
    dxbc_raw, d_conv_w, d_conv_b = _conv_bwd(xbc_raw, dxbc, conv_w, conv_b, "l0_dconv")
    duv, d_lng, d_lnb, d_ws, d_bst = _gmlp_bwd(uv, da_out, lng, lnb, ws, bst, "l0_dgmlp")
    ddtr16 = ddtr.astype(BF16)
    dhm = mm(duv, w_uv, tb=True, name="l0_dh_uv")
    dhm = mm(dzz, w_z, tb=True, res=dhm, name="l0_dh_z")
    dhm = mm(dxbc_raw, w_xbc, tb=True, res=dhm, name="l0_dh_xbc")
    dhm = mm(ddtr16, w_dt, tb=True, res=dhm, name="l0_dh_dt")
    dw_uv = mm(hm, duv, ta=True, out_dtype=BF16, name="l0_dwuv")
    dw_z = mm(hm, dzz, ta=True, out_dtype=BF16, name="l0_dwz")
    dw_xbc = mm(hm, dxbc_raw, ta=True, out_dtype=BF16, name="l0_dwxbc")
    dw_dt = mm(hm, ddtr16, ta=True, out_dtype=BF16, name="l0_dwdt")
    dx, d_gain = _rms_bwd(x, dhm, dx, gain, "l0_dnormmix")
    small = (d_gain, d_lng, d_lnb, d_ws, d_bst, d_conv_w, d_conv_b, d_dtb, d_alog, d_dsk, d_ssd_norm)
    return dx, small, (dw_uv, dw_z, dw_xbc, dw_dt, dw_out)


def _odd_fwd(x, weights, params, freight=None):
    mm = functools.partial(_matmul, freight=freight)
    w_q, w_kv, w_o = weights
    gain, b_q, b_kv, tab, snk = params
    hm = _rms_fwd(x, gain, "l1_normmix")
    q = mm(hm, w_q, bias=b_q, name="l1_q")
    kv = mm(hm, w_kv, bias=b_kv, name="l1_kv")
    o = _swa_fwd(q, kv, tab, snk, "l1_swa")
    y = mm(o, w_o, res=x, name="l1_o")
    return y, (x, hm, q, kv, o)


def _odd_bwd(dx, saved, weights, params, freight=None):
    mm = functools.partial(_matmul, freight=freight)
    w_q, w_kv, w_o = weights
    gain, b_q, b_kv, tab, snk = params
    x, hm, q, kv, o = saved
    dx, dx16 = dx
    do = mm(dx16, w_o, tb=True, out_dtype=BF16, name="l1_do")
    dw_o = mm(o, dx16, ta=True, out_dtype=BF16, name="l1_dwo")
    dq, dkv_cur, dkv_prev, d_snk = _swa_bwd(q, kv, tab, snk, do, "l1_dswa")
    dkv, db_q, db_kv = _swa_combine(dkv_cur, dkv_prev, dq, "l1_dkv")
    dhm = mm(dq, w_q, tb=True, name="l1_dh_q")
    dhm = mm(dkv, w_kv, tb=True, res=dhm, name="l1_dh_kv")
    dw_q = mm(hm, dq, ta=True, out_dtype=BF16, name="l1_dwq")
    dw_kv = mm(hm, dkv, ta=True, out_dtype=BF16, name="l1_dwkv")
    dx, d_gain = _rms_bwd(x, dhm, dx, gain, "l1_dnormmix")
    return dx, (d_gain, db_q, db_kv, d_snk), (dw_q, dw_kv, dw_o)


def _rope_table(positions):
    seq = positions.size
    inv_freq = ROPE_THETA ** (-jnp.arange(0, 2 * ROT_HALF, 2, dtype=F32) / (2 * ROT_HALF))
    ang = positions.reshape(seq, 1).astype(F32) * inv_freq
    cos, sin, zero = jnp.cos(ang), jnp.sin(ang), jnp.zeros((seq, 128 - 2 * ROT_HALF), F32)
    z8 = jnp.zeros((seq, ROT_HALF), F32)
    return jnp.concatenate([cos, cos, zero + 1.0, -sin, z8, zero, z8, sin, zero], axis=1)


def _pad_heads_cols(w, heads):
    k = w.shape[0]
    return jnp.pad(w.reshape(k, heads, ATT_HEAD_DIM), ((0, 0), (0, 0), (0, HEAD_PAD - ATT_HEAD_DIM))).reshape(k, heads * HEAD_PAD)


def _unpad_heads_cols(w, heads):
    k = w.shape[0]
    return w.reshape(k, heads, HEAD_PAD)[:, :, :ATT_HEAD_DIM].reshape(k, heads * ATT_HEAD_DIM)


def kernel(x, mem, positions, norm_ffn1, w_ffn1_gu, w_ffn1_down, norm_mix, w_in_even, gm_ln_g, gm_ln_b, gm_ws, gm_bs, conv_w, conv_b, dt_bias, a_log, d_skip, ssd_norm, w_out_even, w_qkv, b_qkv, sinks, w_o_odd, norm_xq, norm_mem, w_xq, w_xkv, w_xo, norm_ffn2, w_ffn2_gu, w_ffn2_down, final_norm, loss_target, m_norm_ffn1, m_w_ffn1_gu, m_w_ffn1_down, m_norm_mix, m_w_in_even, m_gm_ln_g, m_gm_ln_b, m_gm_ws, m_gm_bs, m_conv_w, m_conv_b, m_dt_bias, m_a_log, m_d_skip, m_ssd_norm, m_w_out_even, m_w_qkv, m_b_qkv, m_sinks, m_w_o_odd, m_norm_xq, m_norm_mem, m_w_xq, m_w_xkv, m_w_xo, m_norm_ffn2, m_w_ffn2_gu, m_w_ffn2_down, m_final_norm, v_norm_ffn1, v_w_ffn1_gu, v_w_ffn1_down, v_norm_mix, v_w_in_even, v_gm_ln_g, v_gm_ln_b, v_gm_ws, v_gm_bs, v_conv_w, v_conv_b, v_dt_bias, v_a_log, v_d_skip, v_ssd_norm, v_w_out_even, v_w_qkv, v_b_qkv, v_sinks, v_w_o_odd, v_norm_xq, v_norm_mem, v_w_xq, v_w_xkv, v_w_xo, v_norm_ffn2, v_w_ffn2_gu, v_w_ffn2_down, v_final_norm):
    env = dict(locals())
    W = {n: env[n] for n in ORDER}
    M = {n: env["m_" + n] for n in ORDER}
    V = {n: env["v_" + n] for n in ORDER}
    seq = x.shape[1]
    x0 = x.reshape(seq, D_MODEL)
    mem2 = mem.reshape(-1, D_MODEL)
    target = loss_target.reshape(seq, D_MODEL)

    fr = _Freight()
    b16 = lambda a: a.astype(BF16)
    grp_a = {"out": w_out_even[0], "xq0": w_xq[0], "xkv0": w_xkv[0], "xo0": w_xo[0]}
    grp_b = {"qkv": w_qkv[0], "o": w_o_odd[0], "xq1": w_xq[1], "xkv1": w_xkv[1], "xo1": w_xo[1]}
    half_rows = D_MODEL // 2
    fr.put("l0f1_gate", "dn1_0", b16(w_ffn1_down[0]), GATHER)
    fr.put("l0f1_gate", "dn2_0", b16(w_ffn2_down[0]), GATHER)
    fr.put("l0f1_up", "in", b16(w_in_even[0]), GATHER)
    for key, w in grp_a.items():
        fr.put("l0f1_down", key, b16(w), GATHER)
    fr.put("l0_proj", "gu2_0", b16(w_ffn2_gu[0]), GATHER)
    fr.put("l0f2_gate", "gu1_1a", b16(w_ffn1_gu[1, :half_rows]), GATHER)
    fr.put("l0f2_gate", "dn1_1", b16(w_ffn1_down[1]), GATHER)
    fr.put("l0f2_up", "gu1_1b", b16(w_ffn1_gu[1, half_rows:]), GATHER)
    for key, w in grp_b.items():
        fr.put("l0f2_down", key, b16(w), GATHER)
    fr.put("l1f1_gate", "gu2_1a", b16(w_ffn2_gu[1, :half_rows]), GATHER)
    fr.put("l1f1_gate", "dn2_1", b16(w_ffn2_down[1]), GATHER)
    fr.put("l1f1_up", "gu2_1b", b16(w_ffn2_gu[1, half_rows:]), GATHER)
    gu_first = _exchange(b16(w_ffn1_gu[0]), GATHER, "ag_l0f1_gu")
    gs = _all_gather(_pack([conv_w, b_qkv]), "ag_small")
    gs = [_unpack(gs[k], [conv_w.shape, b_qkv.shape]) for k in range(N_DEV)]
    conv_w_full = jnp.concatenate([g[0][0] for g in gs], axis=1)
    b_qkv_full = jnp.concatenate([g[1][0] for g in gs], axis=0)

    def gate_up(g):
        return _cols_to_full(g[:N_DEV // 2]), _cols_to_full(g[N_DEV // 2:])

    down = lambda key: fr.landed[key].reshape(D_FF, D_MODEL)

    row = lambda a: a.reshape(1, -1)
    pad128 = lambda a: jnp.pad(a.reshape(1, -1), ((0, 0), (0, 128 - a.size)))
    bst = jnp.pad(gm_bs[0].T, ((0, 0), (0, 128 - GM_GROUPS)))
    ws = gm_ws[0]
    dtb, alog, dsk, snk = pad128(dt_bias), pad128(a_log), pad128(d_skip), pad128(sinks)
    expand = (jnp.arange(128)[:, None] == (jnp.arange(D_MODEL) // SSD_HEAD_DIM)[None, :]).astype(F32)
    tab = _rope_table(positions)

    wg_f1a, wu_f1a = gate_up(gu_first)
    xa, s_f1a, wd_f1a = _ffn_fwd(x0, row(norm_ffn1[0]), wg_f1a, wu_f1a, lambda: down("dn1_0"), "l0f1", fr)
    w_in = _cols_to_full(fr.landed["in"])
    n_main = 2 * GM_GROUPS * GM_GDIM + D_MODEL + XBC_WIDTH
    w_main = w_in[:, :n_main]
    w_dt = jnp.pad(w_in[:, n_main:], ((0, 0), (0, 128 - SSD_HEADS)))
    a_xq, a_xkv, a_xo = fr.landed["xq0"], fr.landed["xkv0"], fr.landed["xo0"]
    w_out = fr.landed["out"].reshape(2 * D_MODEL, D_MODEL)
    even_w = (w_main, w_dt, w_out[:D_MODEL], w_out[D_MODEL:])
    even_p = (row(norm_mix[0]), gm_ln_g, gm_ln_b, ws, bst, conv_w_full, conv_b, dtb, alog, dsk, ssd_norm, expand)
    xb, s_even = _even_fwd(xa, even_w, even_p, fr)
    x0_w = (a_xq.reshape(D_MODEL, -1), a_xkv.reshape(D_MODEL, -1), _cols_to_full(a_xo))
    xc, s_x0 = _xattn_layer_fwd(xb, mem2, row(norm_xq[0]), row(norm_mem[0]), *x0_w, "l0x", fr)
    wg_f2a, wu_f2a = gate_up(fr.landed["gu2_0"])
    xd, s_f2a, wd_f2a = _ffn_fwd(xc, row(norm_ffn2[0]), wg_f2a, wu_f2a, down("dn2_0"), "l0f2", fr)
    wg_f1b, wu_f1b = gate_up(jnp.concatenate([fr.landed["gu1_1a"], fr.landed["gu1_1b"]], axis=1))
    xe, s_f1b, wd_f1b = _ffn_fwd(xd, row(norm_ffn1[1]), wg_f1b, wu_f1b, down("dn1_1"), "l1f1", fr)
    b_xq, b_xkv, b_xo = fr.landed["xq1"], fr.landed["xkv1"], fr.landed["xo1"]
    nq = ATT_HEADS * ATT_HEAD_DIM
    wqkv = _cols_to_full(fr.landed["qkv"])
    w_o = _pad_heads_cols(fr.landed["o"].reshape(D_MODEL, D_MODEL).T, ATT_HEADS).T
    odd_w = (_pad_heads_cols(wqkv[:, :nq], ATT_HEADS), _pad_heads_cols(wqkv[:, nq:], 2 * ATT_KV_HEADS), w_o)
    odd_p = (row(norm_mix[1]), _pad_heads_cols(b_qkv_full[None, :nq], ATT_HEADS),
             _pad_heads_cols(b_qkv_full[None, nq:], 2 * ATT_KV_HEADS), tab, snk)
    xf, s_odd = _odd_fwd(xe, odd_w, odd_p, fr)
    x1_w = (b_xq.reshape(D_MODEL, -1), b_xkv.reshape(D_MODEL, -1), _cols_to_full(b_xo))
    xg, s_x1 = _xattn_layer_fwd(xf, mem2, row(norm_xq[1]), row(norm_mem[1]), *x1_w, "l1x", fr)
    wg_f2b, wu_f2b = gate_up(jnp.concatenate([fr.landed["gu2_1a"], fr.landed["gu2_1b"]], axis=1))
    xh, s_f2b, wd_f2b = _ffn_fwd(xg, row(norm_ffn2[1]), wg_f2b, wu_f2b, down("dn2_1"), "l1f2", fr)
    loss8, dx, d_final = _loss_fwd_bwd(xh, target, row(final_norm), "loss")
    loss = lax.psum(loss8[0, 0], ("x", "y", "c"))
    assert not fr.load, sorted(fr.load)

    def shipper(tag):
        return lambda carrier, key, parts: fr.put(carrier, f"{tag}_{key}", parts, SCATTER)

    dx, dn_f2b, _, _, _ = _ffn_bwd(dx, s_f2b, row(norm_ffn2[1]), wg_f2b, wu_f2b, wd_f2b, "l1f2", fr, shipper("l1f2"))
    dx, dn_xq1, dn_mem1, dwxq1, dwxkv1, dwxo1 = _xattn_layer_bwd(dx, s_x1, mem2, row(norm_xq[1]), row(norm_mem[1]), *x1_w, "l1x", fr)
    dx, (dn_mix1, db_q, db_kv, d_snk), (dw_q, dw_kv, dw_o) = _odd_bwd(dx, s_odd, odd_w, odd_p, fr)
    rows_parts = lambda g: g.reshape(N_DEV, -1, g.shape[1])
    dwqkv = jnp.concatenate([_unpad_heads_cols(dw_q, ATT_HEADS), _unpad_heads_cols(dw_kv, 2 * ATT_KV_HEADS)], axis=1)
    parts_b = {"d_qkv": _full_to_cols(dwqkv), "d_o": rows_parts(_unpad_heads_cols(dw_o.T, ATT_HEADS).T), "d_xq1": rows_parts(dwxq1),
               "d_xkv1": rows_parts(dwxkv1), "d_xo1": _full_to_cols(dwxo1)}
    for key, parts in parts_b.items():
        fr.put("l1f1_dact" if key in ("d_qkv", "d_o") else "l1f1_dwd", key, parts, SCATTER)
    dx, dn_f1b, _, _, _ = _ffn_bwd(dx, s_f1b, row(norm_ffn1[1]), wg_f1b, wu_f1b, wd_f1b, "l1f1", fr, shipper("l1f1"))
    dx, dn_f2a, _, _, _ = _ffn_bwd(dx, s_f2a, row(norm_ffn2[0]), wg_f2a, wu_f2a, wd_f2a, "l0f2", fr, shipper("l0f2"))
    dx, dn_xq0, dn_mem0, dwxq0, dwxkv0, dwxo0 = _xattn_layer_bwd(dx, s_x0, mem2, row(norm_xq[0]), row(norm_mem[0]), *x0_w, "l0x", fr)
    for key, parts in {"d_xq0": rows_parts(dwxq0), "d_xkv0": rows_parts(dwxkv0), "d_xo0": _full_to_cols(dwxo0)}.items():
        fr.put("l0_da", key, parts, SCATTER)
    ship_out = lambda dw_out: fr.put("l0_dh_uv", "d_out", rows_parts(dw_out), SCATTER)
    dx, small_even, (dw_uv, dw_z, dw_xbc, dw_dt, _) = _even_bwd(dx, s_even, even_w, even_p, fr, ship_out)
    dn_mix0, d_lng, d_lnb, d_ws, d_bst, d_conv_w, d_conv_b, d_dtb, d_alog, d_dsk, d_ssd_norm = small_even
    d_in = _full_to_cols(jnp.concatenate([dw_uv, dw_z, dw_xbc, dw_dt[:, :SSD_HEADS]], axis=1))
    fr.put("l0f1_dact", "d_in_a", d_in[:, :half_rows], SCATTER)
    fr.put("l0f1_dwd", "d_in_b", d_in[:, half_rows:], SCATTER)
    dx, dn_f1a, _, _, _ = _ffn_bwd(dx, s_f1a, row(norm_ffn1[0]), wg_f1a, wu_f1a, wd_f1a, "l0f1", fr, shipper("l0f1"))
    assert not fr.load, sorted(fr.load)
    grad_x = dx[0].reshape(x.shape)

    got = fr.landed
    received = {
        "w_ffn1_gu": [got["l0f1_gu_a"], got["l0f1_gu_b"], got["l1f1_gu_a"], got["l1f1_gu_b"]],
        "w_ffn2_gu": [got["l0f2_gu_a"], got["l0f2_gu_b"], got["l1f2_gu_a"], got["l1f2_gu_b"]],
        "w_ffn1_down": [got["l0f1_dn"], got["l1f1_dn"]], "w_ffn2_down": [got["l0f2_dn"], got["l1f2_dn"]],
        "w_in_even": [got["d_in_a"], got["d_in_b"]], "w_out_even": [got["d_out"]], "w_qkv": [got["d_qkv"]], "w_o_odd": [got["d_o"]],
        "w_xq": [got["d_xq0"], got["d_xq1"]], "w_xkv": [got["d_xkv0"], got["d_xkv1"]], "w_xo": [got["d_xo0"], got["d_xo1"]],
    }
    out = {}
    for n in SHARDED:
        shp = W[n].shape
        two = lambda a: a.reshape(-1, shp[-1])
        res = _adamw(received[n], two(W[n]), two(M[n]), two(V[n]), f"adam_{n}")
        out[n] = [r.reshape(shp) for r in res]

    db_qkv = jnp.concatenate([_unpad_heads_cols(db_q, ATT_HEADS), _unpad_heads_cols(db_kv, 2 * ATT_KV_HEADS)], axis=1).reshape(-1)
    cw_parts = _full_to_cols(d_conv_w)
    bq_parts = db_qkv.reshape(N_DEV, -1)
    ss_parts = jnp.stack([_pack([cw_parts[k], bq_parts[k]]) for k in range(N_DEV)])
    recv = _all_to_all(ss_parts, "a2a_small")
    res = _adamw(recv, _pack([conv_w, b_qkv]), _pack([m_conv_w, m_b_qkv]), _pack([v_conv_w, v_b_qkv]), "adam_small_sharded")
    res = [_unpack(r, [conv_w.shape, b_qkv.shape]) for r in res]
    out["conv_w"] = [r[0] for r in res]
    out["b_qkv"] = [r[1] for r in res]

    small_grads = {
        "norm_ffn1": jnp.concatenate([dn_f1a, dn_f1b]), "norm_mix": jnp.concatenate([dn_mix0, dn_mix1]),
        "gm_ln_g": d_lng, "gm_ln_b": d_lnb, "gm_ws": d_ws[None], "gm_bs": d_bst[:, :GM_GROUPS].T[None],
        "conv_b": d_conv_b, "dt_bias": d_dtb[:, :SSD_HEADS], "a_log": d_alog[:, :SSD_HEADS], "d_skip": d_dsk[:, :SSD_HEADS],
        "ssd_norm": d_ssd_norm, "sinks": d_snk[:, :ATT_HEADS], "norm_xq": jnp.concatenate([dn_xq0, dn_xq1]),
        "norm_mem": jnp.concatenate([dn_mem0, dn_mem1]), "norm_ffn2": jnp.concatenate([dn_f2a, dn_f2b]),
        "final_norm": d_final.reshape(-1),
    }
    shapes = [W[n].shape for n in SMALL]
    recv = _all_gather(_pack([small_grads[n] for n in SMALL]), "ag_small_grads")
    res = _adamw(recv, _pack([W[n] for n in SMALL]), _pack([M[n] for n in SMALL]), _pack([V[n] for n in SMALL]), "adam_small")
    res = [_unpack(r, shapes) for r in res]
    for i, n in enumerate(SMALL):
        out[n] = [r[i] for r in res]

    return (loss, grad_x, *[out[n][0] for n in ORDER], *[out[n][1] for n in ORDER], *[out[n][2] for n in ORDER],
            *[out[n][3] for n in ORDER])
```

```python
import functools

import jax
import jax.numpy as jnp
from jax import lax
from jax.experimental import pallas as pl
from jax.experimental.pallas import tpu as pltpu

F32, BF16 = jnp.float32, jnp.bfloat16

N_DEV = 8
D_MODEL = 2048
D_FF = 5632
EPS = 1e-5
CHUNK = 128
GM_GROUPS, GM_GDIM = 4, 512
SSD_HEADS, SSD_HEAD_DIM, SSD_GROUPS, SSD_STATE = 32, 64, 4, 128
XBC_WIDTH = D_MODEL + 2 * SSD_GROUPS * SSD_STATE
Z_COL = 2 * GM_GROUPS * GM_GDIM // D_MODEL
XBC_COL = (2 * GM_GROUPS * GM_GDIM + D_MODEL) // XBC_WIDTH
ATT_HEADS, ATT_KV_HEADS, ATT_HEAD_DIM, ATT_REP = 32, 4, 64, 8
HEAD_PAD = 128
ROT_HALF = 8
ROPE_THETA = 500000.0
ATT_SCALE = ATT_HEAD_DIM ** -0.5
X_HEADS, X_HEAD_DIM = 4, 128
X_SCALE = X_HEAD_DIM ** -0.5
ADAM_LR, ADAM_B1, ADAM_B2, ADAM_EPS, ADAM_WD, ADAM_STEP = 0.001, 0.9, 0.999, 1e-08, 0.01, 10

VMEM_LIMIT_BYTES = 56 * 1024 * 1024
MESH_ID = pl.DeviceIdType.MESH


def _params(*sem):
    return pltpu.CompilerParams(dimension_semantics=sem, vmem_limit_bytes=VMEM_LIMIT_BYTES)


def _pick(n, cands):
    for c in cands:
        if n % c == 0:
            return c
    return n


def _dg(a, b, ca, cb, precision=None):
    return lax.dot_general(a, b, (((ca,), (cb,)), ((), ())), precision=precision, preferred_element_type=F32)


@jax.custom_vjp
def _bdot(a, b):
    return _dg(a.astype(BF16), b.astype(BF16), 1, 0)


def _bdot_fwd(a, b):
    return _bdot(a, b), (a, b)


def _bdot_bwd(r, g):
    a, b = r
    g = g.astype(BF16)
    return _dg(g, b.astype(BF16), 1, 1), _dg(a.astype(BF16), g, 0, 0)


_bdot.defvjp(_bdot_fwd, _bdot_bwd)


@jax.custom_vjp
def _bdot_nt(a, b):
    return _dg(a.astype(BF16), b.astype(BF16), 1, 1)


def _bdot_nt_fwd(a, b):
    return _bdot_nt(a, b), (a, b)


def _bdot_nt_bwd(r, g):
    a, b = r
    g = g.astype(BF16)
    return _dg(g, b.astype(BF16), 1, 0), _dg(g, a.astype(BF16), 0, 0)


_bdot_nt.defvjp(_bdot_nt_fwd, _bdot_nt_bwd)


@jax.custom_vjp
def _bdot_tn(a, b):
    return _dg(a.astype(BF16), b.astype(BF16), 0, 0)


def _bdot_tn_fwd(a, b):
    return _bdot_tn(a, b), (a, b)


def _bdot_tn_bwd(r, g):
    a, b = r
    g = g.astype(BF16)
    return _dg(b.astype(BF16), g, 1, 1), _dg(a.astype(BF16), g, 1, 0)


_bdot_tn.defvjp(_bdot_tn_fwd, _bdot_tn_bwd)


def _split3(x):
    hi = x.astype(BF16)
    r1 = x - hi.astype(F32)
    mid = r1.astype(BF16)
    lo = (r1 - mid.astype(F32)).astype(BF16)
    return hi, mid, lo


def _dot_exact(x, one, cx, co, x_is_lhs):
    one = one.astype(BF16)
    out = None
    for piece in _split3(x):
        term = _dg(piece, one, cx, co) if x_is_lhs else _dg(one, piece, co, cx)
        out = term if out is None else out + term
    return out


@jax.custom_vjp
def _spread(a, e):
    return _dot_exact(a, e, 1, 0, True)


def _spread_fwd(a, e):
    return _spread(a, e), e


def _spread_bwd(e, g):
    return _dot_exact(g, e, 1, 1, True), jnp.zeros_like(e)


_spread.defvjp(_spread_fwd, _spread_bwd)


@jax.custom_vjp
def _running_sum(t, a):
    return _dot_exact(a, t, 0, 1, False)


def _running_sum_fwd(t, a):
    return _running_sum(t, a), t


def _running_sum_bwd(t, g):
    return jnp.zeros_like(t), _dot_exact(g, t, 0, 0, False)


_running_sum.defvjp(_running_sum_fwd, _running_sum_bwd)


def _sigmoid(x):
    return 1.0 / (1.0 + jnp.exp(-x))


def _silu(x):
    return x * _sigmoid(x)


def _gelu(x):
    return 0.5 * x * (1.0 + lax.erf(x * 0.7071067811865476))


def _softplus(x):
    return jnp.maximum(x, 0.0) + jnp.log1p(jnp.exp(-jnp.abs(x)))


def _rms(x, g):
    return x * lax.rsqrt(jnp.mean(x * x, -1, keepdims=True) + EPS) * g


def _iota(shape, dim):
    return lax.broadcasted_iota(jnp.int32, shape, dim)


MXU_DIM = 256
MATMUL_VMEM_BYTES = 44 * 1024 * 1024
MXU_FLOPS = 9.0e14
HBM_BYTES_PER_S = 3.0e12
STEP_SECONDS = 0.35e-6


def _matmul_tiles(m, n, kk, a_item, b_item, o_item, has_res):
    def divisors(d, cands):
        return [c for c in cands if d % c == 0] or [d]

    def pad(d):
        return -(-d // MXU_DIM) * MXU_DIM

    best = None
    for tm in divisors(m, (1024, 512, 256, 128)):
        for tn in divisors(n, (2816, 2048, 1408, 1024, 512, 256, 128)):
            for tk in divisors(kk, (2816, 2048, 1408, 1024, 512, 256, 128)):
                o_bytes = tm * tn * (o_item + (4 if has_res else 0))
                if 2 * (tm * tk * a_item + tk * tn * b_item + o_bytes) + tm * tn * 4 > MATMUL_VMEM_BYTES:
                    continue
                nk = kk // tk
                a_bytes = tm * tk * a_item / (1 if nk > 1 else n // tn)
                seconds = max(2 * tm * pad(tn) * pad(tk) / MXU_FLOPS, (a_bytes + tk * tn * b_item + o_bytes / nk) / HBM_BYTES_PER_S)
                total = (m // tm) * (n // tn) * nk * (seconds + STEP_SECONDS)
                if best is None or total < best[0]:
                    best = (total, tm, tn, tk)
    return best[1:]


def _matmul(a, b, *, ta=False, tb=False, out_dtype=F32, alpha=1.0, bias=None, res=None, name, freight=None,
            extras=(), epilogue=None):
    cargo = freight.take(name) if freight is not None else []
    out_dtypes = tuple(out_dtype) if epilogue is not None else (out_dtype,)
    n_out, n_ex = len(out_dtypes), len(extras)
    if ta:
        kk, m = a.shape
    else:
        m, kk = a.shape
    if tb:
        n, k2 = b.shape
    else:
        k2, n = b.shape
    assert kk == k2, (a.shape, b.shape, ta, tb)
    tile_item = sum(jnp.dtype(d).itemsize for d in out_dtypes) + sum(e.dtype.itemsize for e in extras)
    tm, tn, tk = _matmul_tiles(m, n, kk, a.dtype.itemsize, b.dtype.itemsize, tile_item, res is not None)
    nk = kk // tk
    nc = len(cargo)
    has_bias, has_res = bias is not None, res is not None

    def body(*refs):
        a_ref, b_ref = refs[0], refs[1]
        pos = 2
        bias_ref = res_ref = None
        if has_bias:
            bias_ref = refs[pos]
            pos += 1
        if has_res:
            res_ref = refs[pos]
            pos += 1
        extra_refs = refs[pos:pos + n_ex]
        pos += n_ex
        cargo_in = refs[pos:pos + nc]
        pos += nc
        o_refs = refs[pos:pos + n_out]
        pos += n_out
        cargo_out = refs[pos:pos + nc]
        acc_ref = refs[pos + nc]
        sems = refs[pos + nc + 1:]
        i, j, k = pl.program_id(0), pl.program_id(1), pl.program_id(2)

        if nc:
            @pl.when((i == 0) & (j == 0) & (k == 0))
            def _():
                for c, (_, _, kind) in enumerate(cargo):
                    _EX_START[kind](cargo_in[c], cargo_out[c], *sems[3 * c:3 * c + 3])

        @pl.when(k == 0)
        def _():
            acc_ref[...] = jnp.zeros_like(acc_ref)

        acc_ref[...] += _dg(a_ref[...].astype(BF16), b_ref[...].astype(BF16), 0 if ta else 1, 1 if tb else 0)

        @pl.when(k == nk - 1)
        def _():
            r = acc_ref[...]
            if alpha != 1.0:
                r = r * alpha
            if has_bias:
                r = r + bias_ref[...]
            if has_res:
                r = r + res_ref[...]
            vals = epilogue(r, *[e[...] for e in extra_refs]) if epilogue is not None else (r,)
            for o_ref, val in zip(o_refs, vals, strict=True):
                o_ref[...] = val.astype(o_ref.dtype)

        if nc:
            @pl.when((i == m // tm - 1) & (j == n // tn - 1) & (k == nk - 1))
            def _():
                for c, (_, _, kind) in enumerate(cargo):
                    _EX_FINISH[kind](cargo_in[c], cargo_out[c], *sems[3 * c:3 * c + 3])

    in_specs = [
        pl.BlockSpec((tk, tm), lambda i, j, k: (k, i)) if ta else pl.BlockSpec((tm, tk), lambda i, j, k: (i, k)),
        pl.BlockSpec((tn, tk), lambda i, j, k: (j, k)) if tb else pl.BlockSpec((tk, tn), lambda i, j, k: (k, j)),
    ]
    args = [a, b]
    if has_bias:
        in_specs.append(pl.BlockSpec((1, tn), lambda i, j, k: (0, j)))
        args.append(bias)
    if has_res:
        in_specs.append(pl.BlockSpec((tm, tn), lambda i, j, k: (i, j)))
        args.append(res)
    anyspec = pl.BlockSpec(memory_space=pl.ANY)
    tile = pl.BlockSpec((tm, tn), lambda i, j, k: (i, j))
    for e in extras:
        assert e.shape == (m, n), (e.shape, m, n, name)
    out_specs = [tile] * n_out + [anyspec] * nc
    out_shape = [jax.ShapeDtypeStruct((m, n), d) for d in out_dtypes] + [_ex_out_shape(arr, kind) for _, arr, kind in cargo]
    scratch = [pltpu.VMEM((tm, tn), F32)]
    for _ in cargo:
        scratch += _ex_scratch()
    res_all = pl.pallas_call(
        body, name=name, grid=(m // tm, n // tn, nk), in_specs=in_specs + [tile] * n_ex + [anyspec] * nc,
        out_specs=out_specs, out_shape=out_shape, scratch_shapes=scratch,
        compiler_params=_params("arbitrary", "arbitrary", "arbitrary") if nc else _params("parallel", "parallel", "arbitrary"),
    )(*args, *extras, *[arr for _, arr, _ in cargo])
    for (key, _, _), landed in zip(cargo, res_all[n_out:]):
        freight.landed[key] = landed
    return tuple(res_all[:n_out]) if epilogue is not None else res_all[0]


def _row(arr, width=None, cidx=0, shift=0):
    return (arr, arr.shape[-1] if width is None else width, cidx, shift)


def _rows(fn, rows, consts, outs, accs=(), *, tm, ncol=1, n_rows=None, name):
    n = rows[0][0].shape[-2] if n_rows is None else n_rows
    assert n % tm == 0, (n, tm, name)
    nb = n // tm
    n_in = len(rows) + len(consts)
    n_out = len(outs)

    def cfun(cidx):
        return cidx if callable(cidx) else (lambda j, c=cidx: c)

    in_specs = []
    for arr, width, cidx, shift in rows:
        cf = cfun(cidx)
        if callable(shift):
            rf = shift
        elif shift:
            rf = lambda i, s=shift: jnp.clip(i + s, 0, nb - 1)
        else:
            rf = lambda i: i
        if arr.ndim == 3:
            in_specs.append(pl.BlockSpec((arr.shape[0], tm, width), lambda i, j, rf=rf, cf=cf: (0, rf(i), cf(j))))
        else:
            in_specs.append(pl.BlockSpec((tm, width), lambda i, j, rf=rf, cf=cf: (rf(i), cf(j))))
    for c in consts:
        in_specs.append(pl.BlockSpec(c.shape, lambda i, j, nd=c.ndim: (0,) * nd))
    out_shape, out_specs = [], []
    for o in outs:
        width, dt = o[0], o[1]
        total = o[2] if len(o) > 2 else width * ncol
        out_shape.append(jax.ShapeDtypeStruct((n, total), dt))
        out_specs.append(pl.BlockSpec((tm, width), lambda i, j: (i, j)))
    for shp in accs:
        out_shape.append(jax.ShapeDtypeStruct(shp, F32))
        out_specs.append(pl.BlockSpec(shp, lambda i, j, nd=len(shp): (0,) * nd))

    def body(*refs):
        i, j = pl.program_id(0), pl.program_id(1)
        ins = [r[...] for r in refs[:n_in]]
        ro, ao = fn(i, j, *ins)
        for r, val in zip(refs[n_in:n_in + n_out], ro):
            r[...] = val.astype(r.dtype)
        if accs:
            acc_refs = refs[n_in + n_out:]

            @pl.when((i == 0) & (j == 0))
            def _():
                for r in acc_refs:
                    r[...] = jnp.zeros_like(r)

            for r, val in zip(acc_refs, ao):
                r[...] += val

    res = pl.pallas_call(
        body, name=name, grid=(nb, ncol), in_specs=in_specs, out_specs=out_specs, out_shape=out_shape,
        compiler_params=_params("arbitrary", "arbitrary"),
    )(*[r[0] for r in rows], *consts)
    return res


def _rms_fwd(x, g, name):
    def fn(i, j, x, g):
        return (_rms(x, g),), ()
    return _rows(fn, [_row(x)], [g], [(x.shape[1], BF16)], tm=_pick(x.shape[0], (512, 256)), name=name)[0]


def _rms_bwd(x, dh, dres, g, name):
    def fn(i, j, x, dh, dres, g):
        _, vjp = jax.vjp(_rms, x, g)
        dx, dg = vjp(dh)
        return (dx + dres, dx + dres), (dg,)
    dx, dx16, dg = _rows(fn, [_row(x), _row(dh), _row(dres)], [g], [(x.shape[1], F32), (x.shape[1], BF16)], [g.shape],
                         tm=_pick(x.shape[0], (256,)), name=name)
    return (dx, dx16), dg


def _rms_bwd_gain(x, dh, g, name):
    def fn(i, j, x, dh, g):
        _, vjp = jax.vjp(_rms, x, g)
        return (), (vjp(dh)[1],)
    return _rows(fn, [_row(x), _row(dh)], [g], [], [g.shape], tm=_pick(x.shape[0], (256,)), name=name)[0]


def _swiglu_tile(u, g):
    return u, _silu(g.astype(F32)) * u


def _swiglu_bwd_tile(da, g, u):
    g, u = g.astype(F32), u.astype(F32)
    s = _sigmoid(g)
    return da * u * (s * (1.0 + g * (1.0 - s))), da * (g * s)


def _gmlp_fn(uv, lng, lnb, ws0, ws1, ws2, ws3, bst):
    ws = (ws0, ws1, ws2, ws3)
    q = uv.shape[0]
    u = _gelu(uv[:, :D_MODEL])
    v = _gelu(uv[:, D_MODEL:])
    tril = _iota((q, q), 0) >= _iota((q, q), 1)
    outs = []
    for g in range(GM_GROUPS):
        sl = slice(GM_GDIM * g, GM_GDIM * (g + 1))
        vg = v[:, sl]
        mu = jnp.mean(vg, -1, keepdims=True)
        var = jnp.mean(jnp.square(vg - mu), -1, keepdims=True)
        vn = (vg - mu) * lax.rsqrt(var + EPS) * lng[:, sl] + lnb[:, sl]
        w = jnp.where(tril, ws[g], 0.0)
        bcol = jnp.sum(bst * (_iota((1, 128), 1) == g).astype(F32), axis=1, keepdims=True)
        outs.append(u[:, sl] * (_bdot(w, vn) + bcol))
    return jnp.concatenate(outs, axis=1)


def _gmlp_fwd(uv, lng, lnb, ws, bst, name):
    def fn(i, j, uv, lng, lnb, ws, bst):
        return (_gmlp_fn(uv, lng, lnb, ws[0], ws[1], ws[2], ws[3], bst),), ()
    return _rows(fn, [_row(uv, 2 * D_MODEL, 0)], [lng, lnb, ws, bst], [(D_MODEL, BF16)], tm=CHUNK, name=name)[0]


def _gmlp_bwd(uv, da, lng, lnb, ws, bst, name):
    def fn(i, j, uv, da, lng, lnb, ws, bst):
        _, vjp = jax.vjp(_gmlp_fn, uv, lng, lnb, ws[0], ws[1], ws[2], ws[3], bst)
        duv, dlng, dlnb, d0, d1, d2, d3, dbst = vjp(da)
        return (duv,), (dlng, dlnb, jnp.stack([d0, d1, d2, d3]), dbst)
    return _rows(fn, [_row(uv, 2 * D_MODEL, 0), _row(da)], [lng, lnb, ws, bst], [(2 * D_MODEL, BF16)],
                 [lng.shape, lnb.shape, ws.shape, bst.shape], tm=CHUNK, name=name)


def _conv_taps(scr, x, halo, first_row):
    q = x.shape[0]
    scr[pl.ds(0, 8), :] = halo
    scr[pl.ds(8, q), :] = x
    return [scr[pl.ds(first_row + k, q), :] for k in range(4)]


def _halo_before(q, c, col):
    return pl.BlockSpec((8, c), lambda i: (jnp.maximum(i * (q // 8) - 1, 0), col))


def _conv_fwd(proj, w, b, name):
    n, c = proj.shape[0], w.shape[1]
    xbc = proj
    q = CHUNK

    def body(x_ref, xp_ref, w_ref, b_ref, o_ref, scr):
        i = pl.program_id(0)
        halo = jnp.where(i > 0, xp_ref[...], 0.0)
        taps = _conv_taps(scr, x_ref[...], halo, 5)
        pre = b_ref[...] + sum(taps[k] * w_ref[pl.ds(k, 1), :] for k in range(4))
        o_ref[...] = _silu(pre)

    return pl.pallas_call(
        body, name=name, grid=(n // q,),
        in_specs=[pl.BlockSpec((q, c), lambda i: (i, XBC_COL)), _halo_before(q, c, XBC_COL),
                  pl.BlockSpec(w.shape, lambda i: (0, 0)), pl.BlockSpec(b.shape, lambda i: (0, 0))],
        out_specs=pl.BlockSpec((q, c), lambda i: (i, 0)), out_shape=jax.ShapeDtypeStruct((n, c), F32),
        scratch_shapes=[pltpu.VMEM((q + 8, c), F32)], compiler_params=_params("arbitrary"),
    )(xbc, xbc, w, b)


def _conv_bwd(proj, dy, w, b, name):
    n, c = proj.shape[0], w.shape[1]
    q = CHUNK
    nb = n // q

    def body(x_ref, xp_ref, dy_ref, w_ref, b_ref, dx_ref, dw_ref, db_ref, scr_x, scr_d):
        i = pl.program_id(0)

        @pl.when(i == 0)
        def _():
            dw_ref[...] = jnp.zeros_like(dw_ref)
            db_ref[...] = jnp.zeros_like(db_ref)
            scr_d[pl.ds(q, 8), :] = jnp.zeros((8, c), F32)

        halo = jnp.where(i < nb - 1, xp_ref[...], 0.0)
        taps = _conv_taps(scr_x, x_ref[...], halo, 5)
        pre = b_ref[...] + sum(taps[k] * w_ref[pl.ds(k, 1), :] for k in range(4))
        s = _sigmoid(pre)
        dp = dy_ref[...] * (s * (1.0 + pre * (1.0 - s)))
        db_ref[...] += jnp.sum(dp, axis=0, keepdims=True)
        for k in range(4):
            dw_ref[pl.ds(k, 1), :] += jnp.sum(dp * taps[k], axis=0, keepdims=True)
        scr_d[pl.ds(0, q), :] = dp
        dx_ref[...] = sum(scr_d[pl.ds(3 - k, q), :] * w_ref[pl.ds(k, 1), :] for k in range(4)).astype(dx_ref.dtype)
        scr_d[pl.ds(q, 8), :] = dp[0:8]

    rev = lambda i: nb - 1 - i
    return pl.pallas_call(
        body, name=name, grid=(nb,),
        in_specs=[pl.BlockSpec((q, c), lambda i: (rev(i), XBC_COL)),
                  pl.BlockSpec((8, c), lambda i: (jnp.maximum(rev(i) * (q // 8) - 1, 0), XBC_COL)),
                  pl.BlockSpec((q, c), lambda i: (rev(i), 0)),
                  pl.BlockSpec(w.shape, lambda i: (0, 0)), pl.BlockSpec(b.shape, lambda i: (0, 0))],
        out_specs=[pl.BlockSpec((q, c), lambda i: (rev(i), 0)), pl.BlockSpec(w.shape, lambda i: (0, 0)),
                   pl.BlockSpec(b.shape, lambda i: (0, 0))],
        out_shape=[jax.ShapeDtypeStruct((n, c), BF16), jax.ShapeDtypeStruct(w.shape, F32), jax.ShapeDtypeStruct(b.shape, F32)],
        scratch_shapes=[pltpu.VMEM((q + 8, c), F32), pltpu.VMEM((q + 8, c), F32)], compiler_params=_params("arbitrary"),
    )(proj, proj, dy, w, b)


def _ssd_chunk(xs, bc, dtr, dtb, alog, dsk, expand, hp):
    q = xs.shape[0]
    tril = _iota((q, q), 0) >= _iota((q, q), 1)
    dt = _softplus(dtr + dtb)
    a = dt * (-jnp.exp(alog))
    cs = _running_sum(tril.astype(F32), a)
    cs_t = cs.T
    last = (_iota((q, 1), 0) == q - 1).astype(F32)
    gw = SSD_HEADS // SSD_GROUPS * SSD_HEAD_DIM
    lane = _iota((1, gw), 1)
    ys, hs = [], []
    for g in range(SSD_GROUPS):
        sl = slice(gw * g, gw * (g + 1))
        eg = expand[:, sl]
        dt_e, cs_e = _spread(dt, eg), _spread(cs, eg)
        cl_e = jnp.sum(cs_e * last, axis=0, keepdims=True)
        d_e = jnp.sum(_spread(jnp.broadcast_to(dsk, (8, 128)), eg), axis=0, keepdims=True) * 0.125
        xg = xs[:, sl]
        xdt = xg * dt_e
        bg = bc[:, SSD_STATE * g:SSD_STATE * (g + 1)]
        cg = bc[:, SSD_GROUPS * SSD_STATE + SSD_STATE * g:SSD_GROUPS * SSD_STATE + SSD_STATE * (g + 1)]
        cb = _bdot_nt(cg, bg)
        ms, xm = [], []
        for r in range(SSD_HEADS // SSD_GROUPS):
            h = g * (SSD_HEADS // SSD_GROUPS) + r
            col = jnp.sum(cs * (_iota((1, 128), 1) == h).astype(F32), axis=1, keepdims=True)
            row = jnp.sum(cs_t * (_iota((128, 1), 0) == h).astype(F32), axis=0, keepdims=True)
            decay = jnp.where(tril, jnp.exp(jnp.where(tril, col - row, 0.0)), 0.0)
            ms.append(cb * decay)
            xm.append(xdt * ((lane >= SSD_HEAD_DIM * r) & (lane < SSD_HEAD_DIM * (r + 1))).astype(F32))
        y_diag = _bdot(jnp.concatenate(ms, axis=1), jnp.concatenate(xm, axis=0))
        hg = hp[:, sl]
        y_off = _bdot(cg, hg) * jnp.exp(cs_e)
        states = _bdot_tn(bg, xdt * jnp.exp(cl_e - cs_e))
        hs.append(hg * jnp.exp(cl_e) + states)
        ys.append(y_diag + y_off + xg * d_e)
    return jnp.concatenate(ys, axis=1), jnp.concatenate(hs, axis=1)


def _ssd_fwd(xbc, dtr, dtb, alog, dsk, expand, name):
    n = xbc.shape[0]
    q, w = CHUNK, D_MODEL
    nc = n // q

    def body(xs_ref, bc_ref, dtr_ref, dtb_ref, alog_ref, dsk_ref, e_ref, y_ref, hp_ref, h_scr):
        @pl.when(pl.program_id(0) == 0)
        def _():
            h_scr[...] = jnp.zeros_like(h_scr)

        hp = h_scr[...]
        y, hn = _ssd_chunk(xs_ref[...], bc_ref[...], dtr_ref[...], dtb_ref[...], alog_ref[...], dsk_ref[...], e_ref[...], hp)
        y_ref[...] = y
        hp_ref[...] = hp
        h_scr[...] = hn

    small = pl.BlockSpec((1, 128), lambda c: (0, 0))
    return pl.pallas_call(
        body, name=name, grid=(nc,),
        in_specs=[pl.BlockSpec((q, w), lambda c: (c, 0)), pl.BlockSpec((q, 1024), lambda c: (c, 2)),
                  pl.BlockSpec((q, 128), lambda c: (c, 0)), small, small, small, pl.BlockSpec((128, w), lambda c: (0, 0))],
        out_specs=[pl.BlockSpec((q, w), lambda c: (c, 0)), pl.BlockSpec((SSD_STATE, w), lambda c: (c, 0))],
        out_shape=[jax.ShapeDtypeStruct((n, w), F32), jax.ShapeDtypeStruct((nc * SSD_STATE, w), F32)],
        scratch_shapes=[pltpu.VMEM((SSD_STATE, w), F32)], compiler_params=_params("arbitrary"),
    )(xbc, xbc, dtr, dtb, alog, dsk, expand)


def _ssd_bwd(xbc, dtr, dtb, alog, dsk, expand, hp_all, dy, name):
    n = xbc.shape[0]
    q, w = CHUNK, D_MODEL
    nc = n // q

    def body(xs_ref, bc_ref, dtr_ref, dtb_ref, alog_ref, dsk_ref, e_ref, hp_ref, dy_ref,
             dxbc_ref, ddtr_ref, ddtb_ref, dalog_ref, ddsk_ref, dh_scr):
        @pl.when(pl.program_id(0) == 0)
        def _():
            dh_scr[...] = jnp.zeros_like(dh_scr)
            ddtb_ref[...] = jnp.zeros_like(ddtb_ref)
            dalog_ref[...] = jnp.zeros_like(dalog_ref)
            ddsk_ref[...] = jnp.zeros_like(ddsk_ref)

        e = e_ref[...]
        _, vjp = jax.vjp(lambda xs, bc, dtr, dtb, alog, dsk, hp: _ssd_chunk(xs, bc, dtr, dtb, alog, dsk, e, hp),
                         xs_ref[...], bc_ref[...], dtr_ref[...], dtb_ref[...], alog_ref[...], dsk_ref[...], hp_ref[...])
        dxs, dbc, ddtr, ddtb, dalog, ddsk, dhp = vjp((dy_ref[...], dh_scr[...]))
        dxbc_ref[...] = jnp.concatenate([dxs, dbc], axis=1)
        ddtr_ref[...] = ddtr
        ddtb_ref[...] += ddtb
        dalog_ref[...] += dalog
        ddsk_ref[...] += ddsk
        dh_scr[...] = dhp

    rev = lambda c: nc - 1 - c
    small = pl.BlockSpec((1, 128), lambda c: (0, 0))
    return pl.pallas_call(
        body, name=name, grid=(nc,),
        in_specs=[pl.BlockSpec((q, w), lambda c: (rev(c), 0)), pl.BlockSpec((q, 1024), lambda c: (rev(c), 2)),
                  pl.BlockSpec((q, 128), lambda c: (rev(c), 0)), small, small, small,
                  pl.BlockSpec((128, w), lambda c: (0, 0)), pl.BlockSpec((SSD_STATE, w), lambda c: (rev(c), 0)),
                  pl.BlockSpec((q, w), lambda c: (rev(c), 0))],
        out_specs=[pl.BlockSpec((q, w + 1024), lambda c: (rev(c), 0)),
                   pl.BlockSpec((q, 128), lambda c: (rev(c), 0)), small, small, small],
        out_shape=[jax.ShapeDtypeStruct((n, w + 1024), F32), jax.ShapeDtypeStruct((n, 128), F32),
                   jax.ShapeDtypeStruct((1, 128), F32), jax.ShapeDtypeStruct((1, 128), F32), jax.ShapeDtypeStruct((1, 128), F32)],
        scratch_shapes=[pltpu.VMEM((SSD_STATE, w), F32)], compiler_params=_params("arbitrary"),
    )(xbc, xbc, dtr, dtb, alog, dsk, expand, hp_all, dy)


def _gate_fn(y, z, g):
    outs = []
    for k in range(SSD_GROUPS):
        sl = slice(512 * k, 512 * (k + 1))
        yg = y[:, sl] * _silu(z[:, sl])
        outs.append(yg * lax.rsqrt(jnp.mean(yg * yg, -1, keepdims=True) + EPS) * g[:, sl])
    return jnp.concatenate(outs, axis=1)


def _gate_fwd(y, z, g, name):
    def fn(i, j, y, z, g):
        return (_gate_fn(y, z, g),), ()
    return _rows(fn, [_row(y), _row(z, D_MODEL, Z_COL)], [g], [(D_MODEL, BF16)], tm=_pick(y.shape[0], (256, 128)), name=name)[0]


def _gate_bwd(y, z, db, g, name):
    def fn(i, j, y, z, db, g):
        _, vjp = jax.vjp(_gate_fn, y, z, g)
        dy, dz, dg = vjp(db)
        return (dy, dz), (dg,)
    return _rows(fn, [_row(y), _row(z, D_MODEL, Z_COL), _row(db)], [g], [(D_MODEL, F32), (D_MODEL, BF16)], [g.shape],
                 tm=_pick(y.shape[0], (256, 128)), name=name)


def _rope(x, tab, sign=1.0):
    return x * tab[:, 0:128] + sign * (pltpu.roll(x, 128 - ROT_HALF, 1) * tab[:, 128:256] + pltpu.roll(x, ROT_HALF, 1) * tab[:, 256:384])


def _sink_softmax_parts(s, sink):
    m = jnp.maximum(jnp.max(s, -1, keepdims=True), sink)
    p = jnp.exp(s - m)
    e_sink = jnp.exp(sink - m)
    inv = 1.0 / (jnp.sum(p, -1, keepdims=True) + e_sink)
    return p * inv, e_sink * inv


@jax.custom_vjp
def _sink_softmax(s, sink):
    return _sink_softmax_parts(s, sink)[0]


def _sink_softmax_fwd(s, sink):
    pr, pr_sink = _sink_softmax_parts(s, sink)
    return pr, (pr, pr_sink)


def _sink_softmax_bwd(r, g):
    pr, pr_sink = r
    t = jnp.sum(g * pr, -1, keepdims=True)
    return pr * (g - t), -pr_sink * t


_sink_softmax.defvjp(_sink_softmax_fwd, _sink_softmax_bwd)


def _swa_mask(first, q=CHUNK):
    rows = ATT_REP * q
    iq = _iota((rows, 2 * q), 0) & (q - 1)
    js = _iota((rows, 2 * q), 1)
    rel = iq + q - js
    mask = (rel >= 0) & (rel < q) & ((js >= q) | jnp.logical_not(first))
    return mask, lax.shift_right_logical(_iota((rows, 1), 0), q.bit_length() - 1)


def _swa_core(qq, kc, vc, sinks, kv_head, mask_rep):
    mask, rep = mask_rep
    s = jnp.where(mask, _bdot_nt(qq, kc) * ATT_SCALE, -jnp.inf)
    sink = jnp.zeros((qq.shape[0], 1), F32)
    for r in range(ATT_REP):
        s_r = jnp.sum(sinks * (_iota((1, 128), 1) == kv_head * ATT_REP + r).astype(F32), axis=1, keepdims=True)
        sink = sink + jnp.where(rep == r, s_r, 0.0)
    return _bdot(_sink_softmax(s, sink), vc)


def _swa_prep(q, kv, kvp, tab, tabp, kv_head):
    w = HEAD_PAD
    kc = jnp.concatenate([_rope(kvp[:, w * kv_head:w * (kv_head + 1)], tabp), _rope(kv[:, w * kv_head:w * (kv_head + 1)], tab)], axis=0)
    o = ATT_KV_HEADS * w
    vc = jnp.concatenate([kvp[:, o + w * kv_head:o + w * (kv_head + 1)], kv[:, o + w * kv_head:o + w * (kv_head + 1)]], axis=0)
    qq = jnp.concatenate([_rope(q[:, w * (kv_head * ATT_REP + r):w * (kv_head * ATT_REP + r + 1)], tab) for r in range(ATT_REP)], axis=0)
    return qq, kc, vc


def _swa_fwd(q, kv, tab, sinks, name):
    def fn(i, j, q, kv, kvp, tab, tabp, sinks):
        mask_rep = _swa_mask(i == 0)
        outs = []
        for h in range(ATT_KV_HEADS):
            qq, kc, vc = _swa_prep(q, kv, kvp, tab, tabp, h)
            o = _swa_core(qq, kc, vc, sinks, h, mask_rep)
            outs += [o[CHUNK * r:CHUNK * (r + 1)] for r in range(ATT_REP)]
        return (jnp.concatenate(outs, axis=1),), ()
    return _rows(fn, [_row(q), _row(kv), _row(kv, shift=-1), _row(tab), _row(tab, shift=-1)], [sinks],
                 [(q.shape[1], BF16)], tm=CHUNK, name=name)[0]


def _swa_bwd(q, kv, tab, sinks, do, name):
    w = HEAD_PAD

    def fn(i, j, q, kv, kvp, tab, tabp, do, sinks):
        mask_rep = _swa_mask(i == 0)
        dqs, dkc, dkp, dvc, dvp = [], [], [], [], []
        dsink = jnp.zeros_like(sinks)
        for h in range(ATT_KV_HEADS):
            qq, kc, vc = _swa_prep(q, kv, kvp, tab, tabp, h)
            dout = jnp.concatenate([do[:, w * (h * ATT_REP + r):w * (h * ATT_REP + r + 1)] for r in range(ATT_REP)], axis=0)
            _, vjp = jax.vjp(lambda a, b, c, s: _swa_core(a, b, c, s, h, mask_rep), qq, kc, vc, sinks)
            dqq, dk, dv, ds = vjp(dout.astype(F32))
            dsink = dsink + ds
            dqs += [_rope(dqq[CHUNK * r:CHUNK * (r + 1)], tab, -1.0) for r in range(ATT_REP)]
            dkp.append(_rope(dk[:CHUNK], tabp, -1.0))
            dkc.append(_rope(dk[CHUNK:], tab, -1.0))
            dvp.append(dv[:CHUNK])
            dvc.append(dv[CHUNK:])
        return (jnp.concatenate(dqs, axis=1), jnp.concatenate(dkc + dvc, axis=1), jnp.concatenate(dkp + dvp, axis=1)), (dsink,)
    return _rows(fn, [_row(q), _row(kv), _row(kv, shift=-1), _row(tab), _row(tab, shift=-1), _row(do)], [sinks],
                 [(q.shape[1], BF16), (kv.shape[1], F32), (kv.shape[1], F32)], [sinks.shape], tm=CHUNK, name=name)


def _swa_combine(dkv_cur, dkv_prev, dq, name):
    nb = dq.shape[0] // CHUNK

    def fn(i, j, cur, nxt, dq):
        dkv = (cur + jnp.where(i < nb - 1, nxt, 0.0)).astype(BF16)
        return (dkv,), (jnp.sum(dq.astype(F32), axis=0, keepdims=True), jnp.sum(dkv.astype(F32), axis=0, keepdims=True))
    return _rows(fn, [_row(dkv_cur), _row(dkv_prev, shift=1), _row(dq)], [], [(dkv_cur.shape[1], BF16)],
                 [(1, dq.shape[1]), (1, dkv_cur.shape[1])], tm=CHUNK, name=name)


def _xattn_fn(q, k, v):
    outs = []
    for h in range(X_HEADS):
        sl = slice(X_HEAD_DIM * h, X_HEAD_DIM * (h + 1))
        s = _bdot_nt(q[:, sl], k[:, sl]) * X_SCALE
        p = jnp.exp(s - jnp.max(s, -1, keepdims=True))
        outs.append(_bdot(p / jnp.sum(p, -1, keepdims=True), v[:, sl]))
    return jnp.concatenate(outs, axis=1)


def _xattn_fwd(q, k, v, name):
    def fn(i, j, q, k, v):
        return (_xattn_fn(q, k, v),), ()
    return _rows(fn, [_row(q)], [k, v], [(q.shape[1], BF16)], tm=_pick(q.shape[0], (512, 256)), name=name)[0]


def _xattn_bwd(q, k, v, do, name):
    def fn(i, j, q, do, k, v):
        _, vjp = jax.vjp(_xattn_fn, q, k, v)
        dq, dk, dv = vjp(do)
        return (dq,), (dk, dv)
    return _rows(fn, [_row(q), _row(do)], [k, v], [(q.shape[1], BF16)], [k.shape, v.shape],
                 tm=_pick(q.shape[0], (512, 256)), name=name)


def _loss_fn(x, t, g):
    return 0.5 * jnp.sum(jnp.mean(jnp.square(_rms(x, g) - t), axis=-1))


def _loss_fwd_bwd(x, t, g, name):
    def fn(i, j, x, t, g):
        loss, vjp = jax.vjp(_loss_fn, x, t, g)
        dx, _, dg = vjp(jnp.ones((), F32))
        return (dx, dx), (jnp.broadcast_to(loss, (8, 128)), dg)
    dx, dx16, loss, dg = _rows(fn, [_row(x), _row(t)], [g], [(x.shape[1], F32), (x.shape[1], BF16)], [(8, 128), g.shape],
                               tm=_pick(x.shape[0], (256,)), name=name)
    return loss, (dx, dx16), dg


def _adamw(parts, w, m, v, name):
    plist = list(parts) if isinstance(parts, (list, tuple)) else [parts]
    nl = len(plist)
    r, c = w.shape
    tm = _pick(r // nl, (128, 64, 32, 16, 8))
    nbl = r // nl // tm

    def fn(i, j, *blocks):
        parts, (w, m, v) = blocks[0], blocks[nl:]
        for l in range(1, nl):
            parts = jnp.where(i >= l * nbl, blocks[l], parts)
        g = parts[0].astype(F32)
        for k in range(1, N_DEV):
            g = g + parts[k].astype(F32)
        m2 = ADAM_B1 * m + (1.0 - ADAM_B1) * g
        v2 = ADAM_B2 * v + (1.0 - ADAM_B2) * jnp.square(g)
        m_hat = m2 / (1.0 - ADAM_B1 ** ADAM_STEP)
        v_hat = v2 / (1.0 - ADAM_B2 ** ADAM_STEP)
        delta = -ADAM_LR * (m_hat / (jnp.sqrt(v_hat) + ADAM_EPS) + ADAM_WD * w)
        return (g, delta, m2, v2), ()
    prow = [_row(p, shift=(lambda i, l=l: jnp.clip(i - l * nbl, 0, nbl - 1))) for l, p in enumerate(plist)]
    return _rows(fn, prow + [_row(w), _row(m), _row(v)], [], [(c, F32)] * 4, tm=tm, n_rows=r, name=name)


def _peer(k):
    return (k // 4, (k // 2) % 2, k % 2)


GATHER, SCATTER = "gather", "scatter"


def _ex_scratch():
    return [pltpu.SemaphoreType.DMA((N_DEV - 1,)), pltpu.SemaphoreType.DMA((N_DEV - 1,)), pltpu.SemaphoreType.DMA]


def _ex_out_shape(x, kind):
    return jax.ShapeDtypeStruct((N_DEV,) + x.shape[-2:], x.dtype)


def _gather_plan():
    x, y, c = lax.axis_index("x"), lax.axis_index("y"), lax.axis_index("c")
    return (x, y, c), (x, y, 1 - c), [(1 - x, y), (x, 1 - y), (1 - x, 1 - y)], c


def _gather_copy(x_ref, o_ref, send, recv, k, block, to, from_input=False):
    slot = o_ref.at[4 * block[0] + 2 * block[1] + block[2]]
    return pltpu.make_async_remote_copy(src_ref=x_ref if from_input else slot, dst_ref=slot, send_sem=send.at[k],
                                        recv_sem=recv.at[k], device_id=to, device_id_type=MESH_ID)


def _gather_start(x_ref, o_ref, send, recv, local):
    me, sib, chips, c = _gather_plan()
    pltpu.make_async_copy(x_ref, o_ref.at[4 * me[0] + 2 * me[1] + me[2]], local).start()
    _gather_copy(x_ref, o_ref, send, recv, 0, me, sib, True).start()
    for j, chip in enumerate(chips):
        _gather_copy(x_ref, o_ref, send, recv, 1 + j, me, (*chip, c), True).start()


def _gather_finish(x_ref, o_ref, send, recv, local):
    me, sib, chips, c = _gather_plan()
    for j, chip in enumerate(chips):
        _gather_copy(x_ref, o_ref, send, recv, 1 + j, (*chip, c), me).wait_recv()
        _gather_copy(x_ref, o_ref, send, recv, 4 + j, (*chip, c), sib).start()
    _gather_copy(x_ref, o_ref, send, recv, 0, sib, me).wait_recv()
    for j, chip in enumerate(chips):
        _gather_copy(x_ref, o_ref, send, recv, 4 + j, (*chip, 1 - c), me).wait_recv()
    _gather_copy(x_ref, o_ref, send, recv, 0, me, sib, True).wait_send()
    for j, chip in enumerate(chips):
        _gather_copy(x_ref, o_ref, send, recv, 1 + j, me, (*chip, c), True).wait_send()
        _gather_copy(x_ref, o_ref, send, recv, 4 + j, (*chip, c), sib).wait_send()
    pltpu.make_async_copy(x_ref, o_ref.at[4 * me[0] + 2 * me[1] + me[2]], local).wait()


def _scatter_copy(x_ref, o_ref, send, recv, d, frm, to):
    return pltpu.make_async_remote_copy(src_ref=x_ref.at[to], dst_ref=o_ref.at[frm], send_sem=send.at[d - 1],
                                        recv_sem=recv.at[d - 1], device_id=_peer(to), device_id_type=MESH_ID)


def _scatter_start(x_ref, o_ref, send, recv, local):
    me = 4 * lax.axis_index("x") + 2 * lax.axis_index("y") + lax.axis_index("c")
    pltpu.make_async_copy(x_ref.at[me], o_ref.at[me], local).start()
    for d in range(1, N_DEV):
        _scatter_copy(x_ref, o_ref, send, recv, d, me, (me + d) % N_DEV).start()


def _scatter_finish(x_ref, o_ref, send, recv, local):
    me = 4 * lax.axis_index("x") + 2 * lax.axis_index("y") + lax.axis_index("c")
    for d in range(1, N_DEV):
        _scatter_copy(x_ref, o_ref, send, recv, d, (me + N_DEV - d) % N_DEV, me).wait_recv()
    for d in range(1, N_DEV):
        _scatter_copy(x_ref, o_ref, send, recv, d, me, (me + d) % N_DEV).wait_send()
    pltpu.make_async_copy(x_ref.at[me], o_ref.at[me], local).wait()


_EX_START = {GATHER: _gather_start, SCATTER: _scatter_start}
_EX_FINISH = {GATHER: _gather_finish, SCATTER: _scatter_finish}


def _exchange(x, kind, name):
    def body(x_ref, o_ref, send, recv, local):
        _EX_START[kind](x_ref, o_ref, send, recv, local)
        _EX_FINISH[kind](x_ref, o_ref, send, recv, local)

    return pl.pallas_call(
        body, name=name, in_specs=[pl.BlockSpec(memory_space=pl.ANY)], out_specs=pl.BlockSpec(memory_space=pl.ANY),
        out_shape=_ex_out_shape(x, kind), scratch_shapes=_ex_scratch(),
    )(x)


class _Freight:
    def __init__(self):
        self.load = {}
        self.landed = {}

    def put(self, carrier, key, arr, kind):
        self.load.setdefault(carrier, []).append((key, arr, kind))

    def take(self, carrier):
        return self.load.pop(carrier, [])


def _all_gather(x, name):
    return _exchange(x.reshape(-1, x.shape[-1]), GATHER, name).reshape((N_DEV,) + x.shape)


def _all_to_all(x, name):
    return _exchange(x.reshape(N_DEV, -1, x.shape[-1]), SCATTER, name).reshape(x.shape)


def _pack(arrs):
    flat = []
    for a in arrs:
        a = a.reshape(-1).astype(F32)
        flat.append(jnp.pad(a, (0, (-a.shape[0]) % 1024)))
    return jnp.concatenate(flat).reshape(-1, 128)


def _unpack(p, shapes):
    p = p.reshape(-1)
    out, off = [], 0
    for s in shapes:
        n = 1
        for d in s:
            n *= d
        out.append(p[off:off + n].reshape(s))
        off += n + (-n) % 1024
    return out


def _cols_to_full(g):
    return g.transpose(1, 0, 2).reshape(g.shape[1], -1)


def _full_to_cols(w, shards=N_DEV):
    return w.reshape(w.shape[0], shards, -1).transpose(1, 0, 2)


SMALL = ("norm_ffn1", "norm_mix", "gm_ln_g", "gm_ln_b", "gm_ws", "gm_bs", "conv_b", "dt_bias", "a_log", "d_skip", "ssd_norm",
         "sinks", "norm_xq", "norm_mem", "norm_ffn2", "final_norm")
SHARDED = ("w_ffn1_gu", "w_ffn1_down", "w_in_even", "w_out_even", "w_qkv", "w_o_odd", "w_xq", "w_xkv", "w_xo", "w_ffn2_gu",
           "w_ffn2_down")
SMALL_SHARDED = ("conv_w", "b_qkv")
ORDER = ("norm_ffn1", "w_ffn1_gu", "w_ffn1_down", "norm_mix", "w_in_even", "gm_ln_g", "gm_ln_b", "gm_ws", "gm_bs", "conv_w",
         "conv_b", "dt_bias", "a_log", "d_skip", "ssd_norm", "w_out_even", "w_qkv", "b_qkv", "sinks", "w_o_odd", "norm_xq",
         "norm_mem", "w_xq", "w_xkv", "w_xo", "norm_ffn2", "w_ffn2_gu", "w_ffn2_down", "final_norm")


def _ffn_fwd(x, gain, wg, wu, wd, tag, freight=None):
    mm = functools.partial(_matmul, freight=freight)
    h = _rms_fwd(x, gain, f"{tag}_norm")
    g = mm(h, wg, out_dtype=BF16, name=f"{tag}_gate")
    u, act = mm(h, wu, out_dtype=(BF16, BF16), extras=[g], epilogue=_swiglu_tile, name=f"{tag}_up")
    wd = wd() if callable(wd) else wd
    y = mm(act, wd, alpha=0.5, res=x, name=f"{tag}_down")
    return y, (x, h, g, u, act), wd


def _ffn_bwd(dy, saved, gain, wg, wu, wd, tag, freight=None, ship=None):
    mm = functools.partial(_matmul, freight=freight)
    x, h, g, u, act = saved
    dy, dy16 = dy
    dg, du = mm(dy16, wd, tb=True, alpha=0.5, out_dtype=(BF16, BF16), extras=[g, u], epilogue=_swiglu_bwd_tile, name=f"{tag}_dact")
    dwd = mm(act, dy16, ta=True, alpha=0.5, out_dtype=BF16, name=f"{tag}_dwd")
    if ship is not None:
        ship(f"{tag}_dwg", "dn", dwd.reshape(N_DEV, -1, dwd.shape[1]))
    dwg = mm(h, dg, ta=True, out_dtype=BF16, name=f"{tag}_dwg")
    dwu = mm(h, du, ta=True, out_dtype=BF16, name=f"{tag}_dwu")
    if ship is not None:
        half = N_DEV // 2
        parts = jnp.concatenate([_full_to_cols(dwg, half), _full_to_cols(dwu, half)], axis=0)
        rows = parts.shape[1] // 2
        ship(f"{tag}_dh_g", "gu_a", parts[:, :rows])
        ship(f"{tag}_dh_u", "gu_b", parts[:, rows:])
    dh = mm(dg, wg, tb=True, name=f"{tag}_dh_g")
    dh = mm(du, wu, tb=True, res=dh, name=f"{tag}_dh_u")
    dx, dgain = _rms_bwd(x, dh, dy, gain, f"{tag}_dnorm")
    return dx, dgain, dwg, dwu, dwd


def _xattn_layer_fwd(x, mem, gq, gm, wq, wkv, wo, tag, freight=None):
    mm = functools.partial(_matmul, freight=freight)
    hq = _rms_fwd(x, gq, f"{tag}_normq")
    mn = _rms_fwd(mem, gm, f"{tag}_normm")
    q = mm(hq, wq, name=f"{tag}_q")
    kv = mm(mn, wkv, name=f"{tag}_kv")
    k, v = kv[:, :X_HEADS * X_HEAD_DIM], kv[:, X_HEADS * X_HEAD_DIM:]
    o = _xattn_fwd(q, k, v, f"{tag}_attn")
    y = mm(o, wo, res=x, name=f"{tag}_o")
    return y, (x, hq, mn, q, k, v, o)


def _xattn_layer_bwd(dy, saved, mem, gq, gm, wq, wkv, wo, tag, freight=None):
    mm = functools.partial(_matmul, freight=freight)
    x, hq, mn, q, k, v, o = saved
    dy, dy16 = dy
    do = mm(dy16, wo, tb=True, name=f"{tag}_do")
    dwo = mm(o, dy16, ta=True, out_dtype=BF16, name=f"{tag}_dwo")
    dq, dk, dv = _xattn_bwd(q, k, v, do, f"{tag}_dattn")
    dkv = jnp.concatenate([dk, dv], axis=1)
    dwq = mm(hq, dq, ta=True, out_dtype=BF16, name=f"{tag}_dwq")
    dwkv = mm(mn, dkv, ta=True, out_dtype=BF16, name=f"{tag}_dwkv")
    dhq = mm(dq, wq, tb=True, name=f"{tag}_dhq")
    dmn = mm(dkv, wkv, tb=True, name=f"{tag}_dmn")
    dx, dgq = _rms_bwd(x, dhq, dy, gq, f"{tag}_dnormq")
    dgm = _rms_bwd_gain(mem, dmn, gm, f"{tag}_dnormm")
    return dx, dgq, dgm, dwq, dwkv, dwo


def _even_fwd(x, weights, params, freight=None):
    mm = functools.partial(_matmul, freight=freight)
    w_main, w_dt, w_out_a, w_out_b = weights
    gain, lng, lnb, ws, bst, conv_w, conv_b, dtb, alog, dsk, ssd_norm, expand = params
    hm = _rms_fwd(x, gain, "l0_normmix")
    proj = mm(hm, w_main, name="l0_proj")
    dtr = mm(hm, w_dt, name="l0_dt")
    a_out = _gmlp_fwd(proj, lng, lnb, ws, bst, "l0_gmlp")
    xbc = _conv_fwd(proj, conv_w, conv_b, "l0_conv")
    y_ssd, hp_all = _ssd_fwd(xbc, dtr, dtb, alog, dsk, expand, "l0_ssd")
    b_out = _gate_fwd(y_ssd, proj, ssd_norm, "l0_gate")
    y = mm(a_out, w_out_a, res=x, name="l0_out_a")
    y = mm(b_out, w_out_b, res=y, name="l0_out_b")
    return y, (x, hm, proj, dtr, a_out, xbc, y_ssd, hp_all, b_out)


def _even_bwd(dx, saved, weights, params, freight=None, ship_out=None):
    mm = functools.partial(_matmul, freight=freight)
    w_main, w_dt, w_out_a, w_out_b = weights
    w_uv, w_z, w_xbc = w_main[:, :4096], w_main[:, 4096:6144], w_main[:, 6144:]
    gain, lng, lnb, ws, bst, conv_w, conv_b, dtb, alog, dsk, ssd_norm, expand = params
    x, hm, proj, dtr, a_out, xbc, y_ssd, hp_all, b_out = saved
    uv = zz = xbc_raw = proj
    dx, dx16 = dx
    da_out = mm(dx16, w_out_a, tb=True, name="l0_da")
    db_out = mm(dx16, w_out_b, tb=True, name="l0_db")
    dw_out_a = mm(a_out, dx16, ta=True, out_dtype=BF16, name="l0_dwout_a")
    dw_out_b = mm(b_out, dx16, ta=True, out_dtype=BF16, name="l0_dwout_b")
    dw_out = jnp.concatenate([dw_out_a, dw_out_b], axis=0)
    if ship_out is not None:
        ship_out(dw_out)
    dy_ssd, dzz, d_ssd_norm = _gate_bwd(y_ssd, zz, db_out, ssd_norm, "l0_dgate")
    dxbc, ddtr, d_dtb, d_alog, d_dsk = _ssd_bwd(xbc, dtr, dtb, alog, dsk, expand, hp_all, dy_ssd, "l0_dssd")
    dxbc_raw, d_conv_w, d_conv_b = _conv_bwd(xbc_raw, dxbc, conv_w, conv_b, "l0_dconv")
    duv, d_lng, d_lnb, d_ws, d_bst = _gmlp_bwd(uv, da_out, lng, lnb, ws, bst, "l0_dgmlp")
    ddtr16 = ddtr.astype(BF16)
    dhm = mm(duv, w_uv, tb=True, name="l0_dh_uv")
    dhm = mm(dzz, w_z, tb=True, res=dhm, name="l0_dh_z")
    dhm = mm(dxbc_raw, w_xbc, tb=True, res=dhm, name="l0_dh_xbc")
    dhm = mm(ddtr16, w_dt, tb=True, res=dhm, name="l0_dh_dt")
    dw_uv = mm(hm, duv, ta=True, out_dtype=BF16, name="l0_dwuv")
    dw_z = mm(hm, dzz, ta=True, out_dtype=BF16, name="l0_dwz")
    dw_xbc = mm(hm, dxbc_raw, ta=True, out_dtype=BF16, name="l0_dwxbc")
    dw_dt = mm(hm, ddtr16, ta=True, out_dtype=BF16, name="l0_dwdt")
    dx, d_gain = _rms_bwd(x, dhm, dx, gain, "l0_dnormmix")
    small = (d_gain, d_lng, d_lnb, d_ws, d_bst, d_conv_w, d_conv_b, d_dtb, d_alog, d_dsk, d_ssd_norm)
    return dx, small, (dw_uv, dw_z, dw_xbc, dw_dt, dw_out)


def _odd_fwd(x, weights, params, freight=None):
    mm = functools.partial(_matmul, freight=freight)
    w_q, w_kv, w_o = weights
    gain, b_q, b_kv, tab, snk = params
    hm = _rms_fwd(x, gain, "l1_normmix")
    q = mm(hm, w_q, bias=b_q, name="l1_q")
    kv = mm(hm, w_kv, bias=b_kv, name="l1_kv")
    o = _swa_fwd(q, kv, tab, snk, "l1_swa")
    y = mm(o, w_o, res=x, name="l1_o")
    return y, (x, hm, q, kv, o)


def _odd_bwd(dx, saved, weights, params, freight=None):
    mm = functools.partial(_matmul, freight=freight)
    w_q, w_kv, w_o = weights
    gain, b_q, b_kv, tab, snk = params
    x, hm, q, kv, o = saved
    dx, dx16 = dx
    do = mm(dx16, w_o, tb=True, out_dtype=BF16, name="l1_do")
    dw_o = mm(o, dx16, ta=True, out_dtype=BF16, name="l1_dwo")
    dq, dkv_cur, dkv_prev, d_snk = _swa_bwd(q, kv, tab, snk, do, "l1_dswa")
    dkv, db_q, db_kv = _swa_combine(dkv_cur, dkv_prev, dq, "l1_dkv")
    dhm = mm(dq, w_q, tb=True, name="l1_dh_q")
    dhm = mm(dkv, w_kv, tb=True, res=dhm, name="l1_dh_kv")
    dw_q = mm(hm, dq, ta=True, out_dtype=BF16, name="l1_dwq")
    dw_kv = mm(hm, dkv, ta=True, out_dtype=BF16, name="l1_dwkv")
    dx, d_gain = _rms_bwd(x, dhm, dx, gain, "l1_dnormmix")
    return dx, (d_gain, db_q, db_kv, d_snk), (dw_q, dw_kv, dw_o)


def _rope_table(positions):
    seq = positions.size
    inv_freq = ROPE_THETA ** (-jnp.arange(0, 2 * ROT_HALF, 2, dtype=F32) / (2 * ROT_HALF))
    ang = positions.reshape(seq, 1).astype(F32) * inv_freq
    cos, sin, zero = jnp.cos(ang), jnp.sin(ang), jnp.zeros((seq, 128 - 2 * ROT_HALF), F32)
    z8 = jnp.zeros((seq, ROT_HALF), F32)
    return jnp.concatenate([cos, cos, zero + 1.0, -sin, z8, zero, z8, sin, zero], axis=1)


def _pad_heads_cols(w, heads):
    k = w.shape[0]
    return jnp.pad(w.reshape(k, heads, ATT_HEAD_DIM), ((0, 0), (0, 0), (0, HEAD_PAD - ATT_HEAD_DIM))).reshape(k, heads * HEAD_PAD)


def _unpad_heads_cols(w, heads):
    k = w.shape[0]
    return w.reshape(k, heads, HEAD_PAD)[:, :, :ATT_HEAD_DIM].reshape(k, heads * ATT_HEAD_DIM)


def kernel(x, mem, positions, norm_ffn1, w_ffn1_gu, w_ffn1_down, norm_mix, w_in_even, gm_ln_g, gm_ln_b, gm_ws, gm_bs, conv_w, conv_b, dt_bias, a_log, d_skip, ssd_norm, w_out_even, w_qkv, b_qkv, sinks, w_o_odd, norm_xq, norm_mem, w_xq, w_xkv, w_xo, norm_ffn2, w_ffn2_gu, w_ffn2_down, final_norm, loss_target, m_norm_ffn1, m_w_ffn1_gu, m_w_ffn1_down, m_norm_mix, m_w_in_even, m_gm_ln_g, m_gm_ln_b, m_gm_ws, m_gm_bs, m_conv_w, m_conv_b, m_dt_bias, m_a_log, m_d_skip, m_ssd_norm, m_w_out_even, m_w_qkv, m_b_qkv, m_sinks, m_w_o_odd, m_norm_xq, m_norm_mem, m_w_xq, m_w_xkv, m_w_xo, m_norm_ffn2, m_w_ffn2_gu, m_w_ffn2_down, m_final_norm, v_norm_ffn1, v_w_ffn1_gu, v_w_ffn1_down, v_norm_mix, v_w_in_even, v_gm_ln_g, v_gm_ln_b, v_gm_ws, v_gm_bs, v_conv_w, v_conv_b, v_dt_bias, v_a_log, v_d_skip, v_ssd_norm, v_w_out_even, v_w_qkv, v_b_qkv, v_sinks, v_w_o_odd, v_norm_xq, v_norm_mem, v_w_xq, v_w_xkv, v_w_xo, v_norm_ffn2, v_w_ffn2_gu, v_w_ffn2_down, v_final_norm):
    env = dict(locals())
    W = {n: env[n] for n in ORDER}
    M = {n: env["m_" + n] for n in ORDER}
    V = {n: env["v_" + n] for n in ORDER}
    seq = x.shape[1]
    x0 = x.reshape(seq, D_MODEL)
    mem2 = mem.reshape(-1, D_MODEL)
    target = loss_target.reshape(seq, D_MODEL)

    fr = _Freight()
    b16 = lambda a: a.astype(BF16)
    grp_a = {"out": w_out_even[0], "xq0": w_xq[0], "xkv0": w_xkv[0], "xo0": w_xo[0]}
    grp_b = {"qkv": w_qkv[0], "o": w_o_odd[0], "xq1": w_xq[1], "xkv1": w_xkv[1], "xo1": w_xo[1]}
    half_rows = D_MODEL // 2
    fr.put("l0f1_gate", "dn1_0", b16(w_ffn1_down[0]), GATHER)
    fr.put("l0f1_gate", "dn2_0", b16(w_ffn2_down[0]), GATHER)
    fr.put("l0f1_up", "in", b16(w_in_even[0]), GATHER)
    for key, w in grp_a.items():
        fr.put("l0f1_down", key, b16(w), GATHER)
    fr.put("l0_proj", "gu2_0", b16(w_ffn2_gu[0]), GATHER)
    fr.put("l0f2_gate", "gu1_1a", b16(w_ffn1_gu[1, :half_rows]), GATHER)
    fr.put("l0f2_gate", "dn1_1", b16(w_ffn1_down[1]), GATHER)
    fr.put("l0f2_up", "gu1_1b", b16(w_ffn1_gu[1, half_rows:]), GATHER)
    for key, w in grp_b.items():
        fr.put("l0f2_down", key, b16(w), GATHER)
    fr.put("l1f1_gate", "gu2_1a", b16(w_ffn2_gu[1, :half_rows]), GATHER)
    fr.put("l1f1_gate", "dn2_1", b16(w_ffn2_down[1]), GATHER)
    fr.put("l1f1_up", "gu2_1b", b16(w_ffn2_gu[1, half_rows:]), GATHER)
    gu_first = _exchange(b16(w_ffn1_gu[0]), GATHER, "ag_l0f1_gu")
    gs = _all_gather(_pack([conv_w, b_qkv]), "ag_small")
    gs = [_unpack(gs[k], [conv_w.shape, b_qkv.shape]) for k in range(N_DEV)]
    conv_w_full = jnp.concatenate([g[0][0] for g in gs], axis=1)
    b_qkv_full = jnp.concatenate([g[1][0] for g in gs], axis=0)

    def gate_up(g):
        return _cols_to_full(g[:N_DEV // 2]), _cols_to_full(g[N_DEV // 2:])

    down = lambda key: fr.landed[key].reshape(D_FF, D_MODEL)

    row = lambda a: a.reshape(1, -1)
    pad128 = lambda a: jnp.pad(a.reshape(1, -1), ((0, 0), (0, 128 - a.size)))
    bst = jnp.pad(gm_bs[0].T, ((0, 0), (0, 128 - GM_GROUPS)))
    ws = gm_ws[0]
    dtb, alog, dsk, snk = pad128(dt_bias), pad128(a_log), pad128(d_skip), pad128(sinks)
    expand = (jnp.arange(128)[:, None] == (jnp.arange(D_MODEL) // SSD_HEAD_DIM)[None, :]).astype(F32)
    tab = _rope_table(positions)

    wg_f1a, wu_f1a = gate_up(gu_first)
    xa, s_f1a, wd_f1a = _ffn_fwd(x0, row(norm_ffn1[0]), wg_f1a, wu_f1a, lambda: down("dn1_0"), "l0f1", fr)
    w_in = _cols_to_full(fr.landed["in"])
    n_main = 2 * GM_GROUPS * GM_GDIM + D_MODEL + XBC_WIDTH
    w_main = w_in[:, :n_main]
    w_dt = jnp.pad(w_in[:, n_main:], ((0, 0), (0, 128 - SSD_HEADS)))
    a_xq, a_xkv, a_xo = fr.landed["xq0"], fr.landed["xkv0"], fr.landed["xo0"]
    w_out = fr.landed["out"].reshape(2 * D_MODEL, D_MODEL)
    even_w = (w_main, w_dt, w_out[:D_MODEL], w_out[D_MODEL:])
    even_p = (row(norm_mix[0]), gm_ln_g, gm_ln_b, ws, bst, conv_w_full, conv_b, dtb, alog, dsk, ssd_norm, expand)
    xb, s_even = _even_fwd(xa, even_w, even_p, fr)
    x0_w = (a_xq.reshape(D_MODEL, -1), a_xkv.reshape(D_MODEL, -1), _cols_to_full(a_xo))
    xc, s_x0 = _xattn_layer_fwd(xb, mem2, row(norm_xq[0]), row(norm_mem[0]), *x0_w, "l0x", fr)
    wg_f2a, wu_f2a = gate_up(fr.landed["gu2_0"])
    xd, s_f2a, wd_f2a = _ffn_fwd(xc, row(norm_ffn2[0]), wg_f2a, wu_f2a, down("dn2_0"), "l0f2", fr)
    wg_f1b, wu_f1b = gate_up(jnp.concatenate([fr.landed["gu1_1a"], fr.landed["gu1_1b"]], axis=1))
    xe, s_f1b, wd_f1b = _ffn_fwd(xd, row(norm_ffn1[1]), wg_f1b, wu_f1b, down("dn1_1"), "l1f1", fr)
    b_xq, b_xkv, b_xo = fr.landed["xq1"], fr.landed["xkv1"], fr.landed["xo1"]
    nq = ATT_HEADS * ATT_HEAD_DIM
    wqkv = _cols_to_full(fr.landed["qkv"])
    w_o = _pad_heads_cols(fr.landed["o"].reshape(D_MODEL, D_MODEL).T, ATT_HEADS).T
    odd_w = (_pad_heads_cols(wqkv[:, :nq], ATT_HEADS), _pad_heads_cols(wqkv[:, nq:], 2 * ATT_KV_HEADS), w_o)
    odd_p = (row(norm_mix[1]), _pad_heads_cols(b_qkv_full[None, :nq], ATT_HEADS),
             _pad_heads_cols(b_qkv_full[None, nq:], 2 * ATT_KV_HEADS), tab, snk)
    xf, s_odd = _odd_fwd(xe, odd_w, odd_p, fr)
    x1_w = (b_xq.reshape(D_MODEL, -1), b_xkv.reshape(D_MODEL, -1), _cols_to_full(b_xo))
    xg, s_x1 = _xattn_layer_fwd(xf, mem2, row(norm_xq[1]), row(norm_mem[1]), *x1_w, "l1x", fr)
    wg_f2b, wu_f2b = gate_up(jnp.concatenate([fr.landed["gu2_1a"], fr.landed["gu2_1b"]], axis=1))
    xh, s_f2b, wd_f2b = _ffn_fwd(xg, row(norm_ffn2[1]), wg_f2b, wu_f2b, down("dn2_1"), "l1f2", fr)
    loss8, dx, d_final = _loss_fwd_bwd(xh, target, row(final_norm), "loss")
    loss = lax.psum(loss8[0, 0], ("x", "y", "c"))
    assert not fr.load, sorted(fr.load)

    later = {"l1f2_dh_u": "l1f1_dact", "l1f1_dh_u": "l0f2_dact"}

    def shipper(tag):
        return lambda carrier, key, parts: fr.put(later.get(carrier, carrier), f"{tag}_{key}", parts, SCATTER)

    dx, dn_f2b, _, _, _ = _ffn_bwd(dx, s_f2b, row(norm_ffn2[1]), wg_f2b, wu_f2b, wd_f2b, "l1f2", fr, shipper("l1f2"))
    dx, dn_xq1, dn_mem1, dwxq1, dwxkv1, dwxo1 = _xattn_layer_bwd(dx, s_x1, mem2, row(norm_xq[1]), row(norm_mem[1]), *x1_w, "l1x", fr)
    dx, (dn_mix1, db_q, db_kv, d_snk), (dw_q, dw_kv, dw_o) = _odd_bwd(dx, s_odd, odd_w, odd_p, fr)
    rows_parts = lambda g: g.reshape(N_DEV, -1, g.shape[1])
    dwqkv = jnp.concatenate([_unpad_heads_cols(dw_q, ATT_HEADS), _unpad_heads_cols(dw_kv, 2 * ATT_KV_HEADS)], axis=1)
    parts_b = {"d_qkv": _full_to_cols(dwqkv), "d_o": rows_parts(_unpad_heads_cols(dw_o.T, ATT_HEADS).T), "d_xq1": rows_parts(dwxq1),
               "d_xkv1": rows_parts(dwxkv1), "d_xo1": _full_to_cols(dwxo1)}
    for key, parts in parts_b.items():
        fr.put("l1f1_dwd" if key in ("d_qkv", "d_o") else "l1f1_dwu", key, parts, SCATTER)
    dx, dn_f1b, _, _, _ = _ffn_bwd(dx, s_f1b, row(norm_ffn1[1]), wg_f1b, wu_f1b, wd_f1b, "l1f1", fr, shipper("l1f1"))
    dx, dn_f2a, _, _, _ = _ffn_bwd(dx, s_f2a, row(norm_ffn2[0]), wg_f2a, wu_f2a, wd_f2a, "l0f2", fr, shipper("l0f2"))
    dx, dn_xq0, dn_mem0, dwxq0, dwxkv0, dwxo0 = _xattn_layer_bwd(dx, s_x0, mem2, row(norm_xq[0]), row(norm_mem[0]), *x0_w, "l0x", fr)
    for key, parts in {"d_xq0": rows_parts(dwxq0), "d_xkv0": rows_parts(dwxkv0), "d_xo0": _full_to_cols(dwxo0)}.items():
        fr.put("l0_da", key, parts, SCATTER)
    ship_out = lambda dw_out: fr.put("l0_dh_uv", "d_out", rows_parts(dw_out), SCATTER)
    dx, small_even, (dw_uv, dw_z, dw_xbc, dw_dt, _) = _even_bwd(dx, s_even, even_w, even_p, fr, ship_out)
    dn_mix0, d_lng, d_lnb, d_ws, d_bst, d_conv_w, d_conv_b, d_dtb, d_alog, d_dsk, d_ssd_norm = small_even
    d_in = _full_to_cols(jnp.concatenate([dw_uv, dw_z, dw_xbc, dw_dt[:, :SSD_HEADS]], axis=1))
    fr.put("l0f1_dact", "d_in_a", d_in[:, :half_rows], SCATTER)
    fr.put("l0f1_dwd", "d_in_b", d_in[:, half_rows:], SCATTER)
    dx, dn_f1a, _, _, _ = _ffn_bwd(dx, s_f1a, row(norm_ffn1[0]), wg_f1a, wu_f1a, wd_f1a, "l0f1", fr, shipper("l0f1"))
    assert not fr.load, sorted(fr.load)
    grad_x = dx[0].reshape(x.shape)

    got = fr.landed
    received = {
        "w_ffn1_gu": [got["l0f1_gu_a"], got["l0f1_gu_b"], got["l1f1_gu_a"], got["l1f1_gu_b"]],
        "w_ffn2_gu": [got["l0f2_gu_a"], got["l0f2_gu_b"], got["l1f2_gu_a"], got["l1f2_gu_b"]],
        "w_ffn1_down": [got["l0f1_dn"], got["l1f1_dn"]], "w_ffn2_down": [got["l0f2_dn"], got["l1f2_dn"]],
        "w_in_even": [got["d_in_a"], got["d_in_b"]], "w_out_even": [got["d_out"]], "w_qkv": [got["d_qkv"]], "w_o_odd": [got["d_o"]],
        "w_xq": [got["d_xq0"], got["d_xq1"]], "w_xkv": [got["d_xkv0"], got["d_xkv1"]], "w_xo": [got["d_xo0"], got["d_xo1"]],
    }
    out = {}
    for n in SHARDED:
        shp = W[n].shape
        two = lambda a: a.reshape(-1, shp[-1])
        res = _adamw(received[n], two(W[n]), two(M[n]), two(V[n]), f"adam_{n}")
        out[n] = [r.reshape(shp) for r in res]

    db_qkv = jnp.concatenate([_unpad_heads_cols(db_q, ATT_HEADS), _unpad_heads_cols(db_kv, 2 * ATT_KV_HEADS)], axis=1).reshape(-1)
    cw_parts = _full_to_cols(d_conv_w)
    bq_parts = db_qkv.reshape(N_DEV, -1)
    ss_parts = jnp.stack([_pack([cw_parts[k], bq_parts[k]]) for k in range(N_DEV)])
    recv = _all_to_all(ss_parts, "a2a_small")
    res = _adamw(recv, _pack([conv_w, b_qkv]), _pack([m_conv_w, m_b_qkv]), _pack([v_conv_w, v_b_qkv]), "adam_small_sharded")
    res = [_unpack(r, [conv_w.shape, b_qkv.shape]) for r in res]
    out["conv_w"] = [r[0] for r in res]
    out["b_qkv"] = [r[1] for r in res]

    small_grads = {
        "norm_ffn1": jnp.concatenate([dn_f1a, dn_f1b]), "norm_mix": jnp.concatenate([dn_mix0, dn_mix1]),
        "gm_ln_g": d_lng, "gm_ln_b": d_lnb, "gm_ws": d_ws[None], "gm_bs": d_bst[:, :GM_GROUPS].T[None],
        "conv_b": d_conv_b, "dt_bias": d_dtb[:, :SSD_HEADS], "a_log": d_alog[:, :SSD_HEADS], "d_skip": d_dsk[:, :SSD_HEADS],
        "ssd_norm": d_ssd_norm, "sinks": d_snk[:, :ATT_HEADS], "norm_xq": jnp.concatenate([dn_xq0, dn_xq1]),
        "norm_mem": jnp.concatenate([dn_mem0, dn_mem1]), "norm_ffn2": jnp.concatenate([dn_f2a, dn_f2b]),
        "final_norm": d_final.reshape(-1),
    }
    shapes = [W[n].shape for n in SMALL]
    recv = _all_gather(_pack([small_grads[n] for n in SMALL]), "ag_small_grads")
    res = _adamw(recv, _pack([W[n] for n in SMALL]), _pack([M[n] for n in SMALL]), _pack([V[n] for n in SMALL]), "adam_small")
    res = [_unpack(r, shapes) for r in res]
    for i, n in enumerate(SMALL):
        out[n] = [r[i] for r in res]

    return (loss, grad_x, *[out[n][0] for n in ORDER], *[out[n][1] for n in ORDER], *[out[n][2] for n in ORDER],
            *[out[n][3] for n in ORDER])
```
